```python
import math
import jax, jax.numpy as jnp
from jax import lax
import numpy as np

D_MODEL = 2048
BATCH = 8
SEQ = 2048
DEPTH = 2

CTX_LEN = 256
GRID_W = 64
EPS = 1e-6

D_MIX = D_MODEL
CONV_W = D_MIX // 4
MLA_HEADS = 8
MLA_V_DIM = 128
MLA_W = MLA_HEADS * MLA_V_DIM
HYENA_W = D_MIX - CONV_W - MLA_W

CONV_KSIZE = 31

MLA_Q_RANK = 768
MLA_KV_RANK = 512
MLA_NOPE = 128
MLA_ROPE = 64
ROPE_FREQS = MLA_ROPE // 4
ROPE_THETA = 10000.0
MLA_SCALE = (MLA_NOPE + MLA_ROPE) ** -0.5
Q_BLOCK = 128

HY_ORDER = 2
HY_SHORT = 3
HY_EMB = 33
HY_BANDS = (HY_EMB - 1) // 2
HY_FFN = 64
HY_TARGET = 1e-2
HY_FAST_PCT = 0.3
HY_SLOW_PCT = 1.5
HY_MIN_DECAY = math.log(HY_TARGET) / HY_SLOW_PCT
HY_MAX_DECAY = math.log(HY_TARGET) / HY_FAST_PCT

N_EXPERTS = 16
N_GROUPS = 4
EXPERTS_PER_GROUP = N_EXPERTS // N_GROUPS
TOP_K = 2
D_FF_EXPERT = 1024

OFF_Q = 2 * CONV_W
OFF_KV = OFF_Q + MLA_Q_RANK
OFF_HY = OFF_KV + MLA_KV_RANK + MLA_ROPE
N_IN = OFF_HY + (HY_ORDER + 1) * HYENA_W

kernel_name = "hybrid_conv_mla_hyena_moe_dit"


def rmsnorm(x, g):
    xf = x.astype(jnp.float32)
    y = xf * lax.rsqrt(jnp.mean(xf * xf, axis=-1, keepdims=True) + EPS)
    return (y * g.astype(jnp.float32)).astype(x.dtype)


def layernorm(x, g, b):
    xf = x.astype(jnp.float32)
    mu = jnp.mean(xf, axis=-1, keepdims=True)
    var = jnp.mean(jnp.square(xf - mu), axis=-1, keepdims=True)
    y = (xf - mu) * lax.rsqrt(var + EPS)
    return (y * g.astype(jnp.float32) + b.astype(jnp.float32)).astype(x.dtype)


def modulation(cvec, w_ada, b_ada):
    m = jax.nn.silu(cvec) @ w_ada + b_ada
    return jnp.split(m[:, None, :], 6, axis=-1)


def modulate(h, shift, scale):
    return h * (1 + scale) + shift


def depthwise_conv(z, w, b):
    k = w.shape[0]
    y = lax.conv_general_dilated(z, w[:, None, :].astype(z.dtype), window_strides=(1,),
                                 padding=[(k // 2, k // 2)],
                                 dimension_numbers=('NWC', 'WIO', 'NWC'),
                                 feature_group_count=z.shape[-1])
    return y + b


def axial_rope_tables(L, dtype):
    rows = L // GRID_W
    r = jnp.repeat(jnp.arange(rows), GRID_W)
    col = jnp.tile(jnp.arange(GRID_W), rows)
    pos = jnp.stack([r, col], axis=-1).astype(jnp.float32)
    inv = ROPE_THETA ** (-jnp.arange(ROPE_FREQS, dtype=jnp.float32) / ROPE_FREQS)
    ang = pos[:, :, None] * inv
    return jnp.cos(ang).astype(dtype), jnp.sin(ang).astype(dtype)


def apply_rope(x, cos, sin):
    xs = x.reshape(x.shape[:-1] + (2, 2, ROPE_FREQS))
    x1, x2 = xs[..., 0, :], xs[..., 1, :]
    out = jnp.stack([x1 * cos - x2 * sin, x1 * sin + x2 * cos], axis=-2)
    return out.reshape(x.shape)


def conformer_conv(p, dw_w, dw_b, ln_g, ln_b):
    a, g = jnp.split(p, 2, axis=-1)
    z = depthwise_conv(a * jax.nn.sigmoid(g), dw_w, dw_b)
    return jax.nn.silu(layernorm(z, ln_g, ln_b))


def mla_q(p_q, g, w_uq, rope):
    B, L, _ = p_q.shape
    q = (rmsnorm(p_q, g) @ w_uq).reshape(B, L, MLA_HEADS, MLA_NOPE + MLA_ROPE)
    if rope is None:
        return q
    cos, sin = rope
    return jnp.concatenate([q[..., :MLA_NOPE],
                            apply_rope(q[..., MLA_NOPE:], cos[:, None], sin[:, None])], axis=-1)


def mla_kv(p_kv, g, w_ukv, rope):
    B, L, _ = p_kv.shape
    kv = (rmsnorm(p_kv[..., :MLA_KV_RANK], g) @ w_ukv).reshape(B, L, MLA_HEADS, MLA_NOPE + MLA_V_DIM)
    k_rope = p_kv[..., MLA_KV_RANK:]
    if rope is not None:
        k_rope = apply_rope(k_rope, *rope)
    k = jnp.concatenate([kv[..., :MLA_NOPE],
                         jnp.broadcast_to(k_rope[:, :, None, :], (B, L, MLA_HEADS, MLA_ROPE))], axis=-1)
    return k, kv[..., MLA_NOPE:]


def block_attention(q, k, v):
    B, L, H, Dq = q.shape
    nb = L // Q_BLOCK
    qb = q.reshape(B, nb, Q_BLOCK, H, Dq).transpose(1, 0, 2, 3, 4)

    def one_block(qi):
        s = jnp.einsum('bqhd,bkhd->bhqk', qi, k, preferred_element_type=jnp.float32) * MLA_SCALE
        prob = jax.nn.softmax(s, axis=-1).astype(v.dtype)
        return jnp.einsum('bhqk,bkhd->bqhd', prob, v)

    o = lax.map(one_block, qb)
    return o.transpose(1, 0, 2, 3, 4).reshape(B, L, H * v.shape[-1])


def hyena_filters(L, w1, b1, fr1, w2, b2, fr2, w3, b3):
    f32 = jnp.float32
    t = jnp.arange(L, dtype=f32)
    t_norm = t / max(L - 1, 1)
    ang = (2.0 * math.pi / L) * t[:, None] * jnp.linspace(1e-4, HY_BANDS - 1, HY_BANDS, dtype=f32)[None, :]
    z = jnp.concatenate([t_norm[:, None], jnp.cos(ang), -jnp.sin(ang)], axis=-1)
    h = jnp.sin(fr1.astype(f32) * (z @ w1.astype(f32) + b1.astype(f32)))
    h = jnp.sin(fr2.astype(f32) * (h @ w2.astype(f32) + b2.astype(f32)))
    h = h @ w3.astype(f32) + b3.astype(f32)
    deltas = jnp.abs(jnp.linspace(HY_MIN_DECAY, HY_MAX_DECAY, HYENA_W, dtype=f32))
    window = jnp.exp(-t_norm[:, None] * deltas[None, :])
    return h.reshape(L, HY_ORDER, 2, HYENA_W) * window[:, None, None, :]


def bidirectional_long_conv(u, h, bias):
    L, C = h.shape[0], h.shape[-1]
    kern = jnp.concatenate([h[:, 0], jnp.zeros((1, C), h.dtype), h[:0:-1, 1]], axis=0)
    uf = u.astype(jnp.float32)
    spec = jnp.fft.rfft(uf, n=2 * L, axis=1) * jnp.fft.rfft(kern, n=2 * L, axis=0)[None]
    y = jnp.fft.irfft(spec, n=2 * L, axis=1)[:, :L]
    return (y + uf * bias.astype(jnp.float32)).astype(u.dtype)


def hyena(p, short_w, short_b, filt, bias):
    u = depthwise_conv(p, short_w, short_b)
    v, x1, x2 = jnp.split(u, 3, axis=-1)
    z = x1 * bidirectional_long_conv(v, filt[:, 0], bias[0])
    return x2 * bidirectional_long_conv(z, filt[:, 1], bias[1])


def mixer_outputs(p, k_all, v_all, rope, filt, dw_w, dw_b, ln_g, ln_b, q_g, w_uq, sh_w, sh_b, hy_bias):
    conv_out = conformer_conv(p[..., :OFF_Q], dw_w, dw_b, ln_g, ln_b)
    q = mla_q(p[..., OFF_Q:OFF_KV], q_g, w_uq, rope)
    att_out = block_attention(q, k_all, v_all)
    hy_out = hyena(p[..., OFF_HY:], sh_w, sh_b, filt, hy_bias)
    return jnp.concatenate([conv_out, att_out, hy_out], axis=-1)


def moe(h, w_router, router_bias, w_gate, w_up, w_down):
    B, L, D = h.shape
    t = h.reshape(B * L, D)
    scores = jax.nn.sigmoid((t @ w_router).astype(jnp.float32))
    sel = scores + router_bias.astype(jnp.float32)
    gscore = lax.top_k(sel.reshape(-1, N_GROUPS, EXPERTS_PER_GROUP), TOP_K)[0].sum(-1)
    gbest = jnp.argmax(gscore, axis=-1)
    in_group = (jnp.arange(N_EXPERTS) // EXPERTS_PER_GROUP)[None, :] == gbest[:, None]
    _, idx = lax.top_k(jnp.where(in_group, sel, -jnp.inf), TOP_K)
    gsel = jnp.take_along_axis(scores, idx, axis=-1)
    gsel = gsel / jnp.sum(gsel, axis=-1, keepdims=True)
    combine = jnp.sum(jax.nn.one_hot(idx, N_EXPERTS, dtype=jnp.float32) * gsel[..., None], axis=1)
    y = jnp.zeros_like(t)
    for e in range(N_EXPERTS):
        he = jax.nn.silu(t @ w_gate[e]) * (t @ w_up[e])
        y = y + combine[:, e:e + 1].astype(t.dtype) * (he @ w_down[e])
    return y.reshape(B, L, D)


def setup_inputs(seed: int = 0) -> dict:
    key = jax.random.key(seed)
    keys = iter(jax.random.split(key, 40))

    def nrm(shape, scale):
        return scale * jax.random.normal(next(keys), shape, jnp.float32)

    def gain(shape):
        return 1.0 + nrm(shape, 0.02)

    D = D_MODEL
    return {
        "x": nrm((BATCH, SEQ, D), 1.0),
        "c": nrm((BATCH, D), 1.0),
        "ctx": nrm((BATCH, CTX_LEN, D), 1.0),
        "c_ctx": nrm((D,), 1.0),
        "norm1_g": gain((DEPTH, D)),
        "norm2_g": gain((DEPTH, D)),
        "w_ada": nrm((DEPTH, D, 6 * D), 0.5 * D ** -0.5),
        "b_ada": nrm((DEPTH, 6 * D), 0.02),
        "w_in": nrm((DEPTH, D, N_IN), D ** -0.5),
        "conv_dw_w": nrm((DEPTH, CONV_KSIZE, CONV_W), CONV_KSIZE ** -0.5),
        "conv_dw_b": nrm((DEPTH, CONV_W), 0.02),
        "conv_ln_g": gain((DEPTH, CONV_W)),
        "conv_ln_b": nrm((DEPTH, CONV_W), 0.02),
        "q_norm_g": gain((DEPTH, MLA_Q_RANK)),
        "w_uq": nrm((DEPTH, MLA_Q_RANK, MLA_HEADS * (MLA_NOPE + MLA_ROPE)), MLA_Q_RANK ** -0.5),
        "kv_norm_g": gain((DEPTH, MLA_KV_RANK)),
        "w_ukv": nrm((DEPTH, MLA_KV_RANK, MLA_HEADS * (MLA_NOPE + MLA_V_DIM)), MLA_KV_RANK ** -0.5),
        "hy_short_w": nrm((DEPTH, HY_SHORT, (HY_ORDER + 1) * HYENA_W), HY_SHORT ** -0.5),
        "hy_short_b": nrm((DEPTH, (HY_ORDER + 1) * HYENA_W), 0.02),
        "hy_w1": nrm((DEPTH, HY_EMB, HY_FFN), HY_EMB ** -0.5),
        "hy_b1": nrm((DEPTH, HY_FFN), 0.02),
        "hy_freq1": gain((DEPTH, HY_FFN)),
        "hy_w2": nrm((DEPTH, HY_FFN, HY_FFN), HY_FFN ** -0.5),
        "hy_b2": nrm((DEPTH, HY_FFN), 0.02),
        "hy_freq2": gain((DEPTH, HY_FFN)),
        "hy_w3": nrm((DEPTH, HY_FFN, HY_ORDER * 2 * HYENA_W), 0.02),
        "hy_b3": nrm((DEPTH, HY_ORDER * 2 * HYENA_W), 0.01),
        "hy_bias": nrm((DEPTH, HY_ORDER, HYENA_W), 0.1),
        "w_out": nrm((DEPTH, D_MIX, D), D_MIX ** -0.5),
        "w_router": nrm((D, N_EXPERTS), D ** -0.5),
        "router_bias": nrm((N_EXPERTS,), 0.01),
        "w_gate": nrm((DEPTH, N_EXPERTS, D, D_FF_EXPERT), D ** -0.5),
        "w_up": nrm((DEPTH, N_EXPERTS, D, D_FF_EXPERT), D ** -0.5),
        "w_down": nrm((DEPTH, N_EXPERTS, D_FF_EXPERT, D), D_FF_EXPERT ** -0.5),
        "final_norm_g": gain((D,)),
    }


def reference(x, c, ctx, c_ctx, norm1_g, norm2_g, w_ada, b_ada, w_in, conv_dw_w, conv_dw_b, conv_ln_g,
              conv_ln_b, q_norm_g, w_uq, kv_norm_g, w_ukv, hy_short_w, hy_short_b, hy_w1, hy_b1, hy_freq1,
              hy_w2, hy_b2, hy_freq2, hy_w3, hy_b3, hy_bias, w_out, w_router, router_bias, w_gate, w_up,
              w_down, final_norm_g):
    L = x.shape[1]
    Lc = ctx.shape[1]
    rope = axial_rope_tables(L, x.dtype)
    xl, xc = x, ctx
    for i in range(DEPTH):
        last = i == DEPTH - 1
        sl1, scl1, gl1, sl2, scl2, gl2 = modulation(c, w_ada[i], b_ada[i])
        sc1, scc1, gc1, sc2, scc2, gc2 = modulation(c_ctx[None, :], w_ada[i], b_ada[i])

        hl = modulate(rmsnorm(xl, norm1_g[i]), sl1, scl1)
        hc = modulate(rmsnorm(xc, norm1_g[i]), sc1, scc1)
        pl = hl @ w_in[i]
        if last:
            pc_kv = hc @ w_in[i][:, OFF_KV:OFF_HY]
        else:
            pc = hc @ w_in[i]
            pc_kv = pc[..., OFF_KV:OFF_HY]
        kc, vc = mla_kv(pc_kv, kv_norm_g[i], w_ukv[i], None)
        kl, vl = mla_kv(pl[..., OFF_KV:OFF_HY], kv_norm_g[i], w_ukv[i], rope)
        k_all = jnp.concatenate([kc, kl], axis=1)
        v_all = jnp.concatenate([vc, vl], axis=1)

        filt_l = hyena_filters(L, hy_w1[i], hy_b1[i], hy_freq1[i], hy_w2[i], hy_b2[i], hy_freq2[i],
                               hy_w3[i], hy_b3[i])
        ol = mixer_outputs(pl, k_all, v_all, rope, filt_l, conv_dw_w[i], conv_dw_b[i], conv_ln_g[i],
                           conv_ln_b[i], q_norm_g[i], w_uq[i], hy_short_w[i], hy_short_b[i], hy_bias[i])
        xl = xl + gl1 * (ol @ w_out[i])

        if not last:
            filt_c = hyena_filters(Lc, hy_w1[i], hy_b1[i], hy_freq1[i], hy_w2[i], hy_b2[i], hy_freq2[i],
                                   hy_w3[i], hy_b3[i])
            oc = mixer_outputs(pc, kc, vc, None, filt_c, conv_dw_w[i], conv_dw_b[i], conv_ln_g[i],
                               conv_ln_b[i], q_norm_g[i], w_uq[i], hy_short_w[i], hy_short_b[i], hy_bias[i])
            xc = xc + gc1 * (oc @ w_out[i])

        hl2 = modulate(rmsnorm(xl, norm2_g[i]), sl2, scl2)
        xl = xl + gl2 * moe(hl2, w_router, router_bias, w_gate[i], w_up[i], w_down[i])
        if not last:
            hc2 = modulate(rmsnorm(xc, norm2_g[i]), sc2, scc2)
            xc = xc + gc2 * moe(hc2, w_router, router_bias, w_gate[i], w_up[i], w_down[i])

    return rmsnorm(xl, final_norm_g)
```

```python
import functools
import math

import jax
import jax.numpy as jnp
from jax import lax
from jax.experimental import pallas as pl
from jax.experimental.pallas import tpu as pltpu

F32 = jnp.float32
BF16 = jnp.bfloat16

EPS = 1e-6
GRID_W = 64
CONV_W = 512
CONV_KSIZE = 31
MLA_HEADS = 8
MLA_NOPE = 128
MLA_ROPE = 64
MLA_V_DIM = 128
MLA_Q_RANK = 768
MLA_KV_RANK = 512
ROPE_FREQS = MLA_ROPE // 4
ROPE_THETA = 10000.0
MLA_SCALE = (MLA_NOPE + MLA_ROPE) ** -0.5
HYENA_W = 512
HY_ORDER = 2
HY_SHORT = 3
HY_EMB = 33
HY_BANDS = (HY_EMB - 1) // 2
HY_FFN = 64
HY_MIN_DECAY = math.log(1e-2) / 1.5
HY_MAX_DECAY = math.log(1e-2) / 0.3
N_EXPERTS = 16
N_GROUPS = 4
EXPERTS_PER_GROUP = N_EXPERTS // N_GROUPS
OFF_Q = 2 * CONV_W
OFF_KV = OFF_Q + MLA_Q_RANK
OFF_KR = OFF_KV + MLA_KV_RANK
OFF_HY = OFF_KR + MLA_ROPE

LANES = 128
QK_HEAD = 2 * LANES
V7X_VMEM_BYTES = 64 * 1024 * 1024
VMEM_LIMIT = (V7X_VMEM_BYTES * 7) // 8
MOE_TILE = 512
NEG_INF = float("-inf")


def _cp(*sem):
    return pltpu.CompilerParams(dimension_semantics=sem, vmem_limit_bytes=VMEM_LIMIT)


def _dot(a, b):
    return jnp.dot(a, b, preferred_element_type=F32)


def _rms(x, g):
    ms = jnp.mean(x * x, axis=-1, keepdims=True)
    return x * lax.rsqrt(ms + EPS) * g


def _ada_kernel(c_ref, w_ref, b_ref, o_ref):
    c = c_ref[...]
    a = (c * jax.nn.sigmoid(c)).astype(BF16)
    o_ref[...] = _dot(a, w_ref[...].astype(BF16)) + b_ref[...]


def _ada_call(cvec, w, b):
    m, d = cvec.shape
    n = w.shape[1]
    tn = 1024
    return pl.pallas_call(
        _ada_kernel,
        grid=(n // tn,),
        in_specs=[pl.BlockSpec((m, d), lambda j: (0, 0)),
                  pl.BlockSpec((d, tn), lambda j: (0, j)),
                  pl.BlockSpec((1, tn), lambda j: (0, j))],
        out_specs=pl.BlockSpec((m, tn), lambda j: (0, j)),
        out_shape=jax.ShapeDtypeStruct((m, n), F32),
        compiler_params=_cp("parallel"),
        name="ada",
    )(cvec, w, b.reshape(1, n))


def _norm_mod_kernel(x_ref, g_ref, sh_ref, sc_ref, o_ref):
    y = _rms(x_ref[0], g_ref[...])
    o_ref[0] = (y * (1.0 + sc_ref[0]) + sh_ref[0]).astype(o_ref.dtype)


def _norm_mod_call(x, g, shift, scale):
    b, l, d = x.shape
    tl = min(l, 512)
    row = pl.BlockSpec((1, 1, d), lambda i, j: (i, 0, 0))
    return pl.pallas_call(
        _norm_mod_kernel,
        grid=(b, l // tl),
        in_specs=[pl.BlockSpec((1, tl, d), lambda i, j: (i, j, 0)),
                  pl.BlockSpec((1, d), lambda i, j: (0, 0)), row, row],
        out_specs=pl.BlockSpec((1, tl, d), lambda i, j: (i, j, 0)),
        out_shape=jax.ShapeDtypeStruct((b, l, d), BF16),
        compiler_params=_cp("parallel", "parallel"),
        name="norm_mod",
    )(x, g.reshape(1, d), shift, scale)


def _linear_kernel(a_ref, w_ref, o_ref):
    o_ref[...] = _dot(a_ref[...], w_ref[...]).astype(o_ref.dtype)


def _linear_call(a, w, tn, out_dtype=F32):
    m, k = a.shape
    n = w.shape[1]
    tm = min(m, 1024)
    return pl.pallas_call(
        _linear_kernel,
        grid=(m // tm, n // tn),
        in_specs=[pl.BlockSpec((tm, k), lambda i, j: (i, 0)),
                  pl.BlockSpec((k, tn), lambda i, j: (0, j))],
        out_specs=pl.BlockSpec((tm, tn), lambda i, j: (i, j)),
        out_shape=jax.ShapeDtypeStruct((m, n), out_dtype),
        compiler_params=_cp("parallel", "parallel"),
        name="linear",
    )(a, w)


CONV_ROWS = 128
CONV_PAD = 16


def _conformer_kernel(p_ref, w_ref, b_ref, g_ref, be_ref, o_ref, zp_ref):
    l = p_ref.shape[1]
    c = CONV_W
    zeros = jnp.zeros((CONV_PAD, c), F32)
    zp_ref[0:CONV_PAD, :] = zeros
    zp_ref[CONV_PAD + l:CONV_PAD + l + CONV_PAD, :] = zeros

    def glu(r, carry):
        base = pl.multiple_of(r * CONV_ROWS, CONV_ROWS)
        blk = p_ref[0, pl.ds(base, CONV_ROWS), :]
        zp_ref[pl.ds(base + CONV_PAD, CONV_ROWS), :] = blk[:, :c] * jax.nn.sigmoid(blk[:, c:])
        return carry

    lax.fori_loop(0, l // CONV_ROWS, glu, 0)

    win_rows = CONV_ROWS + 2 * CONV_PAD
    first = CONV_PAD - CONV_KSIZE // 2

    def conv(r, carry):
        base = pl.multiple_of(r * CONV_ROWS, CONV_ROWS)
        parts = []
        for cc in range(c // LANES):
            lanes = slice(cc * LANES, (cc + 1) * LANES)
            win = zp_ref[pl.ds(base, win_rows), lanes]
            acc = jnp.zeros((CONV_ROWS, LANES), F32)
            for res in range(8):
                taps = [k for k in range(CONV_KSIZE) if (k + first) % 8 == res]
                if not taps:
                    continue
                span = max(k + first for k in taps) - res + CONV_ROWS
                shifted = win[res:res + span, :]
                for k in taps:
                    off = k + first - res
                    acc = acc + shifted[off:off + CONV_ROWS, :] * w_ref[k:k + 1, lanes]
            parts.append(acc)
        z = jnp.concatenate(parts, axis=1) + b_ref[...]
        mu = jnp.mean(z, axis=-1, keepdims=True)
        zc = z - mu
        var = jnp.mean(zc * zc, axis=-1, keepdims=True)
        y = zc * lax.rsqrt(var + EPS) * g_ref[...] + be_ref[...]
        o_ref[0, pl.ds(base, CONV_ROWS), :] = (y * jax.nn.sigmoid(y)).astype(o_ref.dtype)
        return carry

    lax.fori_loop(0, l // CONV_ROWS, conv, 0)


def _conformer_call(p, dw_w, dw_b, ln_g, ln_b):
    b, l, c2 = p.shape
    c = c2 // 2
    vec = pl.BlockSpec((1, c), lambda i: (0, 0))
    return pl.pallas_call(
        _conformer_kernel,
        grid=(b,),
        in_specs=[pl.BlockSpec((1, l, c2), lambda i: (i, 0, 0)),
                  pl.BlockSpec((CONV_KSIZE, c), lambda i: (0, 0)), vec, vec, vec],
        out_specs=pl.BlockSpec((1, l, c), lambda i: (i, 0, 0)),
        out_shape=jax.ShapeDtypeStruct((b, l, c), BF16),
        scratch_shapes=[pltpu.VMEM((l + 2 * CONV_PAD, c), F32)],
        compiler_params=_cp("parallel"),
        name="conformer",
    )(p, dw_w, dw_b.reshape(1, c), ln_g.reshape(1, c), ln_b.reshape(1, c))


def _rotate(group, cs):
    t = group * cs
    return t + pltpu.roll(t, MLA_ROPE, axis=1)


def _q_kernel(p_ref, g_ref, w_ref, cs_ref, o_ref, xn_ref):
    @pl.when(pl.program_id(1) == 0)
    def _():
        xn_ref[...] = _rms(p_ref[...], g_ref[...]).astype(BF16)

    r = _dot(xn_ref[...], w_ref[...])
    o_ref[:, :LANES] = (r[:, :LANES] * MLA_SCALE).astype(BF16)
    o_ref[:, LANES:] = (_rotate(r[:, LANES:], cs_ref[...]) * MLA_SCALE).astype(BF16)


def _q_call(p_q, g, w_q, cs):
    m, k = p_q.shape
    l = cs.shape[0]
    tm = min(l, 512)
    return pl.pallas_call(
        _q_kernel,
        grid=(m // tm, MLA_HEADS),
        in_specs=[pl.BlockSpec((tm, k), lambda i, h: (i, 0)),
                  pl.BlockSpec((1, k), lambda i, h: (0, 0)),
                  pl.BlockSpec((k, QK_HEAD), lambda i, h: (0, h)),
                  pl.BlockSpec((tm, LANES), lambda i, h: (i % (l // tm), 0))],
        out_specs=pl.BlockSpec((tm, QK_HEAD), lambda i, h: (i, h)),
        out_shape=jax.ShapeDtypeStruct((m, MLA_HEADS * QK_HEAD), BF16),
        scratch_shapes=[pltpu.VMEM((tm, k), BF16)],
        compiler_params=_cp("parallel", "arbitrary"),
        name="mla_q",
    )(p_q, g.reshape(1, k), w_q, cs)


def _kv_kernel(p_ref, g_ref, wk_ref, wv_ref, cs_ref, k_ref, v_ref):
    p = p_ref[...]
    xn = _rms(p[:, :MLA_KV_RANK], g_ref[...]).astype(BF16)
    kk = _dot(xn, wk_ref[...])
    v_ref[...] = _dot(xn, wv_ref[...]).astype(BF16)
    rot = _rotate(p[:, MLA_KV_RANK:], cs_ref[...])
    lane = lax.broadcasted_iota(jnp.int32, rot.shape, 1)
    rot = jnp.where(lane < MLA_ROPE, rot, 0.0).astype(BF16)
    for h in range(MLA_HEADS):
        k_ref[:, h * QK_HEAD:h * QK_HEAD + LANES] = kk[:, h * LANES:(h + 1) * LANES].astype(BF16)
        k_ref[:, h * QK_HEAD + LANES:(h + 1) * QK_HEAD] = rot


def _kv_call(p_kv, g, w_k, w_v, cs):
    m, kin = p_kv.shape
    l = cs.shape[0]
    tm = min(l, 512)
    r = MLA_KV_RANK
    return pl.pallas_call(
        _kv_kernel,
        grid=(m // tm,),
        in_specs=[pl.BlockSpec((tm, kin), lambda i: (i, 0)),
                  pl.BlockSpec((1, r), lambda i: (0, 0)),
                  pl.BlockSpec(w_k.shape, lambda i: (0, 0)),
                  pl.BlockSpec(w_v.shape, lambda i: (0, 0)),
                  pl.BlockSpec((tm, LANES), lambda i: (i % (l // tm), 0))],
        out_specs=[pl.BlockSpec((tm, MLA_HEADS * QK_HEAD), lambda i: (i, 0)),
                   pl.BlockSpec((tm, MLA_HEADS * MLA_V_DIM), lambda i: (i, 0))],
        out_shape=[jax.ShapeDtypeStruct((m, MLA_HEADS * QK_HEAD), BF16),
                   jax.ShapeDtypeStruct((m, MLA_HEADS * MLA_V_DIM), BF16)],
        compiler_params=_cp("parallel"),
        name="mla_kv",
    )(p_kv, g.reshape(1, r), w_k, w_v, cs)


def _attn_kernel(*refs, n_seg):
    q_ref, o_ref = refs[0], refs[1 + 2 * n_seg]
    q = q_ref[0]
    nt = (((1,), (1,)), ((), ()))
    scores = [lax.dot_general(q, refs[1 + 2 * s][0], nt, preferred_element_type=F32) for s in range(n_seg)]
    m = functools.reduce(jnp.maximum, [jnp.max(s, axis=-1, keepdims=True) for s in scores])
    den = 0.0
    acc = 0.0
    for s in range(n_seg):
        p = jnp.exp(scores[s] - m)
        den = den + jnp.sum(p, axis=-1, keepdims=True)
        acc = acc + _dot(p.astype(BF16), refs[2 + 2 * s][0])
    o_ref[0] = (acc * (1.0 / den)).astype(o_ref.dtype)


def _attn_call(q, segs):
    b, l, _ = q.shape
    tq = min(l, 256)
    in_specs = [pl.BlockSpec((1, tq, QK_HEAD), lambda i, h, j: (i, j, h))]
    args = [q]
    for k, v in segs:
        lk = k.shape[1]
        in_specs.append(pl.BlockSpec((1, lk, QK_HEAD), lambda i, h, j: (i, 0, h)))
        in_specs.append(pl.BlockSpec((1, lk, MLA_V_DIM), lambda i, h, j: (i, 0, h)))
        args += [k, v]
    return pl.pallas_call(
        functools.partial(_attn_kernel, n_seg=len(segs)),
        grid=(b, MLA_HEADS, l // tq),
        in_specs=in_specs,
        out_specs=pl.BlockSpec((1, tq, MLA_V_DIM), lambda i, h, j: (i, j, h)),
        out_shape=jax.ShapeDtypeStruct((b, l, MLA_HEADS * MLA_V_DIM), BF16),
        compiler_params=_cp("parallel", "parallel", "parallel"),
        name="attention",
    )(*args)


HY_PAD = 8
HY_ROWS = 128


def _hy_short_kernel(p_ref, w_ref, b_ref, o_ref, up_ref):
    l = p_ref.shape[1]
    c = p_ref.shape[2]
    zeros = jnp.zeros((HY_PAD, c), F32)
    up_ref[0:HY_PAD, :] = zeros
    up_ref[HY_PAD + l:HY_PAD + l + HY_PAD, :] = zeros

    def copy(r, carry):
        base = pl.multiple_of(r * HY_ROWS, HY_ROWS)
        up_ref[pl.ds(base + HY_PAD, HY_ROWS), :] = p_ref[0, pl.ds(base, HY_ROWS), :]
        return carry

    lax.fori_loop(0, l // HY_ROWS, copy, 0)
    first = HY_PAD - HY_SHORT // 2

    def conv(r, carry):
        base = pl.multiple_of(r * HY_ROWS, HY_ROWS)
        for cc in range(c // LANES):
            lanes = slice(cc * LANES, (cc + 1) * LANES)
            win = up_ref[pl.ds(base, HY_ROWS + 2 * HY_PAD), lanes]
            acc = jnp.zeros((HY_ROWS, LANES), F32) + b_ref[:, lanes]
            for k in range(HY_SHORT):
                acc = acc + win[first + k:first + k + HY_ROWS, :] * w_ref[k:k + 1, lanes]
            o_ref[0, 0, pl.ds(base, HY_ROWS), lanes] = acc
        return carry

    lax.fori_loop(0, l // HY_ROWS, conv, 0)


def _hy_short_call(p, w, bias):
    b, l, c3 = p.shape
    c = HYENA_W
    return pl.pallas_call(
        _hy_short_kernel,
        grid=(b, c3 // c),
        in_specs=[pl.BlockSpec((1, l, c), lambda i, g: (i, 0, g)),
                  pl.BlockSpec((HY_SHORT, c), lambda i, g: (0, g)),
                  pl.BlockSpec((1, c), lambda i, g: (0, g))],
        out_specs=pl.BlockSpec((1, 1, l, c), lambda i, g: (g, i, 0, 0)),
        out_shape=jax.ShapeDtypeStruct((c3 // c, b, l, c), F32),
        scratch_shapes=[pltpu.VMEM((l + 2 * HY_PAD, c), F32)],
        compiler_params=_cp("parallel", "parallel"),
        name="hyena_short",
    )(p, w, bias.reshape(1, c3))


def _hy_filter_kernel(z_ref, w1_ref, b1_ref, f1_ref, w2_ref, b2_ref, f2_ref, w3_ref, b3_ref, win_ref, o_ref):
    hp = lax.Precision.HIGHEST

    def mm(a, b):
        return jnp.dot(a, b, preferred_element_type=F32, precision=hp)

    h = jnp.sin(f1_ref[...] * (mm(z_ref[...], w1_ref[...]) + b1_ref[...]))
    h = jnp.sin(f2_ref[...] * (mm(h, w2_ref[...]) + b2_ref[...]))
    h = mm(h, w3_ref[...]) + b3_ref[...]
    c = HYENA_W
    win = win_ref[...]
    tl = h.shape[0]
    row = lax.broadcasted_iota(jnp.int32, (tl, c), 0) + pl.program_id(0) * tl
    for o in range(HY_ORDER):
        fwd = h[:, (2 * o) * c:(2 * o + 1) * c] * win
        bwd = jnp.where(row == 0, 0.0, h[:, (2 * o + 1) * c:(2 * o + 2) * c] * win)
        o_ref[:, o * c:(o + 1) * c] = (fwd + bwd).astype(o_ref.dtype)
        o_ref[:, (HY_ORDER + o) * c:(HY_ORDER + o + 1) * c] = (fwd - bwd).astype(o_ref.dtype)


def _pad2(a, rows, cols):
    return jnp.pad(a, ((0, rows - a.shape[0]), (0, cols - a.shape[1])))


def _hy_filter_call(l, w1, b1, f1, w2, b2, f2, w3, b3):
    t = jnp.arange(l, dtype=F32)
    t_norm = t / max(l - 1, 1)
    ang = (2.0 * math.pi / l) * t[:, None] * jnp.linspace(1e-4, HY_BANDS - 1, HY_BANDS, dtype=F32)[None, :]
    z = jnp.concatenate([t_norm[:, None], jnp.cos(ang), -jnp.sin(ang)], axis=-1)
    deltas = jnp.abs(jnp.linspace(HY_MIN_DECAY, HY_MAX_DECAY, HYENA_W, dtype=F32))
    window = jnp.exp(-t_norm[:, None] * deltas[None, :])
    n3 = w3.shape[1]
    tl = min(l, 256)
    full = lambda shape: pl.BlockSpec(shape, lambda i: (0, 0))
    return pl.pallas_call(
        _hy_filter_kernel,
        grid=(l // tl,),
        in_specs=[pl.BlockSpec((tl, LANES), lambda i: (i, 0)),
                  full((LANES, LANES)), full((1, LANES)), full((1, LANES)),
                  full((LANES, LANES)), full((1, LANES)), full((1, LANES)),
                  full((LANES, n3)), full((1, n3)),
                  pl.BlockSpec((tl, HYENA_W), lambda i: (i, 0))],
        out_specs=pl.BlockSpec((tl, n3), lambda i: (i, 0)),
        out_shape=jax.ShapeDtypeStruct((l, n3), BF16),
        compiler_params=_cp("parallel"),
        name="hyena_filter",
    )(_pad2(z, l, LANES), _pad2(w1, LANES, LANES), _pad2(b1[None], 1, LANES), _pad2(f1[None], 1, LANES),
      _pad2(w2, LANES, LANES), _pad2(b2[None], 1, LANES), _pad2(f2[None], 1, LANES),
      _pad2(w3, LANES, n3), b3[None], window)


def _dft_mats(l):
    n = 2 * l
    f = jnp.arange(l, dtype=jnp.int32)[:, None]
    t = jnp.arange(l, dtype=jnp.int32)[None, :]
    ang = ((f * t) % n).astype(F32) * (2.0 * math.pi / n)
    c = jnp.cos(ang)
    s = jnp.sin(ang)
    nyq = jnp.where(t % 2 == 0, 1.0, -1.0).astype(F32)
    first = f == 0
    fwd = jnp.stack([c, jnp.where(first, nyq, -s)]).astype(BF16)
    col0 = t == 0
    inv_r = jnp.where(col0, 1.0 / n, c * (2.0 / n))
    inv_i = jnp.where(col0, nyq.T / n, -s * (2.0 / n))
    inv = jnp.concatenate([inv_r, inv_i], axis=1).astype(BF16)
    return fwd, inv


def _spec_kernel(f_ref, a_ref, o_ref):
    half = a_ref.shape[1] // 2
    hi = _dot(f_ref[1], a_ref[:, half:])
    o_ref[0] = _dot(f_ref[0], a_ref[:, :half])
    o_ref[1] = hi

    @pl.when(pl.program_id(0) == 0)
    def _():
        nyq = _dot(f_ref[1, 0:16, :], a_ref[:, :half])
        row = lax.broadcasted_iota(jnp.int32, nyq.shape, 0)
        o_ref[1, 0:16, :] = jnp.where(row == 0, nyq, hi[0:16])


def _spec_call(fwd, a):
    _, lf, l = fwd.shape
    n = a.shape[1] // 2
    tm = min(lf, 512)
    return pl.pallas_call(
        _spec_kernel,
        grid=(lf // tm,),
        in_specs=[pl.BlockSpec((2, tm, l), lambda i: (0, i, 0)),
                  pl.BlockSpec(a.shape, lambda i: (0, 0))],
        out_specs=pl.BlockSpec((2, tm, n), lambda i: (0, i, 0)),
        out_shape=jax.ShapeDtypeStruct((2, lf, n), F32),
        compiler_params=_cp("arbitrary"),
        name="hyena_spectrum",
    )(fwd, a)


def _dft_fwd_kernel(f_ref, u_ref, h_ref, y_ref):
    u = u_ref[0, 0].astype(BF16)
    xr = _dot(f_ref[0], u)
    xi = _dot(f_ref[1], u)
    hr = h_ref[0]
    hi = h_ref[1]
    yr = xr * hr - xi * hi
    yi = xr * hi + xi * hr
    y_ref[0, 0] = yr.astype(BF16)
    y_ref[0, 1] = yi.astype(BF16)

    @pl.when(pl.program_id(0) == 0)
    def _():
        row = lax.broadcasted_iota(jnp.int32, (16, u.shape[1]), 0)
        y_ref[0, 0, 0:16, :] = jnp.where(row == 0, xr[0:16] * hr[0:16], yr[0:16]).astype(BF16)
        y_ref[0, 1, 0:16, :] = jnp.where(row == 0, xi[0:16] * hi[0:16], yi[0:16]).astype(BF16)


def _dft_fwd_call(fwd, u, group, spec, order):
    _, b, l, c = u.shape
    lf = fwd.shape[1]
    tm = min(lf, 512)
    return pl.pallas_call(
        _dft_fwd_kernel,
        grid=(lf // tm, b),
        in_specs=[pl.BlockSpec((2, tm, l), lambda i, j: (0, i, 0)),
                  pl.BlockSpec((1, 1, l, c), lambda i, j: (group, j, 0, 0)),
                  pl.BlockSpec((2, tm, c), lambda i, j: (0, i, order))],
        out_specs=pl.BlockSpec((1, 2, tm, c), lambda i, j: (j, 0, i, 0)),
        out_shape=jax.ShapeDtypeStruct((b, 2, lf, c), BF16),
        compiler_params=_cp("arbitrary", "parallel"),
        name="hyena_dft_fwd",
    )(fwd, u, spec)


def _dft_inv_kernel(g_ref, y_ref, u_ref, b_ref, x_ref, o_ref):
    r = _dot(g_ref[...], y_ref[0]) + u_ref[0, 0].astype(F32) * b_ref[...]
    o_ref[0, 0] = (x_ref[0, 0] * r).astype(o_ref.dtype)


def _dft_inv_call(inv, y, u, u_group, bias, gate, gate_group):
    _, b, l, c = u.shape
    n = inv.shape[1]
    tm = min(l, 512)
    y = y.reshape(b, n, c)
    row = lambda g: pl.BlockSpec((1, 1, tm, c), lambda i, j: (g, j, i, 0))
    return pl.pallas_call(
        _dft_inv_kernel,
        grid=(l // tm, b),
        in_specs=[pl.BlockSpec((tm, n), lambda i, j: (i, 0)),
                  pl.BlockSpec((1, n, c), lambda i, j: (j, 0, 0)),
                  row(u_group), pl.BlockSpec((1, c), lambda i, j: (0, 0)), row(gate_group)],
        out_specs=row(0),
        out_shape=jax.ShapeDtypeStruct((1, b, l, c), BF16),
        compiler_params=_cp("parallel", "parallel"),
        name="hyena_dft_inv",
    )(inv, y, u, bias.reshape(1, c), gate)


def _hyena_call(p_hy, sh_w, sh_b, spec, hy_bias, fwd, inv):
    u3 = _hy_short_call(p_hy, sh_w, sh_b)
    z = _dft_inv_call(inv, _dft_fwd_call(fwd, u3, 0, spec, 0), u3, 0, hy_bias[0], u3, 1)
    return _dft_inv_call(inv, _dft_fwd_call(fwd, z, 0, spec, 1), z, 0, hy_bias[1], u3, 2)[0]


def _out_proj_kernel(c_ref, a_ref, h_ref, w1_ref, w2_ref, w3_ref, x_ref, g_ref, o_ref):
    acc = _dot(c_ref[...], w1_ref[...]) + _dot(a_ref[...], w2_ref[...]) + _dot(h_ref[...], w3_ref[...])
    o_ref[...] = x_ref[...] + g_ref[0] * acc


def _out_proj_call(conv_o, att_o, hy_o, w_out, x, gate, l):
    m, d = x.shape
    tm = min(l, 1024)
    tn = 512
    k1, k2 = conv_o.shape[1], att_o.shape[1]
    w1, w2, w3 = w_out[:k1], w_out[k1:k1 + k2], w_out[k1 + k2:]
    act = lambda k: pl.BlockSpec((tm, k), lambda i, j: (i, 0))
    wsp = lambda k: pl.BlockSpec((k, tn), lambda i, j: (0, j))
    tile = pl.BlockSpec((tm, tn), lambda i, j: (i, j))
    return pl.pallas_call(
        _out_proj_kernel,
        grid=(m // tm, d // tn),
        in_specs=[act(k1), act(k2), act(w3.shape[0]), wsp(k1), wsp(k2), wsp(w3.shape[0]), tile,
                  pl.BlockSpec((1, 1, tn), lambda i, j: ((i * tm) // l, 0, j))],
        out_specs=tile,
        out_shape=jax.ShapeDtypeStruct((m, d), F32),
        compiler_params=_cp("parallel", "parallel"),
        name="out_proj",
    )(conv_o, att_o, hy_o, w1, w2, w3, x, gate)


def _norm_router_kernel(x_ref, g_ref, sh_ref, sc_ref, wr_ref, rb_ref, h_ref, idx_ref, gate_ref):
    y = _rms(x_ref[0], g_ref[...])
    hm = y * (1.0 + sc_ref[0]) + sh_ref[0]
    hi = hm.astype(BF16)
    lo = (hm - hi.astype(F32)).astype(BF16)
    h_ref[0] = hi
    nt = (((1,), (1,)), ((), ()))
    wr = wr_ref[...]
    r = (lax.dot_general(wr, hi, nt, preferred_element_type=F32)
         + lax.dot_general(wr, lo, nt, preferred_element_type=F32))
    logits = r[:N_EXPERTS] + r[N_EXPERTS:]
    score = jax.nn.sigmoid(logits)
    sel = score + rb_ref[...]
    rows = [sel[e:e + 1] for e in range(N_EXPERTS)]
    gscore = []
    for g in range(N_GROUPS):
        a, b, c, d = rows[4 * g:4 * g + 4]
        h1, l1, h2, l2 = jnp.maximum(a, b), jnp.minimum(a, b), jnp.maximum(c, d), jnp.minimum(c, d)
        gscore.append(jnp.maximum(h1, h2) + jnp.maximum(jnp.minimum(h1, h2), jnp.maximum(l1, l2)))
    best = gscore[0]
    gbest = jnp.zeros(best.shape, jnp.int32)
    for g in range(1, N_GROUPS):
        upd = gscore[g] > best
        gbest = jnp.where(upd, g, gbest)
        best = jnp.where(upd, gscore[g], best)
    masked = [jnp.where(gbest == e // EXPERTS_PER_GROUP, rows[e], NEG_INF) for e in range(N_EXPERTS)]
    picks = []
    for _ in range(2):
        val = jnp.full(best.shape, NEG_INF, F32)
        idx = jnp.zeros(best.shape, jnp.int32)
        for e in range(N_EXPERTS):
            cand = masked[e]
            if picks:
                cand = jnp.where(picks[0] == e, NEG_INF, cand)
            upd = cand > val
            idx = jnp.where(upd, e, idx)
            val = jnp.where(upd, cand, val)
        picks.append(idx)
    gates = [functools.reduce(jnp.add, [jnp.where(p == e, score[e:e + 1], 0.0) for e in range(N_EXPERTS)])
             for p in picks]
    den = gates[0] + gates[1]
    idx_ref[...] = jnp.concatenate(picks, axis=0)
    gate_ref[...] = jnp.concatenate([gates[0] / den, gates[1] / den], axis=0)


def _norm_router_call(x, g, shift, scale, wr, rbias):
    b, l, d = x.shape
    tl = min(l, 512)
    nt = l // tl
    row = pl.BlockSpec((1, 1, d), lambda i, j: (i, 0, 0))
    tok = pl.BlockSpec((2, tl), lambda i, j: (0, i * nt + j))
    return pl.pallas_call(
        _norm_router_kernel,
        grid=(b, nt),
        in_specs=[pl.BlockSpec((1, tl, d), lambda i, j: (i, j, 0)),
                  pl.BlockSpec((1, d), lambda i, j: (0, 0)), row, row,
                  pl.BlockSpec(wr.shape, lambda i, j: (0, 0)),
                  pl.BlockSpec((N_EXPERTS, 1), lambda i, j: (0, 0))],
        out_specs=[pl.BlockSpec((1, tl, d), lambda i, j: (i, j, 0)), tok, tok],
        out_shape=[jax.ShapeDtypeStruct((b, l, d), BF16),
                   jax.ShapeDtypeStruct((2, b * l), jnp.int32),
                   jax.ShapeDtypeStruct((2, b * l), F32)],
        compiler_params=_cp("parallel", "parallel"),
        name="norm_router",
    )(x, g.reshape(1, d), shift, scale, wr, rbias.reshape(N_EXPERTS, 1))


def _expert_changed(te_ref, i):
    return (i == 0) | (te_ref[i] != te_ref[jnp.maximum(i - 1, 0)])


def _moe_up_kernel(te_ref, nu_ref, x_ref, wg_ref, wu_ref, h_ref, wgb_ref, wub_ref):
    i = pl.program_id(1)

    @pl.when(_expert_changed(te_ref, i))
    def _():
        wgb_ref[...] = wg_ref[0].astype(BF16)
        wub_ref[...] = wu_ref[0].astype(BF16)

    @pl.when(i < nu_ref[0])
    def _():
        x = x_ref[...]
        a = _dot(x, wgb_ref[...])
        h_ref[...] = (a * jax.nn.sigmoid(a) * _dot(x, wub_ref[...])).astype(BF16)

    @pl.when(i >= nu_ref[0])
    def _():
        h_ref[...] = jnp.zeros(h_ref.shape, BF16)


def _moe_up_call(xs, w_gate, w_up, te, nu, tile):
    p, d = xs.shape
    f = w_gate.shape[2]
    tf = 512
    wsp = pl.BlockSpec((1, d, tf), lambda j, i, te, nu: (te[i], 0, j))
    return pl.pallas_call(
        _moe_up_kernel,
        grid_spec=pltpu.PrefetchScalarGridSpec(
            num_scalar_prefetch=2,
            grid=(f // tf, p // tile),
            in_specs=[pl.BlockSpec((tile, d), lambda j, i, te, nu: (i, 0)), wsp, wsp],
            out_specs=pl.BlockSpec((tile, tf), lambda j, i, te, nu: (i, j)),
            scratch_shapes=[pltpu.VMEM((d, tf), BF16), pltpu.VMEM((d, tf), BF16)]),
        out_shape=jax.ShapeDtypeStruct((p, f), BF16),
        compiler_params=_cp("arbitrary", "arbitrary"),
        name="moe_up",
    )(te, nu, xs, w_gate, w_up)


def _moe_down_kernel(te_ref, nu_ref, h_ref, wd_ref, gs_ref, y_ref, wdb_ref):
    i = pl.program_id(0)

    @pl.when(_expert_changed(te_ref, i))
    def _():
        wdb_ref[...] = wd_ref[0].astype(BF16)

    @pl.when(i < nu_ref[0])
    def _():
        y_ref[...] = gs_ref[...] * _dot(h_ref[...], wdb_ref[...])

    @pl.when(i >= nu_ref[0])
    def _():
        y_ref[...] = jnp.zeros(y_ref.shape, F32)


def _moe_down_call(h, w_down, gate_sorted, te, nu, tile):
    p, f = h.shape
    d = w_down.shape[2]
    return pl.pallas_call(
        _moe_down_kernel,
        grid_spec=pltpu.PrefetchScalarGridSpec(
            num_scalar_prefetch=2,
            grid=(p // tile,),
            in_specs=[pl.BlockSpec((tile, f), lambda i, te, nu: (i, 0)),
                      pl.BlockSpec((1, f, d), lambda i, te, nu: (te[i], 0, 0)),
                      pl.BlockSpec((tile, 1), lambda i, te, nu: (i, 0))],
            out_specs=pl.BlockSpec((tile, d), lambda i, te, nu: (i, 0)),
            scratch_shapes=[pltpu.VMEM((f, d), BF16)]),
        out_shape=jax.ShapeDtypeStruct((p, d), F32),
        compiler_params=_cp("arbitrary"),
        name="moe_down",
    )(te, nu, h, w_down, gate_sorted)


def _route_plan(idx, tile, n_tiles):
    t = idx.shape[1]
    e = idx.reshape(2 * t)
    onehot = (e[:, None] == jnp.arange(N_EXPERTS, dtype=jnp.int32)[None, :]).astype(jnp.int32)
    csum = jnp.cumsum(onehot, axis=0)
    counts = csum[-1]
    rank = jnp.take_along_axis(csum, e[:, None], axis=1)[:, 0] - 1
    padded = (counts + tile - 1) // tile * tile
    ends = jnp.cumsum(padded)
    pos = (ends - padded)[e] + rank
    src = jnp.zeros((n_tiles * tile,), jnp.int32).at[pos].set(jnp.arange(2 * t, dtype=jnp.int32) % t)
    n_used = ends[-1] // tile
    tiles = jnp.arange(n_tiles, dtype=jnp.int32)
    te = jnp.searchsorted(ends, jnp.minimum(tiles, n_used - 1) * tile, side="right").astype(jnp.int32)
    return pos, src, te, n_used.astype(jnp.int32).reshape(1)


def _moe_call(h, idx, gates, w_gate, w_up, w_down):
    t, d = h.shape
    tile = min(MOE_TILE, max(t // N_EXPERTS, 128))
    n_tiles = (2 * t) // tile + N_EXPERTS
    pos, src, te, nu = _route_plan(idx, tile, n_tiles)
    gate_sorted = jnp.zeros((n_tiles * tile,), F32).at[pos].set(gates.reshape(2 * t))[:, None]
    xs = jnp.take(h, src, axis=0)
    hid = _moe_up_call(xs, w_gate, w_up, te, nu, tile)
    ys = _moe_down_call(hid, w_down, gate_sorted, te, nu, tile)
    return jnp.take(ys, pos[:t], axis=0), jnp.take(ys, pos[t:], axis=0)


def _moe_resid_kernel(x_ref, y0_ref, y1_ref, g_ref, *rest, final):
    x = x_ref[0] + g_ref[0] * (y0_ref[0] + y1_ref[0])
    if final:
        fg_ref, o_ref = rest
        x = _rms(x, fg_ref[...])
    else:
        (o_ref,) = rest
    o_ref[0] = x


def _moe_resid_call(x, y0, y1, gate, final_g=None):
    b, l, d = x.shape
    tl = min(l, 512)
    tile = pl.BlockSpec((1, tl, d), lambda i, j: (i, j, 0))
    in_specs = [tile, tile, tile, pl.BlockSpec((1, 1, d), lambda i, j: (i, 0, 0))]
    args = [x, y0.reshape(b, l, d), y1.reshape(b, l, d), gate]
    if final_g is not None:
        in_specs.append(pl.BlockSpec((1, d), lambda i, j: (0, 0)))
        args.append(final_g.reshape(1, d))
    return pl.pallas_call(
        functools.partial(_moe_resid_kernel, final=final_g is not None),
        grid=(b, l // tl),
        in_specs=in_specs,
        out_specs=tile,
        out_shape=jax.ShapeDtypeStruct((b, l, d), F32),
        compiler_params=_cp("parallel", "parallel"),
        name="moe_residual",
    )(*args)


def _rope_perm(w):
    r = w.reshape(w.shape[:-1] + (2, 2, ROPE_FREQS))
    return jnp.stack([-r[..., 1, :], r[..., 0, :]], axis=-2).reshape(w.shape)


def _rope_table(l, rotary):
    if not rotary:
        return jnp.concatenate([jnp.ones((l, MLA_ROPE), F32), jnp.zeros((l, MLA_ROPE), F32)], axis=1)
    rows = l // GRID_W
    r = jnp.repeat(jnp.arange(rows), GRID_W)
    col = jnp.tile(jnp.arange(GRID_W), rows)
    pos = jnp.stack([r, col], axis=-1).astype(F32)
    inv = ROPE_THETA ** (-jnp.arange(ROPE_FREQS, dtype=F32) / ROPE_FREQS)
    ang = pos[:, :, None] * inv
    lay = lambda a: jnp.concatenate([a[:, 0], a[:, 0], a[:, 1], a[:, 1]], axis=1)
    return jnp.concatenate([lay(jnp.cos(ang)), lay(jnp.sin(ang))], axis=1)


def _layer_weights(w_in, w_uq, w_ukv, w_out):
    d = w_in.shape[0]
    kr = w_in[:, OFF_KR:OFF_HY]
    w_kv_in = jnp.concatenate([w_in[:, OFF_KV:OFF_KR], kr, _rope_perm(kr)], axis=1)
    uq = w_uq.reshape(MLA_Q_RANK, MLA_HEADS, MLA_NOPE + MLA_ROPE)
    w_q = jnp.concatenate([uq, _rope_perm(uq[:, :, MLA_NOPE:])], axis=2).reshape(MLA_Q_RANK, MLA_HEADS * QK_HEAD)
    ukv = w_ukv.reshape(MLA_KV_RANK, MLA_HEADS, MLA_NOPE + MLA_V_DIM)
    w_k = ukv[:, :, :MLA_NOPE].reshape(MLA_KV_RANK, MLA_HEADS * MLA_NOPE)
    w_v = ukv[:, :, MLA_NOPE:].reshape(MLA_KV_RANK, MLA_HEADS * MLA_V_DIM)
    cast = lambda a: a.astype(BF16)
    return dict(conv=cast(w_in[:, :OFF_Q]), q=cast(w_in[:, OFF_Q:OFF_KV]), kv=cast(w_kv_in),
                hy=cast(w_in[:, OFF_HY:]), uq=cast(w_q), uk=cast(w_k), uv=cast(w_v), out=cast(w_out))


def _router_weights(w_router):
    hi = w_router.astype(BF16)
    lo = (w_router - hi.astype(F32)).astype(BF16)
    return jnp.concatenate([hi.T, lo.T], axis=0)


def _mixer(x, mod, w, lp, cs, spec, fwd, inv, kv_extra, full):
    b, l, d = x.shape
    h = _norm_mod_call(x, lp["norm1_g"], mod[0], mod[1]).reshape(b * l, d)
    p_kv = _linear_call(h, w["kv"], w["kv"].shape[1])
    k, v = _kv_call(p_kv, lp["kv_norm_g"], w["uk"], w["uv"], cs)
    k = k.reshape(b, l, -1)
    v = v.reshape(b, l, -1)
    if not full:
        return None, k, v
    conv_o = _conformer_call(_linear_call(h, w["conv"], 512).reshape(b, l, -1),
                             lp["conv_dw_w"], lp["conv_dw_b"], lp["conv_ln_g"], lp["conv_ln_b"])
    q = _q_call(_linear_call(h, w["q"], MLA_Q_RANK), lp["q_norm_g"], w["uq"], cs).reshape(b, l, -1)
    att_o = _attn_call(q, kv_extra + [(k, v)])
    hy_o = _hyena_call(_linear_call(h, w["hy"], 512).reshape(b, l, -1), lp["hy_short_w"], lp["hy_short_b"],
                       spec, lp["hy_bias"], fwd, inv)
    x = _out_proj_call(conv_o.reshape(b * l, -1), att_o.reshape(b * l, -1), hy_o.reshape(b * l, -1),
                       w["out"], x.reshape(b * l, d), mod[2], l).reshape(b, l, d)
    return x, k, v


def _channel(x, mod, lp, wr, rbias, final_g=None):
    b, l, d = x.shape
    h, idx, gates = _norm_router_call(x, lp["norm2_g"], mod[3], mod[4], wr, rbias)
    y0, y1 = _moe_call(h.reshape(b * l, d), idx, gates, lp["w_gate"], lp["w_up"], lp["w_down"])
    return _moe_resid_call(x, y0, y1, mod[5], final_g)


def _filter_spectrum(l, lp, fwd):
    a = _hy_filter_call(l, lp["hy_w1"], lp["hy_b1"], lp["hy_freq1"], lp["hy_w2"], lp["hy_b2"], lp["hy_freq2"],
                        lp["hy_w3"], lp["hy_b3"])
    return _spec_call(fwd, a)


def kernel(x, c, ctx, c_ctx, norm1_g, norm2_g, w_ada, b_ada, w_in, conv_dw_w, conv_dw_b, conv_ln_g, conv_ln_b, q_norm_g, w_uq, kv_norm_g, w_ukv, hy_short_w, hy_short_b, hy_w1, hy_b1, hy_freq1, hy_w2, hy_b2, hy_freq2, hy_w3, hy_b3, hy_bias, w_out, w_router, router_bias, w_gate, w_up, w_down, final_norm_g):
    b, l, d = x.shape
    lc = ctx.shape[1]
    depth = w_in.shape[0]
    per_layer = dict(norm1_g=norm1_g, norm2_g=norm2_g, conv_dw_w=conv_dw_w, conv_dw_b=conv_dw_b,
                     conv_ln_g=conv_ln_g, conv_ln_b=conv_ln_b, q_norm_g=q_norm_g, kv_norm_g=kv_norm_g,
                     hy_short_w=hy_short_w, hy_short_b=hy_short_b, hy_w1=hy_w1, hy_b1=hy_b1, hy_freq1=hy_freq1,
                     hy_w2=hy_w2, hy_b2=hy_b2, hy_freq2=hy_freq2, hy_w3=hy_w3, hy_b3=hy_b3, hy_bias=hy_bias,
                     w_gate=w_gate, w_up=w_up, w_down=w_down)
    cs_l = _rope_table(l, True)
    cs_c = _rope_table(lc, False)
    fwd_l, inv_l = _dft_mats(l)
    fwd_c, inv_c = _dft_mats(lc)
    wr = _router_weights(w_router)
    rows = 16
    cvec = jnp.concatenate([c, c_ctx[None, :], jnp.zeros((rows - b - 1, d), F32)], axis=0)
    xl, xc = x, ctx
    for i in range(depth):
        last = i == depth - 1
        lp = {k: v[i] for k, v in per_layer.items()}
        w = _layer_weights(w_in[i], w_uq[i], w_ukv[i], w_out[i])
        m = _ada_call(cvec, w_ada[i], b_ada[i])
        mod_l = [m[:b, None, j * d:(j + 1) * d] for j in range(6)]
        mod_c = [jnp.broadcast_to(m[b:b + 1, None, j * d:(j + 1) * d], (b, 1, d)) for j in range(6)]
        spec_l = _filter_spectrum(l, lp, fwd_l)
        spec_c = None if last else _filter_spectrum(lc, lp, fwd_c)
        xc_new, kc, vc = _mixer(xc, mod_c, w, lp, cs_c, spec_c, fwd_c, inv_c, [], not last)
        xl, _, _ = _mixer(xl, mod_l, w, lp, cs_l, spec_l, fwd_l, inv_l, [(kc, vc)], True)
        xl = _channel(xl, mod_l, lp, wr, router_bias, final_norm_g if last else None)
        if not last:
            xc = _channel(xc_new, mod_c, lp, wr, router_bias)
    return xl
```

```python
import functools
import math

import jax
import jax.numpy as jnp
from jax import lax
from jax.experimental import pallas as pl
from jax.experimental.pallas import tpu as pltpu
from jax.experimental.pallas import tpu_sc as plsc

F32 = jnp.float32
BF16 = jnp.bfloat16

EPS = 1e-6
GRID_W = 64
CONV_W = 512
CONV_KSIZE = 31
MLA_HEADS = 8
MLA_NOPE = 128
MLA_ROPE = 64
MLA_V_DIM = 128
MLA_Q_RANK = 768
MLA_KV_RANK = 512
ROPE_FREQS = MLA_ROPE // 4
ROPE_THETA = 10000.0
MLA_SCALE = (MLA_NOPE + MLA_ROPE) ** -0.5
HYENA_W = 512
HY_ORDER = 2
HY_SHORT = 3
HY_EMB = 33
HY_BANDS = (HY_EMB - 1) // 2
HY_FFN = 64
HY_MIN_DECAY = math.log(1e-2) / 1.5
HY_MAX_DECAY = math.log(1e-2) / 0.3
N_EXPERTS = 16
N_GROUPS = 4
EXPERTS_PER_GROUP = N_EXPERTS // N_GROUPS
OFF_Q = 2 * CONV_W
OFF_KV = OFF_Q + MLA_Q_RANK
OFF_KR = OFF_KV + MLA_KV_RANK
OFF_HY = OFF_KR + MLA_ROPE

LANES = 128
QK_HEAD = 2 * LANES
V7X_VMEM_BYTES = 64 * 1024 * 1024
VMEM_LIMIT = (V7X_VMEM_BYTES * 7) // 8
MOE_TILE = 512
NEG_INF = float("-inf")


def _cp(*sem):
    return pltpu.CompilerParams(dimension_semantics=sem, vmem_limit_bytes=VMEM_LIMIT)


def _dot(a, b):
    return jnp.dot(a, b, preferred_element_type=F32)


def _rms(x, g):
    ms = jnp.mean(x * x, axis=-1, keepdims=True)
    return x * lax.rsqrt(ms + EPS) * g


def _ada_kernel(c_ref, w_ref, b_ref, o_ref):
    c = c_ref[...]
    a = (c * jax.nn.sigmoid(c)).astype(BF16)
    o_ref[...] = _dot(a, w_ref[0].astype(BF16)) + b_ref[0]


def _ada_call(cvec, w, b, layer):
    m, d = cvec.shape
    n = w.shape[2]
    tn = 1024
    return pl.pallas_call(
        _ada_kernel,
        grid=(n // tn,),
        in_specs=[pl.BlockSpec((m, d), lambda j: (0, 0)),
                  pl.BlockSpec((1, d, tn), lambda j: (layer, 0, j)),
                  pl.BlockSpec((1, 1, tn), lambda j: (layer, 0, j))],
        out_specs=pl.BlockSpec((m, tn), lambda j: (0, j)),
        out_shape=jax.ShapeDtypeStruct((m, n), F32),
        compiler_params=_cp("parallel"),
        name="ada",
    )(cvec, w, b.reshape(b.shape[0], 1, n))


def _norm_mod_kernel(x_ref, g_ref, sh_ref, sc_ref, o_ref):
    y = _rms(x_ref[0], g_ref[...])
    o_ref[0] = (y * (1.0 + sc_ref[0]) + sh_ref[0]).astype(o_ref.dtype)


def _norm_mod_call(x, g, shift, scale):
    b, l, d = x.shape
    tl = min(l, 512)
    row = pl.BlockSpec((1, 1, d), lambda i, j: (i, 0, 0))
    return pl.pallas_call(
        _norm_mod_kernel,
        grid=(b, l // tl),
        in_specs=[pl.BlockSpec((1, tl, d), lambda i, j: (i, j, 0)),
                  pl.BlockSpec((1, d), lambda i, j: (0, 0)), row, row],
        out_specs=pl.BlockSpec((1, tl, d), lambda i, j: (i, j, 0)),
        out_shape=jax.ShapeDtypeStruct((b, l, d), BF16),
        compiler_params=_cp("parallel", "parallel"),
        name="norm_mod",
    )(x, g.reshape(1, d), shift, scale)


def _linear_kernel(a_ref, w_ref, o_ref):
    o_ref[...] = _dot(a_ref[...], w_ref[...]).astype(o_ref.dtype)


def _linear_call(a, w, tn, out_dtype=F32):
    m, k = a.shape
    n = w.shape[1]
    tm = min(m, 1024)
    return pl.pallas_call(
        _linear_kernel,
        grid=(m // tm, n // tn),
        in_specs=[pl.BlockSpec((tm, k), lambda i, j: (i, 0)),
                  pl.BlockSpec((k, tn), lambda i, j: (0, j))],
        out_specs=pl.BlockSpec((tm, tn), lambda i, j: (i, j)),
        out_shape=jax.ShapeDtypeStruct((m, n), out_dtype),
        compiler_params=_cp("parallel", "parallel"),
        name="linear",
    )(a, w)


CONV_ROWS = 128
CONV_PAD = 16


def _conformer_kernel(p_ref, w_ref, b_ref, g_ref, be_ref, o_ref, zp_ref):
    l = p_ref.shape[1]
    c = CONV_W
    zeros = jnp.zeros((CONV_PAD, c), F32)
    zp_ref[0:CONV_PAD, :] = zeros
    zp_ref[CONV_PAD + l:CONV_PAD + l + CONV_PAD, :] = zeros

    def glu(r, carry):
        base = pl.multiple_of(r * CONV_ROWS, CONV_ROWS)
        blk = p_ref[0, pl.ds(base, CONV_ROWS), :]
        zp_ref[pl.ds(base + CONV_PAD, CONV_ROWS), :] = blk[:, :c] * jax.nn.sigmoid(blk[:, c:])
        return carry

    lax.fori_loop(0, l // CONV_ROWS, glu, 0)

    win_rows = CONV_ROWS + 2 * CONV_PAD
    first = CONV_PAD - CONV_KSIZE // 2

    def conv(r, carry):
        base = pl.multiple_of(r * CONV_ROWS, CONV_ROWS)
        parts = []
        for cc in range(c // LANES):
            lanes = slice(cc * LANES, (cc + 1) * LANES)
            win = zp_ref[pl.ds(base, win_rows), lanes]
            acc = jnp.zeros((CONV_ROWS, LANES), F32)
            for res in range(8):
                taps = [k for k in range(CONV_KSIZE) if (k + first) % 8 == res]
                if not taps:
                    continue
                span = max(k + first for k in taps) - res + CONV_ROWS
                shifted = win[res:res + span, :]
                for k in taps:
                    off = k + first - res
                    acc = acc + shifted[off:off + CONV_ROWS, :] * w_ref[k:k + 1, lanes]
            parts.append(acc)
        z = jnp.concatenate(parts, axis=1) + b_ref[...]
        mu = jnp.mean(z, axis=-1, keepdims=True)
        zc = z - mu
        var = jnp.mean(zc * zc, axis=-1, keepdims=True)
        y = zc * lax.rsqrt(var + EPS) * g_ref[...] + be_ref[...]
        o_ref[0, pl.ds(base, CONV_ROWS), :] = (y * jax.nn.sigmoid(y)).astype(o_ref.dtype)
        return carry

    lax.fori_loop(0, l // CONV_ROWS, conv, 0)


def _conformer_call(p, dw_w, dw_b, ln_g, ln_b):
    b, l, c2 = p.shape
    c = c2 // 2
    vec = pl.BlockSpec((1, c), lambda i: (0, 0))
    return pl.pallas_call(
        _conformer_kernel,
        grid=(b,),
        in_specs=[pl.BlockSpec((1, l, c2), lambda i: (i, 0, 0)),
                  pl.BlockSpec((CONV_KSIZE, c), lambda i: (0, 0)), vec, vec, vec],
        out_specs=pl.BlockSpec((1, l, c), lambda i: (i, 0, 0)),
        out_shape=jax.ShapeDtypeStruct((b, l, c), BF16),
        scratch_shapes=[pltpu.VMEM((l + 2 * CONV_PAD, c), F32)],
        compiler_params=_cp("parallel"),
        name="conformer",
    )(p, dw_w, dw_b.reshape(1, c), ln_g.reshape(1, c), ln_b.reshape(1, c))


def _rotate(group, cs):
    t = group * cs
    return t + pltpu.roll(t, MLA_ROPE, axis=1)


def _q_kernel(p_ref, g_ref, w_ref, cs_ref, o_ref):
    xn = _rms(p_ref[...], g_ref[...]).astype(BF16)
    cs = cs_ref[...] * MLA_SCALE
    for h in range(MLA_HEADS):
        r = _dot(xn, w_ref[:, h * QK_HEAD:(h + 1) * QK_HEAD])
        o_ref[:, h * QK_HEAD:h * QK_HEAD + LANES] = (r[:, :LANES] * MLA_SCALE).astype(BF16)
        o_ref[:, h * QK_HEAD + LANES:(h + 1) * QK_HEAD] = _rotate(r[:, LANES:], cs).astype(BF16)


def _q_call(p_q, g, w_q, cs):
    m, k = p_q.shape
    l = cs.shape[0]
    tm = min(l, 512)
    n = MLA_HEADS * QK_HEAD
    return pl.pallas_call(
        _q_kernel,
        grid=(m // tm,),
        in_specs=[pl.BlockSpec((tm, k), lambda i: (i, 0)),
                  pl.BlockSpec((1, k), lambda i: (0, 0)),
                  pl.BlockSpec((k, n), lambda i: (0, 0)),
                  pl.BlockSpec((tm, LANES), lambda i: (i % (l // tm), 0))],
        out_specs=pl.BlockSpec((tm, n), lambda i: (i, 0)),
        out_shape=jax.ShapeDtypeStruct((m, n), BF16),
        compiler_params=_cp("parallel"),
        name="mla_q",
    )(p_q, g.reshape(1, k), w_q, cs)


def _kv_kernel(p_ref, g_ref, wk_ref, wv_ref, cs_ref, k_ref, v_ref):
    p = p_ref[...]
    xn = _rms(p[:, :MLA_KV_RANK], g_ref[...]).astype(BF16)
    kk = _dot(xn, wk_ref[...])
    v_ref[...] = _dot(xn, wv_ref[...]).astype(BF16)
    rot = _rotate(p[:, MLA_KV_RANK:], cs_ref[...])
    lane = lax.broadcasted_iota(jnp.int32, rot.shape, 1)
    rot = jnp.where(lane < MLA_ROPE, rot, 0.0).astype(BF16)
    for h in range(MLA_HEADS):
        k_ref[:, h * QK_HEAD:h * QK_HEAD + LANES] = kk[:, h * LANES:(h + 1) * LANES].astype(BF16)
        k_ref[:, h * QK_HEAD + LANES:(h + 1) * QK_HEAD] = rot


def _kv_call(p_kv, g, w_k, w_v, cs):
    m, kin = p_kv.shape
    l = cs.shape[0]
    tm = min(l, 512)
    r = MLA_KV_RANK
    return pl.pallas_call(
        _kv_kernel,
        grid=(m // tm,),
        in_specs=[pl.BlockSpec((tm, kin), lambda i: (i, 0)),
                  pl.BlockSpec((1, r), lambda i: (0, 0)),
                  pl.BlockSpec(w_k.shape, lambda i: (0, 0)),
                  pl.BlockSpec(w_v.shape, lambda i: (0, 0)),
                  pl.BlockSpec((tm, LANES), lambda i: (i % (l // tm), 0))],
        out_specs=[pl.BlockSpec((tm, MLA_HEADS * QK_HEAD), lambda i: (i, 0)),
                   pl.BlockSpec((tm, MLA_HEADS * MLA_V_DIM), lambda i: (i, 0))],
        out_shape=[jax.ShapeDtypeStruct((m, MLA_HEADS * QK_HEAD), BF16),
                   jax.ShapeDtypeStruct((m, MLA_HEADS * MLA_V_DIM), BF16)],
        compiler_params=_cp("parallel"),
        name="mla_kv",
    )(p_kv, g.reshape(1, r), w_k, w_v, cs)


def _attn_kernel(*refs, n_seg):
    q_ref, o_ref = refs[0], refs[1 + 2 * n_seg]
    q = q_ref[0]
    nt = (((1,), (1,)), ((), ()))
    scores = [lax.dot_general(q, refs[1 + 2 * s][0], nt, preferred_element_type=F32) for s in range(n_seg)]
    m = functools.reduce(jnp.maximum, [jnp.max(s, axis=-1, keepdims=True) for s in scores])
    den = 0.0
    acc = 0.0
    for s in range(n_seg):
        p = jnp.exp(scores[s] - m)
        den = den + jnp.sum(p, axis=-1, keepdims=True)
        acc = acc + _dot(p.astype(BF16), refs[2 + 2 * s][0])
    o_ref[0] = (acc * (1.0 / den)).astype(o_ref.dtype)


def _attn_call(q, segs):
    b, l, _ = q.shape
    tq = min(l, 256)
    in_specs = [pl.BlockSpec((1, tq, QK_HEAD), lambda i, h, j: (i, j, h))]
    args = [q]
    for k, v in segs:
        lk = k.shape[1]
        in_specs.append(pl.BlockSpec((1, lk, QK_HEAD), lambda i, h, j: (i, 0, h)))
        in_specs.append(pl.BlockSpec((1, lk, MLA_V_DIM), lambda i, h, j: (i, 0, h)))
        args += [k, v]
    return pl.pallas_call(
        functools.partial(_attn_kernel, n_seg=len(segs)),
        grid=(b, MLA_HEADS, l // tq),
        in_specs=in_specs,
        out_specs=pl.BlockSpec((1, tq, MLA_V_DIM), lambda i, h, j: (i, j, h)),
        out_shape=jax.ShapeDtypeStruct((b, l, MLA_HEADS * MLA_V_DIM), BF16),
        compiler_params=_cp("parallel", "parallel", "parallel"),
        name="attention",
    )(*args)


HY_PAD = 8
HY_ROWS = 128


def _hy_short_kernel(p_ref, w_ref, b_ref, o_ref, up_ref):
    l = p_ref.shape[1]
    c = p_ref.shape[2]
    zeros = jnp.zeros((HY_PAD, c), F32)
    up_ref[0:HY_PAD, :] = zeros
    up_ref[HY_PAD + l:HY_PAD + l + HY_PAD, :] = zeros

    def copy(r, carry):
        base = pl.multiple_of(r * HY_ROWS, HY_ROWS)
        up_ref[pl.ds(base + HY_PAD, HY_ROWS), :] = p_ref[0, pl.ds(base, HY_ROWS), :]
        return carry

    lax.fori_loop(0, l // HY_ROWS, copy, 0)
    first = HY_PAD - HY_SHORT // 2

    def conv(r, carry):
        base = pl.multiple_of(r * HY_ROWS, HY_ROWS)
        for cc in range(c // LANES):
            lanes = slice(cc * LANES, (cc + 1) * LANES)
            win = up_ref[pl.ds(base, HY_ROWS + 2 * HY_PAD), lanes]
            acc = jnp.zeros((HY_ROWS, LANES), F32) + b_ref[:, lanes]
            for k in range(HY_SHORT):
                acc = acc + win[first + k:first + k + HY_ROWS, :] * w_ref[k:k + 1, lanes]
            o_ref[0, 0, pl.ds(base, HY_ROWS), lanes] = acc
        return carry

    lax.fori_loop(0, l // HY_ROWS, conv, 0)


def _hy_short_call(p, w, bias):
    b, l, c3 = p.shape
    c = HYENA_W
    return pl.pallas_call(
        _hy_short_kernel,
        grid=(b, c3 // c),
        in_specs=[pl.BlockSpec((1, l, c), lambda i, g: (i, 0, g)),
                  pl.BlockSpec((HY_SHORT, c), lambda i, g: (0, g)),
                  pl.BlockSpec((1, c), lambda i, g: (0, g))],
        out_specs=pl.BlockSpec((1, 1, l, c), lambda i, g: (g, i, 0, 0)),
        out_shape=jax.ShapeDtypeStruct((c3 // c, b, l, c), F32),
        scratch_shapes=[pltpu.VMEM((l + 2 * HY_PAD, c), F32)],
        compiler_params=_cp("parallel", "parallel"),
        name="hyena_short",
    )(p, w, bias.reshape(1, c3))


def _hy_filter_kernel(z_ref, w1_ref, b1_ref, f1_ref, w2_ref, b2_ref, f2_ref, w3_ref, b3_ref, win_ref, o_ref):
    hp = lax.Precision.HIGHEST

    def mm(a, b):
        return jnp.dot(a, b, preferred_element_type=F32, precision=hp)

    h = jnp.sin(f1_ref[...] * (mm(z_ref[...], w1_ref[...]) + b1_ref[...]))
    h = jnp.sin(f2_ref[...] * (mm(h, w2_ref[...]) + b2_ref[...]))
    h = mm(h, w3_ref[...]) + b3_ref[...]
    c = HYENA_W
    win = win_ref[...]
    tl = h.shape[0]
    row = lax.broadcasted_iota(jnp.int32, (tl, c), 0) + pl.program_id(0) * tl
    for o in range(HY_ORDER):
        fwd = h[:, (2 * o) * c:(2 * o + 1) * c] * win
        bwd = jnp.where(row == 0, 0.0, h[:, (2 * o + 1) * c:(2 * o + 2) * c] * win)
        o_ref[:, o * c:(o + 1) * c] = (fwd + bwd).astype(o_ref.dtype)
        o_ref[:, (HY_ORDER + o) * c:(HY_ORDER + o + 1) * c] = (fwd - bwd).astype(o_ref.dtype)


def _pad2(a, rows, cols):
    return jnp.pad(a, ((0, rows - a.shape[0]), (0, cols - a.shape[1])))


def _hy_filter_call(l, w1, b1, f1, w2, b2, f2, w3, b3):
    t = jnp.arange(l, dtype=F32)
    t_norm = t / max(l - 1, 1)
    ang = (2.0 * math.pi / l) * t[:, None] * jnp.linspace(1e-4, HY_BANDS - 1, HY_BANDS, dtype=F32)[None, :]
    z = jnp.concatenate([t_norm[:, None], jnp.cos(ang), -jnp.sin(ang)], axis=-1)
    deltas = jnp.abs(jnp.linspace(HY_MIN_DECAY, HY_MAX_DECAY, HYENA_W, dtype=F32))
    window = jnp.exp(-t_norm[:, None] * deltas[None, :])
    n3 = w3.shape[1]
    tl = min(l, 256)
    full = lambda shape: pl.BlockSpec(shape, lambda i: (0, 0))
    return pl.pallas_call(
        _hy_filter_kernel,
        grid=(l // tl,),
        in_specs=[pl.BlockSpec((tl, LANES), lambda i: (i, 0)),
                  full((LANES, LANES)), full((1, LANES)), full((1, LANES)),
                  full((LANES, LANES)), full((1, LANES)), full((1, LANES)),
                  full((LANES, n3)), full((1, n3)),
                  pl.BlockSpec((tl, HYENA_W), lambda i: (i, 0))],
        out_specs=pl.BlockSpec((tl, n3), lambda i: (i, 0)),
        out_shape=jax.ShapeDtypeStruct((l, n3), BF16),
        compiler_params=_cp("parallel"),
        name="hyena_filter",
    )(_pad2(z, l, LANES), _pad2(w1, LANES, LANES), _pad2(b1[None], 1, LANES), _pad2(f1[None], 1, LANES),
      _pad2(w2, LANES, LANES), _pad2(b2[None], 1, LANES), _pad2(f2[None], 1, LANES),
      _pad2(w3, LANES, n3), b3[None], window)


def _dft_mats(l):
    n = 2 * l
    f = jnp.arange(l, dtype=jnp.int32)[:, None]
    t = jnp.arange(l, dtype=jnp.int32)[None, :]
    ang = ((f * t) % n).astype(F32) * (2.0 * math.pi / n)
    c = jnp.cos(ang)
    s = jnp.sin(ang)
    nyq = jnp.where(t % 2 == 0, 1.0, -1.0).astype(F32)
    first = f == 0
    fwd = jnp.stack([c, jnp.where(first, nyq, -s)]).astype(BF16)
    col0 = t == 0
    inv_r = jnp.where(col0, 1.0 / n, c * (2.0 / n))
    inv_i = jnp.where(col0, nyq.T / n, -s * (2.0 / n))
    inv = jnp.concatenate([inv_r, inv_i], axis=1).astype(BF16)
    return fwd, inv


def _spec_kernel(f_ref, a_ref, o_ref):
    half = a_ref.shape[1] // 2
    hi = _dot(f_ref[1], a_ref[:, half:])
    o_ref[0] = _dot(f_ref[0], a_ref[:, :half])
    o_ref[1] = hi

    @pl.when(pl.program_id(0) == 0)
    def _():
        nyq = _dot(f_ref[1, 0:16, :], a_ref[:, :half])
        row = lax.broadcasted_iota(jnp.int32, nyq.shape, 0)
        o_ref[1, 0:16, :] = jnp.where(row == 0, nyq, hi[0:16])


def _spec_call(fwd, a):
    _, lf, l = fwd.shape
    n = a.shape[1] // 2
    tm = min(lf, 512)
    return pl.pallas_call(
        _spec_kernel,
        grid=(lf // tm,),
        in_specs=[pl.BlockSpec((2, tm, l), lambda i: (0, i, 0)),
                  pl.BlockSpec(a.shape, lambda i: (0, 0))],
        out_specs=pl.BlockSpec((2, tm, n), lambda i: (0, i, 0)),
        out_shape=jax.ShapeDtypeStruct((2, lf, n), F32),
        compiler_params=_cp("arbitrary"),
        name="hyena_spectrum",
    )(fwd, a)


def _dft_fwd_kernel(f_ref, u_ref, h_ref, y_ref):
    u = u_ref[0, 0].astype(BF16)
    xr = _dot(f_ref[0], u)
    xi = _dot(f_ref[1], u)
    hr = h_ref[0]
    hi = h_ref[1]
    yr = xr * hr - xi * hi
    yi = xr * hi + xi * hr
    y_ref[0, 0] = yr.astype(BF16)
    y_ref[0, 1] = yi.astype(BF16)

    @pl.when(pl.program_id(0) == 0)
    def _():
        row = lax.broadcasted_iota(jnp.int32, (16, u.shape[1]), 0)
        y_ref[0, 0, 0:16, :] = jnp.where(row == 0, xr[0:16] * hr[0:16], yr[0:16]).astype(BF16)
        y_ref[0, 1, 0:16, :] = jnp.where(row == 0, xi[0:16] * hi[0:16], yi[0:16]).astype(BF16)


def _dft_fwd_call(fwd, u, group, spec, order):
    _, b, l, c = u.shape
    lf = fwd.shape[1]
    tm = min(lf, 512)
    return pl.pallas_call(
        _dft_fwd_kernel,
        grid=(lf // tm, b),
        in_specs=[pl.BlockSpec((2, tm, l), lambda i, j: (0, i, 0)),
                  pl.BlockSpec((1, 1, l, c), lambda i, j: (group, j, 0, 0)),
                  pl.BlockSpec((2, tm, c), lambda i, j: (0, i, order))],
        out_specs=pl.BlockSpec((1, 2, tm, c), lambda i, j: (j, 0, i, 0)),
        out_shape=jax.ShapeDtypeStruct((b, 2, lf, c), BF16),
        compiler_params=_cp("arbitrary", "parallel"),
        name="hyena_dft_fwd",
    )(fwd, u, spec)


def _dft_inv_kernel(g_ref, y_ref, u_ref, b_ref, x_ref, o_ref):
    r = _dot(g_ref[...], y_ref[0]) + u_ref[0, 0].astype(F32) * b_ref[...]
    o_ref[0, 0] = (x_ref[0, 0] * r).astype(o_ref.dtype)


def _dft_inv_call(inv, y, u, u_group, bias, gate, gate_group):
    _, b, l, c = u.shape
    n = inv.shape[1]
    tm = min(l, 512)
    y = y.reshape(b, n, c)
    row = lambda g: pl.BlockSpec((1, 1, tm, c), lambda i, j: (g, j, i, 0))
    return pl.pallas_call(
        _dft_inv_kernel,
        grid=(l // tm, b),
        in_specs=[pl.BlockSpec((tm, n), lambda i, j: (i, 0)),
                  pl.BlockSpec((1, n, c), lambda i, j: (j, 0, 0)),
                  row(u_group), pl.BlockSpec((1, c), lambda i, j: (0, 0)), row(gate_group)],
        out_specs=row(0),
        out_shape=jax.ShapeDtypeStruct((1, b, l, c), BF16),
        compiler_params=_cp("parallel", "parallel"),
        name="hyena_dft_inv",
    )(inv, y, u, bias.reshape(1, c), gate)


def _hyena_call(p_hy, sh_w, sh_b, spec, hy_bias, fwd, inv):
    u3 = _hy_short_call(p_hy, sh_w, sh_b)
    z = _dft_inv_call(inv, _dft_fwd_call(fwd, u3, 0, spec, 0), u3, 0, hy_bias[0], u3, 1)
    return _dft_inv_call(inv, _dft_fwd_call(fwd, z, 0, spec, 1), z, 0, hy_bias[1], u3, 2)[0]


def _out_proj_kernel(c_ref, a_ref, h_ref, w1_ref, w2_ref, w3_ref, x_ref, g_ref, o_ref):
    acc = _dot(c_ref[...], w1_ref[...]) + _dot(a_ref[...], w2_ref[...]) + _dot(h_ref[...], w3_ref[...])
    o_ref[...] = x_ref[...] + g_ref[0] * acc


def _out_proj_call(conv_o, att_o, hy_o, w_out, x, gate, l):
    m, d = x.shape
    tm = min(l, 1024)
    tn = 512
    k1, k2 = conv_o.shape[1], att_o.shape[1]
    w1, w2, w3 = w_out[:k1], w_out[k1:k1 + k2], w_out[k1 + k2:]
    act = lambda k: pl.BlockSpec((tm, k), lambda i, j: (i, 0))
    wsp = lambda k: pl.BlockSpec((k, tn), lambda i, j: (0, j))
    tile = pl.BlockSpec((tm, tn), lambda i, j: (i, j))
    return pl.pallas_call(
        _out_proj_kernel,
        grid=(m // tm, d // tn),
        in_specs=[act(k1), act(k2), act(w3.shape[0]), wsp(k1), wsp(k2), wsp(w3.shape[0]), tile,
                  pl.BlockSpec((1, 1, tn), lambda i, j: ((i * tm) // l, 0, j))],
        out_specs=tile,
        out_shape=jax.ShapeDtypeStruct((m, d), F32),
        compiler_params=_cp("parallel", "parallel"),
        name="out_proj",
    )(conv_o, att_o, hy_o, w1, w2, w3, x, gate)


def _norm_router_kernel(x_ref, g_ref, sh_ref, sc_ref, wr_ref, rb_ref, h_ref, idx_ref, gate_ref):
    y = _rms(x_ref[0], g_ref[...])
    hm = y * (1.0 + sc_ref[0]) + sh_ref[0]
    hi = hm.astype(BF16)
    hi32 = hi.astype(F32)
    lo = (hm - hi32).astype(BF16)
    bits = pltpu.bitcast(hi32, jnp.uint32)
    half = bits.shape[1] // 2
    packed = (bits[:, :half] >> 16) | (bits[:, half:] & jnp.uint32(0xFFFF0000))
    h_ref[0] = pltpu.bitcast(packed, jnp.int32)
    nt = (((1,), (1,)), ((), ()))
    wr = wr_ref[...]
    r = (lax.dot_general(wr, hi, nt, preferred_element_type=F32)
         + lax.dot_general(wr, lo, nt, preferred_element_type=F32))
    logits = r[:N_EXPERTS] + r[N_EXPERTS:]
    score = jax.nn.sigmoid(logits)
    sel = score + rb_ref[...]
    rows = [sel[e:e + 1] for e in range(N_EXPERTS)]
    gscore = []
    for g in range(N_GROUPS):
        a, b, c, d = rows[4 * g:4 * g + 4]
        h1, l1, h2, l2 = jnp.maximum(a, b), jnp.minimum(a, b), jnp.maximum(c, d), jnp.minimum(c, d)
        gscore.append(jnp.maximum(h1, h2) + jnp.maximum(jnp.minimum(h1, h2), jnp.maximum(l1, l2)))
    best = gscore[0]
    gbest = jnp.zeros(best.shape, jnp.int32)
    for g in range(1, N_GROUPS):
        upd = gscore[g] > best
        gbest = jnp.where(upd, g, gbest)
        best = jnp.where(upd, gscore[g], best)
    masked = [jnp.where(gbest == e // EXPERTS_PER_GROUP, rows[e], NEG_INF) for e in range(N_EXPERTS)]
    picks = []
    for _ in range(2):
        val = jnp.full(best.shape, NEG_INF, F32)
        idx = jnp.zeros(best.shape, jnp.int32)
        for e in range(N_EXPERTS):
            cand = masked[e]
            if picks:
                cand = jnp.where(picks[0] == e, NEG_INF, cand)
            upd = cand > val
            idx = jnp.where(upd, e, idx)
            val = jnp.where(upd, cand, val)
        picks.append(idx)
    gates = [functools.reduce(jnp.add, [jnp.where(p == e, score[e:e + 1], 0.0) for e in range(N_EXPERTS)])
             for p in picks]
    den = gates[0] + gates[1]
    idx_ref[...] = jnp.concatenate(picks, axis=0)
    gate_ref[...] = jnp.concatenate([gates[0] / den, gates[1] / den], axis=0)


def _norm_router_call(x, g, shift, scale, wr, rbias):
    b, l, d = x.shape
    tl = min(l, 512)
    nt = l // tl
    row = pl.BlockSpec((1, 1, d), lambda i, j: (i, 0, 0))
    tok = pl.BlockSpec((2, tl), lambda i, j: (0, i * nt + j))
    return pl.pallas_call(
        _norm_router_kernel,
        grid=(b, nt),
        in_specs=[pl.BlockSpec((1, tl, d), lambda i, j: (i, j, 0)),
                  pl.BlockSpec((1, d), lambda i, j: (0, 0)), row, row,
                  pl.BlockSpec(wr.shape, lambda i, j: (0, 0)),
                  pl.BlockSpec((N_EXPERTS, 1), lambda i, j: (0, 0))],
        out_specs=[pl.BlockSpec((1, tl, d // 2), lambda i, j: (i, j, 0)), tok, tok],
        out_shape=[jax.ShapeDtypeStruct((b, l, d // 2), jnp.int32),
                   jax.ShapeDtypeStruct((2, b * l), jnp.int32),
                   jax.ShapeDtypeStruct((2, b * l), F32)],
        compiler_params=_cp("parallel", "parallel"),
        name="norm_router",
    )(x, g.reshape(1, d), shift, scale, wr, rbias.reshape(N_EXPERTS, 1))


def _expert_changed(te_ref, i):
    return (i == 0) | (te_ref[i] != te_ref[jnp.maximum(i - 1, 0)])


def _unpack_rows(words):
    w = pltpu.bitcast(words, jnp.uint32)
    lo = pltpu.bitcast(w << 16, F32).astype(BF16)
    hi = pltpu.bitcast(w & jnp.uint32(0xFFFF0000), F32).astype(BF16)
    return lo, hi


def _moe_up_kernel(te_ref, nu_ref, x_ref, wg_ref, wu_ref, h_ref, wgb_ref, wub_ref):
    i = pl.program_id(1)

    @pl.when(_expert_changed(te_ref, i))
    def _():
        wgb_ref[...] = wg_ref[0, 0].astype(BF16)
        wub_ref[...] = wu_ref[0, 0].astype(BF16)

    @pl.when(i < nu_ref[0])
    def _():
        xa, xb = _unpack_rows(x_ref[...])
        half = xa.shape[1]
        a = _dot(xa, wgb_ref[:half, :]) + _dot(xb, wgb_ref[half:, :])
        u = _dot(xa, wub_ref[:half, :]) + _dot(xb, wub_ref[half:, :])
        h_ref[...] = (a * jax.nn.sigmoid(a) * u).astype(BF16)

    @pl.when(i >= nu_ref[0])
    def _():
        h_ref[...] = jnp.zeros(h_ref.shape, BF16)


def _moe_up_call(xs, w_gate, w_up, layer, te, nu, tile):
    p, dw = xs.shape
    _, _, d, f = w_gate.shape
    tf = 512
    wsp = pl.BlockSpec((1, 1, d, tf), lambda j, i, te, nu: (layer, te[i], 0, j))
    return pl.pallas_call(
        _moe_up_kernel,
        grid_spec=pltpu.PrefetchScalarGridSpec(
            num_scalar_prefetch=2,
            grid=(f // tf, p // tile),
            in_specs=[pl.BlockSpec((tile, dw), lambda j, i, te, nu: (i, 0)), wsp, wsp],
            out_specs=pl.BlockSpec((tile, tf), lambda j, i, te, nu: (i, j)),
            scratch_shapes=[pltpu.VMEM((d, tf), BF16), pltpu.VMEM((d, tf), BF16)]),
        out_shape=jax.ShapeDtypeStruct((p, f), BF16),
        compiler_params=_cp("arbitrary", "arbitrary"),
        name="moe_up",
    )(te, nu, xs, w_gate, w_up)


def _moe_down_kernel(te_ref, nu_ref, h_ref, wd_ref, y_ref, wdb_ref):
    i = pl.program_id(0)

    @pl.when(_expert_changed(te_ref, i))
    def _():
        wdb_ref[...] = wd_ref[0, 0].astype(BF16)

    @pl.when(i < nu_ref[0])
    def _():
        y_ref[...] = _dot(h_ref[...], wdb_ref[...])

    @pl.when(i >= nu_ref[0])
    def _():
        y_ref[...] = jnp.zeros(y_ref.shape, F32)


def _moe_down_call(h, w_down, layer, te, nu, tile):
    p, f = h.shape
    d = w_down.shape[3]
    return pl.pallas_call(
        _moe_down_kernel,
        grid_spec=pltpu.PrefetchScalarGridSpec(
            num_scalar_prefetch=2,
            grid=(p // tile,),
            in_specs=[pl.BlockSpec((tile, f), lambda i, te, nu: (i, 0)),
                      pl.BlockSpec((1, 1, f, d), lambda i, te, nu: (layer, te[i], 0, 0))],
            out_specs=pl.BlockSpec((tile, d), lambda i, te, nu: (i, 0)),
            scratch_shapes=[pltpu.VMEM((f, d), BF16)]),
        out_shape=jax.ShapeDtypeStruct((p, d), F32),
        compiler_params=_cp("arbitrary"),
        name="moe_down",
    )(te, nu, h, w_down)


PLAN_BLOCK = 512


def _plan_kernel(idx_ref, pos_ref, te_ref, nu_ref, rank_ref, *, tile):
    t = idx_ref.shape[1]
    nb = t // PLAN_BLOCK
    experts = lax.broadcasted_iota(jnp.int32, (N_EXPERTS, PLAN_BLOCK), 0)
    r = lax.broadcasted_iota(jnp.int32, (PLAN_BLOCK, PLAN_BLOCK), 0)
    c = lax.broadcasted_iota(jnp.int32, (PLAN_BLOCK, PLAN_BLOCK), 1)
    before = jnp.where(r < c, 1.0, 0.0).astype(BF16)

    def rank_pass(k):
        def body(j, carry):
            cols = pl.ds(pl.multiple_of(j * PLAN_BLOCK, PLAN_BLOCK), PLAN_BLOCK)
            onehot = jnp.where(experts == idx_ref[k:k + 1, cols], 1.0, 0.0)
            earlier = _dot(onehot.astype(BF16), before) + carry
            rank_ref[k:k + 1, cols] = jnp.sum(onehot * earlier, axis=0, keepdims=True)
            return carry + jnp.sum(onehot, axis=1, keepdims=True)
        return body

    counts = jnp.zeros((N_EXPERTS, 1), F32)
    for k in range(2):
        counts = lax.fori_loop(0, nb, rank_pass(k), counts)
    padded = jnp.ceil(counts * (1.0 / tile)) * tile
    starts, ends = [], []
    acc = jnp.zeros((1, 1), F32)
    for e in range(N_EXPERTS):
        starts.append(acc)
        acc = acc + padded[e:e + 1]
        ends.append(acc)

    def place_pass(k):
        def body(j, carry):
            cols = pl.ds(pl.multiple_of(j * PLAN_BLOCK, PLAN_BLOCK), PLAN_BLOCK)
            row = idx_ref[k:k + 1, cols]
            start = functools.reduce(jnp.add, [jnp.where(row == e, starts[e], 0.0) for e in range(N_EXPERTS)])
            pos_ref[k:k + 1, cols] = (rank_ref[k:k + 1, cols] + start).astype(jnp.int32)
            return carry
        return body

    for k in range(2):
        lax.fori_loop(0, nb, place_pass(k), 0)
    n_used = acc * (1.0 / tile)
    tiles = lax.broadcasted_iota(jnp.int32, te_ref.shape, 1).astype(F32)
    first_row = jnp.minimum(tiles, n_used - 1.0) * tile
    te = functools.reduce(jnp.add, [jnp.where(ends[e] <= first_row, 1.0, 0.0) for e in range(N_EXPERTS)])
    te_ref[...] = te.astype(jnp.int32)
    nu_ref[...] = jnp.broadcast_to(n_used, nu_ref.shape).astype(jnp.int32)


def _plan_call(idx, tile, n_tiles):
    t = idx.shape[1]
    whole = lambda shape: pl.BlockSpec(shape, lambda: (0,) * len(shape))
    pos, te, nu = pl.pallas_call(
        functools.partial(_plan_kernel, tile=tile),
        in_specs=[whole(idx.shape)],
        out_specs=[whole(idx.shape), whole((1, LANES)), whole((1, LANES))],
        out_shape=[jax.ShapeDtypeStruct(idx.shape, jnp.int32),
                   jax.ShapeDtypeStruct((1, LANES), jnp.int32),
                   jax.ShapeDtypeStruct((1, LANES), jnp.int32)],
        scratch_shapes=[pltpu.VMEM(idx.shape, F32)],
        name="route_plan",
    )(idx)
    return pos.reshape(2 * t), te[0, :n_tiles], nu[0, :1]


V7X_SC_CORES = 2
V7X_SC_SUBCORES = 16
SC_WORKERS = V7X_SC_CORES * V7X_SC_SUBCORES
SC_ROW_BYTES = 256 * 1024


def _sc_worker():
    return lax.axis_index("s") * V7X_SC_CORES + lax.axis_index("c")


def _sc_dispatch_call(h32, pos, n_rows):
    t, w = h32.shape
    per_w = t // SC_WORKERS
    ch = min(per_w, SC_ROW_BYTES // (4 * w))
    mesh = plsc.VectorSubcoreMesh(core_axis_name="c", subcore_axis_name="s")

    @functools.partial(pl.kernel, mesh=mesh, out_type=jax.ShapeDtypeStruct((n_rows, w), jnp.int32),
                       scratch_types=[pltpu.VMEM((ch,), jnp.int32), pltpu.VMEM((ch, w), jnp.int32)],
                       name="sc_dispatch")
    def run(h_hbm, pos_hbm, xs_hbm, idx_v, rows_v):
        first = _sc_worker() * per_w

        @pl.loop(0, per_w // ch)
        def _(j):
            base = first + j * ch
            pltpu.sync_copy(h_hbm.at[pl.ds(base, ch)], rows_v)
            for k in range(2):
                pltpu.sync_copy(pos_hbm.at[pl.ds(k * t + base, ch)], idx_v)
                pltpu.sync_copy(rows_v, xs_hbm.at[idx_v])

    return run(h32, pos)


def _sc_combine_call(ys, pos):
    _, d = ys.shape
    n = pos.shape[0]
    per_w = n // SC_WORKERS
    ch = min(per_w, SC_ROW_BYTES // (4 * d))
    mesh = plsc.VectorSubcoreMesh(core_axis_name="c", subcore_axis_name="s")

    @functools.partial(pl.kernel, mesh=mesh, out_type=jax.ShapeDtypeStruct((n, d), F32),
                       scratch_types=[pltpu.VMEM((ch,), jnp.int32), pltpu.VMEM((ch, d), F32)],
                       name="sc_combine")
    def run(ys_hbm, pos_hbm, out_hbm, idx_v, rows_v):
        first = _sc_worker() * per_w

        @pl.loop(0, per_w // ch)
        def _(j):
            base = first + j * ch
            pltpu.sync_copy(pos_hbm.at[pl.ds(base, ch)], idx_v)
            pltpu.sync_copy(ys_hbm.at[idx_v], rows_v)
            pltpu.sync_copy(rows_v, out_hbm.at[pl.ds(base, ch)])

    return run(ys, pos)


def _moe_call(h32, idx, w_gate, w_up, w_down, layer):
    t = h32.shape[0]
    tile = min(MOE_TILE, max(t // N_EXPERTS, 128))
    n_tiles = (2 * t) // tile + N_EXPERTS
    pos, te, nu = _plan_call(idx, tile, n_tiles)
    xs = _sc_dispatch_call(h32, pos, n_tiles * tile)
    hid = _moe_up_call(xs, w_gate, w_up, layer, te, nu, tile)
    ys = _moe_down_call(hid, w_down, layer, te, nu, tile)
    return _sc_combine_call(ys, pos).reshape(2, t, ys.shape[1])


def _moe_resid_kernel(x_ref, y0_ref, y1_ref, gt_ref, g_ref, *rest, final):
    gt = gt_ref[0]
    x = x_ref[0] + g_ref[0] * (gt[:, 0:1] * y0_ref[0, 0] + gt[:, 1:2] * y1_ref[0, 0])
    if final:
        fg_ref, o_ref = rest
        x = _rms(x, fg_ref[...])
    else:
        (o_ref,) = rest
    o_ref[0] = x


def _moe_resid_call(x, y, gates, gate, final_g=None):
    b, l, d = x.shape
    tl = min(l, 512)
    tile = pl.BlockSpec((1, tl, d), lambda i, j: (i, j, 0))
    choice = lambda k: pl.BlockSpec((1, 1, tl, d), lambda i, j: (k, i, j, 0))
    in_specs = [tile, choice(0), choice(1), pl.BlockSpec((1, tl, 2), lambda i, j: (i, j, 0)),
                pl.BlockSpec((1, 1, d), lambda i, j: (i, 0, 0))]
    y = y.reshape(2, b, l, d)
    args = [x, y, y, gates.T.reshape(b, l, 2), gate]
    if final_g is not None:
        in_specs.append(pl.BlockSpec((1, d), lambda i, j: (0, 0)))
        args.append(final_g.reshape(1, d))
    return pl.pallas_call(
        functools.partial(_moe_resid_kernel, final=final_g is not None),
        grid=(b, l // tl),
        in_specs=in_specs,
        out_specs=tile,
        out_shape=jax.ShapeDtypeStruct((b, l, d), F32),
        compiler_params=_cp("parallel", "parallel"),
        name="moe_residual",
    )(*args)


def _rope_perm(w):
    r = w.reshape(w.shape[:-1] + (2, 2, ROPE_FREQS))
    return jnp.stack([-r[..., 1, :], r[..., 0, :]], axis=-2).reshape(w.shape)


def _rope_table(l, rotary):
    if not rotary:
        return jnp.concatenate([jnp.ones((l, MLA_ROPE), F32), jnp.zeros((l, MLA_ROPE), F32)], axis=1)
    rows = l // GRID_W
    r = jnp.repeat(jnp.arange(rows), GRID_W)
    col = jnp.tile(jnp.arange(GRID_W), rows)
    pos = jnp.stack([r, col], axis=-1).astype(F32)
    inv = ROPE_THETA ** (-jnp.arange(ROPE_FREQS, dtype=F32) / ROPE_FREQS)
    ang = pos[:, :, None] * inv
    lay = lambda a: jnp.concatenate([a[:, 0], a[:, 0], a[:, 1], a[:, 1]], axis=1)
    return jnp.concatenate([lay(jnp.cos(ang)), lay(jnp.sin(ang))], axis=1)


def _layer_weights(w_in, w_uq, w_ukv, w_out):
    d = w_in.shape[0]
    kr = w_in[:, OFF_KR:OFF_HY]
    w_kv_in = jnp.concatenate([w_in[:, OFF_KV:OFF_KR], kr, _rope_perm(kr)], axis=1)
    uq = w_uq.reshape(MLA_Q_RANK, MLA_HEADS, MLA_NOPE + MLA_ROPE)
    w_q = jnp.concatenate([uq, _rope_perm(uq[:, :, MLA_NOPE:])], axis=2).reshape(MLA_Q_RANK, MLA_HEADS * QK_HEAD)
    ukv = w_ukv.reshape(MLA_KV_RANK, MLA_HEADS, MLA_NOPE + MLA_V_DIM)
    w_k = ukv[:, :, :MLA_NOPE].reshape(MLA_KV_RANK, MLA_HEADS * MLA_NOPE)
    w_v = ukv[:, :, MLA_NOPE:].reshape(MLA_KV_RANK, MLA_HEADS * MLA_V_DIM)
    cast = lambda a: a.astype(BF16)
    return dict(conv=cast(w_in[:, :OFF_Q]), q=cast(w_in[:, OFF_Q:OFF_KV]), kv=cast(w_kv_in),
                hy=cast(w_in[:, OFF_HY:]), uq=cast(w_q), uk=cast(w_k), uv=cast(w_v), out=cast(w_out))


def _router_weights(w_router):
    hi = w_router.astype(BF16)
    lo = (w_router - hi.astype(F32)).astype(BF16)
    return jnp.concatenate([hi.T, lo.T], axis=0)


def _mixer(x, mod, w, lp, cs, spec, fwd, inv, kv_extra, full):
    b, l, d = x.shape
    h = _norm_mod_call(x, lp["norm1_g"], mod[0], mod[1]).reshape(b * l, d)
    p_kv = _linear_call(h, w["kv"], w["kv"].shape[1])
    k, v = _kv_call(p_kv, lp["kv_norm_g"], w["uk"], w["uv"], cs)
    k = k.reshape(b, l, -1)
    v = v.reshape(b, l, -1)
    if not full:
        return None, k, v
    conv_o = _conformer_call(_linear_call(h, w["conv"], 512).reshape(b, l, -1),
                             lp["conv_dw_w"], lp["conv_dw_b"], lp["conv_ln_g"], lp["conv_ln_b"])
    q = _q_call(_linear_call(h, w["q"], MLA_Q_RANK), lp["q_norm_g"], w["uq"], cs).reshape(b, l, -1)
    att_o = _attn_call(q, kv_extra + [(k, v)])
    hy_o = _hyena_call(_linear_call(h, w["hy"], 512).reshape(b, l, -1), lp["hy_short_w"], lp["hy_short_b"],
                       spec, lp["hy_bias"], fwd, inv)
    x = _out_proj_call(conv_o.reshape(b * l, -1), att_o.reshape(b * l, -1), hy_o.reshape(b * l, -1),
                       w["out"], x.reshape(b * l, d), mod[2], l).reshape(b, l, d)
    return x, k, v


def _channel(x, mod, lp, wr, rbias, experts, layer, final_g=None):
    b, l, d = x.shape
    h32, idx, gates = _norm_router_call(x, lp["norm2_g"], mod[3], mod[4], wr, rbias)
    y = _moe_call(h32.reshape(b * l, d // 2), idx, *experts, layer)
    return _moe_resid_call(x, y, gates, mod[5], final_g)


def _filter_spectrum(l, lp, fwd):
    a = _hy_filter_call(l, lp["hy_w1"], lp["hy_b1"], lp["hy_freq1"], lp["hy_w2"], lp["hy_b2"], lp["hy_freq2"],
                        lp["hy_w3"], lp["hy_b3"])
    return _spec_call(fwd, a)


def kernel(x, c, ctx, c_ctx, norm1_g, norm2_g, w_ada, b_ada, w_in, conv_dw_w, conv_dw_b, conv_ln_g, conv_ln_b, q_norm_g, w_uq, kv_norm_g, w_ukv, hy_short_w, hy_short_b, hy_w1, hy_b1, hy_freq1, hy_w2, hy_b2, hy_freq2, hy_w3, hy_b3, hy_bias, w_out, w_router, router_bias, w_gate, w_up, w_down, final_norm_g):
    b, l, d = x.shape
    lc = ctx.shape[1]
    depth = w_in.shape[0]
    per_layer = dict(norm1_g=norm1_g, norm2_g=norm2_g, conv_dw_w=conv_dw_w, conv_dw_b=conv_dw_b,
                     conv_ln_g=conv_ln_g, conv_ln_b=conv_ln_b, q_norm_g=q_norm_g, kv_norm_g=kv_norm_g,
                     hy_short_w=hy_short_w, hy_short_b=hy_short_b, hy_w1=hy_w1, hy_b1=hy_b1, hy_freq1=hy_freq1,
                     hy_w2=hy_w2, hy_b2=hy_b2, hy_freq2=hy_freq2, hy_w3=hy_w3, hy_b3=hy_b3, hy_bias=hy_bias)
    experts = (w_gate, w_up, w_down)
    cs_l = _rope_table(l, True)
    cs_c = _rope_table(lc, False)
    fwd_l, inv_l = _dft_mats(l)
    fwd_c, inv_c = _dft_mats(lc)
    wr = _router_weights(w_router)
    rows = 16
    cvec = jnp.concatenate([c, c_ctx[None, :], jnp.zeros((rows - b - 1, d), F32)], axis=0)
    xl, xc = x, ctx
    for i in range(depth):
        last = i == depth - 1
        lp = {k: v[i] for k, v in per_layer.items()}
        w = _layer_weights(w_in[i], w_uq[i], w_ukv[i], w_out[i])
        m = _ada_call(cvec, w_ada, b_ada, i)
        mod_l = [m[:b, None, j * d:(j + 1) * d] for j in range(6)]
        mod_c = [jnp.broadcast_to(m[b:b + 1, None, j * d:(j + 1) * d], (b, 1, d)) for j in range(6)]
        spec_l = _filter_spectrum(l, lp, fwd_l)
        spec_c = None if last else _filter_spectrum(lc, lp, fwd_c)
        xc_new, kc, vc = _mixer(xc, mod_c, w, lp, cs_c, spec_c, fwd_c, inv_c, [], not last)
        xl, _, _ = _mixer(xl, mod_l, w, lp, cs_l, spec_l, fwd_l, inv_l, [(kc, vc)], True)
        xl = _channel(xl, mod_l, lp, wr, router_bias, experts, i, final_norm_g if last else None)
        if not last:
            xc = _channel(xc_new, mod_c, lp, wr, router_bias, experts, i)
    return xl
```

```python
import functools
import math

import jax
import jax.numpy as jnp
from jax import lax
from jax.experimental import pallas as pl
from jax.experimental.pallas import tpu as pltpu
from jax.experimental.pallas import tpu_sc as plsc

F32 = jnp.float32
BF16 = jnp.bfloat16

EPS = 1e-6
GRID_W = 64
CONV_W = 512
CONV_KSIZE = 31
MLA_HEADS = 8
MLA_NOPE = 128
MLA_ROPE = 64
MLA_V_DIM = 128
MLA_Q_RANK = 768
MLA_KV_RANK = 512
ROPE_FREQS = MLA_ROPE // 4
ROPE_THETA = 10000.0
MLA_SCALE = (MLA_NOPE + MLA_ROPE) ** -0.5
Q_SCALE = MLA_SCALE * math.log2(math.e)
HYENA_W = 512
HY_ORDER = 2
HY_SHORT = 3
HY_EMB = 33
HY_BANDS = (HY_EMB - 1) // 2
HY_FFN = 64
HY_MIN_DECAY = math.log(1e-2) / 1.5
HY_MAX_DECAY = math.log(1e-2) / 0.3
N_EXPERTS = 16
N_GROUPS = 4
EXPERTS_PER_GROUP = N_EXPERTS // N_GROUPS
OFF_Q = 2 * CONV_W
OFF_KV = OFF_Q + MLA_Q_RANK
OFF_KR = OFF_KV + MLA_KV_RANK
OFF_HY = OFF_KR + MLA_ROPE

LANES = 128
QK_HEAD = 2 * LANES
V7X_VMEM_BYTES = 64 * 1024 * 1024
VMEM_LIMIT = (V7X_VMEM_BYTES * 7) // 8
MOE_TILE = 512
NEG_INF = float("-inf")


def _cp(*sem):
    return pltpu.CompilerParams(dimension_semantics=sem, vmem_limit_bytes=VMEM_LIMIT)


def _dot(a, b):
    return jnp.dot(a, b, preferred_element_type=F32)


def _rms(x, g):
    ms = jnp.mean(x * x, axis=-1, keepdims=True)
    return x * lax.rsqrt(ms + EPS) * g


def _ada_kernel(c_ref, w_ref, b_ref, o_ref):
    c = c_ref[...]
    a = (c * jax.nn.sigmoid(c)).astype(BF16)
    o_ref[...] = _dot(a, w_ref[0].astype(BF16)) + b_ref[0]


def _ada_call(cvec, w, b, layer):
    m, d = cvec.shape
    n = w.shape[2]
    tn = 1024
    return pl.pallas_call(
        _ada_kernel,
        grid=(n // tn,),
        in_specs=[pl.BlockSpec((m, d), lambda j: (0, 0)),
                  pl.BlockSpec((1, d, tn), lambda j: (layer, 0, j)),
                  pl.BlockSpec((1, 1, tn), lambda j: (layer, 0, j))],
        out_specs=pl.BlockSpec((m, tn), lambda j: (0, j)),
        out_shape=jax.ShapeDtypeStruct((m, n), F32),
        compiler_params=_cp("parallel"),
        name="ada",
    )(cvec, w, b.reshape(b.shape[0], 1, n))


def _in_proj_kernel(x_ref, g_ref, sh_ref, sc_ref, w_ref, *o_refs):
    y = _rms(x_ref[0], g_ref[...])
    h = (y * (1.0 + sc_ref[0]) + sh_ref[0]).astype(BF16)
    off = 0
    for o_ref in o_refs:
        n = o_ref.shape[-1]
        o_ref[0] = _dot(h, w_ref[:, off:off + n]).astype(o_ref.dtype)
        off += n


def _in_proj_call(x, g, shift, scale, w, splits):
    b, l, d = x.shape
    n = w.shape[1]
    tm = min(l, 512)
    row = pl.BlockSpec((1, 1, d), lambda i, j: (i, 0, 0))
    return pl.pallas_call(
        _in_proj_kernel,
        grid=(b, l // tm),
        in_specs=[pl.BlockSpec((1, tm, d), lambda i, j: (i, j, 0)),
                  pl.BlockSpec((1, d), lambda i, j: (0, 0)), row, row,
                  pl.BlockSpec((d, n), lambda i, j: (0, 0), pipeline_mode=pl.Buffered(1))],
        out_specs=[pl.BlockSpec((1, tm, k), lambda i, j: (i, j, 0)) for k in splits],
        out_shape=[jax.ShapeDtypeStruct((b, l, k), BF16) for k in splits],
        compiler_params=_cp("parallel", "parallel"),
        name="in_proj",
    )(x, g.reshape(1, d), shift, scale, w)


CONV_ROWS = 128
CONV_PAD = 16


def _conformer_kernel(p_ref, w_ref, b_ref, g_ref, be_ref, o_ref, zp_ref):
    l = p_ref.shape[1]
    c = CONV_W
    zeros = jnp.zeros((CONV_PAD, c), F32)
    zp_ref[0:CONV_PAD, :] = zeros
    zp_ref[CONV_PAD + l:CONV_PAD + l + CONV_PAD, :] = zeros

    def glu(r, carry):
        base = pl.multiple_of(r * CONV_ROWS, CONV_ROWS)
        blk = p_ref[0, pl.ds(base, CONV_ROWS), :].astype(F32)
        zp_ref[pl.ds(base + CONV_PAD, CONV_ROWS), :] = blk[:, :c] * jax.nn.sigmoid(blk[:, c:])
        return carry

    lax.fori_loop(0, l // CONV_ROWS, glu, 0)

    win_rows = CONV_ROWS + 2 * CONV_PAD
    first = CONV_PAD - CONV_KSIZE // 2

    def conv(r, carry):
        base = pl.multiple_of(r * CONV_ROWS, CONV_ROWS)
        parts = []
        for cc in range(c // LANES):
            lanes = slice(cc * LANES, (cc + 1) * LANES)
            win = zp_ref[pl.ds(base, win_rows), lanes]
            acc = jnp.zeros((CONV_ROWS, LANES), F32)
            for res in range(8):
                taps = [k for k in range(CONV_KSIZE) if (k + first) % 8 == res]
                if not taps:
                    continue
                shifted = pltpu.roll(win, win_rows - res, axis=0) if res else win
                for k in taps:
                    off = k + first - res
                    acc = acc + shifted[off:off + CONV_ROWS, :] * w_ref[k:k + 1, lanes]
            parts.append(acc)
        z = jnp.concatenate(parts, axis=1) + b_ref[...]
        mu = jnp.mean(z, axis=-1, keepdims=True)
        zc = z - mu
        var = jnp.mean(zc * zc, axis=-1, keepdims=True)
        y = zc * lax.rsqrt(var + EPS) * g_ref[...] + be_ref[...]
        o_ref[0, pl.ds(base, CONV_ROWS), :] = (y * jax.nn.sigmoid(y)).astype(o_ref.dtype)
        return carry

    lax.fori_loop(0, l // CONV_ROWS, conv, 0)


def _conformer_call(p, dw_w, dw_b, ln_g, ln_b):
    b, l, c2 = p.shape
    c = c2 // 2
    vec = pl.BlockSpec((1, c), lambda i: (0, 0))
    return pl.pallas_call(
        _conformer_kernel,
        grid=(b,),
        in_specs=[pl.BlockSpec((1, l, c2), lambda i: (i, 0, 0)),
                  pl.BlockSpec((CONV_KSIZE, c), lambda i: (0, 0)), vec, vec, vec],
        out_specs=pl.BlockSpec((1, l, c), lambda i: (i, 0, 0)),
        out_shape=jax.ShapeDtypeStruct((b, l, c), BF16),
        scratch_shapes=[pltpu.VMEM((l + 2 * CONV_PAD, c), F32)],
        compiler_params=_cp("parallel"),
        name="conformer",
    )(p, dw_w, dw_b.reshape(1, c), ln_g.reshape(1, c), ln_b.reshape(1, c))


def _rotate(group, cs):
    t = group * cs
    return t + pltpu.roll(t, MLA_ROPE, axis=1)


def _q_kernel(p_ref, g_ref, w_ref, cs_ref, o_ref):
    xn = _rms(p_ref[...].astype(F32), g_ref[...]).astype(BF16)
    cs = cs_ref[...] * Q_SCALE
    for h in range(MLA_HEADS):
        r = _dot(xn, w_ref[:, h * QK_HEAD:(h + 1) * QK_HEAD])
        o_ref[:, h * QK_HEAD:h * QK_HEAD + LANES] = (r[:, :LANES] * Q_SCALE).astype(BF16)
        o_ref[:, h * QK_HEAD + LANES:(h + 1) * QK_HEAD] = _rotate(r[:, LANES:], cs).astype(BF16)


def _q_call(p_q, g, w_q, cs):
    m, k = p_q.shape
    l = cs.shape[0]
    tm = min(l, 512)
    n = MLA_HEADS * QK_HEAD
    return pl.pallas_call(
        _q_kernel,
        grid=(m // tm,),
        in_specs=[pl.BlockSpec((tm, k), lambda i: (i, 0)),
                  pl.BlockSpec((1, k), lambda i: (0, 0)),
                  pl.BlockSpec((k, n), lambda i: (0, 0)),
                  pl.BlockSpec((tm, LANES), lambda i: (i % (l // tm), 0))],
        out_specs=pl.BlockSpec((tm, n), lambda i: (i, 0)),
        out_shape=jax.ShapeDtypeStruct((m, n), BF16),
        compiler_params=_cp("parallel"),
        name="mla_q",
    )(p_q, g.reshape(1, k), w_q, cs)


def _kv_kernel(p_ref, g_ref, wk_ref, wv_ref, cs_ref, k_ref, v_ref):
    p = p_ref[...].astype(F32)
    xn = _rms(p[:, :MLA_KV_RANK], g_ref[...]).astype(BF16)
    kk = _dot(xn, wk_ref[...])
    v_ref[...] = _dot(xn, wv_ref[...]).astype(BF16)
    rot = _rotate(p[:, MLA_KV_RANK:], cs_ref[...])
    lane = lax.broadcasted_iota(jnp.int32, rot.shape, 1)
    rot = jnp.where(lane < MLA_ROPE, rot, 0.0).astype(BF16)
    for h in range(MLA_HEADS):
        k_ref[:, h * QK_HEAD:h * QK_HEAD + LANES] = kk[:, h * LANES:(h + 1) * LANES].astype(BF16)
        k_ref[:, h * QK_HEAD + LANES:(h + 1) * QK_HEAD] = rot


def _kv_call(p_kv, g, w_k, w_v, cs):
    m, kin = p_kv.shape
    l = cs.shape[0]
    tm = min(l, 512)
    r = MLA_KV_RANK
    return pl.pallas_call(
        _kv_kernel,
        grid=(m // tm,),
        in_specs=[pl.BlockSpec((tm, kin), lambda i: (i, 0)),
                  pl.BlockSpec((1, r), lambda i: (0, 0)),
                  pl.BlockSpec(w_k.shape, lambda i: (0, 0)),
                  pl.BlockSpec(w_v.shape, lambda i: (0, 0)),
                  pl.BlockSpec((tm, LANES), lambda i: (i % (l // tm), 0))],
        out_specs=[pl.BlockSpec((tm, MLA_HEADS * QK_HEAD), lambda i: (i, 0)),
                   pl.BlockSpec((tm, MLA_HEADS * MLA_V_DIM), lambda i: (i, 0))],
        out_shape=[jax.ShapeDtypeStruct((m, MLA_HEADS * QK_HEAD), BF16),
                   jax.ShapeDtypeStruct((m, MLA_HEADS * MLA_V_DIM), BF16)],
        compiler_params=_cp("parallel"),
        name="mla_kv",
    )(p_kv, g.reshape(1, r), w_k, w_v, cs)


ATTN_HEADS_PER_STEP = 8


def _attn_kernel(*refs, n_seg):
    q_ref, o_ref = refs[0], refs[1 + 2 * n_seg]
    nt = (((1,), (1,)), ((), ()))
    for h in range(ATTN_HEADS_PER_STEP):
        qk = slice(h * QK_HEAD, (h + 1) * QK_HEAD)
        vd = slice(h * MLA_V_DIM, (h + 1) * MLA_V_DIM)
        q = q_ref[0, :, qk]
        scores = [lax.dot_general(q, refs[1 + 2 * s][0, :, qk], nt, preferred_element_type=F32)
                  for s in range(n_seg)]
        m = functools.reduce(jnp.maximum, [jnp.max(s, axis=-1, keepdims=True) for s in scores])
        den = 0.0
        acc = 0.0
        for s in range(n_seg):
            p = jnp.exp2(scores[s] - m)
            den = den + jnp.sum(p, axis=-1, keepdims=True)
            acc = acc + _dot(p.astype(BF16), refs[2 + 2 * s][0, :, vd])
        o_ref[0, :, vd] = (acc * (1.0 / den)).astype(o_ref.dtype)


def _attn_call(q, segs):
    b, l, _ = q.shape
    tq = min(l, 256)
    hs = ATTN_HEADS_PER_STEP
    in_specs = [pl.BlockSpec((1, tq, hs * QK_HEAD), lambda i, h, j: (i, j, h))]
    args = [q]
    for k, v in segs:
        lk = k.shape[1]
        in_specs.append(pl.BlockSpec((1, lk, hs * QK_HEAD), lambda i, h, j: (i, 0, h)))
        in_specs.append(pl.BlockSpec((1, lk, hs * MLA_V_DIM), lambda i, h, j: (i, 0, h)))
        args += [k, v]
    return pl.pallas_call(
        functools.partial(_attn_kernel, n_seg=len(segs)),
        grid=(b, MLA_HEADS // hs, l // tq),
        in_specs=in_specs,
        out_specs=pl.BlockSpec((1, tq, hs * MLA_V_DIM), lambda i, h, j: (i, j, h)),
        out_shape=jax.ShapeDtypeStruct((b, l, MLA_HEADS * MLA_V_DIM), BF16),
        compiler_params=_cp("parallel", "parallel", "parallel"),
        name="attention",
    )(*args)


HY_PAD = 8
HY_ROWS = 128


def _hy_short_kernel(p_ref, w_ref, b_ref, o_ref, up_ref):
    l = p_ref.shape[1]
    c = p_ref.shape[2]
    zeros = jnp.zeros((HY_PAD, c), F32)
    up_ref[0:HY_PAD, :] = zeros
    up_ref[HY_PAD + l:HY_PAD + l + HY_PAD, :] = zeros

    def copy(r, carry):
        base = pl.multiple_of(r * HY_ROWS, HY_ROWS)
        up_ref[pl.ds(base + HY_PAD, HY_ROWS), :] = p_ref[0, pl.ds(base, HY_ROWS), :].astype(F32)
        return carry

    lax.fori_loop(0, l // HY_ROWS, copy, 0)
    first = HY_PAD - HY_SHORT // 2

    def conv(r, carry):
        base = pl.multiple_of(r * HY_ROWS, HY_ROWS)
        for cc in range(c // LANES):
            lanes = slice(cc * LANES, (cc + 1) * LANES)
            win = up_ref[pl.ds(base, HY_ROWS + 2 * HY_PAD), lanes]
            acc = jnp.zeros((HY_ROWS, LANES), F32) + b_ref[:, lanes]
            for k in range(HY_SHORT):
                acc = acc + win[first + k:first + k + HY_ROWS, :] * w_ref[k:k + 1, lanes]
            o_ref[0, 0, pl.ds(base, HY_ROWS), lanes] = acc
        return carry

    lax.fori_loop(0, l // HY_ROWS, conv, 0)


def _hy_short_call(p, w, bias):
    b, l, c3 = p.shape
    c = HYENA_W
    return pl.pallas_call(
        _hy_short_kernel,
        grid=(b, c3 // c),
        in_specs=[pl.BlockSpec((1, l, c), lambda i, g: (i, 0, g)),
                  pl.BlockSpec((HY_SHORT, c), lambda i, g: (0, g)),
                  pl.BlockSpec((1, c), lambda i, g: (0, g))],
        out_specs=pl.BlockSpec((1, 1, l, c), lambda i, g: (g, i, 0, 0)),
        out_shape=jax.ShapeDtypeStruct((c3 // c, b, l, c), F32),
        scratch_shapes=[pltpu.VMEM((l + 2 * HY_PAD, c), F32)],
        compiler_params=_cp("parallel", "parallel"),
        name="hyena_short",
    )(p, w, bias.reshape(1, c3))


def _hy_filter_kernel(z_ref, w1_ref, b1_ref, f1_ref, w2_ref, b2_ref, f2_ref, w3_ref, b3_ref, win_ref, o_ref):
    hp = lax.Precision.HIGHEST

    def mm(a, b):
        return jnp.dot(a, b, preferred_element_type=F32, precision=hp)

    h = jnp.sin(f1_ref[...] * (mm(z_ref[...], w1_ref[...]) + b1_ref[...]))
    h = jnp.sin(f2_ref[...] * (mm(h, w2_ref[...]) + b2_ref[...]))
    h = mm(h, w3_ref[...]) + b3_ref[...]
    c = HYENA_W
    win = win_ref[...]
    tl = h.shape[0]
    row = lax.broadcasted_iota(jnp.int32, (tl, c), 0) + pl.program_id(0) * tl
    for o in range(HY_ORDER):
        fwd = h[:, (2 * o) * c:(2 * o + 1) * c] * win
        bwd = jnp.where(row == 0, 0.0, h[:, (2 * o + 1) * c:(2 * o + 2) * c] * win)
        o_ref[:, o * c:(o + 1) * c] = (fwd + bwd).astype(o_ref.dtype)
        o_ref[:, (HY_ORDER + o) * c:(HY_ORDER + o + 1) * c] = (fwd - bwd).astype(o_ref.dtype)


def _pad2(a, rows, cols):
    return jnp.pad(a, ((0, rows - a.shape[0]), (0, cols - a.shape[1])))


def _hy_filter_call(l, w1, b1, f1, w2, b2, f2, w3, b3):
    t = jnp.arange(l, dtype=F32)
    t_norm = t / max(l - 1, 1)
    ang = (2.0 * math.pi / l) * t[:, None] * jnp.linspace(1e-4, HY_BANDS - 1, HY_BANDS, dtype=F32)[None, :]
    z = jnp.concatenate([t_norm[:, None], jnp.cos(ang), -jnp.sin(ang)], axis=-1)
    deltas = jnp.abs(jnp.linspace(HY_MIN_DECAY, HY_MAX_DECAY, HYENA_W, dtype=F32))
    window = jnp.exp(-t_norm[:, None] * deltas[None, :])
    n3 = w3.shape[1]
    tl = min(l, 256)
    full = lambda shape: pl.BlockSpec(shape, lambda i: (0, 0))
    return pl.pallas_call(
        _hy_filter_kernel,
        grid=(l // tl,),
        in_specs=[pl.BlockSpec((tl, LANES), lambda i: (i, 0)),
                  full((LANES, LANES)), full((1, LANES)), full((1, LANES)),
                  full((LANES, LANES)), full((1, LANES)), full((1, LANES)),
                  full((LANES, n3)), full((1, n3)),
                  pl.BlockSpec((tl, HYENA_W), lambda i: (i, 0))],
        out_specs=pl.BlockSpec((tl, n3), lambda i: (i, 0)),
        out_shape=jax.ShapeDtypeStruct((l, n3), BF16),
        compiler_params=_cp("parallel"),
        name="hyena_filter",
    )(_pad2(z, l, LANES), _pad2(w1, LANES, LANES), _pad2(b1[None], 1, LANES), _pad2(f1[None], 1, LANES),
      _pad2(w2, LANES, LANES), _pad2(b2[None], 1, LANES), _pad2(f2[None], 1, LANES),
      _pad2(w3, LANES, n3), b3[None], window)


def _dft_tables_kernel(ac_ref, as_ref, bc_ref, bs_ref, fwd_ref, inv_ref):
    tm, l = fwd_ref.shape[1], fwd_ref.shape[2]
    n = 2 * l
    row = lax.broadcasted_iota(jnp.int32, (tm, LANES), 0) + pl.program_id(0) * tm
    lane = lax.broadcasted_iota(jnp.int32, (tm, LANES), 1)
    row_sign = jnp.where((row & 1) == 0, 1.0, -1.0)
    lane_sign = jnp.where((lane & 1) == 0, 1.0, -1.0)
    bc = bc_ref[...]
    bs = bs_ref[...]
    for a in range(l // LANES):
        cols = slice(a * LANES, (a + 1) * LANES)
        ca = ac_ref[:, a:a + 1]
        sa = as_ref[:, a:a + 1]
        c = ca * bc - sa * bs
        s = sa * bc + ca * bs
        fwd_ref[0, :, cols] = c.astype(BF16)
        fwd_ref[1, :, cols] = jnp.where(row == 0, lane_sign, -s).astype(BF16)
        inv_c = c * (2.0 / n)
        inv_s = -s * (2.0 / n)
        if a == 0:
            inv_c = jnp.where(lane == 0, 1.0 / n, inv_c)
            inv_s = jnp.where(lane == 0, row_sign / n, inv_s)
        inv_ref[:, cols] = inv_c.astype(BF16)
        inv_ref[:, l + a * LANES:l + (a + 1) * LANES] = inv_s.astype(BF16)


def _dft_mats(l):
    n = 2 * l
    f = jnp.arange(l, dtype=jnp.int32)[:, None]

    def table(t):
        ang = ((f * t[None, :]) % n).astype(F32) * (2.0 * math.pi / n)
        return jnp.cos(ang), jnp.sin(ang)

    coarse = table(jnp.arange(l // LANES, dtype=jnp.int32) * LANES)
    fine = table(jnp.arange(LANES, dtype=jnp.int32))
    tm = min(l, 256)
    small = pl.BlockSpec((tm, l // LANES), lambda i: (i, 0))
    wide = pl.BlockSpec((tm, LANES), lambda i: (i, 0))
    return pl.pallas_call(
        _dft_tables_kernel,
        grid=(l // tm,),
        in_specs=[small, small, wide, wide],
        out_specs=[pl.BlockSpec((2, tm, l), lambda i: (0, i, 0)), pl.BlockSpec((tm, n), lambda i: (i, 0))],
        out_shape=[jax.ShapeDtypeStruct((2, l, l), BF16), jax.ShapeDtypeStruct((l, n), BF16)],
        compiler_params=_cp("parallel"),
        name="dft_tables",
    )(*coarse, *fine)


def _spec_kernel(f_ref, a_ref, o_ref):
    half = a_ref.shape[1] // 2
    hi = _dot(f_ref[1], a_ref[:, half:])
    o_ref[0] = _dot(f_ref[0], a_ref[:, :half])
    o_ref[1] = hi

    @pl.when(pl.program_id(0) == 0)
    def _():
        nyq = _dot(f_ref[1, 0:16, :], a_ref[:, :half])
        row = lax.broadcasted_iota(jnp.int32, nyq.shape, 0)
        o_ref[1, 0:16, :] = jnp.where(row == 0, nyq, hi[0:16])


def _spec_call(fwd, a):
    _, lf, l = fwd.shape
    n = a.shape[1] // 2
    tm = min(lf, 512)
    return pl.pallas_call(
        _spec_kernel,
        grid=(lf // tm,),
        in_specs=[pl.BlockSpec((2, tm, l), lambda i: (0, i, 0)),
                  pl.BlockSpec(a.shape, lambda i: (0, 0))],
        out_specs=pl.BlockSpec((2, tm, n), lambda i: (0, i, 0)),
        out_shape=jax.ShapeDtypeStruct((2, lf, n), F32),
        compiler_params=_cp("arbitrary"),
        name="hyena_spectrum",
    )(fwd, a)


def _dft_fwd_kernel(f_ref, u_ref, h_ref, y_ref):
    u = u_ref[0, 0].astype(BF16)
    xr = _dot(f_ref[0], u)
    xi = _dot(f_ref[1], u)
    hr = h_ref[0]
    hi = h_ref[1]
    yr = xr * hr - xi * hi
    yi = xr * hi + xi * hr
    y_ref[0, 0] = yr.astype(BF16)
    y_ref[0, 1] = yi.astype(BF16)

    @pl.when(pl.program_id(0) == 0)
    def _():
        row = lax.broadcasted_iota(jnp.int32, (16, u.shape[1]), 0)
        y_ref[0, 0, 0:16, :] = jnp.where(row == 0, xr[0:16] * hr[0:16], yr[0:16]).astype(BF16)
        y_ref[0, 1, 0:16, :] = jnp.where(row == 0, xi[0:16] * hi[0:16], yi[0:16]).astype(BF16)


def _dft_fwd_call(fwd, u, group, spec, order):
    _, b, l, c = u.shape
    lf = fwd.shape[1]
    tm = min(lf, 512)
    return pl.pallas_call(
        _dft_fwd_kernel,
        grid=(lf // tm, b),
        in_specs=[pl.BlockSpec((2, tm, l), lambda i, j: (0, i, 0)),
                  pl.BlockSpec((1, 1, l, c), lambda i, j: (group, j, 0, 0)),
                  pl.BlockSpec((2, tm, c), lambda i, j: (0, i, order))],
        out_specs=pl.BlockSpec((1, 2, tm, c), lambda i, j: (j, 0, i, 0)),
        out_shape=jax.ShapeDtypeStruct((b, 2, lf, c), BF16),
        compiler_params=_cp("arbitrary", "arbitrary"),
        name="hyena_dft_fwd",
    )(fwd, u, spec)


def _dft_inv_kernel(g_ref, y_ref, u_ref, b_ref, x_ref, o_ref):
    r = _dot(g_ref[...], y_ref[0]) + u_ref[0, 0].astype(F32) * b_ref[...]
    o_ref[0, 0] = (x_ref[0, 0] * r).astype(o_ref.dtype)


def _dft_inv_call(inv, y, u, u_group, bias, gate, gate_group):
    _, b, l, c = u.shape
    n = inv.shape[1]
    tm = min(l, 512)
    y = y.reshape(b, n, c)
    row = lambda g: pl.BlockSpec((1, 1, tm, c), lambda i, j: (g, j, i, 0))
    return pl.pallas_call(
        _dft_inv_kernel,
        grid=(l // tm, b),
        in_specs=[pl.BlockSpec((tm, n), lambda i, j: (i, 0)),
                  pl.BlockSpec((1, n, c), lambda i, j: (j, 0, 0)),
                  row(u_group), pl.BlockSpec((1, c), lambda i, j: (0, 0)), row(gate_group)],
        out_specs=row(0),
        out_shape=jax.ShapeDtypeStruct((1, b, l, c), BF16),
        compiler_params=_cp("parallel", "parallel"),
        name="hyena_dft_inv",
    )(inv, y, u, bias.reshape(1, c), gate)


def _hyena_call(p_hy, sh_w, sh_b, spec, hy_bias, fwd, inv):
    u3 = _hy_short_call(p_hy, sh_w, sh_b)
    z = _dft_inv_call(inv, _dft_fwd_call(fwd, u3, 0, spec, 0), u3, 0, hy_bias[0], u3, 1)
    return _dft_inv_call(inv, _dft_fwd_call(fwd, z, 0, spec, 1), z, 0, hy_bias[1], u3, 2)[0]


def _out_proj_kernel(c_ref, a_ref, h_ref, w1_ref, w2_ref, w3_ref, x_ref, g_ref, o_ref):
    acc = _dot(c_ref[...], w1_ref[...]) + _dot(a_ref[...], w2_ref[...]) + _dot(h_ref[...], w3_ref[...])
    o_ref[...] = x_ref[...] + g_ref[0] * acc


def _out_proj_call(conv_o, att_o, hy_o, w_out, x, gate, l):
    m, d = x.shape
    tm = min(l, 1024)
    tn = 512
    k1, k2 = conv_o.shape[1], att_o.shape[1]
    w1, w2, w3 = w_out[:k1], w_out[k1:k1 + k2], w_out[k1 + k2:]
    act = lambda k: pl.BlockSpec((tm, k), lambda i, j: (i, 0))
    wsp = lambda k: pl.BlockSpec((k, tn), lambda i, j: (0, j))
    tile = pl.BlockSpec((tm, tn), lambda i, j: (i, j))
    return pl.pallas_call(
        _out_proj_kernel,
        grid=(m // tm, d // tn),
        in_specs=[act(k1), act(k2), act(w3.shape[0]), wsp(k1), wsp(k2), wsp(w3.shape[0]), tile,
                  pl.BlockSpec((1, 1, tn), lambda i, j: ((i * tm) // l, 0, j))],
        out_specs=tile,
        out_shape=jax.ShapeDtypeStruct((m, d), F32),
        compiler_params=_cp("parallel", "parallel"),
        name="out_proj",
    )(conv_o, att_o, hy_o, w1, w2, w3, x, gate)


def _norm_router_kernel(x_ref, g_ref, sh_ref, sc_ref, wr_ref, rb_ref, h_ref, idx_ref, gate_ref):
    y = _rms(x_ref[0], g_ref[...])
    hm = y * (1.0 + sc_ref[0]) + sh_ref[0]
    hi = hm.astype(BF16)
    lo = (hm - hi.astype(F32)).astype(BF16)
    h_ref[0] = _pack_rows(hm)
    nt = (((1,), (1,)), ((), ()))
    wr = wr_ref[...]
    r = (lax.dot_general(wr, hi, nt, preferred_element_type=F32)
         + lax.dot_general(wr, lo, nt, preferred_element_type=F32))
    logits = r[:N_EXPERTS] + r[N_EXPERTS:]
    score = jax.nn.sigmoid(logits)
    sel = score + rb_ref[...]
    rows = [sel[e:e + 1] for e in range(N_EXPERTS)]
    gscore = []
    for g in range(N_GROUPS):
        a, b, c, d = rows[4 * g:4 * g + 4]
        h1, l1, h2, l2 = jnp.maximum(a, b), jnp.minimum(a, b), jnp.maximum(c, d), jnp.minimum(c, d)
        gscore.append(jnp.maximum(h1, h2) + jnp.maximum(jnp.minimum(h1, h2), jnp.maximum(l1, l2)))
    best = gscore[0]
    gbest = jnp.zeros(best.shape, jnp.int32)
    for g in range(1, N_GROUPS):
        upd = gscore[g] > best
        gbest = jnp.where(upd, g, gbest)
        best = jnp.where(upd, gscore[g], best)
    masked = [jnp.where(gbest == e // EXPERTS_PER_GROUP, rows[e], NEG_INF) for e in range(N_EXPERTS)]
    picks = []
    for _ in range(2):
        val = jnp.full(best.shape, NEG_INF, F32)
        idx = jnp.zeros(best.shape, jnp.int32)
        for e in range(N_EXPERTS):
            cand = masked[e]
            if picks:
                cand = jnp.where(picks[0] == e, NEG_INF, cand)
            upd = cand > val
            idx = jnp.where(upd, e, idx)
            val = jnp.where(upd, cand, val)
        picks.append(idx)
    gates = [functools.reduce(jnp.add, [jnp.where(p == e, score[e:e + 1], 0.0) for e in range(N_EXPERTS)])
             for p in picks]
    den = gates[0] + gates[1]
    idx_ref[...] = jnp.concatenate(picks, axis=0)
    gate_ref[...] = jnp.concatenate([gates[0] / den, gates[1] / den], axis=0)


def _norm_router_call(x, g, shift, scale, wr, rbias):
    b, l, d = x.shape
    tl = min(l, 512)
    nt = l // tl
    row = pl.BlockSpec((1, 1, d), lambda i, j: (i, 0, 0))
    tok = pl.BlockSpec((2, tl), lambda i, j: (0, i * nt + j))
    return pl.pallas_call(
        _norm_router_kernel,
        grid=(b, nt),
        in_specs=[pl.BlockSpec((1, tl, d), lambda i, j: (i, j, 0)),
                  pl.BlockSpec((1, d), lambda i, j: (0, 0)), row, row,
                  pl.BlockSpec(wr.shape, lambda i, j: (0, 0)),
                  pl.BlockSpec((N_EXPERTS, 1), lambda i, j: (0, 0))],
        out_specs=[pl.BlockSpec((1, tl, d // 2), lambda i, j: (i, j, 0)), tok, tok],
        out_shape=[jax.ShapeDtypeStruct((b, l, d // 2), jnp.int32),
                   jax.ShapeDtypeStruct((2, b * l), jnp.int32),
                   jax.ShapeDtypeStruct((2, b * l), F32)],
        compiler_params=_cp("parallel", "parallel"),
        name="norm_router",
    )(x, g.reshape(1, d), shift, scale, wr, rbias.reshape(N_EXPERTS, 1))


def _expert_changed(te_ref, i):
    return (i == 0) | (te_ref[i] != te_ref[jnp.maximum(i - 1, 0)])


def _pack_rows(x):
    bits = pltpu.bitcast(x.astype(BF16).astype(F32), jnp.uint32)
    half = bits.shape[1] // 2
    return pltpu.bitcast((bits[:, :half] >> 16) | (bits[:, half:] & jnp.uint32(0xFFFF0000)), jnp.int32)


def _unpack_rows(words):
    w = pltpu.bitcast(words, jnp.uint32)
    return pltpu.bitcast(w << 16, F32), pltpu.bitcast(w & jnp.uint32(0xFFFF0000), F32)


def _moe_up_kernel(te_ref, nu_ref, x_ref, wg_ref, wu_ref, h_ref, wgb_ref, wub_ref):
    i = pl.program_id(1)

    @pl.when(_expert_changed(te_ref, i))
    def _():
        wgb_ref[...] = wg_ref[0, 0].astype(BF16)
        wub_ref[...] = wu_ref[0, 0].astype(BF16)

    @pl.when(i < nu_ref[0])
    def _():
        xa, xb = (v.astype(BF16) for v in _unpack_rows(x_ref[...]))
        half = xa.shape[1]
        a = _dot(xa, wgb_ref[:half, :]) + _dot(xb, wgb_ref[half:, :])
        u = _dot(xa, wub_ref[:half, :]) + _dot(xb, wub_ref[half:, :])
        h_ref[...] = (a * jax.nn.sigmoid(a) * u).astype(BF16)

    @pl.when(i >= nu_ref[0])
    def _():
        h_ref[...] = jnp.zeros(h_ref.shape, BF16)


def _moe_up_call(xs, w_gate, w_up, layer, te, nu, tile):
    p, dw = xs.shape
    _, _, d, f = w_gate.shape
    tf = 512
    wsp = pl.BlockSpec((1, 1, d, tf), lambda j, i, te, nu: (layer, te[i], 0, j))
    return pl.pallas_call(
        _moe_up_kernel,
        grid_spec=pltpu.PrefetchScalarGridSpec(
            num_scalar_prefetch=2,
            grid=(f // tf, p // tile),
            in_specs=[pl.BlockSpec((tile, dw), lambda j, i, te, nu: (i, 0)), wsp, wsp],
            out_specs=pl.BlockSpec((tile, tf), lambda j, i, te, nu: (i, j)),
            scratch_shapes=[pltpu.VMEM((d, tf), BF16), pltpu.VMEM((d, tf), BF16)]),
        out_shape=jax.ShapeDtypeStruct((p, f), BF16),
        compiler_params=_cp("arbitrary", "arbitrary"),
        name="moe_up",
    )(te, nu, xs, w_gate, w_up)


def _moe_down_kernel(te_ref, nu_ref, h_ref, wd_ref, y_ref, wdb_ref):
    i = pl.program_id(0)

    @pl.when(_expert_changed(te_ref, i))
    def _():
        wdb_ref[...] = wd_ref[0, 0].astype(BF16)

    @pl.when(i < nu_ref[0])
    def _():
        y_ref[...] = _pack_rows(_dot(h_ref[...], wdb_ref[...]))

    @pl.when(i >= nu_ref[0])
    def _():
        y_ref[...] = jnp.zeros(y_ref.shape, jnp.int32)


def _moe_down_call(h, w_down, layer, te, nu, tile):
    p, f = h.shape
    d = w_down.shape[3]
    return pl.pallas_call(
        _moe_down_kernel,
        grid_spec=pltpu.PrefetchScalarGridSpec(
            num_scalar_prefetch=2,
            grid=(p // tile,),
            in_specs=[pl.BlockSpec((tile, f), lambda i, te, nu: (i, 0)),
                      pl.BlockSpec((1, 1, f, d), lambda i, te, nu: (layer, te[i], 0, 0))],
            out_specs=pl.BlockSpec((tile, d // 2), lambda i, te, nu: (i, 0)),
            scratch_shapes=[pltpu.VMEM((f, d), BF16)]),
        out_shape=jax.ShapeDtypeStruct((p, d // 2), jnp.int32),
        compiler_params=_cp("arbitrary"),
        name="moe_down",
    )(te, nu, h, w_down)


PLAN_BLOCK = 512


def _plan_kernel(idx_ref, pos_ref, te_ref, nu_ref, rank_ref, *, tile):
    t = idx_ref.shape[1]
    nb = t // PLAN_BLOCK
    experts = lax.broadcasted_iota(jnp.int32, (N_EXPERTS, PLAN_BLOCK), 0)
    r = lax.broadcasted_iota(jnp.int32, (PLAN_BLOCK, PLAN_BLOCK), 0)
    c = lax.broadcasted_iota(jnp.int32, (PLAN_BLOCK, PLAN_BLOCK), 1)
    before = jnp.where(r < c, 1.0, 0.0).astype(BF16)

    def rank_pass(k):
        def body(j, carry):
            cols = pl.ds(pl.multiple_of(j * PLAN_BLOCK, PLAN_BLOCK), PLAN_BLOCK)
            onehot = jnp.where(experts == idx_ref[k:k + 1, cols], 1.0, 0.0)
            earlier = _dot(onehot.astype(BF16), before) + carry
            rank_ref[k:k + 1, cols] = jnp.sum(onehot * earlier, axis=0, keepdims=True)
            return carry + jnp.sum(onehot, axis=1, keepdims=True)
        return body

    counts = jnp.zeros((N_EXPERTS, 1), F32)
    for k in range(2):
        counts = lax.fori_loop(0, nb, rank_pass(k), counts)
    padded = jnp.ceil(counts * (1.0 / tile)) * tile
    starts, ends = [], []
    acc = jnp.zeros((1, 1), F32)
    for e in range(N_EXPERTS):
        starts.append(acc)
        acc = acc + padded[e:e + 1]
        ends.append(acc)

    def place_pass(k):
        def body(j, carry):
            cols = pl.ds(pl.multiple_of(j * PLAN_BLOCK, PLAN_BLOCK), PLAN_BLOCK)
            row = idx_ref[k:k + 1, cols]
            start = functools.reduce(jnp.add, [jnp.where(row == e, starts[e], 0.0) for e in range(N_EXPERTS)])
            pos_ref[k:k + 1, cols] = (rank_ref[k:k + 1, cols] + start).astype(jnp.int32)
            return carry
        return body

    for k in range(2):
        lax.fori_loop(0, nb, place_pass(k), 0)
    n_used = acc * (1.0 / tile)
    tiles = lax.broadcasted_iota(jnp.int32, te_ref.shape, 1).astype(F32)
    first_row = jnp.minimum(tiles, n_used - 1.0) * tile
    te = functools.reduce(jnp.add, [jnp.where(ends[e] <= first_row, 1.0, 0.0) for e in range(N_EXPERTS)])
    te_ref[...] = te.astype(jnp.int32)
    nu_ref[...] = jnp.broadcast_to(n_used, nu_ref.shape).astype(jnp.int32)


def _plan_call(idx, tile, n_tiles):
    t = idx.shape[1]
    whole = lambda shape: pl.BlockSpec(shape, lambda: (0,) * len(shape))
    pos, te, nu = pl.pallas_call(
        functools.partial(_plan_kernel, tile=tile),
        in_specs=[whole(idx.shape)],
        out_specs=[whole(idx.shape), whole((1, LANES)), whole((1, LANES))],
        out_shape=[jax.ShapeDtypeStruct(idx.shape, jnp.int32),
                   jax.ShapeDtypeStruct((1, LANES), jnp.int32),
                   jax.ShapeDtypeStruct((1, LANES), jnp.int32)],
        scratch_shapes=[pltpu.VMEM(idx.shape, F32)],
        name="route_plan",
    )(idx)
    return pos.reshape(2 * t), te[0, :n_tiles], nu[0, :1]


V7X_SC_CORES = 2
V7X_SC_SUBCORES = 16
SC_WORKERS = V7X_SC_CORES * V7X_SC_SUBCORES
SC_ROW_BYTES = 256 * 1024


def _sc_worker():
    return lax.axis_index("s") * V7X_SC_CORES + lax.axis_index("c")


def _sc_dispatch_call(streams, pos, n_rows):
    w = streams[0].shape[1]
    sizes = [s.shape[0] for s in streams]
    t = sum(sizes)
    per_w = t // SC_WORKERS
    ch = min(per_w, SC_ROW_BYTES // (4 * w))
    assert per_w % ch == 0 and all(size % ch == 0 for size in sizes)
    mesh = plsc.VectorSubcoreMesh(core_axis_name="c", subcore_axis_name="s")

    @functools.partial(pl.kernel, mesh=mesh, out_type=jax.ShapeDtypeStruct((n_rows, w), jnp.int32),
                       scratch_types=[pltpu.VMEM((ch,), jnp.int32), pltpu.VMEM((ch, w), jnp.int32)],
                       name="sc_dispatch")
    def run(*refs):
        srcs = refs[:len(sizes)]
        pos_hbm, xs_hbm, idx_v, rows_v = refs[len(sizes):]
        first = _sc_worker() * per_w

        @pl.loop(0, per_w // ch)
        def _(j):
            base = first + j * ch
            start = 0
            for src, size in zip(srcs, sizes):
                def load(src=src, start=start):
                    pltpu.sync_copy(src.at[pl.ds(base - start, ch)], rows_v)
                if len(sizes) == 1:
                    load()
                else:
                    pl.when((base >= start) & (base < start + size))(load)
                start += size
            for k in range(2):
                pltpu.sync_copy(pos_hbm.at[pl.ds(k * t + base, ch)], idx_v)
                pltpu.sync_copy(rows_v, xs_hbm.at[idx_v])

    return run(*streams, pos)


def _sc_combine_call(ys, pos):
    _, d = ys.shape
    n = pos.shape[0]
    per_w = n // SC_WORKERS
    ch = min(per_w, SC_ROW_BYTES // (4 * d))
    mesh = plsc.VectorSubcoreMesh(core_axis_name="c", subcore_axis_name="s")

    @functools.partial(pl.kernel, mesh=mesh, out_type=jax.ShapeDtypeStruct((n, d), jnp.int32),
                       scratch_types=[pltpu.VMEM((ch,), jnp.int32), pltpu.VMEM((ch, d), jnp.int32)],
                       name="sc_combine")
    def run(ys_hbm, pos_hbm, out_hbm, idx_v, rows_v):
        first = _sc_worker() * per_w

        @pl.loop(0, per_w // ch)
        def _(j):
            base = first + j * ch
            pltpu.sync_copy(pos_hbm.at[pl.ds(base, ch)], idx_v)
            pltpu.sync_copy(ys_hbm.at[idx_v], rows_v)
            pltpu.sync_copy(rows_v, out_hbm.at[pl.ds(base, ch)])

    return run(ys, pos)


def _moe_call(streams, idx, w_gate, w_up, w_down, layer):
    t = idx.shape[1]
    tile = min(MOE_TILE, max(t // N_EXPERTS, 128))
    n_tiles = (2 * t) // tile + N_EXPERTS
    assert n_tiles <= LANES
    pos, te, nu = _plan_call(idx, tile, n_tiles)
    xs = _sc_dispatch_call(streams, pos, n_tiles * tile)
    hid = _moe_up_call(xs, w_gate, w_up, layer, te, nu, tile)
    ys = _moe_down_call(hid, w_down, layer, te, nu, tile)
    return _sc_combine_call(ys, pos).reshape(2, t, ys.shape[1])


def _moe_resid_kernel(x_ref, y0_ref, y1_ref, gt_ref, g_ref, *rest, final):
    gt = gt_ref[0]
    halves0 = _unpack_rows(y0_ref[0])
    halves1 = _unpack_rows(y1_ref[0])
    y = jnp.concatenate([gt[:, 0:1] * a + gt[:, 1:2] * b for a, b in zip(halves0, halves1)], axis=1)
    x = x_ref[0] + g_ref[0] * y
    if final:
        fg_ref, o_ref = rest
        x = _rms(x, fg_ref[...])
    else:
        (o_ref,) = rest
    o_ref[0] = x


def _moe_resid_call(x, y, first_row, gates, gate, final_g=None):
    b, l, d = x.shape
    tl = min(l, 512)
    nt = l // tl
    tile = pl.BlockSpec((1, tl, d), lambda i, j: (i, j, 0))
    choice = lambda k: pl.BlockSpec((1, tl, d // 2), lambda i, j: (k, first_row // tl + i * nt + j, 0))
    in_specs = [tile, choice(0), choice(1), pl.BlockSpec((1, tl, 2), lambda i, j: (i, j, 0)),
                pl.BlockSpec((1, 1, d), lambda i, j: (i, 0, 0))]
    args = [x, y, y, gates.T.reshape(b, l, 2), gate]
    if final_g is not None:
        in_specs.append(pl.BlockSpec((1, d), lambda i, j: (0, 0)))
        args.append(final_g.reshape(1, d))
    return pl.pallas_call(
        functools.partial(_moe_resid_kernel, final=final_g is not None),
        grid=(b, l // tl),
        in_specs=in_specs,
        out_specs=tile,
        out_shape=jax.ShapeDtypeStruct((b, l, d), F32),
        compiler_params=_cp("parallel", "parallel"),
        name="moe_residual",
    )(*args)


def _rope_perm(w):
    r = w.reshape(w.shape[:-1] + (2, 2, ROPE_FREQS))
    return jnp.stack([-r[..., 1, :], r[..., 0, :]], axis=-2).reshape(w.shape)


def _rope_table(l, rotary):
    if not rotary:
        return jnp.concatenate([jnp.ones((l, MLA_ROPE), F32), jnp.zeros((l, MLA_ROPE), F32)], axis=1)
    rows = l // GRID_W
    r = jnp.repeat(jnp.arange(rows), GRID_W)
    col = jnp.tile(jnp.arange(GRID_W), rows)
    pos = jnp.stack([r, col], axis=-1).astype(F32)
    inv = ROPE_THETA ** (-jnp.arange(ROPE_FREQS, dtype=F32) / ROPE_FREQS)
    ang = pos[:, :, None] * inv
    lay = lambda a: jnp.concatenate([a[:, 0], a[:, 0], a[:, 1], a[:, 1]], axis=1)
    return jnp.concatenate([lay(jnp.cos(ang)), lay(jnp.sin(ang))], axis=1)


def _layer_weights(w_in, w_uq, w_ukv, w_out):
    d = w_in.shape[0]
    kr = w_in[:, OFF_KR:OFF_HY]
    w_kv_in = jnp.concatenate([w_in[:, OFF_KV:OFF_KR], kr, _rope_perm(kr)], axis=1)
    uq = w_uq.reshape(MLA_Q_RANK, MLA_HEADS, MLA_NOPE + MLA_ROPE)
    w_q = jnp.concatenate([uq, _rope_perm(uq[:, :, MLA_NOPE:])], axis=2).reshape(MLA_Q_RANK, MLA_HEADS * QK_HEAD)
    ukv = w_ukv.reshape(MLA_KV_RANK, MLA_HEADS, MLA_NOPE + MLA_V_DIM)
    w_k = ukv[:, :, :MLA_NOPE].reshape(MLA_KV_RANK, MLA_HEADS * MLA_NOPE)
    w_v = ukv[:, :, MLA_NOPE:].reshape(MLA_KV_RANK, MLA_HEADS * MLA_V_DIM)
    cast = lambda a: a.astype(BF16)
    w_all = jnp.concatenate([w_in[:, :OFF_KV], w_kv_in, w_in[:, OFF_HY:]], axis=1)
    return dict(all=cast(w_all), kv=cast(w_kv_in), uq=cast(w_q), uk=cast(w_k), uv=cast(w_v), out=cast(w_out))


def _router_weights(w_router):
    hi = w_router.astype(BF16)
    lo = (w_router - hi.astype(F32)).astype(BF16)
    return jnp.concatenate([hi.T, lo.T], axis=0)


def _mixer(x, mod, w, lp, cs, spec, fwd, inv, kv_extra, full):
    b, l, d = x.shape
    kv_cols = w["kv"].shape[1]
    if full:
        splits = [OFF_Q, MLA_Q_RANK, kv_cols, (HY_ORDER + 1) * HYENA_W]
        p_conv, p_q, p_kv, p_hy = _in_proj_call(x, lp["norm1_g"], mod[0], mod[1], w["all"], splits)
    else:
        (p_kv,) = _in_proj_call(x, lp["norm1_g"], mod[0], mod[1], w["kv"], [kv_cols])
    k, v = _kv_call(p_kv.reshape(b * l, kv_cols), lp["kv_norm_g"], w["uk"], w["uv"], cs)
    k = k.reshape(b, l, -1)
    v = v.reshape(b, l, -1)
    if not full:
        return None, k, v
    conv_o = _conformer_call(p_conv, lp["conv_dw_w"], lp["conv_dw_b"], lp["conv_ln_g"], lp["conv_ln_b"])
    q = _q_call(p_q.reshape(b * l, MLA_Q_RANK), lp["q_norm_g"], w["uq"], cs).reshape(b, l, -1)
    att_o = _attn_call(q, kv_extra + [(k, v)])
    hy_o = _hyena_call(p_hy, lp["hy_short_w"], lp["hy_short_b"], spec, lp["hy_bias"], fwd, inv)
    x = _out_proj_call(conv_o.reshape(b * l, -1), att_o.reshape(b * l, -1), hy_o.reshape(b * l, -1),
                       w["out"], x.reshape(b * l, d), mod[2], l).reshape(b, l, d)
    return x, k, v


def _channel(streams, lp, wr, rbias, experts, layer, final_g=None):
    d = streams[0][0].shape[2]
    routed = [_norm_router_call(x, lp["norm2_g"], mod[3], mod[4], wr, rbias) for x, mod in streams]
    idx = jnp.concatenate([r[1] for r in routed], axis=1)
    y = _moe_call([r[0].reshape(-1, d // 2) for r in routed], idx, *experts, layer)
    outs = []
    first_row = 0
    for (x, mod), (_, _, gates) in zip(streams, routed):
        outs.append(_moe_resid_call(x, y, first_row, gates, mod[5], final_g))
        first_row += x.shape[0] * x.shape[1]
    return outs


def _filter_spectrum(l, lp, fwd):
    a = _hy_filter_call(l, lp["hy_w1"], lp["hy_b1"], lp["hy_freq1"], lp["hy_w2"], lp["hy_b2"], lp["hy_freq2"],
                        lp["hy_w3"], lp["hy_b3"])
    return _spec_call(fwd, a)


def kernel(x, c, ctx, c_ctx, norm1_g, norm2_g, w_ada, b_ada, w_in, conv_dw_w, conv_dw_b, conv_ln_g, conv_ln_b, q_norm_g, w_uq, kv_norm_g, w_ukv, hy_short_w, hy_short_b, hy_w1, hy_b1, hy_freq1, hy_w2, hy_b2, hy_freq2, hy_w3, hy_b3, hy_bias, w_out, w_router, router_bias, w_gate, w_up, w_down, final_norm_g):
    b, l, d = x.shape
    lc = ctx.shape[1]
    depth = w_in.shape[0]
    per_layer = dict(norm1_g=norm1_g, norm2_g=norm2_g, conv_dw_w=conv_dw_w, conv_dw_b=conv_dw_b,
                     conv_ln_g=conv_ln_g, conv_ln_b=conv_ln_b, q_norm_g=q_norm_g, kv_norm_g=kv_norm_g,
                     hy_short_w=hy_short_w, hy_short_b=hy_short_b, hy_w1=hy_w1, hy_b1=hy_b1, hy_freq1=hy_freq1,
                     hy_w2=hy_w2, hy_b2=hy_b2, hy_freq2=hy_freq2, hy_w3=hy_w3, hy_b3=hy_b3, hy_bias=hy_bias)
    experts = (w_gate, w_up, w_down)
    cs_l = _rope_table(l, True)
    cs_c = _rope_table(lc, False)
    fwd_l, inv_l = _dft_mats(l)
    fwd_c, inv_c = _dft_mats(lc)
    wr = _router_weights(w_router)
    rows = 16
    cvec = jnp.concatenate([c, c_ctx[None, :], jnp.zeros((rows - b - 1, d), F32)], axis=0)
    xl, xc = x, ctx
    for i in range(depth):
        last = i == depth - 1
        lp = {k: v[i] for k, v in per_layer.items()}
        w = _layer_weights(w_in[i], w_uq[i], w_ukv[i], w_out[i])
        m = _ada_call(cvec, w_ada, b_ada, i)
        mod_l = [m[:b, None, j * d:(j + 1) * d] for j in range(6)]
        mod_c = [jnp.broadcast_to(m[b:b + 1, None, j * d:(j + 1) * d], (b, 1, d)) for j in range(6)]
        spec_l = _filter_spectrum(l, lp, fwd_l)
        spec_c = None if last else _filter_spectrum(lc, lp, fwd_c)
        xc_new, kc, vc = _mixer(xc, mod_c, w, lp, cs_c, spec_c, fwd_c, inv_c, [], not last)
        xl, _, _ = _mixer(xl, mod_l, w, lp, cs_l, spec_l, fwd_l, inv_l, [(kc, vc)], True)
        if last:
            (xl,) = _channel([(xl, mod_l)], lp, wr, router_bias, experts, i, final_norm_g)
        else:
            xl, xc = _channel([(xl, mod_l), (xc_new, mod_c)], lp, wr, router_bias, experts, i)
    return xl
```

```python
import functools
import math

import jax
import jax.numpy as jnp
from jax import lax
from jax.experimental import pallas as pl
from jax.experimental.pallas import tpu as pltpu
from jax.experimental.pallas import tpu_sc as plsc

F32 = jnp.float32
BF16 = jnp.bfloat16

EPS = 1e-6
GRID_W = 64
CONV_W = 512
CONV_KSIZE = 31
MLA_HEADS = 8
MLA_NOPE = 128
MLA_ROPE = 64
MLA_V_DIM = 128
MLA_Q_RANK = 768
MLA_KV_RANK = 512
ROPE_FREQS = MLA_ROPE // 4
ROPE_THETA = 10000.0
MLA_SCALE = (MLA_NOPE + MLA_ROPE) ** -0.5
Q_SCALE = MLA_SCALE * math.log2(math.e)
HYENA_W = 512
HY_ORDER = 2
HY_SHORT = 3
HY_EMB = 33
HY_BANDS = (HY_EMB - 1) // 2
HY_FFN = 64
HY_MIN_DECAY = math.log(1e-2) / 1.5
HY_MAX_DECAY = math.log(1e-2) / 0.3
N_EXPERTS = 16
N_GROUPS = 4
EXPERTS_PER_GROUP = N_EXPERTS // N_GROUPS
OFF_Q = 2 * CONV_W
OFF_KV = OFF_Q + MLA_Q_RANK
OFF_KR = OFF_KV + MLA_KV_RANK
OFF_HY = OFF_KR + MLA_ROPE

LANES = 128
QK_HEAD = 2 * LANES
V7X_VMEM_BYTES = 64 * 1024 * 1024
VMEM_LIMIT = (V7X_VMEM_BYTES * 7) // 8
MOE_TILE = 512
NEG_INF = float("-inf")


def _cp(*sem):
    return pltpu.CompilerParams(dimension_semantics=sem, vmem_limit_bytes=VMEM_LIMIT)


def _dot(a, b):
    return jnp.dot(a, b, preferred_element_type=F32)


def _rms(x, g):
    ms = jnp.mean(x * x, axis=-1, keepdims=True)
    return x * lax.rsqrt(ms + EPS) * g


def _ada_kernel(c_ref, w_ref, b_ref, o_ref):
    c = c_ref[...]
    a = (c * jax.nn.sigmoid(c)).astype(BF16)
    o_ref[...] = _dot(a, w_ref[0].astype(BF16)) + b_ref[0]


def _ada_call(cvec, w, b, layer):
    m, d = cvec.shape
    n = w.shape[2]
    tn = 1024
    return pl.pallas_call(
        _ada_kernel,
        grid=(n // tn,),
        in_specs=[pl.BlockSpec((m, d), lambda j: (0, 0)),
                  pl.BlockSpec((1, d, tn), lambda j: (layer, 0, j)),
                  pl.BlockSpec((1, 1, tn), lambda j: (layer, 0, j))],
        out_specs=pl.BlockSpec((m, tn), lambda j: (0, j)),
        out_shape=jax.ShapeDtypeStruct((m, n), F32),
        compiler_params=_cp("parallel"),
        name="ada",
    )(cvec, w, b.reshape(b.shape[0], 1, n))


def _in_proj_kernel(x_ref, g_ref, sh_ref, sc_ref, w_ref, kg_ref, wk_ref, wv_ref, cs_ref, *o_refs):
    *p_refs, k_ref, v_ref = o_refs
    y = _rms(x_ref[0], g_ref[...])
    h = (y * (1.0 + sc_ref[0]) + sh_ref[0]).astype(BF16)
    off = 0
    for o_ref in p_refs:
        n = o_ref.shape[-1]
        o_ref[0] = _dot(h, w_ref[:, off:off + n]).astype(o_ref.dtype)
        off += n
    p = _dot(h, w_ref[:, off:])
    xn = _rms(p[:, :MLA_KV_RANK], kg_ref[...]).astype(BF16)
    kk = _dot(xn, wk_ref[...])
    v_ref[0] = _dot(xn, wv_ref[...]).astype(BF16)
    rot = _rotate(p[:, MLA_KV_RANK:], cs_ref[...])
    lane = lax.broadcasted_iota(jnp.int32, rot.shape, 1)
    rot = jnp.where(lane < MLA_ROPE, rot, 0.0).astype(BF16)
    for hd in range(MLA_HEADS):
        k_ref[0, :, hd * QK_HEAD:hd * QK_HEAD + LANES] = kk[:, hd * LANES:(hd + 1) * LANES].astype(BF16)
        k_ref[0, :, hd * QK_HEAD + LANES:(hd + 1) * QK_HEAD] = rot


def _in_proj_call(x, g, shift, scale, w, splits, kv_g, w_k, w_v, cs):
    b, l, d = x.shape
    n = w.shape[1]
    tm = min(l, 512)
    r = MLA_KV_RANK
    row = pl.BlockSpec((1, 1, d), lambda i, j: (i, 0, 0))
    const = lambda shape: pl.BlockSpec(shape, lambda i, j: (0, 0))
    widths = list(splits) + [MLA_HEADS * QK_HEAD, MLA_HEADS * MLA_V_DIM]
    return pl.pallas_call(
        _in_proj_kernel,
        grid=(b, l // tm),
        in_specs=[pl.BlockSpec((1, tm, d), lambda i, j: (i, j, 0)),
                  const((1, d)), row, row,
                  pl.BlockSpec((d, n), lambda i, j: (0, 0), pipeline_mode=pl.Buffered(1)),
                  const((1, r)), const(w_k.shape), const(w_v.shape),
                  pl.BlockSpec((tm, LANES), lambda i, j: (j, 0))],
        out_specs=[pl.BlockSpec((1, tm, k), lambda i, j: (i, j, 0)) for k in widths],
        out_shape=[jax.ShapeDtypeStruct((b, l, k), BF16) for k in widths],
        compiler_params=_cp("parallel", "parallel"),
        name="in_proj",
    )(x, g.reshape(1, d), shift, scale, w, kv_g.reshape(1, r), w_k, w_v, cs)


CONV_ROWS = 128
CONV_PAD = 16


def _conformer_kernel(p_ref, w_ref, b_ref, g_ref, be_ref, o_ref, zp_ref):
    l = p_ref.shape[1]
    c = CONV_W
    zeros = jnp.zeros((CONV_PAD, c), F32)
    zp_ref[0:CONV_PAD, :] = zeros
    zp_ref[CONV_PAD + l:CONV_PAD + l + CONV_PAD, :] = zeros

    def glu(r, carry):
        base = pl.multiple_of(r * CONV_ROWS, CONV_ROWS)
        blk = p_ref[0, pl.ds(base, CONV_ROWS), :].astype(F32)
        zp_ref[pl.ds(base + CONV_PAD, CONV_ROWS), :] = blk[:, :c] * jax.nn.sigmoid(blk[:, c:])
        return carry

    lax.fori_loop(0, l // CONV_ROWS, glu, 0)

    win_rows = CONV_ROWS + 2 * CONV_PAD
    first = CONV_PAD - CONV_KSIZE // 2

    def conv(r, carry):
        base = pl.multiple_of(r * CONV_ROWS, CONV_ROWS)
        parts = []
        for cc in range(c // LANES):
            lanes = slice(cc * LANES, (cc + 1) * LANES)
            win = zp_ref[pl.ds(base, win_rows), lanes]
            acc = jnp.zeros((CONV_ROWS, LANES), F32)
            for res in range(8):
                taps = [k for k in range(CONV_KSIZE) if (k + first) % 8 == res]
                if not taps:
                    continue
                shifted = pltpu.roll(win, win_rows - res, axis=0) if res else win
                for k in taps:
                    off = k + first - res
                    acc = acc + shifted[off:off + CONV_ROWS, :] * w_ref[k:k + 1, lanes]
            parts.append(acc)
        z = jnp.concatenate(parts, axis=1) + b_ref[...]
        mu = jnp.mean(z, axis=-1, keepdims=True)
        zc = z - mu
        var = jnp.mean(zc * zc, axis=-1, keepdims=True)
        y = zc * lax.rsqrt(var + EPS) * g_ref[...] + be_ref[...]
        o_ref[0, pl.ds(base, CONV_ROWS), :] = (y * jax.nn.sigmoid(y)).astype(o_ref.dtype)
        return carry

    lax.fori_loop(0, l // CONV_ROWS, conv, 0)


def _conformer_call(p, dw_w, dw_b, ln_g, ln_b):
    b, l, c2 = p.shape
    c = c2 // 2
    vec = pl.BlockSpec((1, c), lambda i: (0, 0))
    return pl.pallas_call(
        _conformer_kernel,
        grid=(b,),
        in_specs=[pl.BlockSpec((1, l, c2), lambda i: (i, 0, 0)),
                  pl.BlockSpec((CONV_KSIZE, c), lambda i: (0, 0)), vec, vec, vec],
        out_specs=pl.BlockSpec((1, l, c), lambda i: (i, 0, 0)),
        out_shape=jax.ShapeDtypeStruct((b, l, c), BF16),
        scratch_shapes=[pltpu.VMEM((l + 2 * CONV_PAD, c), F32)],
        compiler_params=_cp("parallel"),
        name="conformer",
    )(p, dw_w, dw_b.reshape(1, c), ln_g.reshape(1, c), ln_b.reshape(1, c))


def _rotate(group, cs):
    t = group * cs
    return t + pltpu.roll(t, MLA_ROPE, axis=1)


def _attn_kernel(p_ref, g_ref, w_ref, cs_ref, *refs, n_seg):
    o_ref, q_ref = refs[2 * n_seg], refs[2 * n_seg + 1]
    xn = _rms(p_ref[0].astype(F32), g_ref[...]).astype(BF16)
    cs = cs_ref[...] * Q_SCALE
    for h in range(MLA_HEADS):
        r = _dot(xn, w_ref[:, h * QK_HEAD:(h + 1) * QK_HEAD])
        q_ref[:, h * QK_HEAD:h * QK_HEAD + LANES] = (r[:, :LANES] * Q_SCALE).astype(BF16)
        q_ref[:, h * QK_HEAD + LANES:(h + 1) * QK_HEAD] = _rotate(r[:, LANES:], cs).astype(BF16)
    nt = (((1,), (1,)), ((), ()))
    for h in range(MLA_HEADS):
        qk = slice(h * QK_HEAD, (h + 1) * QK_HEAD)
        vd = slice(h * MLA_V_DIM, (h + 1) * MLA_V_DIM)
        q = q_ref[:, qk]
        scores = [lax.dot_general(q, refs[2 * s][0, :, qk], nt, preferred_element_type=F32)
                  for s in range(n_seg)]
        m = functools.reduce(jnp.maximum, [jnp.max(s, axis=-1, keepdims=True) for s in scores])
        den = 0.0
        acc = 0.0
        for s in range(n_seg):
            p = jnp.exp2(scores[s] - m)
            den = den + jnp.sum(p, axis=-1, keepdims=True)
            acc = acc + _dot(p.astype(BF16), refs[2 * s + 1][0, :, vd])
        o_ref[0, :, vd] = (acc * (1.0 / den)).astype(o_ref.dtype)


def _attn_call(p_q, q_g, w_q, cs, segs):
    b, l, r = p_q.shape
    tq = min(l, 256)
    const = lambda shape: pl.BlockSpec(shape, lambda i, j: (0, 0))
    in_specs = [pl.BlockSpec((1, tq, r), lambda i, j: (i, j, 0)), const((1, r)), const(w_q.shape),
                pl.BlockSpec((tq, LANES), lambda i, j: (j, 0))]
    args = [p_q, q_g.reshape(1, r), w_q, cs]
    for k, v in segs:
        in_specs.append(pl.BlockSpec((1,) + k.shape[1:], lambda i, j: (i, 0, 0)))
        in_specs.append(pl.BlockSpec((1,) + v.shape[1:], lambda i, j: (i, 0, 0)))
        args += [k, v]
    return pl.pallas_call(
        functools.partial(_attn_kernel, n_seg=len(segs)),
        grid=(b, l // tq),
        in_specs=in_specs,
        out_specs=pl.BlockSpec((1, tq, MLA_HEADS * MLA_V_DIM), lambda i, j: (i, j, 0)),
        out_shape=jax.ShapeDtypeStruct((b, l, MLA_HEADS * MLA_V_DIM), BF16),
        scratch_shapes=[pltpu.VMEM((tq, MLA_HEADS * QK_HEAD), BF16)],
        compiler_params=_cp("parallel", "parallel"),
        name="attention",
    )(*args)


HY_PAD = 8
HY_ROWS = 128


def _hy_short_kernel(p_ref, w_ref, b_ref, o_ref, up_ref):
    l = p_ref.shape[1]
    c = p_ref.shape[2]
    zeros = jnp.zeros((HY_PAD, c), F32)
    up_ref[0:HY_PAD, :] = zeros
    up_ref[HY_PAD + l:HY_PAD + l + HY_PAD, :] = zeros

    def copy(r, carry):
        base = pl.multiple_of(r * HY_ROWS, HY_ROWS)
        up_ref[pl.ds(base + HY_PAD, HY_ROWS), :] = p_ref[0, pl.ds(base, HY_ROWS), :].astype(F32)
        return carry

    lax.fori_loop(0, l // HY_ROWS, copy, 0)
    first = HY_PAD - HY_SHORT // 2

    def conv(r, carry):
        base = pl.multiple_of(r * HY_ROWS, HY_ROWS)
        for cc in range(c // LANES):
            lanes = slice(cc * LANES, (cc + 1) * LANES)
            win = up_ref[pl.ds(base, HY_ROWS + 2 * HY_PAD), lanes]
            acc = jnp.zeros((HY_ROWS, LANES), F32) + b_ref[:, lanes]
            for k in range(HY_SHORT):
                acc = acc + win[first + k:first + k + HY_ROWS, :] * w_ref[k:k + 1, lanes]
            o_ref[0, 0, pl.ds(base, HY_ROWS), lanes] = acc.astype(o_ref.dtype)
        return carry

    lax.fori_loop(0, l // HY_ROWS, conv, 0)


def _hy_short_call(p, w, bias):
    b, l, c3 = p.shape
    c = HYENA_W
    return pl.pallas_call(
        _hy_short_kernel,
        grid=(b, c3 // c),
        in_specs=[pl.BlockSpec((1, l, c), lambda i, g: (i, 0, g)),
                  pl.BlockSpec((HY_SHORT, c), lambda i, g: (0, g)),
                  pl.BlockSpec((1, c), lambda i, g: (0, g))],
        out_specs=pl.BlockSpec((1, 1, l, c), lambda i, g: (g, i, 0, 0)),
        out_shape=jax.ShapeDtypeStruct((c3 // c, b, l, c), BF16),
        scratch_shapes=[pltpu.VMEM((l + 2 * HY_PAD, c), F32)],
        compiler_params=_cp("parallel", "parallel"),
        name="hyena_short",
    )(p, w, bias.reshape(1, c3))


def _hy_filter_kernel(z_ref, w1_ref, b1_ref, f1_ref, w2_ref, b2_ref, f2_ref, w3_ref, b3_ref, win_ref, o_ref):
    hp = lax.Precision.HIGHEST

    def mm(a, b):
        return jnp.dot(a, b, preferred_element_type=F32, precision=hp)

    h = jnp.sin(f1_ref[...] * (mm(z_ref[...], w1_ref[...]) + b1_ref[...]))
    h = jnp.sin(f2_ref[...] * (mm(h, w2_ref[...]) + b2_ref[...]))
    h = mm(h, w3_ref[...]) + b3_ref[...]
    c = HYENA_W
    win = win_ref[...]
    tl = h.shape[0]
    row = lax.broadcasted_iota(jnp.int32, (tl, c), 0) + pl.program_id(0) * tl
    for o in range(HY_ORDER):
        fwd = h[:, (2 * o) * c:(2 * o + 1) * c] * win
        bwd = jnp.where(row == 0, 0.0, h[:, (2 * o + 1) * c:(2 * o + 2) * c] * win)
        o_ref[:, o * c:(o + 1) * c] = (fwd + bwd).astype(o_ref.dtype)
        o_ref[:, (HY_ORDER + o) * c:(HY_ORDER + o + 1) * c] = (fwd - bwd).astype(o_ref.dtype)


def _pad2(a, rows, cols):
    return jnp.pad(a, ((0, rows - a.shape[0]), (0, cols - a.shape[1])))


def _hy_filter_call(l, w1, b1, f1, w2, b2, f2, w3, b3):
    t = jnp.arange(l, dtype=F32)
    t_norm = t / max(l - 1, 1)
    ang = (2.0 * math.pi / l) * t[:, None] * jnp.linspace(1e-4, HY_BANDS - 1, HY_BANDS, dtype=F32)[None, :]
    z = jnp.concatenate([t_norm[:, None], jnp.cos(ang), -jnp.sin(ang)], axis=-1)
    deltas = jnp.abs(jnp.linspace(HY_MIN_DECAY, HY_MAX_DECAY, HYENA_W, dtype=F32))
    window = jnp.exp(-t_norm[:, None] * deltas[None, :])
    n3 = w3.shape[1]
    tl = min(l, 256)
    full = lambda shape: pl.BlockSpec(shape, lambda i: (0, 0))
    return pl.pallas_call(
        _hy_filter_kernel,
        grid=(l // tl,),
        in_specs=[pl.BlockSpec((tl, LANES), lambda i: (i, 0)),
                  full((LANES, LANES)), full((1, LANES)), full((1, LANES)),
                  full((LANES, LANES)), full((1, LANES)), full((1, LANES)),
                  full((LANES, n3)), full((1, n3)),
                  pl.BlockSpec((tl, HYENA_W), lambda i: (i, 0))],
        out_specs=pl.BlockSpec((tl, n3), lambda i: (i, 0)),
        out_shape=jax.ShapeDtypeStruct((l, n3), BF16),
        compiler_params=_cp("parallel"),
        name="hyena_filter",
    )(_pad2(z, l, LANES), _pad2(w1, LANES, LANES), _pad2(b1[None], 1, LANES), _pad2(f1[None], 1, LANES),
      _pad2(w2, LANES, LANES), _pad2(b2[None], 1, LANES), _pad2(f2[None], 1, LANES),
      _pad2(w3, LANES, n3), b3[None], window)


def _dft_tables_kernel(ac_ref, as_ref, bc_ref, bs_ref, fwd_ref, inv_ref):
    tm, l = fwd_ref.shape[1], fwd_ref.shape[2]
    n = 2 * l
    row = lax.broadcasted_iota(jnp.int32, (tm, LANES), 0) + pl.program_id(0) * tm
    lane = lax.broadcasted_iota(jnp.int32, (tm, LANES), 1)
    row_sign = jnp.where((row & 1) == 0, 1.0, -1.0)
    lane_sign = jnp.where((lane & 1) == 0, 1.0, -1.0)
    bc = bc_ref[...]
    bs = bs_ref[...]
    for a in range(l // LANES):
        cols = slice(a * LANES, (a + 1) * LANES)
        ca = ac_ref[:, a:a + 1]
        sa = as_ref[:, a:a + 1]
        c = ca * bc - sa * bs
        s = sa * bc + ca * bs
        fwd_ref[0, :, cols] = c.astype(BF16)
        fwd_ref[1, :, cols] = jnp.where(row == 0, lane_sign, -s).astype(BF16)
        inv_c = c * (2.0 / n)
        inv_s = -s * (2.0 / n)
        if a == 0:
            inv_c = jnp.where(lane == 0, 1.0 / n, inv_c)
            inv_s = jnp.where(lane == 0, row_sign / n, inv_s)
        inv_ref[:, cols] = inv_c.astype(BF16)
        inv_ref[:, l + a * LANES:l + (a + 1) * LANES] = inv_s.astype(BF16)


def _dft_mats(l):
    n = 2 * l
    f = jnp.arange(l, dtype=jnp.int32)[:, None]

    def table(t):
        ang = ((f * t[None, :]) % n).astype(F32) * (2.0 * math.pi / n)
        return jnp.cos(ang), jnp.sin(ang)

    coarse = table(jnp.arange(l // LANES, dtype=jnp.int32) * LANES)
    fine = table(jnp.arange(LANES, dtype=jnp.int32))
    tm = min(l, 256)
    small = pl.BlockSpec((tm, l // LANES), lambda i: (i, 0))
    wide = pl.BlockSpec((tm, LANES), lambda i: (i, 0))
    return pl.pallas_call(
        _dft_tables_kernel,
        grid=(l // tm,),
        in_specs=[small, small, wide, wide],
        out_specs=[pl.BlockSpec((2, tm, l), lambda i: (0, i, 0)), pl.BlockSpec((tm, n), lambda i: (i, 0))],
        out_shape=[jax.ShapeDtypeStruct((2, l, l), BF16), jax.ShapeDtypeStruct((l, n), BF16)],
        compiler_params=_cp("parallel"),
        name="dft_tables",
    )(*coarse, *fine)


def _spec_kernel(f_ref, a_ref, o_ref):
    half = a_ref.shape[1] // 2
    hi = _dot(f_ref[1], a_ref[:, half:])
    o_ref[0] = _dot(f_ref[0], a_ref[:, :half])
    o_ref[1] = hi

    @pl.when(pl.program_id(0) == 0)
    def _():
        nyq = _dot(f_ref[1, 0:16, :], a_ref[:, :half])
        row = lax.broadcasted_iota(jnp.int32, nyq.shape, 0)
        o_ref[1, 0:16, :] = jnp.where(row == 0, nyq, hi[0:16])


def _spec_call(fwd, a):
    _, lf, l = fwd.shape
    n = a.shape[1] // 2
    tm = min(lf, 512)
    return pl.pallas_call(
        _spec_kernel,
        grid=(lf // tm,),
        in_specs=[pl.BlockSpec((2, tm, l), lambda i: (0, i, 0)),
                  pl.BlockSpec(a.shape, lambda i: (0, 0))],
        out_specs=pl.BlockSpec((2, tm, n), lambda i: (0, i, 0)),
        out_shape=jax.ShapeDtypeStruct((2, lf, n), F32),
        compiler_params=_cp("arbitrary"),
        name="hyena_spectrum",
    )(fwd, a)


def _dft_fwd_kernel(f_ref, u_ref, h_ref, y_ref):
    u = u_ref[0, 0].astype(BF16)
    xr = _dot(f_ref[0], u)
    xi = _dot(f_ref[1], u)
    hr = h_ref[0]
    hi = h_ref[1]
    yr = xr * hr - xi * hi
    yi = xr * hi + xi * hr
    y_ref[0, 0] = yr.astype(BF16)
    y_ref[0, 1] = yi.astype(BF16)

    @pl.when(pl.program_id(0) == 0)
    def _():
        row = lax.broadcasted_iota(jnp.int32, (16, u.shape[1]), 0)
        y_ref[0, 0, 0:16, :] = jnp.where(row == 0, xr[0:16] * hr[0:16], yr[0:16]).astype(BF16)
        y_ref[0, 1, 0:16, :] = jnp.where(row == 0, xi[0:16] * hi[0:16], yi[0:16]).astype(BF16)


def _dft_fwd_call(fwd, u, group, spec, order):
    _, b, l, c = u.shape
    lf = fwd.shape[1]
    tm = min(lf, 512)
    return pl.pallas_call(
        _dft_fwd_kernel,
        grid=(lf // tm, b),
        in_specs=[pl.BlockSpec((2, tm, l), lambda i, j: (0, i, 0)),
                  pl.BlockSpec((1, 1, l, c), lambda i, j: (group, j, 0, 0)),
                  pl.BlockSpec((2, tm, c), lambda i, j: (0, i, order))],
        out_specs=pl.BlockSpec((1, 2, tm, c), lambda i, j: (j, 0, i, 0)),
        out_shape=jax.ShapeDtypeStruct((b, 2, lf, c), BF16),
        compiler_params=_cp("arbitrary", "arbitrary"),
        name="hyena_dft_fwd",
    )(fwd, u, spec)


def _dft_inv_kernel(g_ref, y_ref, u_ref, b_ref, x_ref, o_ref):
    r = _dot(g_ref[...], y_ref[0]) + u_ref[0, 0].astype(F32) * b_ref[...]
    o_ref[0, 0] = (x_ref[0, 0] * r).astype(o_ref.dtype)


def _dft_inv_call(inv, y, u, u_group, bias, gate, gate_group):
    _, b, l, c = u.shape
    n = inv.shape[1]
    tm = min(l, 512)
    y = y.reshape(b, n, c)
    row = lambda g: pl.BlockSpec((1, 1, tm, c), lambda i, j: (g, j, i, 0))
    return pl.pallas_call(
        _dft_inv_kernel,
        grid=(l // tm, b),
        in_specs=[pl.BlockSpec((tm, n), lambda i, j: (i, 0)),
                  pl.BlockSpec((1, n, c), lambda i, j: (j, 0, 0)),
                  row(u_group), pl.BlockSpec((1, c), lambda i, j: (0, 0)), row(gate_group)],
        out_specs=row(0),
        out_shape=jax.ShapeDtypeStruct((1, b, l, c), BF16),
        compiler_params=_cp("parallel", "parallel"),
        name="hyena_dft_inv",
    )(inv, y, u, bias.reshape(1, c), gate)


def _hyena_call(p_hy, sh_w, sh_b, spec, hy_bias, fwd, inv):
    u3 = _hy_short_call(p_hy, sh_w, sh_b)
    z = _dft_inv_call(inv, _dft_fwd_call(fwd, u3, 0, spec, 0), u3, 0, hy_bias[0], u3, 1)
    return _dft_inv_call(inv, _dft_fwd_call(fwd, z, 0, spec, 1), z, 0, hy_bias[1], u3, 2)[0]


def _out_proj_kernel(c_ref, a_ref, h_ref, w1_ref, w2_ref, w3_ref, x_ref, g_ref, o_ref):
    acc = _dot(c_ref[...], w1_ref[...]) + _dot(a_ref[...], w2_ref[...]) + _dot(h_ref[...], w3_ref[...])
    o_ref[...] = x_ref[...] + g_ref[0] * acc


def _out_proj_call(conv_o, att_o, hy_o, w_out, x, gate, l):
    m, d = x.shape
    tm = min(l, 1024)
    tn = 512
    k1, k2 = conv_o.shape[1], att_o.shape[1]
    w1, w2, w3 = w_out[:k1], w_out[k1:k1 + k2], w_out[k1 + k2:]
    act = lambda k: pl.BlockSpec((tm, k), lambda i, j: (i, 0))
    wsp = lambda k: pl.BlockSpec((k, tn), lambda i, j: (0, j))
    tile = pl.BlockSpec((tm, tn), lambda i, j: (i, j))
    return pl.pallas_call(
        _out_proj_kernel,
        grid=(m // tm, d // tn),
        in_specs=[act(k1), act(k2), act(w3.shape[0]), wsp(k1), wsp(k2), wsp(w3.shape[0]), tile,
                  pl.BlockSpec((1, 1, tn), lambda i, j: ((i * tm) // l, 0, j))],
        out_specs=tile,
        out_shape=jax.ShapeDtypeStruct((m, d), F32),
        compiler_params=_cp("parallel", "parallel"),
        name="out_proj",
    )(conv_o, att_o, hy_o, w1, w2, w3, x, gate)


def _norm_router_kernel(x_ref, g_ref, sh_ref, sc_ref, wr_ref, rb_ref, h_ref, idx_ref, gate_ref):
    y = _rms(x_ref[0], g_ref[...])
    hm = y * (1.0 + sc_ref[0]) + sh_ref[0]
    hi = hm.astype(BF16)
    lo = (hm - hi.astype(F32)).astype(BF16)
    h_ref[0] = _pack_rows(hm)
    nt = (((1,), (1,)), ((), ()))
    wr = wr_ref[...]
    r = (lax.dot_general(wr, hi, nt, preferred_element_type=F32)
         + lax.dot_general(wr, lo, nt, preferred_element_type=F32))
    logits = r[:N_EXPERTS] + r[N_EXPERTS:]
    score = jax.nn.sigmoid(logits)
    sel = score + rb_ref[...]
    rows = [sel[e:e + 1] for e in range(N_EXPERTS)]
    gscore = []
    for g in range(N_GROUPS):
        a, b, c, d = rows[4 * g:4 * g + 4]
        h1, l1, h2, l2 = jnp.maximum(a, b), jnp.minimum(a, b), jnp.maximum(c, d), jnp.minimum(c, d)
        gscore.append(jnp.maximum(h1, h2) + jnp.maximum(jnp.minimum(h1, h2), jnp.maximum(l1, l2)))
    best = gscore[0]
    gbest = jnp.zeros(best.shape, jnp.int32)
    for g in range(1, N_GROUPS):
        upd = gscore[g] > best
        gbest = jnp.where(upd, g, gbest)
        best = jnp.where(upd, gscore[g], best)
    masked = [jnp.where(gbest == e // EXPERTS_PER_GROUP, rows[e], NEG_INF) for e in range(N_EXPERTS)]
    picks = []
    for _ in range(2):
        val = jnp.full(best.shape, NEG_INF, F32)
        idx = jnp.zeros(best.shape, jnp.int32)
        for e in range(N_EXPERTS):
            cand = masked[e]
            if picks:
                cand = jnp.where(picks[0] == e, NEG_INF, cand)
            upd = cand > val
            idx = jnp.where(upd, e, idx)
            val = jnp.where(upd, cand, val)
        picks.append(idx)
    gates = [functools.reduce(jnp.add, [jnp.where(p == e, score[e:e + 1], 0.0) for e in range(N_EXPERTS)])
             for p in picks]
    den = gates[0] + gates[1]
    idx_ref[...] = jnp.concatenate(picks, axis=0)
    gate_ref[...] = jnp.concatenate([gates[0] / den, gates[1] / den], axis=0)


def _norm_router_call(x, g, shift, scale, wr, rbias):
    b, l, d = x.shape
    tl = min(l, 512)
    nt = l // tl
    row = pl.BlockSpec((1, 1, d), lambda i, j: (i, 0, 0))
    tok = pl.BlockSpec((2, tl), lambda i, j: (0, i * nt + j))
    return pl.pallas_call(
        _norm_router_kernel,
        grid=(b, nt),
        in_specs=[pl.BlockSpec((1, tl, d), lambda i, j: (i, j, 0)),
                  pl.BlockSpec((1, d), lambda i, j: (0, 0)), row, row,
                  pl.BlockSpec(wr.shape, lambda i, j: (0, 0)),
                  pl.BlockSpec((N_EXPERTS, 1), lambda i, j: (0, 0))],
        out_specs=[pl.BlockSpec((1, tl, d // 2), lambda i, j: (i, j, 0)), tok, tok],
        out_shape=[jax.ShapeDtypeStruct((b, l, d // 2), jnp.int32),
                   jax.ShapeDtypeStruct((2, b * l), jnp.int32),
                   jax.ShapeDtypeStruct((2, b * l), F32)],
        compiler_params=_cp("parallel", "parallel"),
        name="norm_router",
    )(x, g.reshape(1, d), shift, scale, wr, rbias.reshape(N_EXPERTS, 1))


def _expert_changed(te_ref, i):
    return (i == 0) | (te_ref[i] != te_ref[jnp.maximum(i - 1, 0)])


def _pack_rows(x):
    bits = pltpu.bitcast(x.astype(BF16).astype(F32), jnp.uint32)
    half = bits.shape[1] // 2
    return pltpu.bitcast((bits[:, :half] >> 16) | (bits[:, half:] & jnp.uint32(0xFFFF0000)), jnp.int32)


def _unpack_rows(words):
    w = pltpu.bitcast(words, jnp.uint32)
    return pltpu.bitcast(w << 16, F32), pltpu.bitcast(w & jnp.uint32(0xFFFF0000), F32)


def _moe_up_kernel(te_ref, nu_ref, x_ref, wg_ref, wu_ref, h_ref, wgb_ref, wub_ref):
    i = pl.program_id(1)

    @pl.when(_expert_changed(te_ref, i))
    def _():
        wgb_ref[...] = wg_ref[0, 0].astype(BF16)
        wub_ref[...] = wu_ref[0, 0].astype(BF16)

    @pl.when(i < nu_ref[0])
    def _():
        xa, xb = (v.astype(BF16) for v in _unpack_rows(x_ref[...]))
        half = xa.shape[1]
        a = _dot(xa, wgb_ref[:half, :]) + _dot(xb, wgb_ref[half:, :])
        u = _dot(xa, wub_ref[:half, :]) + _dot(xb, wub_ref[half:, :])
        h_ref[...] = (a * jax.nn.sigmoid(a) * u).astype(BF16)

    @pl.when(i >= nu_ref[0])
    def _():
        h_ref[...] = jnp.zeros(h_ref.shape, BF16)


def _moe_up_call(xs, w_gate, w_up, layer, te, nu, tile):
    p, dw = xs.shape
    _, _, d, f = w_gate.shape
    tf = 512
    wsp = pl.BlockSpec((1, 1, d, tf), lambda j, i, te, nu: (layer, te[i], 0, j))
    return pl.pallas_call(
        _moe_up_kernel,
        grid_spec=pltpu.PrefetchScalarGridSpec(
            num_scalar_prefetch=2,
            grid=(f // tf, p // tile),
            in_specs=[pl.BlockSpec((tile, dw), lambda j, i, te, nu: (i, 0)), wsp, wsp],
            out_specs=pl.BlockSpec((tile, tf), lambda j, i, te, nu: (i, j)),
            scratch_shapes=[pltpu.VMEM((d, tf), BF16), pltpu.VMEM((d, tf), BF16)]),
        out_shape=jax.ShapeDtypeStruct((p, f), BF16),
        compiler_params=_cp("arbitrary", "arbitrary"),
        name="moe_up",
    )(te, nu, xs, w_gate, w_up)


def _moe_down_kernel(te_ref, nu_ref, h_ref, wd_ref, y_ref, wdb_ref):
    i = pl.program_id(0)

    @pl.when(_expert_changed(te_ref, i))
    def _():
        wdb_ref[...] = wd_ref[0, 0].astype(BF16)

    @pl.when(i < nu_ref[0])
    def _():
        y_ref[...] = _pack_rows(_dot(h_ref[...], wdb_ref[...]))

    @pl.when(i >= nu_ref[0])
    def _():
        y_ref[...] = jnp.zeros(y_ref.shape, jnp.int32)


def _moe_down_call(h, w_down, layer, te, nu, tile):
    p, f = h.shape
    d = w_down.shape[3]
    return pl.pallas_call(
        _moe_down_kernel,
        grid_spec=pltpu.PrefetchScalarGridSpec(
            num_scalar_prefetch=2,
            grid=(p // tile,),
            in_specs=[pl.BlockSpec((tile, f), lambda i, te, nu: (i, 0)),
                      pl.BlockSpec((1, 1, f, d), lambda i, te, nu: (layer, te[i], 0, 0))],
            out_specs=pl.BlockSpec((tile, d // 2), lambda i, te, nu: (i, 0)),
            scratch_shapes=[pltpu.VMEM((f, d), BF16)]),
        out_shape=jax.ShapeDtypeStruct((p, d // 2), jnp.int32),
        compiler_params=_cp("arbitrary"),
        name="moe_down",
    )(te, nu, h, w_down)


PLAN_BLOCK = 512


def _plan_kernel(idx_ref, pos_ref, te_ref, nu_ref, rank_ref, *, tile):
    t = idx_ref.shape[1]
    nb = t // PLAN_BLOCK
    experts = lax.broadcasted_iota(jnp.int32, (N_EXPERTS, PLAN_BLOCK), 0)
    r = lax.broadcasted_iota(jnp.int32, (PLAN_BLOCK, PLAN_BLOCK), 0)
    c = lax.broadcasted_iota(jnp.int32, (PLAN_BLOCK, PLAN_BLOCK), 1)
    before = jnp.where(r < c, 1.0, 0.0).astype(BF16)

    def rank_pass(k):
        def body(j, carry):
            cols = pl.ds(pl.multiple_of(j * PLAN_BLOCK, PLAN_BLOCK), PLAN_BLOCK)
            onehot = jnp.where(experts == idx_ref[k:k + 1, cols], 1.0, 0.0)
            earlier = _dot(onehot.astype(BF16), before) + carry
            rank_ref[k:k + 1, cols] = jnp.sum(onehot * earlier, axis=0, keepdims=True)
            return carry + jnp.sum(onehot, axis=1, keepdims=True)
        return body

    counts = jnp.zeros((N_EXPERTS, 1), F32)
    for k in range(2):
        counts = lax.fori_loop(0, nb, rank_pass(k), counts)
    padded = jnp.ceil(counts * (1.0 / tile)) * tile
    starts, ends = [], []
    acc = jnp.zeros((1, 1), F32)
    for e in range(N_EXPERTS):
        starts.append(acc)
        acc = acc + padded[e:e + 1]
        ends.append(acc)

    def place_pass(k):
        def body(j, carry):
            cols = pl.ds(pl.multiple_of(j * PLAN_BLOCK, PLAN_BLOCK), PLAN_BLOCK)
            row = idx_ref[k:k + 1, cols]
            start = functools.reduce(jnp.add, [jnp.where(row == e, starts[e], 0.0) for e in range(N_EXPERTS)])
            pos_ref[k:k + 1, cols] = (rank_ref[k:k + 1, cols] + start).astype(jnp.int32)
            return carry
        return body

    for k in range(2):
        lax.fori_loop(0, nb, place_pass(k), 0)
    n_used = acc * (1.0 / tile)
    tiles = lax.broadcasted_iota(jnp.int32, te_ref.shape, 1).astype(F32)
    first_row = jnp.minimum(tiles, n_used - 1.0) * tile
    te = functools.reduce(jnp.add, [jnp.where(ends[e] <= first_row, 1.0, 0.0) for e in range(N_EXPERTS)])
    te_ref[...] = te.astype(jnp.int32)
    nu_ref[...] = jnp.broadcast_to(n_used, nu_ref.shape).astype(jnp.int32)


def _plan_call(idx, tile, n_tiles):
    t = idx.shape[1]
    whole = lambda shape: pl.BlockSpec(shape, lambda: (0,) * len(shape))
    pos, te, nu = pl.pallas_call(
        functools.partial(_plan_kernel, tile=tile),
        in_specs=[whole(idx.shape)],
        out_specs=[whole(idx.shape), whole((1, LANES)), whole((1, LANES))],
        out_shape=[jax.ShapeDtypeStruct(idx.shape, jnp.int32),
                   jax.ShapeDtypeStruct((1, LANES), jnp.int32),
                   jax.ShapeDtypeStruct((1, LANES), jnp.int32)],
        scratch_shapes=[pltpu.VMEM(idx.shape, F32)],
        name="route_plan",
    )(idx)
    return pos.reshape(2 * t), te[0, :n_tiles], nu[0, :1]


V7X_SC_CORES = 2
V7X_SC_SUBCORES = 16
SC_WORKERS = V7X_SC_CORES * V7X_SC_SUBCORES
SC_ROW_BYTES = 256 * 1024


def _sc_worker():
    return lax.axis_index("s") * V7X_SC_CORES + lax.axis_index("c")


def _sc_dispatch_call(streams, pos, n_rows):
    w = streams[0].shape[1]
    sizes = [s.shape[0] for s in streams]
    t = sum(sizes)
    per_w = t // SC_WORKERS
    ch = min(per_w, SC_ROW_BYTES // (4 * w))
    assert per_w % ch == 0 and all(size % ch == 0 for size in sizes)
    mesh = plsc.VectorSubcoreMesh(core_axis_name="c", subcore_axis_name="s")

    @functools.partial(pl.kernel, mesh=mesh, out_type=jax.ShapeDtypeStruct((n_rows, w), jnp.int32),
                       scratch_types=[pltpu.VMEM((ch,), jnp.int32), pltpu.VMEM((ch, w), jnp.int32)],
                       name="sc_dispatch")
    def run(*refs):
        srcs = refs[:len(sizes)]
        pos_hbm, xs_hbm, idx_v, rows_v = refs[len(sizes):]
        first = _sc_worker() * per_w

        @pl.loop(0, per_w // ch)
        def _(j):
            base = first + j * ch
            start = 0
            for src, size in zip(srcs, sizes):
                def load(src=src, start=start):
                    pltpu.sync_copy(src.at[pl.ds(base - start, ch)], rows_v)
                if len(sizes) == 1:
                    load()
                else:
                    pl.when((base >= start) & (base < start + size))(load)
                start += size
            for k in range(2):
                pltpu.sync_copy(pos_hbm.at[pl.ds(k * t + base, ch)], idx_v)
                pltpu.sync_copy(rows_v, xs_hbm.at[idx_v])

    return run(*streams, pos)


def _sc_combine_call(ys, pos):
    _, d = ys.shape
    n = pos.shape[0]
    per_w = n // SC_WORKERS
    ch = min(per_w, SC_ROW_BYTES // (4 * d))
    mesh = plsc.VectorSubcoreMesh(core_axis_name="c", subcore_axis_name="s")

    @functools.partial(pl.kernel, mesh=mesh, out_type=jax.ShapeDtypeStruct((n, d), jnp.int32),
                       scratch_types=[pltpu.VMEM((ch,), jnp.int32), pltpu.VMEM((ch, d), jnp.int32)],
                       name="sc_combine")
    def run(ys_hbm, pos_hbm, out_hbm, idx_v, rows_v):
        first = _sc_worker() * per_w

        @pl.loop(0, per_w // ch)
        def _(j):
            base = first + j * ch
            pltpu.sync_copy(pos_hbm.at[pl.ds(base, ch)], idx_v)
            pltpu.sync_copy(ys_hbm.at[idx_v], rows_v)
            pltpu.sync_copy(rows_v, out_hbm.at[pl.ds(base, ch)])

    return run(ys, pos)


def _moe_call(streams, idx, w_gate, w_up, w_down, layer):
    t = idx.shape[1]
    tile = min(MOE_TILE, max(t // N_EXPERTS, 128))
    n_tiles = (2 * t) // tile + N_EXPERTS
    assert n_tiles <= LANES
    pos, te, nu = _plan_call(idx, tile, n_tiles)
    xs = _sc_dispatch_call(streams, pos, n_tiles * tile)
    hid = _moe_up_call(xs, w_gate, w_up, layer, te, nu, tile)
    ys = _moe_down_call(hid, w_down, layer, te, nu, tile)
    return _sc_combine_call(ys, pos).reshape(2, t, ys.shape[1])


def _moe_resid_kernel(x_ref, y0_ref, y1_ref, gt_ref, g_ref, *rest, final):
    gt = gt_ref[0]
    halves0 = _unpack_rows(y0_ref[0])
    halves1 = _unpack_rows(y1_ref[0])
    y = jnp.concatenate([gt[:, 0:1] * a + gt[:, 1:2] * b for a, b in zip(halves0, halves1)], axis=1)
    x = x_ref[0] + g_ref[0] * y
    if final:
        fg_ref, o_ref = rest
        x = _rms(x, fg_ref[...])
    else:
        (o_ref,) = rest
    o_ref[0] = x


def _moe_resid_call(x, y, first_row, gates, gate, final_g=None):
    b, l, d = x.shape
    tl = min(l, 512)
    nt = l // tl
    tile = pl.BlockSpec((1, tl, d), lambda i, j: (i, j, 0))
    choice = lambda k: pl.BlockSpec((1, tl, d // 2), lambda i, j: (k, first_row // tl + i * nt + j, 0))
    in_specs = [tile, choice(0), choice(1), pl.BlockSpec((1, tl, 2), lambda i, j: (i, j, 0)),
                pl.BlockSpec((1, 1, d), lambda i, j: (i, 0, 0))]
    args = [x, y, y, gates.T.reshape(b, l, 2), gate]
    if final_g is not None:
        in_specs.append(pl.BlockSpec((1, d), lambda i, j: (0, 0)))
        args.append(final_g.reshape(1, d))
    return pl.pallas_call(
        functools.partial(_moe_resid_kernel, final=final_g is not None),
        grid=(b, l // tl),
        in_specs=in_specs,
        out_specs=tile,
        out_shape=jax.ShapeDtypeStruct((b, l, d), F32),
        compiler_params=_cp("parallel", "parallel"),
        name="moe_residual",
    )(*args)


def _rope_perm(w):
    r = w.reshape(w.shape[:-1] + (2, 2, ROPE_FREQS))
    return jnp.stack([-r[..., 1, :], r[..., 0, :]], axis=-2).reshape(w.shape)


def _rope_table(l, rotary):
    if not rotary:
        return jnp.concatenate([jnp.ones((l, MLA_ROPE), F32), jnp.zeros((l, MLA_ROPE), F32)], axis=1)
    rows = l // GRID_W
    r = jnp.repeat(jnp.arange(rows), GRID_W)
    col = jnp.tile(jnp.arange(GRID_W), rows)
    pos = jnp.stack([r, col], axis=-1).astype(F32)
    inv = ROPE_THETA ** (-jnp.arange(ROPE_FREQS, dtype=F32) / ROPE_FREQS)
    ang = pos[:, :, None] * inv
    lay = lambda a: jnp.concatenate([a[:, 0], a[:, 0], a[:, 1], a[:, 1]], axis=1)
    return jnp.concatenate([lay(jnp.cos(ang)), lay(jnp.sin(ang))], axis=1)


def _layer_weights(w_in, w_uq, w_ukv, w_out):
    d = w_in.shape[0]
    kr = w_in[:, OFF_KR:OFF_HY]
    w_kv_in = jnp.concatenate([w_in[:, OFF_KV:OFF_KR], kr, _rope_perm(kr)], axis=1)
    uq = w_uq.reshape(MLA_Q_RANK, MLA_HEADS, MLA_NOPE + MLA_ROPE)
    w_q = jnp.concatenate([uq, _rope_perm(uq[:, :, MLA_NOPE:])], axis=2).reshape(MLA_Q_RANK, MLA_HEADS * QK_HEAD)
    ukv = w_ukv.reshape(MLA_KV_RANK, MLA_HEADS, MLA_NOPE + MLA_V_DIM)
    w_k = ukv[:, :, :MLA_NOPE].reshape(MLA_KV_RANK, MLA_HEADS * MLA_NOPE)
    w_v = ukv[:, :, MLA_NOPE:].reshape(MLA_KV_RANK, MLA_HEADS * MLA_V_DIM)
    cast = lambda a: a.astype(BF16)
    w_all = jnp.concatenate([w_in[:, :OFF_KV], w_in[:, OFF_HY:], w_kv_in], axis=1)
    return dict(all=cast(w_all), kv=cast(w_kv_in), uq=cast(w_q), uk=cast(w_k), uv=cast(w_v), out=cast(w_out))


def _router_weights(w_router):
    hi = w_router.astype(BF16)
    lo = (w_router - hi.astype(F32)).astype(BF16)
    return jnp.concatenate([hi.T, lo.T], axis=0)


def _mixer(x, mod, w, lp, cs, spec, fwd, inv, kv_extra, full):
    b, l, d = x.shape
    kv_args = (lp["kv_norm_g"], w["uk"], w["uv"], cs)
    if not full:
        k, v = _in_proj_call(x, lp["norm1_g"], mod[0], mod[1], w["kv"], [], *kv_args)
        return None, k, v
    splits = [OFF_Q, MLA_Q_RANK, (HY_ORDER + 1) * HYENA_W]
    p_conv, p_q, p_hy, k, v = _in_proj_call(x, lp["norm1_g"], mod[0], mod[1], w["all"], splits, *kv_args)
    conv_o = _conformer_call(p_conv, lp["conv_dw_w"], lp["conv_dw_b"], lp["conv_ln_g"], lp["conv_ln_b"])
    att_o = _attn_call(p_q, lp["q_norm_g"], w["uq"], cs, kv_extra + [(k, v)])
    hy_o = _hyena_call(p_hy, lp["hy_short_w"], lp["hy_short_b"], spec, lp["hy_bias"], fwd, inv)
    x = _out_proj_call(conv_o.reshape(b * l, -1), att_o.reshape(b * l, -1), hy_o.reshape(b * l, -1),
                       w["out"], x.reshape(b * l, d), mod[2], l).reshape(b, l, d)
    return x, k, v


def _channel(streams, lp, wr, rbias, experts, layer, final_g=None):
    d = streams[0][0].shape[2]
    routed = [_norm_router_call(x, lp["norm2_g"], mod[3], mod[4], wr, rbias) for x, mod in streams]
    idx = jnp.concatenate([r[1] for r in routed], axis=1)
    y = _moe_call([r[0].reshape(-1, d // 2) for r in routed], idx, *experts, layer)
    outs = []
    first_row = 0
    for (x, mod), (_, _, gates) in zip(streams, routed):
        outs.append(_moe_resid_call(x, y, first_row, gates, mod[5], final_g))
        first_row += x.shape[0] * x.shape[1]
    return outs


def _filter_spectrum(l, lp, fwd):
    a = _hy_filter_call(l, lp["hy_w1"], lp["hy_b1"], lp["hy_freq1"], lp["hy_w2"], lp["hy_b2"], lp["hy_freq2"],
                        lp["hy_w3"], lp["hy_b3"])
    return _spec_call(fwd, a)


def kernel(x, c, ctx, c_ctx, norm1_g, norm2_g, w_ada, b_ada, w_in, conv_dw_w, conv_dw_b, conv_ln_g, conv_ln_b, q_norm_g, w_uq, kv_norm_g, w_ukv, hy_short_w, hy_short_b, hy_w1, hy_b1, hy_freq1, hy_w2, hy_b2, hy_freq2, hy_w3, hy_b3, hy_bias, w_out, w_router, router_bias, w_gate, w_up, w_down, final_norm_g):
    b, l, d = x.shape
    lc = ctx.shape[1]
    depth = w_in.shape[0]
    per_layer = dict(norm1_g=norm1_g, norm2_g=norm2_g, conv_dw_w=conv_dw_w, conv_dw_b=conv_dw_b,
                     conv_ln_g=conv_ln_g, conv_ln_b=conv_ln_b, q_norm_g=q_norm_g, kv_norm_g=kv_norm_g,
                     hy_short_w=hy_short_w, hy_short_b=hy_short_b, hy_w1=hy_w1, hy_b1=hy_b1, hy_freq1=hy_freq1,
                     hy_w2=hy_w2, hy_b2=hy_b2, hy_freq2=hy_freq2, hy_w3=hy_w3, hy_b3=hy_b3, hy_bias=hy_bias)
    experts = (w_gate, w_up, w_down)
    cs_l = _rope_table(l, True)
    cs_c = _rope_table(lc, False)
    fwd_l, inv_l = _dft_mats(l)
    fwd_c, inv_c = _dft_mats(lc)
    wr = _router_weights(w_router)
    rows = 16
    cvec = jnp.concatenate([c, c_ctx[None, :], jnp.zeros((rows - b - 1, d), F32)], axis=0)
    xl, xc = x, ctx
    for i in range(depth):
        last = i == depth - 1
        lp = {k: v[i] for k, v in per_layer.items()}
        w = _layer_weights(w_in[i], w_uq[i], w_ukv[i], w_out[i])
        m = _ada_call(cvec, w_ada, b_ada, i)
        mod_l = [m[:b, None, j * d:(j + 1) * d] for j in range(6)]
        mod_c = [jnp.broadcast_to(m[b:b + 1, None, j * d:(j + 1) * d], (b, 1, d)) for j in range(6)]
        spec_l = _filter_spectrum(l, lp, fwd_l)
        spec_c = None if last else _filter_spectrum(lc, lp, fwd_c)
        xc_new, kc, vc = _mixer(xc, mod_c, w, lp, cs_c, spec_c, fwd_c, inv_c, [], not last)
        xl, _, _ = _mixer(xl, mod_l, w, lp, cs_l, spec_l, fwd_l, inv_l, [(kc, vc)], True)
        if last:
            (xl,) = _channel([(xl, mod_l)], lp, wr, router_bias, experts, i, final_norm_g)
        else:
            xl, xc = _channel([(xl, mod_l), (xc_new, mod_c)], lp, wr, router_bias, experts, i)
    return xl
```

```python
import functools
import math

import jax
import jax.numpy as jnp
from jax import lax
from jax.experimental import pallas as pl
from jax.experimental.pallas import tpu as pltpu
from jax.experimental.pallas import tpu_sc as plsc

F32 = jnp.float32
BF16 = jnp.bfloat16

EPS = 1e-6
GRID_W = 64
CONV_W = 512
CONV_KSIZE = 31
MLA_HEADS = 8
MLA_NOPE = 128
MLA_ROPE = 64
MLA_V_DIM = 128
MLA_Q_RANK = 768
MLA_KV_RANK = 512
ROPE_FREQS = MLA_ROPE // 4
ROPE_THETA = 10000.0
MLA_SCALE = (MLA_NOPE + MLA_ROPE) ** -0.5
Q_SCALE = MLA_SCALE * math.log2(math.e)
HYENA_W = 512
HY_ORDER = 2
HY_SHORT = 3
HY_EMB = 33
HY_BANDS = (HY_EMB - 1) // 2
HY_FFN = 64
HY_MIN_DECAY = math.log(1e-2) / 1.5
HY_MAX_DECAY = math.log(1e-2) / 0.3
N_EXPERTS = 16
N_GROUPS = 4
EXPERTS_PER_GROUP = N_EXPERTS // N_GROUPS
OFF_Q = 2 * CONV_W
OFF_KV = OFF_Q + MLA_Q_RANK
OFF_KR = OFF_KV + MLA_KV_RANK
OFF_HY = OFF_KR + MLA_ROPE

LANES = 128
QK_HEAD = 2 * LANES
V_HEAD = MLA_V_DIM + LANES
V7X_VMEM_BYTES = 64 * 1024 * 1024
VMEM_LIMIT = (V7X_VMEM_BYTES * 7) // 8
MOE_TILE = 512
NEG_INF = float("-inf")


def _cp(*sem):
    return pltpu.CompilerParams(dimension_semantics=sem, vmem_limit_bytes=VMEM_LIMIT)


def _dot(a, b):
    return jnp.dot(a, b, preferred_element_type=F32)


def _rms(x, g):
    ms = jnp.mean(x * x, axis=-1, keepdims=True)
    return x * lax.rsqrt(ms + EPS) * g


def _ada_kernel(c_ref, w_ref, b_ref, o_ref):
    c = c_ref[...]
    a = (c * jax.nn.sigmoid(c)).astype(BF16)
    o_ref[...] = _dot(a, w_ref[0].astype(BF16)) + b_ref[0]


def _ada_call(cvec, w, b, layer):
    m, d = cvec.shape
    n = w.shape[2]
    tn = 1024
    return pl.pallas_call(
        _ada_kernel,
        grid=(n // tn,),
        in_specs=[pl.BlockSpec((m, d), lambda j: (0, 0)),
                  pl.BlockSpec((1, d, tn), lambda j: (layer, 0, j)),
                  pl.BlockSpec((1, 1, tn), lambda j: (layer, 0, j))],
        out_specs=pl.BlockSpec((m, tn), lambda j: (0, j)),
        out_shape=jax.ShapeDtypeStruct((m, n), F32),
        compiler_params=_cp("parallel"),
        name="ada",
    )(cvec, w, b.reshape(b.shape[0], 1, n))


def _in_proj_kernel(x_ref, g_ref, sh_ref, sc_ref, w_ref, kg_ref, wk_ref, wv_ref, cs_ref, *o_refs):
    *p_refs, k_ref, v_ref = o_refs
    y = _rms(x_ref[0], g_ref[...])
    h = (y * (1.0 + sc_ref[0]) + sh_ref[0]).astype(BF16)
    off = 0
    for o_ref in p_refs:
        n = o_ref.shape[-1]
        o_ref[0] = _dot(h, w_ref[:, off:off + n]).astype(o_ref.dtype)
        off += n
    p = _dot(h, w_ref[:, off:])
    xn = _rms(p[:, :MLA_KV_RANK], kg_ref[...]).astype(BF16)
    kk = _dot(xn, wk_ref[...])
    vv = _dot(xn, wv_ref[...])
    rot = _rotate(p[:, MLA_KV_RANK:], cs_ref[...])
    lane = lax.broadcasted_iota(jnp.int32, rot.shape, 1)
    rot = jnp.where(lane < MLA_ROPE, rot, 0.0).astype(BF16)
    ones = jnp.where(lane == 0, 1.0, 0.0).astype(BF16)
    for hd in range(MLA_HEADS):
        k_ref[0, :, hd * QK_HEAD:hd * QK_HEAD + LANES] = kk[:, hd * LANES:(hd + 1) * LANES].astype(BF16)
        k_ref[0, :, hd * QK_HEAD + LANES:(hd + 1) * QK_HEAD] = rot
        v_ref[0, :, hd * V_HEAD:hd * V_HEAD + MLA_V_DIM] = vv[:, hd * MLA_V_DIM:(hd + 1) * MLA_V_DIM].astype(BF16)
        v_ref[0, :, hd * V_HEAD + MLA_V_DIM:(hd + 1) * V_HEAD] = ones


def _in_proj_call(x, g, shift, scale, w, splits, kv_g, w_k, w_v, cs):
    b, l, d = x.shape
    n = w.shape[1]
    tm = min(l, 512)
    r = MLA_KV_RANK
    row = pl.BlockSpec((1, 1, d), lambda i, j: (i, 0, 0))
    const = lambda shape: pl.BlockSpec(shape, lambda i, j: (0, 0))
    widths = list(splits) + [MLA_HEADS * QK_HEAD, MLA_HEADS * V_HEAD]
    return pl.pallas_call(
        _in_proj_kernel,
        grid=(b, l // tm),
        in_specs=[pl.BlockSpec((1, tm, d), lambda i, j: (i, j, 0)),
                  const((1, d)), row, row,
                  pl.BlockSpec((d, n), lambda i, j: (0, 0), pipeline_mode=pl.Buffered(1)),
                  const((1, r)), const(w_k.shape), const(w_v.shape),
                  pl.BlockSpec((tm, LANES), lambda i, j: (j, 0))],
        out_specs=[pl.BlockSpec((1, tm, k), lambda i, j: (i, j, 0)) for k in widths],
        out_shape=[jax.ShapeDtypeStruct((b, l, k), BF16) for k in widths],
        compiler_params=_cp("parallel", "parallel"),
        name="in_proj",
    )(x, g.reshape(1, d), shift, scale, w, kv_g.reshape(1, r), w_k, w_v, cs)


CONV_ROWS = 128
CONV_PAD = 16


def _conformer_kernel(p_ref, w_ref, b_ref, g_ref, be_ref, o_ref, zp_ref):
    l = p_ref.shape[1]
    c = CONV_W
    zeros = jnp.zeros((CONV_PAD, c), F32)
    zp_ref[0:CONV_PAD, :] = zeros
    zp_ref[CONV_PAD + l:CONV_PAD + l + CONV_PAD, :] = zeros

    def glu(r, carry):
        base = pl.multiple_of(r * CONV_ROWS, CONV_ROWS)
        blk = p_ref[0, pl.ds(base, CONV_ROWS), :].astype(F32)
        zp_ref[pl.ds(base + CONV_PAD, CONV_ROWS), :] = blk[:, :c] * jax.nn.sigmoid(blk[:, c:])
        return carry

    lax.fori_loop(0, l // CONV_ROWS, glu, 0)

    win_rows = CONV_ROWS + 2 * CONV_PAD
    first = CONV_PAD - CONV_KSIZE // 2

    def conv(r, carry):
        base = pl.multiple_of(r * CONV_ROWS, CONV_ROWS)
        parts = []
        for cc in range(c // LANES):
            lanes = slice(cc * LANES, (cc + 1) * LANES)
            win = zp_ref[pl.ds(base, win_rows), lanes]
            acc = jnp.zeros((CONV_ROWS, LANES), F32)
            for res in range(8):
                taps = [k for k in range(CONV_KSIZE) if (k + first) % 8 == res]
                if not taps:
                    continue
                shifted = pltpu.roll(win, win_rows - res, axis=0) if res else win
                for k in taps:
                    off = k + first - res
                    acc = acc + shifted[off:off + CONV_ROWS, :] * w_ref[k:k + 1, lanes]
            parts.append(acc)
        z = jnp.concatenate(parts, axis=1) + b_ref[...]
        mu = jnp.mean(z, axis=-1, keepdims=True)
        zc = z - mu
        var = jnp.mean(zc * zc, axis=-1, keepdims=True)
        y = zc * lax.rsqrt(var + EPS) * g_ref[...] + be_ref[...]
        o_ref[0, pl.ds(base, CONV_ROWS), :] = (y * jax.nn.sigmoid(y)).astype(o_ref.dtype)
        return carry

    lax.fori_loop(0, l // CONV_ROWS, conv, 0)


def _conformer_call(p, dw_w, dw_b, ln_g, ln_b):
    b, l, c2 = p.shape
    c = c2 // 2
    vec = pl.BlockSpec((1, c), lambda i: (0, 0))
    return pl.pallas_call(
        _conformer_kernel,
        grid=(b,),
        in_specs=[pl.BlockSpec((1, l, c2), lambda i: (i, 0, 0)),
                  pl.BlockSpec((CONV_KSIZE, c), lambda i: (0, 0)), vec, vec, vec],
        out_specs=pl.BlockSpec((1, l, c), lambda i: (i, 0, 0)),
        out_shape=jax.ShapeDtypeStruct((b, l, c), BF16),
        scratch_shapes=[pltpu.VMEM((l + 2 * CONV_PAD, c), F32)],
        compiler_params=_cp("parallel"),
        name="conformer",
    )(p, dw_w, dw_b.reshape(1, c), ln_g.reshape(1, c), ln_b.reshape(1, c))


def _rotate(group, cs):
    t = group * cs
    return t + pltpu.roll(t, MLA_ROPE, axis=1)


def _attn_kernel(p_ref, g_ref, w_ref, cs_ref, *refs, n_seg):
    o_ref, q_ref = refs[2 * n_seg], refs[2 * n_seg + 1]
    xn = _rms(p_ref[0].astype(F32), g_ref[...]).astype(BF16)
    cs = cs_ref[...] * Q_SCALE
    for h in range(MLA_HEADS):
        r = _dot(xn, w_ref[:, h * QK_HEAD:(h + 1) * QK_HEAD])
        q_ref[:, h * QK_HEAD:h * QK_HEAD + LANES] = (r[:, :LANES] * Q_SCALE).astype(BF16)
        q_ref[:, h * QK_HEAD + LANES:(h + 1) * QK_HEAD] = _rotate(r[:, LANES:], cs).astype(BF16)
    nt = (((1,), (1,)), ((), ()))
    for h in range(MLA_HEADS):
        qk = slice(h * QK_HEAD, (h + 1) * QK_HEAD)
        vd = slice(h * MLA_V_DIM, (h + 1) * MLA_V_DIM)
        q = q_ref[:, qk]
        scores = [lax.dot_general(q, refs[2 * s][0, :, qk], nt, preferred_element_type=F32)
                  for s in range(n_seg)]
        m = functools.reduce(jnp.maximum, [jnp.max(s, axis=-1, keepdims=True) for s in scores])
        acc = 0.0
        for s in range(n_seg):
            p = jnp.exp2(scores[s] - m).astype(BF16)
            acc = acc + _dot(p, refs[2 * s + 1][0, :, h * V_HEAD:(h + 1) * V_HEAD])
        o_ref[0, :, vd] = (acc[:, :MLA_V_DIM] * (1.0 / acc[:, MLA_V_DIM:MLA_V_DIM + 1])).astype(o_ref.dtype)


def _attn_call(p_q, q_g, w_q, cs, segs):
    b, l, r = p_q.shape
    tq = min(l, 512)
    const = lambda shape: pl.BlockSpec(shape, lambda i, j: (0, 0))
    in_specs = [pl.BlockSpec((1, tq, r), lambda i, j: (i, j, 0)), const((1, r)), const(w_q.shape),
                pl.BlockSpec((tq, LANES), lambda i, j: (j, 0))]
    args = [p_q, q_g.reshape(1, r), w_q, cs]
    for k, v in segs:
        in_specs.append(pl.BlockSpec((1,) + k.shape[1:], lambda i, j: (i, 0, 0), pipeline_mode=pl.Buffered(1)))
        in_specs.append(pl.BlockSpec((1,) + v.shape[1:], lambda i, j: (i, 0, 0), pipeline_mode=pl.Buffered(1)))
        args += [k, v]
    return pl.pallas_call(
        functools.partial(_attn_kernel, n_seg=len(segs)),
        grid=(b, l // tq),
        in_specs=in_specs,
        out_specs=pl.BlockSpec((1, tq, MLA_HEADS * MLA_V_DIM), lambda i, j: (i, j, 0)),
        out_shape=jax.ShapeDtypeStruct((b, l, MLA_HEADS * MLA_V_DIM), BF16),
        scratch_shapes=[pltpu.VMEM((tq, MLA_HEADS * QK_HEAD), BF16)],
        compiler_params=_cp("parallel", "parallel"),
        name="attention",
    )(*args)


HY_PAD = 8
HY_ROWS = 128


def _hy_short_kernel(p_ref, w_ref, b_ref, o_ref, up_ref):
    l = p_ref.shape[1]
    c = p_ref.shape[2]
    zeros = jnp.zeros((HY_PAD, c), F32)
    up_ref[0:HY_PAD, :] = zeros
    up_ref[HY_PAD + l:HY_PAD + l + HY_PAD, :] = zeros

    def copy(r, carry):
        base = pl.multiple_of(r * HY_ROWS, HY_ROWS)
        up_ref[pl.ds(base + HY_PAD, HY_ROWS), :] = p_ref[0, pl.ds(base, HY_ROWS), :].astype(F32)
        return carry

    lax.fori_loop(0, l // HY_ROWS, copy, 0)
    first = HY_PAD - HY_SHORT // 2

    def conv(r, carry):
        base = pl.multiple_of(r * HY_ROWS, HY_ROWS)
        for cc in range(c // LANES):
            lanes = slice(cc * LANES, (cc + 1) * LANES)
            win = up_ref[pl.ds(base, HY_ROWS + 2 * HY_PAD), lanes]
            acc = jnp.zeros((HY_ROWS, LANES), F32) + b_ref[:, lanes]
            for k in range(HY_SHORT):
                acc = acc + win[first + k:first + k + HY_ROWS, :] * w_ref[k:k + 1, lanes]
            o_ref[0, 0, pl.ds(base, HY_ROWS), lanes] = acc.astype(o_ref.dtype)
        return carry

    lax.fori_loop(0, l // HY_ROWS, conv, 0)


def _hy_short_call(p, w, bias):
    b, l, c3 = p.shape
    c = HYENA_W
    return pl.pallas_call(
        _hy_short_kernel,
        grid=(b, c3 // c),
        in_specs=[pl.BlockSpec((1, l, c), lambda i, g: (i, 0, g)),
                  pl.BlockSpec((HY_SHORT, c), lambda i, g: (0, g)),
                  pl.BlockSpec((1, c), lambda i, g: (0, g))],
        out_specs=pl.BlockSpec((1, 1, l, c), lambda i, g: (g, i, 0, 0)),
        out_shape=jax.ShapeDtypeStruct((c3 // c, b, l, c), BF16),
        scratch_shapes=[pltpu.VMEM((l + 2 * HY_PAD, c), F32)],
        compiler_params=_cp("parallel", "parallel"),
        name="hyena_short",
    )(p, w, bias.reshape(1, c3))


def _hy_filter_kernel(z_ref, w1_ref, b1_ref, f1_ref, w2_ref, b2_ref, f2_ref, w3_ref, b3_ref, win_ref, o_ref):
    hp = lax.Precision.HIGHEST

    def mm(a, b):
        return jnp.dot(a, b, preferred_element_type=F32, precision=hp)

    h = jnp.sin(f1_ref[...] * (mm(z_ref[...], w1_ref[...]) + b1_ref[...]))
    h = jnp.sin(f2_ref[...] * (mm(h, w2_ref[...]) + b2_ref[...]))
    h = mm(h, w3_ref[...]) + b3_ref[...]
    c = HYENA_W
    win = win_ref[...]
    tl = h.shape[0]
    row = lax.broadcasted_iota(jnp.int32, (tl, c), 0) + pl.program_id(0) * tl
    for o in range(HY_ORDER):
        fwd = h[:, (2 * o) * c:(2 * o + 1) * c] * win
        bwd = jnp.where(row == 0, 0.0, h[:, (2 * o + 1) * c:(2 * o + 2) * c] * win)
        o_ref[:, o * c:(o + 1) * c] = (fwd + bwd).astype(o_ref.dtype)
        o_ref[:, (HY_ORDER + o) * c:(HY_ORDER + o + 1) * c] = (fwd - bwd).astype(o_ref.dtype)


def _pad2(a, rows, cols):
    return jnp.pad(a, ((0, rows - a.shape[0]), (0, cols - a.shape[1])))


def _hy_filter_call(l, w1, b1, f1, w2, b2, f2, w3, b3):
    t = jnp.arange(l, dtype=F32)
    t_norm = t / max(l - 1, 1)
    ang = (2.0 * math.pi / l) * t[:, None] * jnp.linspace(1e-4, HY_BANDS - 1, HY_BANDS, dtype=F32)[None, :]
    z = jnp.concatenate([t_norm[:, None], jnp.cos(ang), -jnp.sin(ang)], axis=-1)
    deltas = jnp.abs(jnp.linspace(HY_MIN_DECAY, HY_MAX_DECAY, HYENA_W, dtype=F32))
    window = jnp.exp(-t_norm[:, None] * deltas[None, :])
    n3 = w3.shape[1]
    tl = min(l, 256)
    full = lambda shape: pl.BlockSpec(shape, lambda i: (0, 0))
    return pl.pallas_call(
        _hy_filter_kernel,
        grid=(l // tl,),
        in_specs=[pl.BlockSpec((tl, LANES), lambda i: (i, 0)),
                  full((LANES, LANES)), full((1, LANES)), full((1, LANES)),
                  full((LANES, LANES)), full((1, LANES)), full((1, LANES)),
                  full((LANES, n3)), full((1, n3)),
                  pl.BlockSpec((tl, HYENA_W), lambda i: (i, 0))],
        out_specs=pl.BlockSpec((tl, n3), lambda i: (i, 0)),
        out_shape=jax.ShapeDtypeStruct((l, n3), BF16),
        compiler_params=_cp("parallel"),
        name="hyena_filter",
    )(_pad2(z, l, LANES), _pad2(w1, LANES, LANES), _pad2(b1[None], 1, LANES), _pad2(f1[None], 1, LANES),
      _pad2(w2, LANES, LANES), _pad2(b2[None], 1, LANES), _pad2(f2[None], 1, LANES),
      _pad2(w3, LANES, n3), b3[None], window)


def _dft_tables_kernel(ac_ref, as_ref, bc_ref, bs_ref, fwd_ref, inv_ref):
    tm, l = fwd_ref.shape[1], fwd_ref.shape[2]
    n = 2 * l
    row = lax.broadcasted_iota(jnp.int32, (tm, LANES), 0) + pl.program_id(0) * tm
    lane = lax.broadcasted_iota(jnp.int32, (tm, LANES), 1)
    row_sign = jnp.where((row & 1) == 0, 1.0, -1.0)
    lane_sign = jnp.where((lane & 1) == 0, 1.0, -1.0)
    bc = bc_ref[...]
    bs = bs_ref[...]
    for a in range(l // LANES):
        cols = slice(a * LANES, (a + 1) * LANES)
        ca = ac_ref[:, a:a + 1]
        sa = as_ref[:, a:a + 1]
        c = ca * bc - sa * bs
        s = sa * bc + ca * bs
        fwd_ref[0, :, cols] = c.astype(BF16)
        fwd_ref[1, :, cols] = jnp.where(row == 0, lane_sign, -s).astype(BF16)
        inv_c = c * (2.0 / n)
        inv_s = -s * (2.0 / n)
        if a == 0:
            inv_c = jnp.where(lane == 0, 1.0 / n, inv_c)
            inv_s = jnp.where(lane == 0, row_sign / n, inv_s)
        inv_ref[:, cols] = inv_c.astype(BF16)
        inv_ref[:, l + a * LANES:l + (a + 1) * LANES] = inv_s.astype(BF16)


def _dft_mats(l):
    n = 2 * l
    f = jnp.arange(l, dtype=jnp.int32)[:, None]

    def table(t):
        ang = ((f * t[None, :]) % n).astype(F32) * (2.0 * math.pi / n)
        return jnp.cos(ang), jnp.sin(ang)

    coarse = table(jnp.arange(l // LANES, dtype=jnp.int32) * LANES)
    fine = table(jnp.arange(LANES, dtype=jnp.int32))
    tm = min(l, 256)
    small = pl.BlockSpec((tm, l // LANES), lambda i: (i, 0))
    wide = pl.BlockSpec((tm, LANES), lambda i: (i, 0))
    return pl.pallas_call(
        _dft_tables_kernel,
        grid=(l // tm,),
        in_specs=[small, small, wide, wide],
        out_specs=[pl.BlockSpec((2, tm, l), lambda i: (0, i, 0)), pl.BlockSpec((tm, n), lambda i: (i, 0))],
        out_shape=[jax.ShapeDtypeStruct((2, l, l), BF16), jax.ShapeDtypeStruct((l, n), BF16)],
        compiler_params=_cp("parallel"),
        name="dft_tables",
    )(*coarse, *fine)


def _spec_kernel(f_ref, a_ref, o_ref):
    half = a_ref.shape[1] // 2
    hi = _dot(f_ref[1], a_ref[:, half:])
    o_ref[0] = _dot(f_ref[0], a_ref[:, :half])
    o_ref[1] = hi

    @pl.when(pl.program_id(0) == 0)
    def _():
        nyq = _dot(f_ref[1, 0:16, :], a_ref[:, :half])
        row = lax.broadcasted_iota(jnp.int32, nyq.shape, 0)
        o_ref[1, 0:16, :] = jnp.where(row == 0, nyq, hi[0:16])


def _spec_call(fwd, a):
    _, lf, l = fwd.shape
    n = a.shape[1] // 2
    tm = min(lf, 512)
    return pl.pallas_call(
        _spec_kernel,
        grid=(lf // tm,),
        in_specs=[pl.BlockSpec((2, tm, l), lambda i: (0, i, 0)),
                  pl.BlockSpec(a.shape, lambda i: (0, 0))],
        out_specs=pl.BlockSpec((2, tm, n), lambda i: (0, i, 0)),
        out_shape=jax.ShapeDtypeStruct((2, lf, n), F32),
        compiler_params=_cp("arbitrary"),
        name="hyena_spectrum",
    )(fwd, a)


def _dft_fwd_kernel(f_ref, u_ref, h_ref, y_ref):
    u = u_ref[0, 0].astype(BF16)
    xr = _dot(f_ref[0], u)
    xi = _dot(f_ref[1], u)
    hr = h_ref[0]
    hi = h_ref[1]
    yr = xr * hr - xi * hi
    yi = xr * hi + xi * hr
    y_ref[0, 0] = yr.astype(BF16)
    y_ref[0, 1] = yi.astype(BF16)

    @pl.when(pl.program_id(0) == 0)
    def _():
        row = lax.broadcasted_iota(jnp.int32, (16, u.shape[1]), 0)
        y_ref[0, 0, 0:16, :] = jnp.where(row == 0, xr[0:16] * hr[0:16], yr[0:16]).astype(BF16)
        y_ref[0, 1, 0:16, :] = jnp.where(row == 0, xi[0:16] * hi[0:16], yi[0:16]).astype(BF16)


def _dft_fwd_call(fwd, u, group, spec, order):
    _, b, l, c = u.shape
    lf = fwd.shape[1]
    tm = min(lf, 512)
    return pl.pallas_call(
        _dft_fwd_kernel,
        grid=(lf // tm, b),
        in_specs=[pl.BlockSpec((2, tm, l), lambda i, j: (0, i, 0)),
                  pl.BlockSpec((1, 1, l, c), lambda i, j: (group, j, 0, 0)),
                  pl.BlockSpec((2, tm, c), lambda i, j: (0, i, order))],
        out_specs=pl.BlockSpec((1, 2, tm, c), lambda i, j: (j, 0, i, 0)),
        out_shape=jax.ShapeDtypeStruct((b, 2, lf, c), BF16),
        compiler_params=_cp("arbitrary", "arbitrary"),
        name="hyena_dft_fwd",
    )(fwd, u, spec)


def _dft_inv_kernel(g_ref, y_ref, u_ref, b_ref, x_ref, o_ref):
    r = _dot(g_ref[...], y_ref[0]) + u_ref[0, 0].astype(F32) * b_ref[...]
    o_ref[0, 0] = (x_ref[0, 0] * r).astype(o_ref.dtype)


def _dft_inv_call(inv, y, u, u_group, bias, gate, gate_group):
    _, b, l, c = u.shape
    n = inv.shape[1]
    tm = min(l, 512)
    y = y.reshape(b, n, c)
    row = lambda g: pl.BlockSpec((1, 1, tm, c), lambda i, j: (g, j, i, 0))
    return pl.pallas_call(
        _dft_inv_kernel,
        grid=(l // tm, b),
        in_specs=[pl.BlockSpec((tm, n), lambda i, j: (i, 0)),
                  pl.BlockSpec((1, n, c), lambda i, j: (j, 0, 0)),
                  row(u_group), pl.BlockSpec((1, c), lambda i, j: (0, 0)), row(gate_group)],
        out_specs=row(0),
        out_shape=jax.ShapeDtypeStruct((1, b, l, c), BF16),
        compiler_params=_cp("parallel", "parallel"),
        name="hyena_dft_inv",
    )(inv, y, u, bias.reshape(1, c), gate)


def _hyena_call(p_hy, sh_w, sh_b, spec, hy_bias, fwd, inv):
    u3 = _hy_short_call(p_hy, sh_w, sh_b)
    z = _dft_inv_call(inv, _dft_fwd_call(fwd, u3, 0, spec, 0), u3, 0, hy_bias[0], u3, 1)
    return _dft_inv_call(inv, _dft_fwd_call(fwd, z, 0, spec, 1), z, 0, hy_bias[1], u3, 2)[0]


def _mix_out_kernel(c_ref, a_ref, h_ref, w1_ref, w2_ref, w3_ref, x_ref, g1_ref, g_ref, sh_ref, sc_ref,
                    wr_ref, rb_ref, o_ref, h32_ref, idx_ref, gate_ref):
    acc = _dot(c_ref[0], w1_ref[...]) + _dot(a_ref[0], w2_ref[...]) + _dot(h_ref[0], w3_ref[...])
    x = x_ref[0] + g1_ref[0] * acc
    o_ref[0] = x
    hm = _rms(x, g_ref[...]) * (1.0 + sc_ref[0]) + sh_ref[0]
    h32_ref[0] = _pack_rows(hm)
    picks, gates = _route(hm, wr_ref[...], rb_ref[...])
    idx_ref[...] = jnp.concatenate(picks, axis=0)
    gate_ref[...] = jnp.concatenate(gates, axis=0)


def _mix_out_call(conv_o, att_o, hy_o, w_out, x, mod, g2, wr, rbias):
    b, l, d = x.shape
    tm = min(l, 512)
    nt = l // tm
    k1, k2 = conv_o.shape[2], att_o.shape[2]
    w1, w2, w3 = w_out[:k1], w_out[k1:k1 + k2], w_out[k1 + k2:]
    act = lambda k: pl.BlockSpec((1, tm, k), lambda i, j: (i, j, 0))
    const = lambda shape: pl.BlockSpec(shape, lambda i, j: (0, 0), pipeline_mode=pl.Buffered(1))
    row = pl.BlockSpec((1, 1, d), lambda i, j: (i, 0, 0))
    tok = pl.BlockSpec((2, tm), lambda i, j: (0, i * nt + j))
    return pl.pallas_call(
        _mix_out_kernel,
        grid=(b, nt),
        in_specs=[act(k1), act(k2), act(w3.shape[0]), const(w1.shape), const(w2.shape), const(w3.shape),
                  act(d), row, pl.BlockSpec((1, d), lambda i, j: (0, 0)), row, row,
                  pl.BlockSpec(wr.shape, lambda i, j: (0, 0)),
                  pl.BlockSpec((N_EXPERTS, 1), lambda i, j: (0, 0))],
        out_specs=[act(d), act(d // 2), tok, tok],
        out_shape=[jax.ShapeDtypeStruct((b, l, d), F32),
                   jax.ShapeDtypeStruct((b, l, d // 2), jnp.int32),
                   jax.ShapeDtypeStruct((2, b * l), jnp.int32),
                   jax.ShapeDtypeStruct((2, b * l), F32)],
        compiler_params=_cp("parallel", "parallel"),
        name="mix_out",
    )(conv_o, att_o, hy_o, w1, w2, w3, x, mod[2], g2.reshape(1, d), mod[3], mod[4], wr,
      rbias.reshape(N_EXPERTS, 1))


def _route(hm, wr, rbias):
    hi = hm.astype(BF16)
    lo = (hm - hi.astype(F32)).astype(BF16)
    nt = (((1,), (1,)), ((), ()))
    r = (lax.dot_general(wr, hi, nt, preferred_element_type=F32)
         + lax.dot_general(wr, lo, nt, preferred_element_type=F32))
    logits = r[:N_EXPERTS] + r[N_EXPERTS:]
    score = jax.nn.sigmoid(logits)
    sel = score + rbias
    rows = [sel[e:e + 1] for e in range(N_EXPERTS)]
    gscore = []
    for g in range(N_GROUPS):
        a, b, c, d = rows[4 * g:4 * g + 4]
        h1, l1, h2, l2 = jnp.maximum(a, b), jnp.minimum(a, b), jnp.maximum(c, d), jnp.minimum(c, d)
        gscore.append(jnp.maximum(h1, h2) + jnp.maximum(jnp.minimum(h1, h2), jnp.maximum(l1, l2)))
    best = gscore[0]
    gbest = jnp.zeros(best.shape, jnp.int32)
    for g in range(1, N_GROUPS):
        upd = gscore[g] > best
        gbest = jnp.where(upd, g, gbest)
        best = jnp.where(upd, gscore[g], best)
    masked = [jnp.where(gbest == e // EXPERTS_PER_GROUP, rows[e], NEG_INF) for e in range(N_EXPERTS)]
    picks = []
    for _ in range(2):
        val = jnp.full(best.shape, NEG_INF, F32)
        idx = jnp.zeros(best.shape, jnp.int32)
        for e in range(N_EXPERTS):
            cand = masked[e]
            if picks:
                cand = jnp.where(picks[0] == e, NEG_INF, cand)
            upd = cand > val
            idx = jnp.where(upd, e, idx)
            val = jnp.where(upd, cand, val)
        picks.append(idx)
    gates = [functools.reduce(jnp.add, [jnp.where(p == e, score[e:e + 1], 0.0) for e in range(N_EXPERTS)])
             for p in picks]
    den = gates[0] + gates[1]
    return picks, [gates[0] / den, gates[1] / den]


def _expert_changed(te_ref, i):
    return (i == 0) | (te_ref[i] != te_ref[jnp.maximum(i - 1, 0)])


def _pack_rows(x):
    bits = pltpu.bitcast(x.astype(BF16).astype(F32), jnp.uint32)
    half = bits.shape[1] // 2
    return pltpu.bitcast((bits[:, :half] >> 16) | (bits[:, half:] & jnp.uint32(0xFFFF0000)), jnp.int32)


def _unpack_rows(words):
    w = pltpu.bitcast(words, jnp.uint32)
    return pltpu.bitcast(w << 16, F32), pltpu.bitcast(w & jnp.uint32(0xFFFF0000), F32)


def _moe_up_kernel(te_ref, tn_ref, nu_ref, x_ref, wg_hbm, wu_hbm, h_ref, gbuf, ubuf, wgb_ref, wub_ref, sems,
                   *, layer):
    j, i = pl.program_id(0), pl.program_id(1)
    tf = gbuf.shape[1]

    def fetch(e, chunk):
        cols = pl.ds(pl.multiple_of(chunk * tf, tf), tf)
        return (pltpu.make_async_copy(wg_hbm.at[layer, e, :, cols], gbuf, sems.at[0]),
                pltpu.make_async_copy(wu_hbm.at[layer, e, :, cols], ubuf, sems.at[1]))

    def start(e, chunk):
        for copy in fetch(e, chunk):
            copy.start()

    @pl.when((i == 0) & (j == 0))
    def _():
        start(te_ref[0], 0)

    @pl.when(_expert_changed(te_ref, i))
    def _():
        for copy in fetch(te_ref[i], j):
            copy.wait()
        wgb_ref[...] = gbuf[...].astype(BF16)
        wub_ref[...] = ubuf[...].astype(BF16)
        nxt = tn_ref[i]
        more = nxt < nu_ref[0]

        @pl.when(more)
        def _():
            start(te_ref[nxt], j)

        @pl.when(jnp.logical_not(more) & (j + 1 < pl.num_programs(0)))
        def _():
            start(te_ref[0], j + 1)

    @pl.when(i < nu_ref[0])
    def _():
        xa, xb = (v.astype(BF16) for v in _unpack_rows(x_ref[...]))
        half = xa.shape[1]
        a = _dot(xa, wgb_ref[:half, :]) + _dot(xb, wgb_ref[half:, :])
        u = _dot(xa, wub_ref[:half, :]) + _dot(xb, wub_ref[half:, :])
        h_ref[...] = (a * jax.nn.sigmoid(a) * u).astype(BF16)

    @pl.when(i >= nu_ref[0])
    def _():
        h_ref[...] = jnp.zeros(h_ref.shape, BF16)


def _moe_up_call(xs, w_gate, w_up, layer, te, tn, nu, tile):
    p, dw = xs.shape
    _, _, d, f = w_gate.shape
    tf = 512
    hbm = pl.BlockSpec(memory_space=pl.ANY)
    return pl.pallas_call(
        functools.partial(_moe_up_kernel, layer=layer),
        grid_spec=pltpu.PrefetchScalarGridSpec(
            num_scalar_prefetch=3,
            grid=(f // tf, p // tile),
            in_specs=[pl.BlockSpec((tile, dw), lambda j, i, te, tn, nu: (i, 0)), hbm, hbm],
            out_specs=pl.BlockSpec((tile, tf), lambda j, i, te, tn, nu: (i, j)),
            scratch_shapes=[pltpu.VMEM((d, tf), F32), pltpu.VMEM((d, tf), F32),
                            pltpu.VMEM((d, tf), BF16), pltpu.VMEM((d, tf), BF16),
                            pltpu.SemaphoreType.DMA((2,))]),
        out_shape=jax.ShapeDtypeStruct((p, f), BF16),
        compiler_params=_cp("arbitrary", "arbitrary"),
        name="moe_up",
    )(te, tn, nu, xs, w_gate, w_up)


def _moe_down_kernel(te_ref, tn_ref, nu_ref, h_ref, wd_hbm, y_ref, wbuf, wdb_ref, sem, *, layer):
    i = pl.program_id(0)

    def fetch(e):
        return pltpu.make_async_copy(wd_hbm.at[layer, e], wbuf, sem)

    @pl.when(i == 0)
    def _():
        fetch(te_ref[0]).start()

    @pl.when(_expert_changed(te_ref, i))
    def _():
        fetch(te_ref[i]).wait()
        wdb_ref[...] = wbuf[...].astype(BF16)
        nxt = tn_ref[i]

        @pl.when(nxt < nu_ref[0])
        def _():
            fetch(te_ref[nxt]).start()

    @pl.when(i < nu_ref[0])
    def _():
        y_ref[...] = _pack_rows(_dot(h_ref[...], wdb_ref[...]))

    @pl.when(i >= nu_ref[0])
    def _():
        y_ref[...] = jnp.zeros(y_ref.shape, jnp.int32)


def _moe_down_call(h, w_down, layer, te, tn, nu, tile):
    p, f = h.shape
    d = w_down.shape[3]
    return pl.pallas_call(
        functools.partial(_moe_down_kernel, layer=layer),
        grid_spec=pltpu.PrefetchScalarGridSpec(
            num_scalar_prefetch=3,
            grid=(p // tile,),
            in_specs=[pl.BlockSpec((tile, f), lambda i, te, tn, nu: (i, 0)),
                      pl.BlockSpec(memory_space=pl.ANY)],
            out_specs=pl.BlockSpec((tile, d // 2), lambda i, te, tn, nu: (i, 0)),
            scratch_shapes=[pltpu.VMEM((f, d), F32), pltpu.VMEM((f, d), BF16), pltpu.SemaphoreType.DMA(())]),
        out_shape=jax.ShapeDtypeStruct((p, d // 2), jnp.int32),
        compiler_params=_cp("arbitrary"),
        name="moe_down",
    )(te, tn, nu, h, w_down)


PLAN_BLOCK = 512


def _plan_kernel(idx_ref, pos_ref, te_ref, tn_ref, nu_ref, rank_ref, *, tile):
    t = idx_ref.shape[1]
    nb = t // PLAN_BLOCK
    experts = lax.broadcasted_iota(jnp.int32, (N_EXPERTS, PLAN_BLOCK), 0)
    r = lax.broadcasted_iota(jnp.int32, (PLAN_BLOCK, PLAN_BLOCK), 0)
    c = lax.broadcasted_iota(jnp.int32, (PLAN_BLOCK, PLAN_BLOCK), 1)
    before = jnp.where(r < c, 1.0, 0.0).astype(BF16)

    def rank_pass(k):
        def body(j, carry):
            cols = pl.ds(pl.multiple_of(j * PLAN_BLOCK, PLAN_BLOCK), PLAN_BLOCK)
            onehot = jnp.where(experts == idx_ref[k:k + 1, cols], 1.0, 0.0)
            earlier = _dot(onehot.astype(BF16), before) + carry
            rank_ref[k:k + 1, cols] = jnp.sum(onehot * earlier, axis=0, keepdims=True)
            return carry + jnp.sum(onehot, axis=1, keepdims=True)
        return body

    counts = jnp.zeros((N_EXPERTS, 1), F32)
    for k in range(2):
        counts = lax.fori_loop(0, nb, rank_pass(k), counts)
    padded = jnp.ceil(counts * (1.0 / tile)) * tile
    starts, ends = [], []
    acc = jnp.zeros((1, 1), F32)
    for e in range(N_EXPERTS):
        starts.append(acc)
        acc = acc + padded[e:e + 1]
        ends.append(acc)

    def place_pass(k):
        def body(j, carry):
            cols = pl.ds(pl.multiple_of(j * PLAN_BLOCK, PLAN_BLOCK), PLAN_BLOCK)
            row = idx_ref[k:k + 1, cols]
            start = functools.reduce(jnp.add, [jnp.where(row == e, starts[e], 0.0) for e in range(N_EXPERTS)])
            pos_ref[k:k + 1, cols] = (rank_ref[k:k + 1, cols] + start).astype(jnp.int32)
            return carry
        return body

    for k in range(2):
        lax.fori_loop(0, nb, place_pass(k), 0)
    n_used = acc * (1.0 / tile)
    tiles = lax.broadcasted_iota(jnp.int32, te_ref.shape, 1).astype(F32)
    first_row = jnp.minimum(tiles, n_used - 1.0) * tile
    te = functools.reduce(jnp.add, [jnp.where(ends[e] <= first_row, 1.0, 0.0) for e in range(N_EXPERTS)])
    te_ref[...] = te.astype(jnp.int32)
    seg_end = functools.reduce(jnp.add, [jnp.where(te == e, ends[e], 0.0) for e in range(N_EXPERTS)])
    tn_ref[...] = (seg_end * (1.0 / tile)).astype(jnp.int32)
    nu_ref[...] = jnp.broadcast_to(n_used, nu_ref.shape).astype(jnp.int32)


def _plan_call(idx, tile, n_tiles):
    t = idx.shape[1]
    whole = lambda shape: pl.BlockSpec(shape, lambda: (0,) * len(shape))
    lanes = jax.ShapeDtypeStruct((1, LANES), jnp.int32)
    pos, te, tn, nu = pl.pallas_call(
        functools.partial(_plan_kernel, tile=tile),
        in_specs=[whole(idx.shape)],
        out_specs=[whole(idx.shape), whole((1, LANES)), whole((1, LANES)), whole((1, LANES))],
        out_shape=[jax.ShapeDtypeStruct(idx.shape, jnp.int32), lanes, lanes, lanes],
        scratch_shapes=[pltpu.VMEM(idx.shape, F32)],
        name="route_plan",
    )(idx)
    return pos.reshape(2 * t), te[0, :n_tiles], tn[0, :n_tiles], nu[0, :1]


V7X_SC_CORES = 2
V7X_SC_SUBCORES = 16
SC_WORKERS = V7X_SC_CORES * V7X_SC_SUBCORES
SC_ROW_BYTES = 256 * 1024


def _sc_worker():
    return lax.axis_index("s") * V7X_SC_CORES + lax.axis_index("c")


def _sc_dispatch_call(streams, pos, n_rows):
    w = streams[0].shape[1]
    sizes = [s.shape[0] for s in streams]
    t = sum(sizes)
    per_w = t // SC_WORKERS
    ch = min(per_w, SC_ROW_BYTES // (4 * w))
    assert per_w % ch == 0 and all(size % ch == 0 for size in sizes)
    mesh = plsc.VectorSubcoreMesh(core_axis_name="c", subcore_axis_name="s")

    @functools.partial(pl.kernel, mesh=mesh, out_type=jax.ShapeDtypeStruct((n_rows, w), jnp.int32),
                       scratch_types=[pltpu.VMEM((ch,), jnp.int32), pltpu.VMEM((ch, w), jnp.int32)],
                       name="sc_dispatch")
    def run(*refs):
        srcs = refs[:len(sizes)]
        pos_hbm, xs_hbm, idx_v, rows_v = refs[len(sizes):]
        first = _sc_worker() * per_w

        @pl.loop(0, per_w // ch)
        def _(j):
            base = first + j * ch
            start = 0
            for src, size in zip(srcs, sizes):
                def load(src=src, start=start):
                    pltpu.sync_copy(src.at[pl.ds(base - start, ch)], rows_v)
                if len(sizes) == 1:
                    load()
                else:
                    pl.when((base >= start) & (base < start + size))(load)
                start += size
            for k in range(2):
                pltpu.sync_copy(pos_hbm.at[pl.ds(k * t + base, ch)], idx_v)
                pltpu.sync_copy(rows_v, xs_hbm.at[idx_v])

    return run(*streams, pos)


def _sc_combine_call(ys, pos):
    _, d = ys.shape
    n = pos.shape[0]
    per_w = n // SC_WORKERS
    ch = min(per_w, SC_ROW_BYTES // (4 * d))
    mesh = plsc.VectorSubcoreMesh(core_axis_name="c", subcore_axis_name="s")

    @functools.partial(pl.kernel, mesh=mesh, out_type=jax.ShapeDtypeStruct((n, d), jnp.int32),
                       scratch_types=[pltpu.VMEM((ch,), jnp.int32), pltpu.VMEM((ch, d), jnp.int32)],
                       name="sc_combine")
    def run(ys_hbm, pos_hbm, out_hbm, idx_v, rows_v):
        first = _sc_worker() * per_w

        @pl.loop(0, per_w // ch)
        def _(j):
            base = first + j * ch
            pltpu.sync_copy(pos_hbm.at[pl.ds(base, ch)], idx_v)
            pltpu.sync_copy(ys_hbm.at[idx_v], rows_v)
            pltpu.sync_copy(rows_v, out_hbm.at[pl.ds(base, ch)])

    return run(ys, pos)


def _moe_call(streams, idx, w_gate, w_up, w_down, layer):
    t = idx.shape[1]
    tile = min(MOE_TILE, max(t // N_EXPERTS, 128))
    n_tiles = (2 * t) // tile + N_EXPERTS
    assert n_tiles <= LANES
    pos, te, tn, nu = _plan_call(idx, tile, n_tiles)
    xs = _sc_dispatch_call(streams, pos, n_tiles * tile)
    hid = _moe_up_call(xs, w_gate, w_up, layer, te, tn, nu, tile)
    ys = _moe_down_call(hid, w_down, layer, te, tn, nu, tile)
    return _sc_combine_call(ys, pos).reshape(2, t, ys.shape[1])


def _moe_resid_kernel(x_ref, y0_ref, y1_ref, gt_ref, g_ref, *rest, final):
    gt = gt_ref[0]
    halves0 = _unpack_rows(y0_ref[0])
    halves1 = _unpack_rows(y1_ref[0])
    y = jnp.concatenate([gt[:, 0:1] * a + gt[:, 1:2] * b for a, b in zip(halves0, halves1)], axis=1)
    x = x_ref[0] + g_ref[0] * y
    if final:
        fg_ref, o_ref = rest
        x = _rms(x, fg_ref[...])
    else:
        (o_ref,) = rest
    o_ref[0] = x


def _moe_resid_call(x, y, first_row, gates, gate, final_g=None):
    b, l, d = x.shape
    tl = min(l, 512)
    nt = l // tl
    tile = pl.BlockSpec((1, tl, d), lambda i, j: (i, j, 0))
    choice = lambda k: pl.BlockSpec((1, tl, d // 2), lambda i, j: (k, first_row // tl + i * nt + j, 0))
    in_specs = [tile, choice(0), choice(1), pl.BlockSpec((1, tl, 2), lambda i, j: (i, j, 0)),
                pl.BlockSpec((1, 1, d), lambda i, j: (i, 0, 0))]
    args = [x, y, y, gates.T.reshape(b, l, 2), gate]
    if final_g is not None:
        in_specs.append(pl.BlockSpec((1, d), lambda i, j: (0, 0)))
        args.append(final_g.reshape(1, d))
    return pl.pallas_call(
        functools.partial(_moe_resid_kernel, final=final_g is not None),
        grid=(b, l // tl),
        in_specs=in_specs,
        out_specs=tile,
        out_shape=jax.ShapeDtypeStruct((b, l, d), F32),
        compiler_params=_cp("parallel", "parallel"),
        name="moe_residual",
    )(*args)


def _rope_perm(w):
    r = w.reshape(w.shape[:-1] + (2, 2, ROPE_FREQS))
    return jnp.stack([-r[..., 1, :], r[..., 0, :]], axis=-2).reshape(w.shape)


def _rope_table(l, rotary):
    if not rotary:
        return jnp.concatenate([jnp.ones((l, MLA_ROPE), F32), jnp.zeros((l, MLA_ROPE), F32)], axis=1)
    rows = l // GRID_W
    r = jnp.repeat(jnp.arange(rows), GRID_W)
    col = jnp.tile(jnp.arange(GRID_W), rows)
    pos = jnp.stack([r, col], axis=-1).astype(F32)
    inv = ROPE_THETA ** (-jnp.arange(ROPE_FREQS, dtype=F32) / ROPE_FREQS)
    ang = pos[:, :, None] * inv
    lay = lambda a: jnp.concatenate([a[:, 0], a[:, 0], a[:, 1], a[:, 1]], axis=1)
    return jnp.concatenate([lay(jnp.cos(ang)), lay(jnp.sin(ang))], axis=1)


def _layer_weights(w_in, w_uq, w_ukv, w_out):
    d = w_in.shape[0]
    kr = w_in[:, OFF_KR:OFF_HY]
    w_kv_in = jnp.concatenate([w_in[:, OFF_KV:OFF_KR], kr, _rope_perm(kr)], axis=1)
    uq = w_uq.reshape(MLA_Q_RANK, MLA_HEADS, MLA_NOPE + MLA_ROPE)
    w_q = jnp.concatenate([uq, _rope_perm(uq[:, :, MLA_NOPE:])], axis=2).reshape(MLA_Q_RANK, MLA_HEADS * QK_HEAD)
    ukv = w_ukv.reshape(MLA_KV_RANK, MLA_HEADS, MLA_NOPE + MLA_V_DIM)
    w_k = ukv[:, :, :MLA_NOPE].reshape(MLA_KV_RANK, MLA_HEADS * MLA_NOPE)
    w_v = ukv[:, :, MLA_NOPE:].reshape(MLA_KV_RANK, MLA_HEADS * MLA_V_DIM)
    cast = lambda a: a.astype(BF16)
    w_all = jnp.concatenate([w_in[:, :OFF_KV], w_in[:, OFF_HY:], w_kv_in], axis=1)
    return dict(all=cast(w_all), kv=cast(w_kv_in), uq=cast(w_q), uk=cast(w_k), uv=cast(w_v), out=cast(w_out))


def _router_weights(w_router):
    hi = w_router.astype(BF16)
    lo = (w_router - hi.astype(F32)).astype(BF16)
    return jnp.concatenate([hi.T, lo.T], axis=0)


def _mixer(x, mod, w, lp, cs, spec, fwd, inv, kv_extra, full, wr, rbias):
    b, l, d = x.shape
    kv_args = (lp["kv_norm_g"], w["uk"], w["uv"], cs)
    if not full:
        k, v = _in_proj_call(x, lp["norm1_g"], mod[0], mod[1], w["kv"], [], *kv_args)
        return None, k, v
    splits = [OFF_Q, MLA_Q_RANK, (HY_ORDER + 1) * HYENA_W]
    p_conv, p_q, p_hy, k, v = _in_proj_call(x, lp["norm1_g"], mod[0], mod[1], w["all"], splits, *kv_args)
    conv_o = _conformer_call(p_conv, lp["conv_dw_w"], lp["conv_dw_b"], lp["conv_ln_g"], lp["conv_ln_b"])
    att_o = _attn_call(p_q, lp["q_norm_g"], w["uq"], cs, kv_extra + [(k, v)])
    hy_o = _hyena_call(p_hy, lp["hy_short_w"], lp["hy_short_b"], spec, lp["hy_bias"], fwd, inv)
    routed = _mix_out_call(conv_o, att_o, hy_o, w["out"], x, mod, lp["norm2_g"], wr, rbias)
    return routed, k, v


def _channel(streams, experts, layer, final_g=None):
    d = streams[0][1][0].shape[2]
    idx = jnp.concatenate([r[2] for _, r in streams], axis=1)
    y = _moe_call([r[1].reshape(-1, d // 2) for _, r in streams], idx, *experts, layer)
    outs = []
    first_row = 0
    for mod, (x, _, _, gates) in streams:
        outs.append(_moe_resid_call(x, y, first_row, gates, mod[5], final_g))
        first_row += x.shape[0] * x.shape[1]
    return outs


def _filter_spectrum(l, lp, fwd):
    a = _hy_filter_call(l, lp["hy_w1"], lp["hy_b1"], lp["hy_freq1"], lp["hy_w2"], lp["hy_b2"], lp["hy_freq2"],
                        lp["hy_w3"], lp["hy_b3"])
    return _spec_call(fwd, a)


def kernel(x, c, ctx, c_ctx, norm1_g, norm2_g, w_ada, b_ada, w_in, conv_dw_w, conv_dw_b, conv_ln_g, conv_ln_b, q_norm_g, w_uq, kv_norm_g, w_ukv, hy_short_w, hy_short_b, hy_w1, hy_b1, hy_freq1, hy_w2, hy_b2, hy_freq2, hy_w3, hy_b3, hy_bias, w_out, w_router, router_bias, w_gate, w_up, w_down, final_norm_g):
    b, l, d = x.shape
    lc = ctx.shape[1]
    depth = w_in.shape[0]
    per_layer = dict(norm1_g=norm1_g, norm2_g=norm2_g, conv_dw_w=conv_dw_w, conv_dw_b=conv_dw_b,
                     conv_ln_g=conv_ln_g, conv_ln_b=conv_ln_b, q_norm_g=q_norm_g, kv_norm_g=kv_norm_g,
                     hy_short_w=hy_short_w, hy_short_b=hy_short_b, hy_w1=hy_w1, hy_b1=hy_b1, hy_freq1=hy_freq1,
                     hy_w2=hy_w2, hy_b2=hy_b2, hy_freq2=hy_freq2, hy_w3=hy_w3, hy_b3=hy_b3, hy_bias=hy_bias)
    experts = (w_gate, w_up, w_down)
    cs_l = _rope_table(l, True)
    cs_c = _rope_table(lc, False)
    fwd_l, inv_l = _dft_mats(l)
    fwd_c, inv_c = _dft_mats(lc)
    wr = _router_weights(w_router)
    rows = 16
    cvec = jnp.concatenate([c, c_ctx[None, :], jnp.zeros((rows - b - 1, d), F32)], axis=0)
    xl, xc = x, ctx
    for i in range(depth):
        last = i == depth - 1
        lp = {k: v[i] for k, v in per_layer.items()}
        w = _layer_weights(w_in[i], w_uq[i], w_ukv[i], w_out[i])
        m = _ada_call(cvec, w_ada, b_ada, i)
        mod_l = [m[:b, None, j * d:(j + 1) * d] for j in range(6)]
        mod_c = [jnp.broadcast_to(m[b:b + 1, None, j * d:(j + 1) * d], (b, 1, d)) for j in range(6)]
        spec_l = _filter_spectrum(l, lp, fwd_l)
        spec_c = None if last else _filter_spectrum(lc, lp, fwd_c)
        routed_c, kc, vc = _mixer(xc, mod_c, w, lp, cs_c, spec_c, fwd_c, inv_c, [], not last, wr, router_bias)
        routed_l, _, _ = _mixer(xl, mod_l, w, lp, cs_l, spec_l, fwd_l, inv_l, [(kc, vc)], True, wr, router_bias)
        if last:
            (xl,) = _channel([(mod_l, routed_l)], experts, i, final_norm_g)
        else:
            xl, xc = _channel([(mod_l, routed_l), (mod_c, routed_c)], experts, i)
    return xl
```

```python
import functools
import math

import jax
import jax.numpy as jnp
from jax import lax
from jax.experimental import pallas as pl
from jax.experimental.pallas import tpu as pltpu
from jax.experimental.pallas import tpu_sc as plsc

F32 = jnp.float32
BF16 = jnp.bfloat16

EPS = 1e-6
GRID_W = 64
CONV_W = 512
CONV_KSIZE = 31
MLA_HEADS = 8
MLA_NOPE = 128
MLA_ROPE = 64
MLA_V_DIM = 128
MLA_Q_RANK = 768
MLA_KV_RANK = 512
ROPE_FREQS = MLA_ROPE // 4
ROPE_THETA = 10000.0
MLA_SCALE = (MLA_NOPE + MLA_ROPE) ** -0.5
Q_SCALE = MLA_SCALE * math.log2(math.e)
HYENA_W = 512
HY_ORDER = 2
HY_SHORT = 3
HY_EMB = 33
HY_BANDS = (HY_EMB - 1) // 2
HY_FFN = 64
HY_MIN_DECAY = math.log(1e-2) / 1.5
HY_MAX_DECAY = math.log(1e-2) / 0.3
N_EXPERTS = 16
N_GROUPS = 4
EXPERTS_PER_GROUP = N_EXPERTS // N_GROUPS
OFF_Q = 2 * CONV_W
OFF_KV = OFF_Q + MLA_Q_RANK
OFF_KR = OFF_KV + MLA_KV_RANK
OFF_HY = OFF_KR + MLA_ROPE

LANES = 128
QK_HEAD = 2 * LANES
V_HEAD = MLA_V_DIM + LANES
V7X_VMEM_BYTES = 64 * 1024 * 1024
VMEM_LIMIT = (V7X_VMEM_BYTES * 7) // 8
MOE_TILE = 512
NEG_INF = float("-inf")


def _cp(*sem):
    return pltpu.CompilerParams(dimension_semantics=sem, vmem_limit_bytes=VMEM_LIMIT)


def _dot(a, b):
    return jnp.dot(a, b, preferred_element_type=F32)


def _rms(x, g):
    ms = jnp.mean(x * x, axis=-1, keepdims=True)
    return x * lax.rsqrt(ms + EPS) * g


def _ada_kernel(c_ref, w_ref, b_ref, o_ref):
    c = c_ref[...]
    a = (c * jax.nn.sigmoid(c)).astype(BF16)
    o_ref[...] = _dot(a, w_ref[0].astype(BF16)) + b_ref[0]


def _ada_call(cvec, w, b, layer):
    m, d = cvec.shape
    n = w.shape[2]
    tn = 1024
    return pl.pallas_call(
        _ada_kernel,
        grid=(n // tn,),
        in_specs=[pl.BlockSpec((m, d), lambda j: (0, 0)),
                  pl.BlockSpec((1, d, tn), lambda j: (layer, 0, j)),
                  pl.BlockSpec((1, 1, tn), lambda j: (layer, 0, j))],
        out_specs=pl.BlockSpec((m, tn), lambda j: (0, j)),
        out_shape=jax.ShapeDtypeStruct((m, n), F32),
        compiler_params=_cp("parallel"),
        name="ada",
    )(cvec, w, b.reshape(b.shape[0], 1, n))


def _in_proj_kernel(x_ref, g_ref, sh_ref, sc_ref, w_ref, kg_ref, wk_ref, wv_ref, cs_ref, *o_refs):
    *p_refs, k_ref, v_ref = o_refs
    y = _rms(x_ref[0], g_ref[...])
    h = (y * (1.0 + sc_ref[0]) + sh_ref[0]).astype(BF16)
    off = 0
    for o_ref in p_refs:
        n = o_ref.shape[-1]
        o_ref[0] = _dot(h, w_ref[:, off:off + n]).astype(o_ref.dtype)
        off += n
    p = _dot(h, w_ref[:, off:])
    xn = _rms(p[:, :MLA_KV_RANK], kg_ref[...]).astype(BF16)
    kk = _dot(xn, wk_ref[...])
    vv = _dot(xn, wv_ref[...])
    rot = _rotate(p[:, MLA_KV_RANK:], cs_ref[...])
    lane = lax.broadcasted_iota(jnp.int32, rot.shape, 1)
    rot = jnp.where(lane < MLA_ROPE, rot, 0.0).astype(BF16)
    ones = jnp.where(lane == 0, 1.0, 0.0).astype(BF16)
    for hd in range(MLA_HEADS):
        k_ref[0, :, hd * QK_HEAD:hd * QK_HEAD + LANES] = kk[:, hd * LANES:(hd + 1) * LANES].astype(BF16)
        k_ref[0, :, hd * QK_HEAD + LANES:(hd + 1) * QK_HEAD] = rot
        v_ref[0, :, hd * V_HEAD:hd * V_HEAD + MLA_V_DIM] = vv[:, hd * MLA_V_DIM:(hd + 1) * MLA_V_DIM].astype(BF16)
        v_ref[0, :, hd * V_HEAD + MLA_V_DIM:(hd + 1) * V_HEAD] = ones


def _in_proj_call(x, g, shift, scale, w, splits, kv_g, w_k, w_v, cs):
    b, l, d = x.shape
    n = w.shape[1]
    tm = min(l, 512)
    r = MLA_KV_RANK
    row = pl.BlockSpec((1, 1, d), lambda i, j: (i, 0, 0))
    const = lambda shape: pl.BlockSpec(shape, lambda i, j: (0, 0))
    widths = list(splits) + [MLA_HEADS * QK_HEAD, MLA_HEADS * V_HEAD]
    return pl.pallas_call(
        _in_proj_kernel,
        grid=(b, l // tm),
        in_specs=[pl.BlockSpec((1, tm, d), lambda i, j: (i, j, 0)),
                  const((1, d)), row, row,
                  pl.BlockSpec((d, n), lambda i, j: (0, 0), pipeline_mode=pl.Buffered(1)),
                  const((1, r)), const(w_k.shape), const(w_v.shape),
                  pl.BlockSpec((tm, LANES), lambda i, j: (j, 0))],
        out_specs=[pl.BlockSpec((1, tm, k), lambda i, j: (i, j, 0)) for k in widths],
        out_shape=[jax.ShapeDtypeStruct((b, l, k), BF16) for k in widths],
        compiler_params=_cp("parallel", "parallel"),
        name="in_proj",
    )(x, g.reshape(1, d), shift, scale, w, kv_g.reshape(1, r), w_k, w_v, cs)


CONV_ROWS = 128
CONV_PAD = 16


def _conformer_kernel(p_ref, w_ref, b_ref, g_ref, be_ref, o_ref, zp_ref):
    l = p_ref.shape[1]
    c = CONV_W
    zeros = jnp.zeros((CONV_PAD, c), F32)
    zp_ref[0:CONV_PAD, :] = zeros
    zp_ref[CONV_PAD + l:CONV_PAD + l + CONV_PAD, :] = zeros

    def glu(r, carry):
        base = pl.multiple_of(r * CONV_ROWS, CONV_ROWS)
        blk = p_ref[0, pl.ds(base, CONV_ROWS), :].astype(F32)
        zp_ref[pl.ds(base + CONV_PAD, CONV_ROWS), :] = blk[:, :c] * jax.nn.sigmoid(blk[:, c:])
        return carry

    lax.fori_loop(0, l // CONV_ROWS, glu, 0)

    win_rows = CONV_ROWS + 2 * CONV_PAD
    first = CONV_PAD - CONV_KSIZE // 2

    def conv(r, carry):
        base = pl.multiple_of(r * CONV_ROWS, CONV_ROWS)
        parts = []
        for cc in range(c // LANES):
            lanes = slice(cc * LANES, (cc + 1) * LANES)
            win = zp_ref[pl.ds(base, win_rows), lanes]
            acc = jnp.zeros((CONV_ROWS, LANES), F32)
            for res in range(8):
                taps = [k for k in range(CONV_KSIZE) if (k + first) % 8 == res]
                if not taps:
                    continue
                shifted = pltpu.roll(win, win_rows - res, axis=0) if res else win
                for k in taps:
                    off = k + first - res
                    acc = acc + shifted[off:off + CONV_ROWS, :] * w_ref[k:k + 1, lanes]
            parts.append(acc)
        z = jnp.concatenate(parts, axis=1) + b_ref[...]
        mu = jnp.mean(z, axis=-1, keepdims=True)
        zc = z - mu
        var = jnp.mean(zc * zc, axis=-1, keepdims=True)
        y = zc * lax.rsqrt(var + EPS) * g_ref[...] + be_ref[...]
        o_ref[0, pl.ds(base, CONV_ROWS), :] = (y * jax.nn.sigmoid(y)).astype(o_ref.dtype)
        return carry

    lax.fori_loop(0, l // CONV_ROWS, conv, 0)


def _conformer_call(p, dw_w, dw_b, ln_g, ln_b):
    b, l, c2 = p.shape
    c = c2 // 2
    vec = pl.BlockSpec((1, c), lambda i: (0, 0))
    return pl.pallas_call(
        _conformer_kernel,
        grid=(b,),
        in_specs=[pl.BlockSpec((1, l, c2), lambda i: (i, 0, 0)),
                  pl.BlockSpec((CONV_KSIZE, c), lambda i: (0, 0)), vec, vec, vec],
        out_specs=pl.BlockSpec((1, l, c), lambda i: (i, 0, 0)),
        out_shape=jax.ShapeDtypeStruct((b, l, c), BF16),
        scratch_shapes=[pltpu.VMEM((l + 2 * CONV_PAD, c), F32)],
        compiler_params=_cp("parallel"),
        name="conformer",
    )(p, dw_w, dw_b.reshape(1, c), ln_g.reshape(1, c), ln_b.reshape(1, c))


def _rotate(group, cs):
    t = group * cs
    return t + pltpu.roll(t, MLA_ROPE, axis=1)


def _attn_kernel(p_ref, g_ref, w_ref, cs_ref, *refs, n_seg):
    o_ref, q_ref = refs[2 * n_seg], refs[2 * n_seg + 1]
    xn = _rms(p_ref[0].astype(F32), g_ref[...]).astype(BF16)
    cs = cs_ref[...] * Q_SCALE
    for h in range(MLA_HEADS):
        r = _dot(xn, w_ref[:, h * QK_HEAD:(h + 1) * QK_HEAD])
        q_ref[:, h * QK_HEAD:h * QK_HEAD + LANES] = (r[:, :LANES] * Q_SCALE).astype(BF16)
        q_ref[:, h * QK_HEAD + LANES:(h + 1) * QK_HEAD] = _rotate(r[:, LANES:], cs).astype(BF16)
    nt = (((1,), (1,)), ((), ()))
    for h in range(MLA_HEADS):
        qk = slice(h * QK_HEAD, (h + 1) * QK_HEAD)
        vd = slice(h * MLA_V_DIM, (h + 1) * MLA_V_DIM)
        q = q_ref[:, qk]
        scores = [lax.dot_general(q, refs[2 * s][0, :, qk], nt, preferred_element_type=F32)
                  for s in range(n_seg)]
        m = functools.reduce(jnp.maximum, [jnp.max(s, axis=-1, keepdims=True) for s in scores])
        acc = 0.0
        for s in range(n_seg):
            p = jnp.exp2(scores[s] - m).astype(BF16)
            acc = acc + _dot(p, refs[2 * s + 1][0, :, h * V_HEAD:(h + 1) * V_HEAD])
        o_ref[0, :, vd] = (acc[:, :MLA_V_DIM] * (1.0 / acc[:, MLA_V_DIM:MLA_V_DIM + 1])).astype(o_ref.dtype)


def _attn_call(p_q, q_g, w_q, cs, segs):
    b, l, r = p_q.shape
    tq = min(l, 512)
    const = lambda shape: pl.BlockSpec(shape, lambda i, j: (0, 0))
    in_specs = [pl.BlockSpec((1, tq, r), lambda i, j: (i, j, 0)), const((1, r)), const(w_q.shape),
                pl.BlockSpec((tq, LANES), lambda i, j: (j, 0))]
    args = [p_q, q_g.reshape(1, r), w_q, cs]
    for k, v in segs:
        in_specs.append(pl.BlockSpec((1,) + k.shape[1:], lambda i, j: (i, 0, 0), pipeline_mode=pl.Buffered(1)))
        in_specs.append(pl.BlockSpec((1,) + v.shape[1:], lambda i, j: (i, 0, 0), pipeline_mode=pl.Buffered(1)))
        args += [k, v]
    return pl.pallas_call(
        functools.partial(_attn_kernel, n_seg=len(segs)),
        grid=(b, l // tq),
        in_specs=in_specs,
        out_specs=pl.BlockSpec((1, tq, MLA_HEADS * MLA_V_DIM), lambda i, j: (i, j, 0)),
        out_shape=jax.ShapeDtypeStruct((b, l, MLA_HEADS * MLA_V_DIM), BF16),
        scratch_shapes=[pltpu.VMEM((tq, MLA_HEADS * QK_HEAD), BF16)],
        compiler_params=_cp("parallel", "parallel"),
        name="attention",
    )(*args)


HY_PAD = 8
HY_ROWS = 128


def _hy_short_kernel(p_ref, w_ref, b_ref, o_ref, up_ref):
    l = p_ref.shape[1]
    c = p_ref.shape[2]
    zeros = jnp.zeros((HY_PAD, c), F32)
    up_ref[0:HY_PAD, :] = zeros
    up_ref[HY_PAD + l:HY_PAD + l + HY_PAD, :] = zeros

    def copy(r, carry):
        base = pl.multiple_of(r * HY_ROWS, HY_ROWS)
        up_ref[pl.ds(base + HY_PAD, HY_ROWS), :] = p_ref[0, pl.ds(base, HY_ROWS), :].astype(F32)
        return carry

    lax.fori_loop(0, l // HY_ROWS, copy, 0)
    first = HY_PAD - HY_SHORT // 2

    def conv(r, carry):
        base = pl.multiple_of(r * HY_ROWS, HY_ROWS)
        for cc in range(c // LANES):
            lanes = slice(cc * LANES, (cc + 1) * LANES)
            win = up_ref[pl.ds(base, HY_ROWS + 2 * HY_PAD), lanes]
            acc = jnp.zeros((HY_ROWS, LANES), F32) + b_ref[:, lanes]
            for k in range(HY_SHORT):
                acc = acc + win[first + k:first + k + HY_ROWS, :] * w_ref[k:k + 1, lanes]
            o_ref[0, 0, pl.ds(base, HY_ROWS), lanes] = acc.astype(o_ref.dtype)
        return carry

    lax.fori_loop(0, l // HY_ROWS, conv, 0)


def _hy_short_call(p, w, bias):
    b, l, c3 = p.shape
    c = HYENA_W
    return pl.pallas_call(
        _hy_short_kernel,
        grid=(b, c3 // c),
        in_specs=[pl.BlockSpec((1, l, c), lambda i, g: (i, 0, g)),
                  pl.BlockSpec((HY_SHORT, c), lambda i, g: (0, g)),
                  pl.BlockSpec((1, c), lambda i, g: (0, g))],
        out_specs=pl.BlockSpec((1, 1, l, c), lambda i, g: (g, i, 0, 0)),
        out_shape=jax.ShapeDtypeStruct((c3 // c, b, l, c), BF16),
        scratch_shapes=[pltpu.VMEM((l + 2 * HY_PAD, c), F32)],
        compiler_params=_cp("parallel", "parallel"),
        name="hyena_short",
    )(p, w, bias.reshape(1, c3))


def _hy_filter_kernel(z_ref, w1_ref, b1_ref, f1_ref, w2_ref, b2_ref, f2_ref, w3_ref, b3_ref, win_ref, o_ref):
    hp = lax.Precision.HIGHEST

    def mm(a, b):
        return jnp.dot(a, b, preferred_element_type=F32, precision=hp)

    h = jnp.sin(f1_ref[...] * (mm(z_ref[...], w1_ref[...]) + b1_ref[...]))
    h = jnp.sin(f2_ref[...] * (mm(h, w2_ref[...]) + b2_ref[...]))
    h = mm(h, w3_ref[...]) + b3_ref[...]
    c = HYENA_W
    win = win_ref[...]
    tl = h.shape[0]
    row = lax.broadcasted_iota(jnp.int32, (tl, c), 0) + pl.program_id(0) * tl
    for o in range(HY_ORDER):
        fwd = h[:, (2 * o) * c:(2 * o + 1) * c] * win
        bwd = jnp.where(row == 0, 0.0, h[:, (2 * o + 1) * c:(2 * o + 2) * c] * win)
        o_ref[:, o * c:(o + 1) * c] = (fwd + bwd).astype(o_ref.dtype)
        o_ref[:, (HY_ORDER + o) * c:(HY_ORDER + o + 1) * c] = (fwd - bwd).astype(o_ref.dtype)


def _pad2(a, rows, cols):
    return jnp.pad(a, ((0, rows - a.shape[0]), (0, cols - a.shape[1])))


def _hy_filter_call(l, w1, b1, f1, w2, b2, f2, w3, b3):
    t = jnp.arange(l, dtype=F32)
    t_norm = t / max(l - 1, 1)
    ang = (2.0 * math.pi / l) * t[:, None] * jnp.linspace(1e-4, HY_BANDS - 1, HY_BANDS, dtype=F32)[None, :]
    z = jnp.concatenate([t_norm[:, None], jnp.cos(ang), -jnp.sin(ang)], axis=-1)
    deltas = jnp.abs(jnp.linspace(HY_MIN_DECAY, HY_MAX_DECAY, HYENA_W, dtype=F32))
    window = jnp.exp(-t_norm[:, None] * deltas[None, :])
    n3 = w3.shape[1]
    tl = min(l, 256)
    full = lambda shape: pl.BlockSpec(shape, lambda i: (0, 0))
    return pl.pallas_call(
        _hy_filter_kernel,
        grid=(l // tl,),
        in_specs=[pl.BlockSpec((tl, LANES), lambda i: (i, 0)),
                  full((LANES, LANES)), full((1, LANES)), full((1, LANES)),
                  full((LANES, LANES)), full((1, LANES)), full((1, LANES)),
                  full((LANES, n3)), full((1, n3)),
                  pl.BlockSpec((tl, HYENA_W), lambda i: (i, 0))],
        out_specs=pl.BlockSpec((tl, n3), lambda i: (i, 0)),
        out_shape=jax.ShapeDtypeStruct((l, n3), BF16),
        compiler_params=_cp("parallel"),
        name="hyena_filter",
    )(_pad2(z, l, LANES), _pad2(w1, LANES, LANES), _pad2(b1[None], 1, LANES), _pad2(f1[None], 1, LANES),
      _pad2(w2, LANES, LANES), _pad2(b2[None], 1, LANES), _pad2(f2[None], 1, LANES),
      _pad2(w3, LANES, n3), b3[None], window)


def _dft_tables_kernel(ac_ref, as_ref, bc_ref, bs_ref, fwd_ref, inv_ref):
    tm, l = fwd_ref.shape[1], fwd_ref.shape[2]
    n = 2 * l
    row = lax.broadcasted_iota(jnp.int32, (tm, LANES), 0) + pl.program_id(0) * tm
    lane = lax.broadcasted_iota(jnp.int32, (tm, LANES), 1)
    row_sign = jnp.where((row & 1) == 0, 1.0, -1.0)
    lane_sign = jnp.where((lane & 1) == 0, 1.0, -1.0)
    bc = bc_ref[...]
    bs = bs_ref[...]
    for a in range(l // LANES):
        cols = slice(a * LANES, (a + 1) * LANES)
        ca = ac_ref[:, a:a + 1]
        sa = as_ref[:, a:a + 1]
        c = ca * bc - sa * bs
        s = sa * bc + ca * bs
        fwd_ref[0, :, cols] = c.astype(BF16)
        fwd_ref[1, :, cols] = jnp.where(row == 0, lane_sign, -s).astype(BF16)
        inv_c = c * (2.0 / n)
        inv_s = -s * (2.0 / n)
        if a == 0:
            inv_c = jnp.where(lane == 0, 1.0 / n, inv_c)
            inv_s = jnp.where(lane == 0, row_sign / n, inv_s)
        inv_ref[:, cols] = inv_c.astype(BF16)
        inv_ref[:, l + a * LANES:l + (a + 1) * LANES] = inv_s.astype(BF16)


def _dft_mats(l):
    n = 2 * l
    f = jnp.arange(l, dtype=jnp.int32)[:, None]

    def table(t):
        ang = ((f * t[None, :]) % n).astype(F32) * (2.0 * math.pi / n)
        return jnp.cos(ang), jnp.sin(ang)

    coarse = table(jnp.arange(l // LANES, dtype=jnp.int32) * LANES)
    fine = table(jnp.arange(LANES, dtype=jnp.int32))
    tm = min(l, 256)
    small = pl.BlockSpec((tm, l // LANES), lambda i: (i, 0))
    wide = pl.BlockSpec((tm, LANES), lambda i: (i, 0))
    return pl.pallas_call(
        _dft_tables_kernel,
        grid=(l // tm,),
        in_specs=[small, small, wide, wide],
        out_specs=[pl.BlockSpec((2, tm, l), lambda i: (0, i, 0)), pl.BlockSpec((tm, n), lambda i: (i, 0))],
        out_shape=[jax.ShapeDtypeStruct((2, l, l), BF16), jax.ShapeDtypeStruct((l, n), BF16)],
        compiler_params=_cp("parallel"),
        name="dft_tables",
    )(*coarse, *fine)


def _spec_kernel(f_ref, a_ref, o_ref):
    half = a_ref.shape[1] // 2
    hi = _dot(f_ref[1], a_ref[:, half:])
    o_ref[0] = _dot(f_ref[0], a_ref[:, :half])
    o_ref[1] = hi

    @pl.when(pl.program_id(0) == 0)
    def _():
        nyq = _dot(f_ref[1, 0:16, :], a_ref[:, :half])
        row = lax.broadcasted_iota(jnp.int32, nyq.shape, 0)
        o_ref[1, 0:16, :] = jnp.where(row == 0, nyq, hi[0:16])


def _spec_call(fwd, a):
    _, lf, l = fwd.shape
    n = a.shape[1] // 2
    tm = min(lf, 512)
    return pl.pallas_call(
        _spec_kernel,
        grid=(lf // tm,),
        in_specs=[pl.BlockSpec((2, tm, l), lambda i: (0, i, 0)),
                  pl.BlockSpec(a.shape, lambda i: (0, 0))],
        out_specs=pl.BlockSpec((2, tm, n), lambda i: (0, i, 0)),
        out_shape=jax.ShapeDtypeStruct((2, lf, n), F32),
        compiler_params=_cp("arbitrary"),
        name="hyena_spectrum",
    )(fwd, a)


def _dft_fwd_kernel(f_ref, u_ref, h_ref, y_ref):
    u = u_ref[0, 0].astype(BF16)
    xr = _dot(f_ref[0], u)
    xi = _dot(f_ref[1], u)
    hr = h_ref[0]
    hi = h_ref[1]
    yr = xr * hr - xi * hi
    yi = xr * hi + xi * hr
    y_ref[0, 0] = yr.astype(BF16)
    y_ref[0, 1] = yi.astype(BF16)

    @pl.when(pl.program_id(0) == 0)
    def _():
        row = lax.broadcasted_iota(jnp.int32, (16, u.shape[1]), 0)
        y_ref[0, 0, 0:16, :] = jnp.where(row == 0, xr[0:16] * hr[0:16], yr[0:16]).astype(BF16)
        y_ref[0, 1, 0:16, :] = jnp.where(row == 0, xi[0:16] * hi[0:16], yi[0:16]).astype(BF16)


def _dft_fwd_call(fwd, u, group, spec, order):
    _, b, l, c = u.shape
    lf = fwd.shape[1]
    tm = min(lf, 1024)
    return pl.pallas_call(
        _dft_fwd_kernel,
        grid=(lf // tm, b),
        in_specs=[pl.BlockSpec((2, tm, l), lambda i, j: (0, i, 0)),
                  pl.BlockSpec((1, 1, l, c), lambda i, j: (group, j, 0, 0)),
                  pl.BlockSpec((2, tm, c), lambda i, j: (0, i, order))],
        out_specs=pl.BlockSpec((1, 2, tm, c), lambda i, j: (j, 0, i, 0)),
        out_shape=jax.ShapeDtypeStruct((b, 2, lf, c), BF16),
        compiler_params=_cp("arbitrary", "arbitrary"),
        name="hyena_dft_fwd",
    )(fwd, u, spec)


def _dft_inv_kernel(g_ref, y_ref, u_ref, b_ref, x_ref, o_ref):
    r = _dot(g_ref[...], y_ref[0]) + u_ref[0, 0].astype(F32) * b_ref[...]
    o_ref[0, 0] = (x_ref[0, 0] * r).astype(o_ref.dtype)


def _dft_inv_call(inv, y, u, u_group, bias, gate, gate_group):
    _, b, l, c = u.shape
    n = inv.shape[1]
    tm = min(l, 1024)
    y = y.reshape(b, n, c)
    row = lambda g: pl.BlockSpec((1, 1, tm, c), lambda i, j: (g, j, i, 0))
    return pl.pallas_call(
        _dft_inv_kernel,
        grid=(l // tm, b),
        in_specs=[pl.BlockSpec((tm, n), lambda i, j: (i, 0)),
                  pl.BlockSpec((1, n, c), lambda i, j: (j, 0, 0)),
                  row(u_group), pl.BlockSpec((1, c), lambda i, j: (0, 0)), row(gate_group)],
        out_specs=row(0),
        out_shape=jax.ShapeDtypeStruct((1, b, l, c), BF16),
        compiler_params=_cp("parallel", "parallel"),
        name="hyena_dft_inv",
    )(inv, y, u, bias.reshape(1, c), gate)


def _hyena_call(p_hy, sh_w, sh_b, spec, hy_bias, fwd, inv):
    u3 = _hy_short_call(p_hy, sh_w, sh_b)
    z = _dft_inv_call(inv, _dft_fwd_call(fwd, u3, 0, spec, 0), u3, 0, hy_bias[0], u3, 1)
    return _dft_inv_call(inv, _dft_fwd_call(fwd, z, 0, spec, 1), z, 0, hy_bias[1], u3, 2)[0]


def _mix_out_kernel(c_ref, a_ref, h_ref, w1_ref, w2_ref, w3_ref, x_ref, g1_ref, g_ref, sh_ref, sc_ref,
                    wr_ref, rb_ref, o_ref, h32_ref, idx_ref, gate_ref):
    acc = _dot(c_ref[0], w1_ref[...]) + _dot(a_ref[0], w2_ref[...]) + _dot(h_ref[0], w3_ref[...])
    x = x_ref[0] + g1_ref[0] * acc
    o_ref[0] = x
    hm = _rms(x, g_ref[...]) * (1.0 + sc_ref[0]) + sh_ref[0]
    h32_ref[0] = _pack_rows(hm)
    picks, gates = _route(hm, wr_ref[...], rb_ref[...])
    idx_ref[...] = jnp.concatenate(picks, axis=0)
    gate_ref[...] = jnp.concatenate(gates, axis=0)


def _mix_out_call(conv_o, att_o, hy_o, w_out, x, mod, g2, wr, rbias):
    b, l, d = x.shape
    tm = min(l, 512)
    nt = l // tm
    k1, k2 = conv_o.shape[2], att_o.shape[2]
    w1, w2, w3 = w_out[:k1], w_out[k1:k1 + k2], w_out[k1 + k2:]
    act = lambda k: pl.BlockSpec((1, tm, k), lambda i, j: (i, j, 0))
    const = lambda shape: pl.BlockSpec(shape, lambda i, j: (0, 0), pipeline_mode=pl.Buffered(1))
    row = pl.BlockSpec((1, 1, d), lambda i, j: (i, 0, 0))
    tok = pl.BlockSpec((2, tm), lambda i, j: (0, i * nt + j))
    return pl.pallas_call(
        _mix_out_kernel,
        grid=(b, nt),
        in_specs=[act(k1), act(k2), act(w3.shape[0]), const(w1.shape), const(w2.shape), const(w3.shape),
                  act(d), row, pl.BlockSpec((1, d), lambda i, j: (0, 0)), row, row,
                  pl.BlockSpec(wr.shape, lambda i, j: (0, 0)),
                  pl.BlockSpec((N_EXPERTS, 1), lambda i, j: (0, 0))],
        out_specs=[act(d), act(d // 2), tok, tok],
        out_shape=[jax.ShapeDtypeStruct((b, l, d), F32),
                   jax.ShapeDtypeStruct((b, l, d // 2), jnp.int32),
                   jax.ShapeDtypeStruct((2, b * l), jnp.int32),
                   jax.ShapeDtypeStruct((2, b * l), F32)],
        compiler_params=_cp("parallel", "parallel"),
        name="mix_out",
    )(conv_o, att_o, hy_o, w1, w2, w3, x, mod[2], g2.reshape(1, d), mod[3], mod[4], wr,
      rbias.reshape(N_EXPERTS, 1))


def _route(hm, wr, rbias):
    hi = hm.astype(BF16)
    lo = (hm - hi.astype(F32)).astype(BF16)
    nt = (((1,), (1,)), ((), ()))
    r = (lax.dot_general(wr, hi, nt, preferred_element_type=F32)
         + lax.dot_general(wr, lo, nt, preferred_element_type=F32))
    logits = r[:N_EXPERTS] + r[N_EXPERTS:]
    score = jax.nn.sigmoid(logits)
    sel = score + rbias
    rows = [sel[e:e + 1] for e in range(N_EXPERTS)]
    gscore = []
    for g in range(N_GROUPS):
        a, b, c, d = rows[4 * g:4 * g + 4]
        h1, l1, h2, l2 = jnp.maximum(a, b), jnp.minimum(a, b), jnp.maximum(c, d), jnp.minimum(c, d)
        gscore.append(jnp.maximum(h1, h2) + jnp.maximum(jnp.minimum(h1, h2), jnp.maximum(l1, l2)))
    best = gscore[0]
    gbest = jnp.zeros(best.shape, jnp.int32)
    for g in range(1, N_GROUPS):
        upd = gscore[g] > best
        gbest = jnp.where(upd, g, gbest)
        best = jnp.where(upd, gscore[g], best)
    masked = [jnp.where(gbest == e // EXPERTS_PER_GROUP, rows[e], NEG_INF) for e in range(N_EXPERTS)]
    picks = []
    for _ in range(2):
        val = jnp.full(best.shape, NEG_INF, F32)
        idx = jnp.zeros(best.shape, jnp.int32)
        for e in range(N_EXPERTS):
            cand = masked[e]
            if picks:
                cand = jnp.where(picks[0] == e, NEG_INF, cand)
            upd = cand > val
            idx = jnp.where(upd, e, idx)
            val = jnp.where(upd, cand, val)
        picks.append(idx)
    gates = [functools.reduce(jnp.add, [jnp.where(p == e, score[e:e + 1], 0.0) for e in range(N_EXPERTS)])
             for p in picks]
    den = gates[0] + gates[1]
    return picks, [gates[0] / den, gates[1] / den]


def _expert_changed(te_ref, i):
    return (i == 0) | (te_ref[i] != te_ref[jnp.maximum(i - 1, 0)])


def _pack_rows(x):
    bits = pltpu.bitcast(x.astype(BF16).astype(F32), jnp.uint32)
    half = bits.shape[1] // 2
    return pltpu.bitcast((bits[:, :half] >> 16) | (bits[:, half:] & jnp.uint32(0xFFFF0000)), jnp.int32)


def _unpack_rows(words):
    w = pltpu.bitcast(words, jnp.uint32)
    return pltpu.bitcast(w << 16, F32), pltpu.bitcast(w & jnp.uint32(0xFFFF0000), F32)


def _moe_up_kernel(te_ref, tn_ref, nu_ref, x_ref, wg_hbm, wu_hbm, h_ref, gbuf, ubuf, wgb_ref, wub_ref, sems,
                   *, layer):
    j, i = pl.program_id(0), pl.program_id(1)
    tf = gbuf.shape[1]

    def fetch(e, chunk):
        cols = pl.ds(pl.multiple_of(chunk * tf, tf), tf)
        return (pltpu.make_async_copy(wg_hbm.at[layer, e, :, cols], gbuf, sems.at[0]),
                pltpu.make_async_copy(wu_hbm.at[layer, e, :, cols], ubuf, sems.at[1]))

    def start(e, chunk):
        for copy in fetch(e, chunk):
            copy.start()

    @pl.when((i == 0) & (j == 0))
    def _():
        start(te_ref[0], 0)

    @pl.when(_expert_changed(te_ref, i))
    def _():
        for copy in fetch(te_ref[i], j):
            copy.wait()
        wgb_ref[...] = gbuf[...].astype(BF16)
        wub_ref[...] = ubuf[...].astype(BF16)
        nxt = tn_ref[i]
        more = nxt < nu_ref[0]

        @pl.when(more)
        def _():
            start(te_ref[nxt], j)

        @pl.when(jnp.logical_not(more) & (j + 1 < pl.num_programs(0)))
        def _():
            start(te_ref[0], j + 1)

    @pl.when(i < nu_ref[0])
    def _():
        xa, xb = (v.astype(BF16) for v in _unpack_rows(x_ref[...]))
        half = xa.shape[1]
        a = _dot(xa, wgb_ref[:half, :]) + _dot(xb, wgb_ref[half:, :])
        u = _dot(xa, wub_ref[:half, :]) + _dot(xb, wub_ref[half:, :])
        h_ref[...] = (a * jax.nn.sigmoid(a) * u).astype(BF16)

    @pl.when(i >= nu_ref[0])
    def _():
        h_ref[...] = jnp.zeros(h_ref.shape, BF16)


def _moe_up_call(xs, w_gate, w_up, layer, te, tn, nu, tile):
    p, dw = xs.shape
    _, _, d, f = w_gate.shape
    tf = 512
    hbm = pl.BlockSpec(memory_space=pl.ANY)
    return pl.pallas_call(
        functools.partial(_moe_up_kernel, layer=layer),
        grid_spec=pltpu.PrefetchScalarGridSpec(
            num_scalar_prefetch=3,
            grid=(f // tf, p // tile),
            in_specs=[pl.BlockSpec((tile, dw), lambda j, i, te, tn, nu: (i, 0)), hbm, hbm],
            out_specs=pl.BlockSpec((tile, tf), lambda j, i, te, tn, nu: (i, j)),
            scratch_shapes=[pltpu.VMEM((d, tf), F32), pltpu.VMEM((d, tf), F32),
                            pltpu.VMEM((d, tf), BF16), pltpu.VMEM((d, tf), BF16),
                            pltpu.SemaphoreType.DMA((2,))]),
        out_shape=jax.ShapeDtypeStruct((p, f), BF16),
        compiler_params=_cp("arbitrary", "arbitrary"),
        name="moe_up",
    )(te, tn, nu, xs, w_gate, w_up)


def _moe_down_kernel(te_ref, tn_ref, nu_ref, h_ref, wd_hbm, y_ref, wbuf, wdb_ref, sem, *, layer):
    i = pl.program_id(0)

    def fetch(e):
        return pltpu.make_async_copy(wd_hbm.at[layer, e], wbuf, sem)

    @pl.when(i == 0)
    def _():
        fetch(te_ref[0]).start()

    @pl.when(_expert_changed(te_ref, i))
    def _():
        fetch(te_ref[i]).wait()
        wdb_ref[...] = wbuf[...].astype(BF16)
        nxt = tn_ref[i]

        @pl.when(nxt < nu_ref[0])
        def _():
            fetch(te_ref[nxt]).start()

    @pl.when(i < nu_ref[0])
    def _():
        y_ref[...] = _pack_rows(_dot(h_ref[...], wdb_ref[...]))

    @pl.when(i >= nu_ref[0])
    def _():
        y_ref[...] = jnp.zeros(y_ref.shape, jnp.int32)


def _moe_down_call(h, w_down, layer, te, tn, nu, tile):
    p, f = h.shape
    d = w_down.shape[3]
    return pl.pallas_call(
        functools.partial(_moe_down_kernel, layer=layer),
        grid_spec=pltpu.PrefetchScalarGridSpec(
            num_scalar_prefetch=3,
            grid=(p // tile,),
            in_specs=[pl.BlockSpec((tile, f), lambda i, te, tn, nu: (i, 0)),
                      pl.BlockSpec(memory_space=pl.ANY)],
            out_specs=pl.BlockSpec((tile, d // 2), lambda i, te, tn, nu: (i, 0)),
            scratch_shapes=[pltpu.VMEM((f, d), F32), pltpu.VMEM((f, d), BF16), pltpu.SemaphoreType.DMA(())]),
        out_shape=jax.ShapeDtypeStruct((p, d // 2), jnp.int32),
        compiler_params=_cp("arbitrary"),
        name="moe_down",
    )(te, tn, nu, h, w_down)


PLAN_BLOCK = 512


def _plan_kernel(idx_ref, pos_ref, te_ref, tn_ref, nu_ref, rank_ref, *, tile):
    t = idx_ref.shape[1]
    nb = t // PLAN_BLOCK
    experts = lax.broadcasted_iota(jnp.int32, (N_EXPERTS, PLAN_BLOCK), 0)
    r = lax.broadcasted_iota(jnp.int32, (PLAN_BLOCK, PLAN_BLOCK), 0)
    c = lax.broadcasted_iota(jnp.int32, (PLAN_BLOCK, PLAN_BLOCK), 1)
    before = jnp.where(r < c, 1.0, 0.0).astype(BF16)

    def rank_pass(k):
        def body(j, carry):
            cols = pl.ds(pl.multiple_of(j * PLAN_BLOCK, PLAN_BLOCK), PLAN_BLOCK)
            onehot = jnp.where(experts == idx_ref[k:k + 1, cols], 1.0, 0.0)
            earlier = _dot(onehot.astype(BF16), before) + carry
            rank_ref[k:k + 1, cols] = jnp.sum(onehot * earlier, axis=0, keepdims=True)
            return carry + jnp.sum(onehot, axis=1, keepdims=True)
        return body

    counts = jnp.zeros((N_EXPERTS, 1), F32)
    for k in range(2):
        counts = lax.fori_loop(0, nb, rank_pass(k), counts)
    padded = jnp.ceil(counts * (1.0 / tile)) * tile
    starts, ends = [], []
    acc = jnp.zeros((1, 1), F32)
    for e in range(N_EXPERTS):
        starts.append(acc)
        acc = acc + padded[e:e + 1]
        ends.append(acc)

    def place_pass(k):
        def body(j, carry):
            cols = pl.ds(pl.multiple_of(j * PLAN_BLOCK, PLAN_BLOCK), PLAN_BLOCK)
            row = idx_ref[k:k + 1, cols]
            start = functools.reduce(jnp.add, [jnp.where(row == e, starts[e], 0.0) for e in range(N_EXPERTS)])
            pos_ref[k:k + 1, cols] = (rank_ref[k:k + 1, cols] + start).astype(jnp.int32)
            return carry
        return body

    for k in range(2):
        lax.fori_loop(0, nb, place_pass(k), 0)
    n_used = acc * (1.0 / tile)
    tiles = lax.broadcasted_iota(jnp.int32, te_ref.shape, 1).astype(F32)
    first_row = jnp.minimum(tiles, n_used - 1.0) * tile
    te = functools.reduce(jnp.add, [jnp.where(ends[e] <= first_row, 1.0, 0.0) for e in range(N_EXPERTS)])
    te_ref[...] = te.astype(jnp.int32)
    seg_end = functools.reduce(jnp.add, [jnp.where(te == e, ends[e], 0.0) for e in range(N_EXPERTS)])
    tn_ref[...] = (seg_end * (1.0 / tile)).astype(jnp.int32)
    nu_ref[...] = jnp.broadcast_to(n_used, nu_ref.shape).astype(jnp.int32)


def _plan_call(idx, tile, n_tiles):
    t = idx.shape[1]
    whole = lambda shape: pl.BlockSpec(shape, lambda: (0,) * len(shape))
    lanes = jax.ShapeDtypeStruct((1, LANES), jnp.int32)
    pos, te, tn, nu = pl.pallas_call(
        functools.partial(_plan_kernel, tile=tile),
        in_specs=[whole(idx.shape)],
        out_specs=[whole(idx.shape), whole((1, LANES)), whole((1, LANES)), whole((1, LANES))],
        out_shape=[jax.ShapeDtypeStruct(idx.shape, jnp.int32), lanes, lanes, lanes],
        scratch_shapes=[pltpu.VMEM(idx.shape, F32)],
        name="route_plan",
    )(idx)
    return pos.reshape(2 * t), te[0, :n_tiles], tn[0, :n_tiles], nu[0, :1]


V7X_SC_CORES = 2
V7X_SC_SUBCORES = 16
SC_WORKERS = V7X_SC_CORES * V7X_SC_SUBCORES
SC_ROW_BYTES = 256 * 1024


def _sc_worker():
    return lax.axis_index("s") * V7X_SC_CORES + lax.axis_index("c")


def _sc_dispatch_call(streams, pos, n_rows):
    w = streams[0].shape[1]
    sizes = [s.shape[0] for s in streams]
    t = sum(sizes)
    per_w = t // SC_WORKERS
    ch = min(per_w, SC_ROW_BYTES // (4 * w))
    assert per_w % ch == 0 and all(size % ch == 0 for size in sizes)
    mesh = plsc.VectorSubcoreMesh(core_axis_name="c", subcore_axis_name="s")

    @functools.partial(pl.kernel, mesh=mesh, out_type=jax.ShapeDtypeStruct((n_rows, w), jnp.int32),
                       scratch_types=[pltpu.VMEM((ch,), jnp.int32), pltpu.VMEM((ch,), jnp.int32),
                                      pltpu.VMEM((ch, w), jnp.int32),
                                      pltpu.SemaphoreType.DMA, pltpu.SemaphoreType.DMA],
                       name="sc_dispatch")
    def run(*refs):
        srcs = refs[:len(sizes)]
        pos_hbm, xs_hbm, idx0_v, idx1_v, rows_v, sem0, sem1 = refs[len(sizes):]
        first = _sc_worker() * per_w

        @pl.loop(0, per_w // ch)
        def _(j):
            base = first + j * ch
            start = 0
            for src, size in zip(srcs, sizes):
                def load(src=src, start=start):
                    pltpu.sync_copy(src.at[pl.ds(base - start, ch)], rows_v)
                if len(sizes) == 1:
                    load()
                else:
                    pl.when((base >= start) & (base < start + size))(load)
                start += size
            pltpu.sync_copy(pos_hbm.at[pl.ds(base, ch)], idx0_v)
            pltpu.sync_copy(pos_hbm.at[pl.ds(t + base, ch)], idx1_v)
            first_choice = pltpu.async_copy(rows_v, xs_hbm.at[idx0_v], sem0)
            second_choice = pltpu.async_copy(rows_v, xs_hbm.at[idx1_v], sem1)
            first_choice.wait()
            second_choice.wait()

    return run(*streams, pos)


def _sc_combine_call(ys, pos):
    _, d = ys.shape
    n = pos.shape[0]
    per_w = n // SC_WORKERS
    ch = min(per_w // 2, SC_ROW_BYTES // (8 * d))
    assert per_w % (2 * ch) == 0
    mesh = plsc.VectorSubcoreMesh(core_axis_name="c", subcore_axis_name="s")
    idx_buf = pltpu.VMEM((ch,), jnp.int32)
    row_buf = pltpu.VMEM((ch, d), jnp.int32)

    @functools.partial(pl.kernel, mesh=mesh, out_type=jax.ShapeDtypeStruct((n, d), jnp.int32),
                       scratch_types=[idx_buf, idx_buf, row_buf, row_buf] + [pltpu.SemaphoreType.DMA] * 4,
                       name="sc_combine")
    def run(ys_hbm, pos_hbm, out_hbm, idx_a, idx_b, rows_a, rows_b, gsem_a, gsem_b, wsem_a, wsem_b):
        first = _sc_worker() * per_w

        @pl.loop(0, per_w // (2 * ch))
        def _(j):
            base = first + j * (2 * ch)
            pltpu.sync_copy(pos_hbm.at[pl.ds(base, ch)], idx_a)
            pltpu.sync_copy(pos_hbm.at[pl.ds(base + ch, ch)], idx_b)
            gather_a = pltpu.async_copy(ys_hbm.at[idx_a], rows_a, gsem_a)
            gather_b = pltpu.async_copy(ys_hbm.at[idx_b], rows_b, gsem_b)
            gather_a.wait()
            write_a = pltpu.async_copy(rows_a, out_hbm.at[pl.ds(base, ch)], wsem_a)
            gather_b.wait()
            write_b = pltpu.async_copy(rows_b, out_hbm.at[pl.ds(base + ch, ch)], wsem_b)
            write_a.wait()
            write_b.wait()

    return run(ys, pos)


def _moe_call(streams, idx, w_gate, w_up, w_down, layer):
    t = idx.shape[1]
    tile = min(MOE_TILE, max(t // N_EXPERTS, 128))
    n_tiles = (2 * t) // tile + N_EXPERTS
    assert n_tiles <= LANES
    pos, te, tn, nu = _plan_call(idx, tile, n_tiles)
    xs = _sc_dispatch_call(streams, pos, n_tiles * tile)
    hid = _moe_up_call(xs, w_gate, w_up, layer, te, tn, nu, tile)
    ys = _moe_down_call(hid, w_down, layer, te, tn, nu, tile)
    return _sc_combine_call(ys, pos).reshape(2, t, ys.shape[1])


def _moe_resid_kernel(x_ref, y0_ref, y1_ref, gt_ref, g_ref, *rest, final):
    gt = gt_ref[0]
    halves0 = _unpack_rows(y0_ref[0])
    halves1 = _unpack_rows(y1_ref[0])
    y = jnp.concatenate([gt[:, 0:1] * a + gt[:, 1:2] * b for a, b in zip(halves0, halves1)], axis=1)
    x = x_ref[0] + g_ref[0] * y
    if final:
        fg_ref, o_ref = rest
        x = _rms(x, fg_ref[...])
    else:
        (o_ref,) = rest
    o_ref[0] = x


def _moe_resid_call(x, y, first_row, gates, gate, final_g=None):
    b, l, d = x.shape
    tl = min(l, 512)
    nt = l // tl
    tile = pl.BlockSpec((1, tl, d), lambda i, j: (i, j, 0))
    choice = lambda k: pl.BlockSpec((1, tl, d // 2), lambda i, j: (k, first_row // tl + i * nt + j, 0))
    in_specs = [tile, choice(0), choice(1), pl.BlockSpec((1, tl, 2), lambda i, j: (i, j, 0)),
                pl.BlockSpec((1, 1, d), lambda i, j: (i, 0, 0))]
    args = [x, y, y, gates.T.reshape(b, l, 2), gate]
    if final_g is not None:
        in_specs.append(pl.BlockSpec((1, d), lambda i, j: (0, 0)))
        args.append(final_g.reshape(1, d))
    return pl.pallas_call(
        functools.partial(_moe_resid_kernel, final=final_g is not None),
        grid=(b, l // tl),
        in_specs=in_specs,
        out_specs=tile,
        out_shape=jax.ShapeDtypeStruct((b, l, d), F32),
        compiler_params=_cp("parallel", "parallel"),
        name="moe_residual",
    )(*args)


def _rope_perm(w):
    r = w.reshape(w.shape[:-1] + (2, 2, ROPE_FREQS))
    return jnp.stack([-r[..., 1, :], r[..., 0, :]], axis=-2).reshape(w.shape)


def _rope_table(l, rotary):
    if not rotary:
        return jnp.concatenate([jnp.ones((l, MLA_ROPE), F32), jnp.zeros((l, MLA_ROPE), F32)], axis=1)
    rows = l // GRID_W
    r = jnp.repeat(jnp.arange(rows), GRID_W)
    col = jnp.tile(jnp.arange(GRID_W), rows)
    pos = jnp.stack([r, col], axis=-1).astype(F32)
    inv = ROPE_THETA ** (-jnp.arange(ROPE_FREQS, dtype=F32) / ROPE_FREQS)
    ang = pos[:, :, None] * inv
    lay = lambda a: jnp.concatenate([a[:, 0], a[:, 0], a[:, 1], a[:, 1]], axis=1)
    return jnp.concatenate([lay(jnp.cos(ang)), lay(jnp.sin(ang))], axis=1)


def _layer_weights(w_in, w_uq, w_ukv, w_out):
    d = w_in.shape[0]
    kr = w_in[:, OFF_KR:OFF_HY]
    w_kv_in = jnp.concatenate([w_in[:, OFF_KV:OFF_KR], kr, _rope_perm(kr)], axis=1)
    uq = w_uq.reshape(MLA_Q_RANK, MLA_HEADS, MLA_NOPE + MLA_ROPE)
    w_q = jnp.concatenate([uq, _rope_perm(uq[:, :, MLA_NOPE:])], axis=2).reshape(MLA_Q_RANK, MLA_HEADS * QK_HEAD)
    ukv = w_ukv.reshape(MLA_KV_RANK, MLA_HEADS, MLA_NOPE + MLA_V_DIM)
    w_k = ukv[:, :, :MLA_NOPE].reshape(MLA_KV_RANK, MLA_HEADS * MLA_NOPE)
    w_v = ukv[:, :, MLA_NOPE:].reshape(MLA_KV_RANK, MLA_HEADS * MLA_V_DIM)
    cast = lambda a: a.astype(BF16)
    w_all = jnp.concatenate([w_in[:, :OFF_KV], w_in[:, OFF_HY:], w_kv_in], axis=1)
    return dict(all=cast(w_all), kv=cast(w_kv_in), uq=cast(w_q), uk=cast(w_k), uv=cast(w_v), out=cast(w_out))


def _router_weights(w_router):
    hi = w_router.astype(BF16)
    lo = (w_router - hi.astype(F32)).astype(BF16)
    return jnp.concatenate([hi.T, lo.T], axis=0)


def _mixer(x, mod, w, lp, cs, spec, fwd, inv, kv_extra, full, wr, rbias):
    b, l, d = x.shape
    kv_args = (lp["kv_norm_g"], w["uk"], w["uv"], cs)
    if not full:
        k, v = _in_proj_call(x, lp["norm1_g"], mod[0], mod[1], w["kv"], [], *kv_args)
        return None, k, v
    splits = [OFF_Q, MLA_Q_RANK, (HY_ORDER + 1) * HYENA_W]
    p_conv, p_q, p_hy, k, v = _in_proj_call(x, lp["norm1_g"], mod[0], mod[1], w["all"], splits, *kv_args)
    conv_o = _conformer_call(p_conv, lp["conv_dw_w"], lp["conv_dw_b"], lp["conv_ln_g"], lp["conv_ln_b"])
    att_o = _attn_call(p_q, lp["q_norm_g"], w["uq"], cs, kv_extra + [(k, v)])
    hy_o = _hyena_call(p_hy, lp["hy_short_w"], lp["hy_short_b"], spec, lp["hy_bias"], fwd, inv)
    routed = _mix_out_call(conv_o, att_o, hy_o, w["out"], x, mod, lp["norm2_g"], wr, rbias)
    return routed, k, v


def _channel(streams, experts, layer, final_g=None):
    d = streams[0][1][0].shape[2]
    idx = jnp.concatenate([r[2] for _, r in streams], axis=1)
    y = _moe_call([r[1].reshape(-1, d // 2) for _, r in streams], idx, *experts, layer)
    outs = []
    first_row = 0
    for mod, (x, _, _, gates) in streams:
        outs.append(_moe_resid_call(x, y, first_row, gates, mod[5], final_g))
        first_row += x.shape[0] * x.shape[1]
    return outs


def _filter_spectrum(l, lp, fwd):
    a = _hy_filter_call(l, lp["hy_w1"], lp["hy_b1"], lp["hy_freq1"], lp["hy_w2"], lp["hy_b2"], lp["hy_freq2"],
                        lp["hy_w3"], lp["hy_b3"])
    return _spec_call(fwd, a)


def kernel(x, c, ctx, c_ctx, norm1_g, norm2_g, w_ada, b_ada, w_in, conv_dw_w, conv_dw_b, conv_ln_g, conv_ln_b, q_norm_g, w_uq, kv_norm_g, w_ukv, hy_short_w, hy_short_b, hy_w1, hy_b1, hy_freq1, hy_w2, hy_b2, hy_freq2, hy_w3, hy_b3, hy_bias, w_out, w_router, router_bias, w_gate, w_up, w_down, final_norm_g):
    b, l, d = x.shape
    lc = ctx.shape[1]
    depth = w_in.shape[0]
    per_layer = dict(norm1_g=norm1_g, norm2_g=norm2_g, conv_dw_w=conv_dw_w, conv_dw_b=conv_dw_b,
                     conv_ln_g=conv_ln_g, conv_ln_b=conv_ln_b, q_norm_g=q_norm_g, kv_norm_g=kv_norm_g,
                     hy_short_w=hy_short_w, hy_short_b=hy_short_b, hy_w1=hy_w1, hy_b1=hy_b1, hy_freq1=hy_freq1,
                     hy_w2=hy_w2, hy_b2=hy_b2, hy_freq2=hy_freq2, hy_w3=hy_w3, hy_b3=hy_b3, hy_bias=hy_bias)
    experts = (w_gate, w_up, w_down)
    cs_l = _rope_table(l, True)
    cs_c = _rope_table(lc, False)
    fwd_l, inv_l = _dft_mats(l)
    fwd_c, inv_c = _dft_mats(lc)
    wr = _router_weights(w_router)
    rows = 16
    cvec = jnp.concatenate([c, c_ctx[None, :], jnp.zeros((rows - b - 1, d), F32)], axis=0)
    xl, xc = x, ctx
    for i in range(depth):
        last = i == depth - 1
        lp = {k: v[i] for k, v in per_layer.items()}
        w = _layer_weights(w_in[i], w_uq[i], w_ukv[i], w_out[i])
        m = _ada_call(cvec, w_ada, b_ada, i)
        mod_l = [m[:b, None, j * d:(j + 1) * d] for j in range(6)]
        mod_c = [jnp.broadcast_to(m[b:b + 1, None, j * d:(j + 1) * d], (b, 1, d)) for j in range(6)]
        spec_l = _filter_spectrum(l, lp, fwd_l)
        spec_c = None if last else _filter_spectrum(lc, lp, fwd_c)
        routed_c, kc, vc = _mixer(xc, mod_c, w, lp, cs_c, spec_c, fwd_c, inv_c, [], not last, wr, router_bias)
        routed_l, _, _ = _mixer(xl, mod_l, w, lp, cs_l, spec_l, fwd_l, inv_l, [(kc, vc)], True, wr, router_bias)
        if last:
            (xl,) = _channel([(mod_l, routed_l)], experts, i, final_norm_g)
        else:
            xl, xc = _channel([(mod_l, routed_l), (mod_c, routed_c)], experts, i)
    return xl
```

```python
import functools
import math

import jax
import jax.numpy as jnp
from jax import lax
from jax.experimental import pallas as pl
from jax.experimental.pallas import tpu as pltpu
from jax.experimental.pallas import tpu_sc as plsc

F32 = jnp.float32
BF16 = jnp.bfloat16

EPS = 1e-6
GRID_W = 64
CONV_W = 512
CONV_KSIZE = 31
MLA_HEADS = 8
MLA_NOPE = 128
MLA_ROPE = 64
MLA_V_DIM = 128
MLA_Q_RANK = 768
MLA_KV_RANK = 512
ROPE_FREQS = MLA_ROPE // 4
ROPE_THETA = 10000.0
MLA_SCALE = (MLA_NOPE + MLA_ROPE) ** -0.5
Q_SCALE = MLA_SCALE * math.log2(math.e)
HYENA_W = 512
HY_ORDER = 2
HY_SHORT = 3
HY_EMB = 33
HY_BANDS = (HY_EMB - 1) // 2
HY_FFN = 64
HY_MIN_DECAY = math.log(1e-2) / 1.5
HY_MAX_DECAY = math.log(1e-2) / 0.3
N_EXPERTS = 16
N_GROUPS = 4
EXPERTS_PER_GROUP = N_EXPERTS // N_GROUPS
OFF_Q = 2 * CONV_W
OFF_KV = OFF_Q + MLA_Q_RANK
OFF_KR = OFF_KV + MLA_KV_RANK
OFF_HY = OFF_KR + MLA_ROPE

LANES = 128
QK_HEAD = 2 * LANES
V_HEAD = MLA_V_DIM + LANES
V7X_VMEM_BYTES = 64 * 1024 * 1024
VMEM_LIMIT = (V7X_VMEM_BYTES * 7) // 8
MOE_TILE = 512
NEG_INF = float("-inf")


def _cp(*sem):
    return pltpu.CompilerParams(dimension_semantics=sem, vmem_limit_bytes=VMEM_LIMIT)


def _dot(a, b):
    return jnp.dot(a, b, preferred_element_type=F32)


def _rms(x, g):
    ms = jnp.mean(x * x, axis=-1, keepdims=True)
    return x * lax.rsqrt(ms + EPS) * g


def _ada_kernel(c_ref, w_ref, b_ref, o_ref):
    c = c_ref[...]
    a = (c * jax.nn.sigmoid(c)).astype(BF16)
    o_ref[...] = _dot(a, w_ref[0].astype(BF16)) + b_ref[0]


def _ada_call(cvec, w, b, layer):
    m, d = cvec.shape
    n = w.shape[2]
    tn = 1024
    return pl.pallas_call(
        _ada_kernel,
        grid=(n // tn,),
        in_specs=[pl.BlockSpec((m, d), lambda j: (0, 0)),
                  pl.BlockSpec((1, d, tn), lambda j: (layer, 0, j)),
                  pl.BlockSpec((1, 1, tn), lambda j: (layer, 0, j))],
        out_specs=pl.BlockSpec((m, tn), lambda j: (0, j)),
        out_shape=jax.ShapeDtypeStruct((m, n), F32),
        compiler_params=_cp("parallel"),
        name="ada",
    )(cvec, w, b.reshape(b.shape[0], 1, n))


def _in_proj_kernel(x_ref, g_ref, sh_ref, sc_ref, w_ref, kg_ref, wk_ref, wv_ref, cs_ref, *o_refs):
    *p_refs, k_ref, v_ref = o_refs
    y = _rms(x_ref[0], g_ref[...])
    h = (y * (1.0 + sc_ref[0]) + sh_ref[0]).astype(BF16)
    off = 0
    for o_ref in p_refs:
        n = o_ref.shape[-1]
        o_ref[0] = _dot(h, w_ref[0, :, off:off + n]).astype(o_ref.dtype)
        off += n
    p = _dot(h, w_ref[0, :, off:])
    xn = _rms(p[:, :MLA_KV_RANK], kg_ref[...]).astype(BF16)
    kk = _dot(xn, wk_ref[0])
    vv = _dot(xn, wv_ref[0])
    rot = _rotate(p[:, MLA_KV_RANK:], cs_ref[...])
    lane = lax.broadcasted_iota(jnp.int32, rot.shape, 1)
    rot = jnp.where(lane < MLA_ROPE, rot, 0.0).astype(BF16)
    ones = jnp.where(lane == 0, 1.0, 0.0).astype(BF16)
    for hd in range(MLA_HEADS):
        k_ref[0, :, hd * QK_HEAD:hd * QK_HEAD + LANES] = kk[:, hd * LANES:(hd + 1) * LANES].astype(BF16)
        k_ref[0, :, hd * QK_HEAD + LANES:(hd + 1) * QK_HEAD] = rot
        v_ref[0, :, hd * V_HEAD:hd * V_HEAD + MLA_V_DIM] = vv[:, hd * MLA_V_DIM:(hd + 1) * MLA_V_DIM].astype(BF16)
        v_ref[0, :, hd * V_HEAD + MLA_V_DIM:(hd + 1) * V_HEAD] = ones


def _in_proj_call(x, g, shift, scale, w, splits, kv_g, w_k, w_v, cs, layer):
    b, l, d = x.shape
    n = w.shape[2]
    tm = min(l, 512)
    r = MLA_KV_RANK
    row = pl.BlockSpec((1, 1, d), lambda i, j: (i, 0, 0))
    const = lambda shape: pl.BlockSpec(shape, lambda i, j: (0, 0))
    of_layer = lambda a, **kw: pl.BlockSpec((1,) + a.shape[1:], lambda i, j: (layer, 0, 0), **kw)
    widths = list(splits) + [MLA_HEADS * QK_HEAD, MLA_HEADS * V_HEAD]
    return pl.pallas_call(
        _in_proj_kernel,
        grid=(b, l // tm),
        in_specs=[pl.BlockSpec((1, tm, d), lambda i, j: (i, j, 0)),
                  const((1, d)), row, row,
                  of_layer(w, pipeline_mode=pl.Buffered(1)),
                  const((1, r)), of_layer(w_k), of_layer(w_v),
                  pl.BlockSpec((tm, LANES), lambda i, j: (j, 0))],
        out_specs=[pl.BlockSpec((1, tm, k), lambda i, j: (i, j, 0)) for k in widths],
        out_shape=[jax.ShapeDtypeStruct((b, l, k), BF16) for k in widths],
        compiler_params=_cp("parallel", "parallel"),
        name="in_proj",
    )(x, g.reshape(1, d), shift, scale, w, kv_g.reshape(1, r), w_k, w_v, cs)


CONV_ROWS = 128
CONV_PAD = 16


def _conformer_kernel(p_ref, w_ref, b_ref, g_ref, be_ref, o_ref, zp_ref):
    l = p_ref.shape[1]
    c = CONV_W
    zeros = jnp.zeros((CONV_PAD, c), F32)
    zp_ref[0:CONV_PAD, :] = zeros
    zp_ref[CONV_PAD + l:CONV_PAD + l + CONV_PAD, :] = zeros

    def glu(r, carry):
        base = pl.multiple_of(r * CONV_ROWS, CONV_ROWS)
        blk = p_ref[0, pl.ds(base, CONV_ROWS), :].astype(F32)
        zp_ref[pl.ds(base + CONV_PAD, CONV_ROWS), :] = blk[:, :c] * jax.nn.sigmoid(blk[:, c:])
        return carry

    lax.fori_loop(0, l // CONV_ROWS, glu, 0)

    win_rows = CONV_ROWS + 2 * CONV_PAD
    first = CONV_PAD - CONV_KSIZE // 2

    def conv(r, carry):
        base = pl.multiple_of(r * CONV_ROWS, CONV_ROWS)
        parts = []
        for cc in range(c // LANES):
            lanes = slice(cc * LANES, (cc + 1) * LANES)
            win = zp_ref[pl.ds(base, win_rows), lanes]
            acc = jnp.zeros((CONV_ROWS, LANES), F32)
            for res in range(8):
                taps = [k for k in range(CONV_KSIZE) if (k + first) % 8 == res]
                if not taps:
                    continue
                shifted = pltpu.roll(win, win_rows - res, axis=0) if res else win
                for k in taps:
                    off = k + first - res
                    acc = acc + shifted[off:off + CONV_ROWS, :] * w_ref[k:k + 1, lanes]
            parts.append(acc)
        z = jnp.concatenate(parts, axis=1) + b_ref[...]
        mu = jnp.mean(z, axis=-1, keepdims=True)
        zc = z - mu
        var = jnp.mean(zc * zc, axis=-1, keepdims=True)
        y = zc * lax.rsqrt(var + EPS) * g_ref[...] + be_ref[...]
        o_ref[0, pl.ds(base, CONV_ROWS), :] = (y * jax.nn.sigmoid(y)).astype(o_ref.dtype)
        return carry

    lax.fori_loop(0, l // CONV_ROWS, conv, 0)


def _conformer_call(p, dw_w, dw_b, ln_g, ln_b):
    b, l, c2 = p.shape
    c = c2 // 2
    vec = pl.BlockSpec((1, c), lambda i: (0, 0))
    return pl.pallas_call(
        _conformer_kernel,
        grid=(b,),
        in_specs=[pl.BlockSpec((1, l, c2), lambda i: (i, 0, 0)),
                  pl.BlockSpec((CONV_KSIZE, c), lambda i: (0, 0)), vec, vec, vec],
        out_specs=pl.BlockSpec((1, l, c), lambda i: (i, 0, 0)),
        out_shape=jax.ShapeDtypeStruct((b, l, c), BF16),
        scratch_shapes=[pltpu.VMEM((l + 2 * CONV_PAD, c), F32)],
        compiler_params=_cp("parallel"),
        name="conformer",
    )(p, dw_w, dw_b.reshape(1, c), ln_g.reshape(1, c), ln_b.reshape(1, c))


def _rotate(group, cs):
    t = group * cs
    return t + pltpu.roll(t, MLA_ROPE, axis=1)


def _attn_kernel(p_ref, g_ref, w_ref, cs_ref, *refs, n_seg):
    o_ref, q_ref = refs[2 * n_seg], refs[2 * n_seg + 1]
    xn = _rms(p_ref[0].astype(F32), g_ref[...]).astype(BF16)
    cs = cs_ref[...] * Q_SCALE
    for h in range(MLA_HEADS):
        r = _dot(xn, w_ref[0, :, h * QK_HEAD:(h + 1) * QK_HEAD])
        q_ref[:, h * QK_HEAD:h * QK_HEAD + LANES] = (r[:, :LANES] * Q_SCALE).astype(BF16)
        q_ref[:, h * QK_HEAD + LANES:(h + 1) * QK_HEAD] = _rotate(r[:, LANES:], cs).astype(BF16)
    nt = (((1,), (1,)), ((), ()))
    for h in range(MLA_HEADS):
        qk = slice(h * QK_HEAD, (h + 1) * QK_HEAD)
        vd = slice(h * MLA_V_DIM, (h + 1) * MLA_V_DIM)
        q = q_ref[:, qk]
        scores = [lax.dot_general(q, refs[2 * s][0, :, qk], nt, preferred_element_type=F32)
                  for s in range(n_seg)]
        m = functools.reduce(jnp.maximum, [jnp.max(s, axis=-1, keepdims=True) for s in scores])
        acc = 0.0
        for s in range(n_seg):
            p = jnp.exp2(scores[s] - m).astype(BF16)
            acc = acc + _dot(p, refs[2 * s + 1][0, :, h * V_HEAD:(h + 1) * V_HEAD])
        o_ref[0, :, vd] = (acc[:, :MLA_V_DIM] * (1.0 / acc[:, MLA_V_DIM:MLA_V_DIM + 1])).astype(o_ref.dtype)


def _attn_call(p_q, q_g, w_q, cs, segs, layer):
    b, l, r = p_q.shape
    tq = min(l, 512)
    const = lambda shape: pl.BlockSpec(shape, lambda i, j: (0, 0))
    in_specs = [pl.BlockSpec((1, tq, r), lambda i, j: (i, j, 0)), const((1, r)),
                pl.BlockSpec((1,) + w_q.shape[1:], lambda i, j: (layer, 0, 0)),
                pl.BlockSpec((tq, LANES), lambda i, j: (j, 0))]
    args = [p_q, q_g.reshape(1, r), w_q, cs]
    for k, v in segs:
        in_specs.append(pl.BlockSpec((1,) + k.shape[1:], lambda i, j: (i, 0, 0), pipeline_mode=pl.Buffered(1)))
        in_specs.append(pl.BlockSpec((1,) + v.shape[1:], lambda i, j: (i, 0, 0), pipeline_mode=pl.Buffered(1)))
        args += [k, v]
    return pl.pallas_call(
        functools.partial(_attn_kernel, n_seg=len(segs)),
        grid=(b, l // tq),
        in_specs=in_specs,
        out_specs=pl.BlockSpec((1, tq, MLA_HEADS * MLA_V_DIM), lambda i, j: (i, j, 0)),
        out_shape=jax.ShapeDtypeStruct((b, l, MLA_HEADS * MLA_V_DIM), BF16),
        scratch_shapes=[pltpu.VMEM((tq, MLA_HEADS * QK_HEAD), BF16)],
        compiler_params=_cp("parallel", "parallel"),
        name="attention",
    )(*args)


HY_PAD = 8
HY_ROWS = 128


def _hy_short_kernel(p_ref, w_ref, b_ref, o_ref, up_ref):
    l = p_ref.shape[1]
    c = p_ref.shape[2]
    zeros = jnp.zeros((HY_PAD, c), F32)
    up_ref[0:HY_PAD, :] = zeros
    up_ref[HY_PAD + l:HY_PAD + l + HY_PAD, :] = zeros

    def copy(r, carry):
        base = pl.multiple_of(r * HY_ROWS, HY_ROWS)
        up_ref[pl.ds(base + HY_PAD, HY_ROWS), :] = p_ref[0, pl.ds(base, HY_ROWS), :].astype(F32)
        return carry

    lax.fori_loop(0, l // HY_ROWS, copy, 0)
    first = HY_PAD - HY_SHORT // 2

    def conv(r, carry):
        base = pl.multiple_of(r * HY_ROWS, HY_ROWS)
        for cc in range(c // LANES):
            lanes = slice(cc * LANES, (cc + 1) * LANES)
            win = up_ref[pl.ds(base, HY_ROWS + 2 * HY_PAD), lanes]
            acc = jnp.zeros((HY_ROWS, LANES), F32) + b_ref[:, lanes]
            for k in range(HY_SHORT):
                acc = acc + win[first + k:first + k + HY_ROWS, :] * w_ref[k:k + 1, lanes]
            o_ref[0, 0, pl.ds(base, HY_ROWS), lanes] = acc.astype(o_ref.dtype)
        return carry

    lax.fori_loop(0, l // HY_ROWS, conv, 0)


def _hy_short_call(p, w, bias):
    b, l, c3 = p.shape
    c = HYENA_W
    return pl.pallas_call(
        _hy_short_kernel,
        grid=(b, c3 // c),
        in_specs=[pl.BlockSpec((1, l, c), lambda i, g: (i, 0, g)),
                  pl.BlockSpec((HY_SHORT, c), lambda i, g: (0, g)),
                  pl.BlockSpec((1, c), lambda i, g: (0, g))],
        out_specs=pl.BlockSpec((1, 1, l, c), lambda i, g: (g, i, 0, 0)),
        out_shape=jax.ShapeDtypeStruct((c3 // c, b, l, c), BF16),
        scratch_shapes=[pltpu.VMEM((l + 2 * HY_PAD, c), F32)],
        compiler_params=_cp("parallel", "parallel"),
        name="hyena_short",
    )(p, w, bias.reshape(1, c3))


def _hy_filter_kernel(z_ref, w1_ref, b1_ref, f1_ref, w2_ref, b2_ref, f2_ref, w3_ref, b3_ref, win_ref, o_ref):
    hp = lax.Precision.HIGHEST

    def mm(a, b):
        return jnp.dot(a, b, preferred_element_type=F32, precision=hp)

    h = jnp.sin(f1_ref[...] * (mm(z_ref[...], w1_ref[...]) + b1_ref[...]))
    h = jnp.sin(f2_ref[...] * (mm(h, w2_ref[...]) + b2_ref[...]))
    h = mm(h, w3_ref[...]) + b3_ref[...]
    c = HYENA_W
    win = win_ref[...]
    tl = h.shape[0]
    row = lax.broadcasted_iota(jnp.int32, (tl, c), 0) + pl.program_id(0) * tl
    for o in range(HY_ORDER):
        fwd = h[:, (2 * o) * c:(2 * o + 1) * c] * win
        bwd = jnp.where(row == 0, 0.0, h[:, (2 * o + 1) * c:(2 * o + 2) * c] * win)
        o_ref[:, o * c:(o + 1) * c] = (fwd + bwd).astype(o_ref.dtype)
        o_ref[:, (HY_ORDER + o) * c:(HY_ORDER + o + 1) * c] = (fwd - bwd).astype(o_ref.dtype)


def _pad2(a, rows, cols):
    return jnp.pad(a, ((0, rows - a.shape[0]), (0, cols - a.shape[1])))


def _hy_filter_call(l, w1, b1, f1, w2, b2, f2, w3, b3):
    t = jnp.arange(l, dtype=F32)
    t_norm = t / max(l - 1, 1)
    ang = (2.0 * math.pi / l) * t[:, None] * jnp.linspace(1e-4, HY_BANDS - 1, HY_BANDS, dtype=F32)[None, :]
    z = jnp.concatenate([t_norm[:, None], jnp.cos(ang), -jnp.sin(ang)], axis=-1)
    deltas = jnp.abs(jnp.linspace(HY_MIN_DECAY, HY_MAX_DECAY, HYENA_W, dtype=F32))
    window = jnp.exp(-t_norm[:, None] * deltas[None, :])
    n3 = w3.shape[1]
    tl = min(l, 256)
    full = lambda shape: pl.BlockSpec(shape, lambda i: (0, 0))
    return pl.pallas_call(
        _hy_filter_kernel,
        grid=(l // tl,),
        in_specs=[pl.BlockSpec((tl, LANES), lambda i: (i, 0)),
                  full((LANES, LANES)), full((1, LANES)), full((1, LANES)),
                  full((LANES, LANES)), full((1, LANES)), full((1, LANES)),
                  full((LANES, n3)), full((1, n3)),
                  pl.BlockSpec((tl, HYENA_W), lambda i: (i, 0))],
        out_specs=pl.BlockSpec((tl, n3), lambda i: (i, 0)),
        out_shape=jax.ShapeDtypeStruct((l, n3), BF16),
        compiler_params=_cp("parallel"),
        name="hyena_filter",
    )(_pad2(z, l, LANES), _pad2(w1, LANES, LANES), _pad2(b1[None], 1, LANES), _pad2(f1[None], 1, LANES),
      _pad2(w2, LANES, LANES), _pad2(b2[None], 1, LANES), _pad2(f2[None], 1, LANES),
      _pad2(w3, LANES, n3), b3[None], window)


def _dft_tables_kernel(ac_ref, as_ref, bc_ref, bs_ref, fwd_ref, inv_ref):
    tm, l = fwd_ref.shape[1], fwd_ref.shape[2]
    n = 2 * l
    row = lax.broadcasted_iota(jnp.int32, (tm, LANES), 0) + pl.program_id(0) * tm
    lane = lax.broadcasted_iota(jnp.int32, (tm, LANES), 1)
    row_sign = jnp.where((row & 1) == 0, 1.0, -1.0)
    lane_sign = jnp.where((lane & 1) == 0, 1.0, -1.0)
    bc = bc_ref[...]
    bs = bs_ref[...]
    for a in range(l // LANES):
        cols = slice(a * LANES, (a + 1) * LANES)
        ca = ac_ref[:, a:a + 1]
        sa = as_ref[:, a:a + 1]
        c = ca * bc - sa * bs
        s = sa * bc + ca * bs
        fwd_ref[0, :, cols] = c.astype(BF16)
        fwd_ref[1, :, cols] = jnp.where(row == 0, lane_sign, -s).astype(BF16)
        inv_c = c * (2.0 / n)
        inv_s = -s * (2.0 / n)
        if a == 0:
            inv_c = jnp.where(lane == 0, 1.0 / n, inv_c)
            inv_s = jnp.where(lane == 0, row_sign / n, inv_s)
        inv_ref[:, cols] = inv_c.astype(BF16)
        inv_ref[:, l + a * LANES:l + (a + 1) * LANES] = inv_s.astype(BF16)


def _dft_mats(l):
    n = 2 * l
    f = jnp.arange(l, dtype=jnp.int32)[:, None]

    def table(t):
        ang = ((f * t[None, :]) % n).astype(F32) * (2.0 * math.pi / n)
        return jnp.cos(ang), jnp.sin(ang)

    coarse = table(jnp.arange(l // LANES, dtype=jnp.int32) * LANES)
    fine = table(jnp.arange(LANES, dtype=jnp.int32))
    tm = min(l, 256)
    small = pl.BlockSpec((tm, l // LANES), lambda i: (i, 0))
    wide = pl.BlockSpec((tm, LANES), lambda i: (i, 0))
    return pl.pallas_call(
        _dft_tables_kernel,
        grid=(l // tm,),
        in_specs=[small, small, wide, wide],
        out_specs=[pl.BlockSpec((2, tm, l), lambda i: (0, i, 0)), pl.BlockSpec((tm, n), lambda i: (i, 0))],
        out_shape=[jax.ShapeDtypeStruct((2, l, l), BF16), jax.ShapeDtypeStruct((l, n), BF16)],
        compiler_params=_cp("parallel"),
        name="dft_tables",
    )(*coarse, *fine)


def _spec_kernel(f_ref, a_ref, o_ref):
    half = a_ref.shape[1] // 2
    hi = _dot(f_ref[1], a_ref[:, half:])
    o_ref[0] = _dot(f_ref[0], a_ref[:, :half])
    o_ref[1] = hi

    @pl.when(pl.program_id(0) == 0)
    def _():
        nyq = _dot(f_ref[1, 0:16, :], a_ref[:, :half])
        row = lax.broadcasted_iota(jnp.int32, nyq.shape, 0)
        o_ref[1, 0:16, :] = jnp.where(row == 0, nyq, hi[0:16])


def _spec_call(fwd, a):
    _, lf, l = fwd.shape
    n = a.shape[1] // 2
    tm = min(lf, 512)
    return pl.pallas_call(
        _spec_kernel,
        grid=(lf // tm,),
        in_specs=[pl.BlockSpec((2, tm, l), lambda i: (0, i, 0)),
                  pl.BlockSpec(a.shape, lambda i: (0, 0))],
        out_specs=pl.BlockSpec((2, tm, n), lambda i: (0, i, 0)),
        out_shape=jax.ShapeDtypeStruct((2, lf, n), F32),
        compiler_params=_cp("arbitrary"),
        name="hyena_spectrum",
    )(fwd, a)


def _dft_fwd_kernel(f_ref, u_ref, h_ref, y_ref):
    u = u_ref[0, 0].astype(BF16)
    xr = _dot(f_ref[0], u)
    xi = _dot(f_ref[1], u)
    hr = h_ref[0]
    hi = h_ref[1]
    yr = xr * hr - xi * hi
    yi = xr * hi + xi * hr
    y_ref[0, 0] = yr.astype(BF16)
    y_ref[0, 1] = yi.astype(BF16)

    @pl.when(pl.program_id(0) == 0)
    def _():
        row = lax.broadcasted_iota(jnp.int32, (16, u.shape[1]), 0)
        y_ref[0, 0, 0:16, :] = jnp.where(row == 0, xr[0:16] * hr[0:16], yr[0:16]).astype(BF16)
        y_ref[0, 1, 0:16, :] = jnp.where(row == 0, xi[0:16] * hi[0:16], yi[0:16]).astype(BF16)


def _dft_fwd_call(fwd, u, group, spec, order):
    _, b, l, c = u.shape
    lf = fwd.shape[1]
    tm = min(lf, 1024)
    return pl.pallas_call(
        _dft_fwd_kernel,
        grid=(lf // tm, b),
        in_specs=[pl.BlockSpec((2, tm, l), lambda i, j: (0, i, 0)),
                  pl.BlockSpec((1, 1, l, c), lambda i, j: (group, j, 0, 0)),
                  pl.BlockSpec((2, tm, c), lambda i, j: (0, i, order))],
        out_specs=pl.BlockSpec((1, 2, tm, c), lambda i, j: (j, 0, i, 0)),
        out_shape=jax.ShapeDtypeStruct((b, 2, lf, c), BF16),
        compiler_params=_cp("arbitrary", "arbitrary"),
        name="hyena_dft_fwd",
    )(fwd, u, spec)


def _dft_inv_kernel(g_ref, y_ref, u_ref, b_ref, x_ref, o_ref):
    r = _dot(g_ref[...], y_ref[0]) + u_ref[0, 0].astype(F32) * b_ref[...]
    o_ref[0, 0] = (x_ref[0, 0] * r).astype(o_ref.dtype)


def _dft_inv_call(inv, y, u, u_group, bias, gate, gate_group):
    _, b, l, c = u.shape
    n = inv.shape[1]
    tm = min(l, 1024)
    y = y.reshape(b, n, c)
    row = lambda g: pl.BlockSpec((1, 1, tm, c), lambda i, j: (g, j, i, 0))
    return pl.pallas_call(
        _dft_inv_kernel,
        grid=(l // tm, b),
        in_specs=[pl.BlockSpec((tm, n), lambda i, j: (i, 0)),
                  pl.BlockSpec((1, n, c), lambda i, j: (j, 0, 0)),
                  row(u_group), pl.BlockSpec((1, c), lambda i, j: (0, 0)), row(gate_group)],
        out_specs=row(0),
        out_shape=jax.ShapeDtypeStruct((1, b, l, c), BF16),
        compiler_params=_cp("parallel", "parallel"),
        name="hyena_dft_inv",
    )(inv, y, u, bias.reshape(1, c), gate)


def _hyena_call(p_hy, sh_w, sh_b, spec, hy_bias, fwd, inv):
    u3 = _hy_short_call(p_hy, sh_w, sh_b)
    z = _dft_inv_call(inv, _dft_fwd_call(fwd, u3, 0, spec, 0), u3, 0, hy_bias[0], u3, 1)
    return _dft_inv_call(inv, _dft_fwd_call(fwd, z, 0, spec, 1), z, 0, hy_bias[1], u3, 2)[0]


def _mix_out_kernel(c_ref, a_ref, h_ref, w_ref, x_ref, g1_ref, g_ref, sh_ref, sc_ref,
                    wr_ref, rb_ref, o_ref, h32_ref, idx_ref, gate_ref):
    k1 = c_ref.shape[2]
    k2 = k1 + a_ref.shape[2]
    acc = _dot(c_ref[0], w_ref[0, :k1, :]) + _dot(a_ref[0], w_ref[0, k1:k2, :]) + _dot(h_ref[0], w_ref[0, k2:, :])
    x = x_ref[0] + g1_ref[0] * acc
    o_ref[0] = x
    hm = _rms(x, g_ref[...]) * (1.0 + sc_ref[0]) + sh_ref[0]
    h32_ref[0] = _pack_rows(hm)
    picks, gates = _route(hm, wr_ref[...], rb_ref[...])
    idx_ref[...] = jnp.concatenate(picks, axis=0)
    gate_ref[...] = jnp.concatenate(gates, axis=0)


def _mix_out_call(conv_o, att_o, hy_o, w_out, layer, x, mod, g2, wr, rbias):
    b, l, d = x.shape
    tm = min(l, 512)
    nt = l // tm
    act = lambda k: pl.BlockSpec((1, tm, k), lambda i, j: (i, j, 0))
    row = pl.BlockSpec((1, 1, d), lambda i, j: (i, 0, 0))
    tok = pl.BlockSpec((2, tm), lambda i, j: (0, i * nt + j))
    return pl.pallas_call(
        _mix_out_kernel,
        grid=(b, nt),
        in_specs=[act(conv_o.shape[2]), act(att_o.shape[2]), act(hy_o.shape[2]),
                  pl.BlockSpec((1,) + w_out.shape[1:], lambda i, j: (layer, 0, 0), pipeline_mode=pl.Buffered(1)),
                  act(d), row, pl.BlockSpec((1, d), lambda i, j: (0, 0)), row, row,
                  pl.BlockSpec(wr.shape, lambda i, j: (0, 0)),
                  pl.BlockSpec((N_EXPERTS, 1), lambda i, j: (0, 0))],
        out_specs=[act(d), act(d // 2), tok, tok],
        out_shape=[jax.ShapeDtypeStruct((b, l, d), F32),
                   jax.ShapeDtypeStruct((b, l, d // 2), jnp.int32),
                   jax.ShapeDtypeStruct((2, b * l), jnp.int32),
                   jax.ShapeDtypeStruct((2, b * l), F32)],
        compiler_params=_cp("parallel", "parallel"),
        name="mix_out",
    )(conv_o, att_o, hy_o, w_out, x, mod[2], g2.reshape(1, d), mod[3], mod[4], wr,
      rbias.reshape(N_EXPERTS, 1))


def _route(hm, wr, rbias):
    hi = hm.astype(BF16)
    lo = (hm - hi.astype(F32)).astype(BF16)
    nt = (((1,), (1,)), ((), ()))
    r = (lax.dot_general(wr, hi, nt, preferred_element_type=F32)
         + lax.dot_general(wr, lo, nt, preferred_element_type=F32))
    logits = r[:N_EXPERTS] + r[N_EXPERTS:]
    score = jax.nn.sigmoid(logits)
    sel = score + rbias
    rows = [sel[e:e + 1] for e in range(N_EXPERTS)]
    gscore = []
    for g in range(N_GROUPS):
        a, b, c, d = rows[4 * g:4 * g + 4]
        h1, l1, h2, l2 = jnp.maximum(a, b), jnp.minimum(a, b), jnp.maximum(c, d), jnp.minimum(c, d)
        gscore.append(jnp.maximum(h1, h2) + jnp.maximum(jnp.minimum(h1, h2), jnp.maximum(l1, l2)))
    best = gscore[0]
    gbest = jnp.zeros(best.shape, jnp.int32)
    for g in range(1, N_GROUPS):
        upd = gscore[g] > best
        gbest = jnp.where(upd, g, gbest)
        best = jnp.where(upd, gscore[g], best)
    masked = [jnp.where(gbest == e // EXPERTS_PER_GROUP, rows[e], NEG_INF) for e in range(N_EXPERTS)]
    picks = []
    for _ in range(2):
        val = jnp.full(best.shape, NEG_INF, F32)
        idx = jnp.zeros(best.shape, jnp.int32)
        for e in range(N_EXPERTS):
            cand = masked[e]
            if picks:
                cand = jnp.where(picks[0] == e, NEG_INF, cand)
            upd = cand > val
            idx = jnp.where(upd, e, idx)
            val = jnp.where(upd, cand, val)
        picks.append(idx)
    gates = [functools.reduce(jnp.add, [jnp.where(p == e, score[e:e + 1], 0.0) for e in range(N_EXPERTS)])
             for p in picks]
    den = gates[0] + gates[1]
    return picks, [gates[0] / den, gates[1] / den]


def _expert_changed(te_ref, i):
    return (i == 0) | (te_ref[i] != te_ref[jnp.maximum(i - 1, 0)])


def _pack_rows(x):
    bits = pltpu.bitcast(x.astype(BF16).astype(F32), jnp.uint32)
    half = bits.shape[1] // 2
    return pltpu.bitcast((bits[:, :half] >> 16) | (bits[:, half:] & jnp.uint32(0xFFFF0000)), jnp.int32)


def _unpack_rows(words):
    w = pltpu.bitcast(words, jnp.uint32)
    return pltpu.bitcast(w << 16, F32), pltpu.bitcast(w & jnp.uint32(0xFFFF0000), F32)


def _moe_up_kernel(te_ref, tn_ref, nu_ref, x_ref, wg_hbm, wu_hbm, h_ref, gbuf, ubuf, wgb_ref, wub_ref, sems,
                   *, layer):
    j, i = pl.program_id(0), pl.program_id(1)
    tf = gbuf.shape[1]

    def fetch(e, chunk):
        cols = pl.ds(pl.multiple_of(chunk * tf, tf), tf)
        return (pltpu.make_async_copy(wg_hbm.at[layer, e, :, cols], gbuf, sems.at[0]),
                pltpu.make_async_copy(wu_hbm.at[layer, e, :, cols], ubuf, sems.at[1]))

    def start(e, chunk):
        for copy in fetch(e, chunk):
            copy.start()

    @pl.when((i == 0) & (j == 0))
    def _():
        start(te_ref[0], 0)

    @pl.when(_expert_changed(te_ref, i))
    def _():
        for copy in fetch(te_ref[i], j):
            copy.wait()
        wgb_ref[...] = gbuf[...].astype(BF16)
        wub_ref[...] = ubuf[...].astype(BF16)
        nxt = tn_ref[i]
        more = nxt < nu_ref[0]

        @pl.when(more)
        def _():
            start(te_ref[nxt], j)

        @pl.when(jnp.logical_not(more) & (j + 1 < pl.num_programs(0)))
        def _():
            start(te_ref[0], j + 1)

    @pl.when(i < nu_ref[0])
    def _():
        xa, xb = (v.astype(BF16) for v in _unpack_rows(x_ref[...]))
        half = xa.shape[1]
        a = _dot(xa, wgb_ref[:half, :]) + _dot(xb, wgb_ref[half:, :])
        u = _dot(xa, wub_ref[:half, :]) + _dot(xb, wub_ref[half:, :])
        h_ref[...] = (a * jax.nn.sigmoid(a) * u).astype(BF16)

    @pl.when(i >= nu_ref[0])
    def _():
        h_ref[...] = jnp.zeros(h_ref.shape, BF16)


def _moe_up_call(xs, w_gate, w_up, layer, te, tn, nu, tile):
    p, dw = xs.shape
    _, _, d, f = w_gate.shape
    tf = 512
    hbm = pl.BlockSpec(memory_space=pl.ANY)
    return pl.pallas_call(
        functools.partial(_moe_up_kernel, layer=layer),
        grid_spec=pltpu.PrefetchScalarGridSpec(
            num_scalar_prefetch=3,
            grid=(f // tf, p // tile),
            in_specs=[pl.BlockSpec((tile, dw), lambda j, i, te, tn, nu: (i, 0)), hbm, hbm],
            out_specs=pl.BlockSpec((tile, tf), lambda j, i, te, tn, nu: (i, j)),
            scratch_shapes=[pltpu.VMEM((d, tf), F32), pltpu.VMEM((d, tf), F32),
                            pltpu.VMEM((d, tf), BF16), pltpu.VMEM((d, tf), BF16),
                            pltpu.SemaphoreType.DMA((2,))]),
        out_shape=jax.ShapeDtypeStruct((p, f), BF16),
        compiler_params=_cp("arbitrary", "arbitrary"),
        name="moe_up",
    )(te, tn, nu, xs, w_gate, w_up)


def _moe_down_kernel(te_ref, tn_ref, nu_ref, h_ref, wd_hbm, y_ref, wbuf, wdb_ref, sem, *, layer):
    i = pl.program_id(0)

    def fetch(e):
        return pltpu.make_async_copy(wd_hbm.at[layer, e], wbuf, sem)

    @pl.when(i == 0)
    def _():
        fetch(te_ref[0]).start()

    @pl.when(_expert_changed(te_ref, i))
    def _():
        fetch(te_ref[i]).wait()
        wdb_ref[...] = wbuf[...].astype(BF16)
        nxt = tn_ref[i]

        @pl.when(nxt < nu_ref[0])
        def _():
            fetch(te_ref[nxt]).start()

    @pl.when(i < nu_ref[0])
    def _():
        y_ref[...] = _pack_rows(_dot(h_ref[...], wdb_ref[...]))

    @pl.when(i >= nu_ref[0])
    def _():
        y_ref[...] = jnp.zeros(y_ref.shape, jnp.int32)


def _moe_down_call(h, w_down, layer, te, tn, nu, tile):
    p, f = h.shape
    d = w_down.shape[3]
    return pl.pallas_call(
        functools.partial(_moe_down_kernel, layer=layer),
        grid_spec=pltpu.PrefetchScalarGridSpec(
            num_scalar_prefetch=3,
            grid=(p // tile,),
            in_specs=[pl.BlockSpec((tile, f), lambda i, te, tn, nu: (i, 0)),
                      pl.BlockSpec(memory_space=pl.ANY)],
            out_specs=pl.BlockSpec((tile, d // 2), lambda i, te, tn, nu: (i, 0)),
            scratch_shapes=[pltpu.VMEM((f, d), F32), pltpu.VMEM((f, d), BF16), pltpu.SemaphoreType.DMA(())]),
        out_shape=jax.ShapeDtypeStruct((p, d // 2), jnp.int32),
        compiler_params=_cp("arbitrary"),
        name="moe_down",
    )(te, tn, nu, h, w_down)


PLAN_BLOCK = 512


def _plan_kernel(idx_ref, pos_ref, te_ref, tn_ref, nu_ref, rank_ref, *, tile):
    t = idx_ref.shape[1]
    nb = t // PLAN_BLOCK
    experts = lax.broadcasted_iota(jnp.int32, (N_EXPERTS, PLAN_BLOCK), 0)
    r = lax.broadcasted_iota(jnp.int32, (PLAN_BLOCK, PLAN_BLOCK), 0)
    c = lax.broadcasted_iota(jnp.int32, (PLAN_BLOCK, PLAN_BLOCK), 1)
    before = jnp.where(r < c, 1.0, 0.0).astype(BF16)

    def rank_pass(k):
        def body(j, carry):
            cols = pl.ds(pl.multiple_of(j * PLAN_BLOCK, PLAN_BLOCK), PLAN_BLOCK)
            onehot = jnp.where(experts == idx_ref[k:k + 1, cols], 1.0, 0.0)
            earlier = _dot(onehot.astype(BF16), before) + carry
            rank_ref[k:k + 1, cols] = jnp.sum(onehot * earlier, axis=0, keepdims=True)
            return carry + jnp.sum(onehot, axis=1, keepdims=True)
        return body

    counts = jnp.zeros((N_EXPERTS, 1), F32)
    for k in range(2):
        counts = lax.fori_loop(0, nb, rank_pass(k), counts)
    padded = jnp.ceil(counts * (1.0 / tile)) * tile
    starts, ends = [], []
    acc = jnp.zeros((1, 1), F32)
    for e in range(N_EXPERTS):
        starts.append(acc)
        acc = acc + padded[e:e + 1]
        ends.append(acc)

    def place_pass(k):
        def body(j, carry):
            cols = pl.ds(pl.multiple_of(j * PLAN_BLOCK, PLAN_BLOCK), PLAN_BLOCK)
            row = idx_ref[k:k + 1, cols]
            start = functools.reduce(jnp.add, [jnp.where(row == e, starts[e], 0.0) for e in range(N_EXPERTS)])
            pos_ref[k:k + 1, cols] = (rank_ref[k:k + 1, cols] + start).astype(jnp.int32)
            return carry
        return body

    for k in range(2):
        lax.fori_loop(0, nb, place_pass(k), 0)
    n_used = acc * (1.0 / tile)
    tiles = lax.broadcasted_iota(jnp.int32, te_ref.shape, 1).astype(F32)
    first_row = jnp.minimum(tiles, n_used - 1.0) * tile
    te = functools.reduce(jnp.add, [jnp.where(ends[e] <= first_row, 1.0, 0.0) for e in range(N_EXPERTS)])
    te_ref[...] = te.astype(jnp.int32)
    seg_end = functools.reduce(jnp.add, [jnp.where(te == e, ends[e], 0.0) for e in range(N_EXPERTS)])
    tn_ref[...] = (seg_end * (1.0 / tile)).astype(jnp.int32)
    nu_ref[...] = jnp.broadcast_to(n_used, nu_ref.shape).astype(jnp.int32)


def _plan_call(idx, tile, n_tiles):
    t = idx.shape[1]
    whole = lambda shape: pl.BlockSpec(shape, lambda: (0,) * len(shape))
    lanes = jax.ShapeDtypeStruct((1, LANES), jnp.int32)
    pos, te, tn, nu = pl.pallas_call(
        functools.partial(_plan_kernel, tile=tile),
        in_specs=[whole(idx.shape)],
        out_specs=[whole(idx.shape), whole((1, LANES)), whole((1, LANES)), whole((1, LANES))],
        out_shape=[jax.ShapeDtypeStruct(idx.shape, jnp.int32), lanes, lanes, lanes],
        scratch_shapes=[pltpu.VMEM(idx.shape, F32)],
        name="route_plan",
    )(idx)
    return pos.reshape(2 * t), te[0, :n_tiles], tn[0, :n_tiles], nu[0, :1]


V7X_SC_CORES = 2
V7X_SC_SUBCORES = 16
SC_WORKERS = V7X_SC_CORES * V7X_SC_SUBCORES
SC_ROW_BYTES = 256 * 1024


def _sc_worker():
    return lax.axis_index("s") * V7X_SC_CORES + lax.axis_index("c")


def _sc_dispatch_call(streams, pos, n_rows):
    w = streams[0].shape[1]
    sizes = [s.shape[0] for s in streams]
    t = sum(sizes)
    per_w = t // SC_WORKERS
    ch = min(per_w, SC_ROW_BYTES // (4 * w))
    assert per_w % ch == 0 and all(size % ch == 0 for size in sizes)
    mesh = plsc.VectorSubcoreMesh(core_axis_name="c", subcore_axis_name="s")

    @functools.partial(pl.kernel, mesh=mesh, out_type=jax.ShapeDtypeStruct((n_rows, w), jnp.int32),
                       scratch_types=[pltpu.VMEM((ch,), jnp.int32), pltpu.VMEM((ch,), jnp.int32),
                                      pltpu.VMEM((ch, w), jnp.int32),
                                      pltpu.SemaphoreType.DMA, pltpu.SemaphoreType.DMA],
                       name="sc_dispatch")
    def run(*refs):
        srcs = refs[:len(sizes)]
        pos_hbm, xs_hbm, idx0_v, idx1_v, rows_v, sem0, sem1 = refs[len(sizes):]
        first = _sc_worker() * per_w

        @pl.loop(0, per_w // ch)
        def _(j):
            base = first + j * ch
            start = 0
            for src, size in zip(srcs, sizes):
                def load(src=src, start=start):
                    pltpu.sync_copy(src.at[pl.ds(base - start, ch)], rows_v)
                if len(sizes) == 1:
                    load()
                else:
                    pl.when((base >= start) & (base < start + size))(load)
                start += size
            pltpu.sync_copy(pos_hbm.at[pl.ds(base, ch)], idx0_v)
            pltpu.sync_copy(pos_hbm.at[pl.ds(t + base, ch)], idx1_v)
            first_choice = pltpu.async_copy(rows_v, xs_hbm.at[idx0_v], sem0)
            second_choice = pltpu.async_copy(rows_v, xs_hbm.at[idx1_v], sem1)
            first_choice.wait()
            second_choice.wait()

    return run(*streams, pos)


def _sc_combine_call(ys, pos):
    _, d = ys.shape
    n = pos.shape[0]
    per_w = n // SC_WORKERS
    ch = min(per_w // 2, SC_ROW_BYTES // (8 * d))
    assert per_w % (2 * ch) == 0
    mesh = plsc.VectorSubcoreMesh(core_axis_name="c", subcore_axis_name="s")
    idx_buf = pltpu.VMEM((ch,), jnp.int32)
    row_buf = pltpu.VMEM((ch, d), jnp.int32)

    @functools.partial(pl.kernel, mesh=mesh, out_type=jax.ShapeDtypeStruct((n, d), jnp.int32),
                       scratch_types=[idx_buf, idx_buf, row_buf, row_buf] + [pltpu.SemaphoreType.DMA] * 4,
                       name="sc_combine")
    def run(ys_hbm, pos_hbm, out_hbm, idx_a, idx_b, rows_a, rows_b, gsem_a, gsem_b, wsem_a, wsem_b):
        first = _sc_worker() * per_w

        @pl.loop(0, per_w // (2 * ch))
        def _(j):
            base = first + j * (2 * ch)
            pltpu.sync_copy(pos_hbm.at[pl.ds(base, ch)], idx_a)
            pltpu.sync_copy(pos_hbm.at[pl.ds(base + ch, ch)], idx_b)
            gather_a = pltpu.async_copy(ys_hbm.at[idx_a], rows_a, gsem_a)
            gather_b = pltpu.async_copy(ys_hbm.at[idx_b], rows_b, gsem_b)
            gather_a.wait()
            write_a = pltpu.async_copy(rows_a, out_hbm.at[pl.ds(base, ch)], wsem_a)
            gather_b.wait()
            write_b = pltpu.async_copy(rows_b, out_hbm.at[pl.ds(base + ch, ch)], wsem_b)
            write_a.wait()
            write_b.wait()

    return run(ys, pos)


def _moe_call(streams, idx, w_gate, w_up, w_down, layer):
    t = idx.shape[1]
    tile = min(MOE_TILE, max(t // N_EXPERTS, 128))
    n_tiles = (2 * t) // tile + N_EXPERTS
    assert n_tiles <= LANES
    pos, te, tn, nu = _plan_call(idx, tile, n_tiles)
    xs = _sc_dispatch_call(streams, pos, n_tiles * tile)
    hid = _moe_up_call(xs, w_gate, w_up, layer, te, tn, nu, tile)
    ys = _moe_down_call(hid, w_down, layer, te, tn, nu, tile)
    return _sc_combine_call(ys, pos).reshape(2, t, ys.shape[1])


def _moe_resid_kernel(x_ref, y0_ref, y1_ref, gt_ref, g_ref, *rest, final):
    gt = gt_ref[0]
    halves0 = _unpack_rows(y0_ref[0])
    halves1 = _unpack_rows(y1_ref[0])
    y = jnp.concatenate([gt[:, 0:1] * a + gt[:, 1:2] * b for a, b in zip(halves0, halves1)], axis=1)
    x = x_ref[0] + g_ref[0] * y
    if final:
        fg_ref, o_ref = rest
        x = _rms(x, fg_ref[...])
    else:
        (o_ref,) = rest
    o_ref[0] = x


def _moe_resid_call(x, y, first_row, gates, gate, final_g=None):
    b, l, d = x.shape
    tl = min(l, 512)
    nt = l // tl
    tile = pl.BlockSpec((1, tl, d), lambda i, j: (i, j, 0))
    choice = lambda k: pl.BlockSpec((1, tl, d // 2), lambda i, j: (k, first_row // tl + i * nt + j, 0))
    in_specs = [tile, choice(0), choice(1), pl.BlockSpec((1, tl, 2), lambda i, j: (i, j, 0)),
                pl.BlockSpec((1, 1, d), lambda i, j: (i, 0, 0))]
    args = [x, y, y, gates.T.reshape(b, l, 2), gate]
    if final_g is not None:
        in_specs.append(pl.BlockSpec((1, d), lambda i, j: (0, 0)))
        args.append(final_g.reshape(1, d))
    return pl.pallas_call(
        functools.partial(_moe_resid_kernel, final=final_g is not None),
        grid=(b, l // tl),
        in_specs=in_specs,
        out_specs=tile,
        out_shape=jax.ShapeDtypeStruct((b, l, d), F32),
        compiler_params=_cp("parallel", "parallel"),
        name="moe_residual",
    )(*args)


def _rope_perm(w):
    r = w.reshape(w.shape[:-1] + (2, 2, ROPE_FREQS))
    return jnp.stack([-r[..., 1, :], r[..., 0, :]], axis=-2).reshape(w.shape)


def _rope_table(l, rotary):
    if not rotary:
        return jnp.concatenate([jnp.ones((l, MLA_ROPE), F32), jnp.zeros((l, MLA_ROPE), F32)], axis=1)
    rows = l // GRID_W
    r = jnp.repeat(jnp.arange(rows), GRID_W)
    col = jnp.tile(jnp.arange(GRID_W), rows)
    pos = jnp.stack([r, col], axis=-1).astype(F32)
    inv = ROPE_THETA ** (-jnp.arange(ROPE_FREQS, dtype=F32) / ROPE_FREQS)
    ang = pos[:, :, None] * inv
    lay = lambda a: jnp.concatenate([a[:, 0], a[:, 0], a[:, 1], a[:, 1]], axis=1)
    return jnp.concatenate([lay(jnp.cos(ang)), lay(jnp.sin(ang))], axis=1)


def _prep_weights(w_in, w_uq, w_ukv, w_out):
    depth = w_in.shape[0]
    cast = lambda a: a.astype(BF16)
    kr = cast(w_in[:, :, OFF_KR:OFF_HY])
    w_kv_in = jnp.concatenate([cast(w_in[:, :, OFF_KV:OFF_KR]), kr, _rope_perm(kr)], axis=2)
    w_all = jnp.concatenate([cast(w_in[:, :, :OFF_KV]), cast(w_in[:, :, OFF_HY:]), w_kv_in], axis=2)
    uq = cast(w_uq).reshape(depth, MLA_Q_RANK, MLA_HEADS, MLA_NOPE + MLA_ROPE)
    w_q = jnp.concatenate([uq, _rope_perm(uq[..., MLA_NOPE:])], axis=3).reshape(depth, MLA_Q_RANK, -1)
    ukv = cast(w_ukv).reshape(depth, MLA_KV_RANK, MLA_HEADS, MLA_NOPE + MLA_V_DIM)
    w_k = ukv[..., :MLA_NOPE].reshape(depth, MLA_KV_RANK, MLA_HEADS * MLA_NOPE)
    w_v = ukv[..., MLA_NOPE:].reshape(depth, MLA_KV_RANK, MLA_HEADS * MLA_V_DIM)
    return dict(all=w_all, kv=w_kv_in, uq=w_q, uk=w_k, uv=w_v, out=cast(w_out))


def _router_weights(w_router):
    hi = w_router.astype(BF16)
    lo = (w_router - hi.astype(F32)).astype(BF16)
    return jnp.concatenate([hi.T, lo.T], axis=0)


def _mixer(x, mod, w, layer, lp, cs, spec, fwd, inv, kv_extra, full, wr, rbias):
    kv_args = (lp["kv_norm_g"], w["uk"], w["uv"], cs, layer)
    if not full:
        k, v = _in_proj_call(x, lp["norm1_g"], mod[0], mod[1], w["kv"], [], *kv_args)
        return None, k, v
    splits = [OFF_Q, MLA_Q_RANK, (HY_ORDER + 1) * HYENA_W]
    p_conv, p_q, p_hy, k, v = _in_proj_call(x, lp["norm1_g"], mod[0], mod[1], w["all"], splits, *kv_args)
    conv_o = _conformer_call(p_conv, lp["conv_dw_w"], lp["conv_dw_b"], lp["conv_ln_g"], lp["conv_ln_b"])
    att_o = _attn_call(p_q, lp["q_norm_g"], w["uq"], cs, kv_extra + [(k, v)], layer)
    hy_o = _hyena_call(p_hy, lp["hy_short_w"], lp["hy_short_b"], spec, lp["hy_bias"], fwd, inv)
    routed = _mix_out_call(conv_o, att_o, hy_o, w["out"], layer, x, mod, lp["norm2_g"], wr, rbias)
    return routed, k, v


def _channel(streams, experts, layer, final_g=None):
    d = streams[0][1][0].shape[2]
    idx = jnp.concatenate([r[2] for _, r in streams], axis=1)
    y = _moe_call([r[1].reshape(-1, d // 2) for _, r in streams], idx, *experts, layer)
    outs = []
    first_row = 0
    for mod, (x, _, _, gates) in streams:
        outs.append(_moe_resid_call(x, y, first_row, gates, mod[5], final_g))
        first_row += x.shape[0] * x.shape[1]
    return outs


def _filter_spectrum(l, lp, fwd):
    a = _hy_filter_call(l, lp["hy_w1"], lp["hy_b1"], lp["hy_freq1"], lp["hy_w2"], lp["hy_b2"], lp["hy_freq2"],
                        lp["hy_w3"], lp["hy_b3"])
    return _spec_call(fwd, a)


def kernel(x, c, ctx, c_ctx, norm1_g, norm2_g, w_ada, b_ada, w_in, conv_dw_w, conv_dw_b, conv_ln_g, conv_ln_b, q_norm_g, w_uq, kv_norm_g, w_ukv, hy_short_w, hy_short_b, hy_w1, hy_b1, hy_freq1, hy_w2, hy_b2, hy_freq2, hy_w3, hy_b3, hy_bias, w_out, w_router, router_bias, w_gate, w_up, w_down, final_norm_g):
    b, l, d = x.shape
    lc = ctx.shape[1]
    depth = w_in.shape[0]
    per_layer = dict(norm1_g=norm1_g, norm2_g=norm2_g, conv_dw_w=conv_dw_w, conv_dw_b=conv_dw_b,
                     conv_ln_g=conv_ln_g, conv_ln_b=conv_ln_b, q_norm_g=q_norm_g, kv_norm_g=kv_norm_g,
                     hy_short_w=hy_short_w, hy_short_b=hy_short_b, hy_w1=hy_w1, hy_b1=hy_b1, hy_freq1=hy_freq1,
                     hy_w2=hy_w2, hy_b2=hy_b2, hy_freq2=hy_freq2, hy_w3=hy_w3, hy_b3=hy_b3, hy_bias=hy_bias)
    experts = (w_gate, w_up, w_down)
    cs_l = _rope_table(l, True)
    cs_c = _rope_table(lc, False)
    fwd_l, inv_l = _dft_mats(l)
    fwd_c, inv_c = _dft_mats(lc)
    wr = _router_weights(w_router)
    w = _prep_weights(w_in, w_uq, w_ukv, w_out)
    rows = 16
    cvec = jnp.concatenate([c, c_ctx[None, :], jnp.zeros((rows - b - 1, d), F32)], axis=0)
    xl, xc = x, ctx
    for i in range(depth):
        last = i == depth - 1
        lp = {k: v[i] for k, v in per_layer.items()}
        m = _ada_call(cvec, w_ada, b_ada, i)
        mod_l = [m[:b, None, j * d:(j + 1) * d] for j in range(6)]
        mod_c = [jnp.broadcast_to(m[b:b + 1, None, j * d:(j + 1) * d], (b, 1, d)) for j in range(6)]
        spec_l = _filter_spectrum(l, lp, fwd_l)
        spec_c = None if last else _filter_spectrum(lc, lp, fwd_c)
        routed_c, kc, vc = _mixer(xc, mod_c, w, i, lp, cs_c, spec_c, fwd_c, inv_c, [], not last, wr, router_bias)
        routed_l, _, _ = _mixer(xl, mod_l, w, i, lp, cs_l, spec_l, fwd_l, inv_l, [(kc, vc)], True, wr, router_bias)
        if last:
            (xl,) = _channel([(mod_l, routed_l)], experts, i, final_norm_g)
        else:
            xl, xc = _channel([(mod_l, routed_l), (mod_c, routed_c)], experts, i)
    return xl
```

```python
import functools
import math

import jax
import jax.numpy as jnp
from jax import lax
from jax.experimental import pallas as pl
from jax.experimental.pallas import tpu as pltpu
from jax.experimental.pallas import tpu_sc as plsc

F32 = jnp.float32
BF16 = jnp.bfloat16

EPS = 1e-6
GRID_W = 64
CONV_W = 512
CONV_KSIZE = 31
MLA_HEADS = 8
MLA_NOPE = 128
MLA_ROPE = 64
MLA_V_DIM = 128
MLA_Q_RANK = 768
MLA_KV_RANK = 512
ROPE_FREQS = MLA_ROPE // 4
ROPE_THETA = 10000.0
MLA_SCALE = (MLA_NOPE + MLA_ROPE) ** -0.5
Q_SCALE = MLA_SCALE * math.log2(math.e)
HYENA_W = 512
HY_ORDER = 2
HY_SHORT = 3
HY_EMB = 33
HY_BANDS = (HY_EMB - 1) // 2
HY_FFN = 64
HY_MIN_DECAY = math.log(1e-2) / 1.5
HY_MAX_DECAY = math.log(1e-2) / 0.3
N_EXPERTS = 16
N_GROUPS = 4
EXPERTS_PER_GROUP = N_EXPERTS // N_GROUPS
OFF_Q = 2 * CONV_W
OFF_KV = OFF_Q + MLA_Q_RANK
OFF_KR = OFF_KV + MLA_KV_RANK
OFF_HY = OFF_KR + MLA_ROPE

LANES = 128
QK_HEAD = 2 * LANES
V_HEAD = MLA_V_DIM + LANES
V7X_VMEM_BYTES = 64 * 1024 * 1024
VMEM_LIMIT = (V7X_VMEM_BYTES * 7) // 8
MOE_TILE = 512
NEG_INF = float("-inf")


def _cp(*sem):
    return pltpu.CompilerParams(dimension_semantics=sem, vmem_limit_bytes=VMEM_LIMIT)


def _dot(a, b):
    return jnp.dot(a, b, preferred_element_type=F32)


def _rms(x, g):
    ms = jnp.mean(x * x, axis=-1, keepdims=True)
    return x * lax.rsqrt(ms + EPS) * g


def _ada_kernel(c_ref, w_ref, b_ref, o_ref):
    c = c_ref[...]
    a = (c * jax.nn.sigmoid(c)).astype(BF16)
    o_ref[...] = _dot(a, w_ref[0].astype(BF16)) + b_ref[0]


def _ada_call(cvec, w, b, layer):
    m, d = cvec.shape
    n = w.shape[2]
    tn = 1024
    return pl.pallas_call(
        _ada_kernel,
        grid=(n // tn,),
        in_specs=[pl.BlockSpec((m, d), lambda j: (0, 0)),
                  pl.BlockSpec((1, d, tn), lambda j: (layer, 0, j)),
                  pl.BlockSpec((1, 1, tn), lambda j: (layer, 0, j))],
        out_specs=pl.BlockSpec((m, tn), lambda j: (0, j)),
        out_shape=jax.ShapeDtypeStruct((m, n), F32),
        compiler_params=_cp("parallel"),
        name="ada",
    )(cvec, w, b.reshape(b.shape[0], 1, n))


def _in_proj_kernel(x_ref, g_ref, sh_ref, sc_ref, w_ref, kg_ref, wk_ref, wv_ref, cs_ref, *o_refs):
    *p_refs, k_ref, v_ref = o_refs
    y = _rms(x_ref[0], g_ref[...])
    h = (y * (1.0 + sc_ref[0]) + sh_ref[0]).astype(BF16)
    off = 0
    for o_ref in p_refs:
        n = o_ref.shape[-1]
        o_ref[0] = _dot(h, w_ref[0, :, off:off + n]).astype(o_ref.dtype)
        off += n
    p = _dot(h, w_ref[0, :, off:])
    xn = _rms(p[:, :MLA_KV_RANK], kg_ref[...]).astype(BF16)
    kk = _dot(xn, wk_ref[0])
    vv = _dot(xn, wv_ref[0])
    rot = _rotate(p[:, MLA_KV_RANK:], cs_ref[...])
    lane = lax.broadcasted_iota(jnp.int32, rot.shape, 1)
    rot = jnp.where(lane < MLA_ROPE, rot, 0.0).astype(BF16)
    ones = jnp.where(lane == 0, 1.0, 0.0).astype(BF16)
    for hd in range(MLA_HEADS):
        k_ref[0, :, hd * QK_HEAD:hd * QK_HEAD + LANES] = kk[:, hd * LANES:(hd + 1) * LANES].astype(BF16)
        k_ref[0, :, hd * QK_HEAD + LANES:(hd + 1) * QK_HEAD] = rot
        v_ref[0, :, hd * V_HEAD:hd * V_HEAD + MLA_V_DIM] = vv[:, hd * MLA_V_DIM:(hd + 1) * MLA_V_DIM].astype(BF16)
        v_ref[0, :, hd * V_HEAD + MLA_V_DIM:(hd + 1) * V_HEAD] = ones


def _in_proj_call(x, g, shift, scale, w, splits, kv_g, w_k, w_v, cs, layer):
    b, l, d = x.shape
    n = w.shape[2]
    tm = min(l, 512)
    r = MLA_KV_RANK
    row = pl.BlockSpec((1, 1, d), lambda i, j: (i, 0, 0))
    const = lambda shape: pl.BlockSpec(shape, lambda i, j: (0, 0))
    of_layer = lambda a, **kw: pl.BlockSpec((1,) + a.shape[1:], lambda i, j: (layer, 0, 0), **kw)
    widths = list(splits) + [MLA_HEADS * QK_HEAD, MLA_HEADS * V_HEAD]
    return pl.pallas_call(
        _in_proj_kernel,
        grid=(b, l // tm),
        in_specs=[pl.BlockSpec((1, tm, d), lambda i, j: (i, j, 0)),
                  const((1, d)), row, row,
                  of_layer(w, pipeline_mode=pl.Buffered(1)),
                  const((1, r)), of_layer(w_k), of_layer(w_v),
                  pl.BlockSpec((tm, LANES), lambda i, j: (j, 0))],
        out_specs=[pl.BlockSpec((1, tm, k), lambda i, j: (i, j, 0)) for k in widths],
        out_shape=[jax.ShapeDtypeStruct((b, l, k), BF16) for k in widths],
        compiler_params=_cp("parallel", "parallel"),
        name="in_proj",
    )(x, g.reshape(1, d), shift, scale, w, kv_g.reshape(1, r), w_k, w_v, cs)


CONV_ROWS = 128
CONV_PAD = 16


def _conformer_kernel(p_ref, w_ref, b_ref, g_ref, be_ref, o_ref, zp_ref):
    l = p_ref.shape[1]
    c = CONV_W
    zeros = jnp.zeros((CONV_PAD, c), F32)
    zp_ref[0:CONV_PAD, :] = zeros
    zp_ref[CONV_PAD + l:CONV_PAD + l + CONV_PAD, :] = zeros

    def glu(r, carry):
        base = pl.multiple_of(r * CONV_ROWS, CONV_ROWS)
        blk = p_ref[0, pl.ds(base, CONV_ROWS), :].astype(F32)
        zp_ref[pl.ds(base + CONV_PAD, CONV_ROWS), :] = blk[:, :c] * jax.nn.sigmoid(blk[:, c:])
        return carry

    lax.fori_loop(0, l // CONV_ROWS, glu, 0)

    win_rows = CONV_ROWS + 2 * CONV_PAD
    first = CONV_PAD - CONV_KSIZE // 2

    def conv(r, carry):
        base = pl.multiple_of(r * CONV_ROWS, CONV_ROWS)
        parts = []
        for cc in range(c // LANES):
            lanes = slice(cc * LANES, (cc + 1) * LANES)
            win = zp_ref[pl.ds(base, win_rows), lanes]
            acc = jnp.zeros((CONV_ROWS, LANES), F32)
            for res in range(8):
                taps = [k for k in range(CONV_KSIZE) if (k + first) % 8 == res]
                if not taps:
                    continue
                shifted = pltpu.roll(win, win_rows - res, axis=0) if res else win
                for k in taps:
                    off = k + first - res
                    acc = acc + shifted[off:off + CONV_ROWS, :] * w_ref[k:k + 1, lanes]
            parts.append(acc)
        z = jnp.concatenate(parts, axis=1) + b_ref[...]
        mu = jnp.mean(z, axis=-1, keepdims=True)
        zc = z - mu
        var = jnp.mean(zc * zc, axis=-1, keepdims=True)
        y = zc * lax.rsqrt(var + EPS) * g_ref[...] + be_ref[...]
        o_ref[0, pl.ds(base, CONV_ROWS), :] = (y * jax.nn.sigmoid(y)).astype(o_ref.dtype)
        return carry

    lax.fori_loop(0, l // CONV_ROWS, conv, 0)


def _conformer_call(p, dw_w, dw_b, ln_g, ln_b):
    b, l, c2 = p.shape
    c = c2 // 2
    vec = pl.BlockSpec((1, c), lambda i: (0, 0))
    return pl.pallas_call(
        _conformer_kernel,
        grid=(b,),
        in_specs=[pl.BlockSpec((1, l, c2), lambda i: (i, 0, 0)),
                  pl.BlockSpec((CONV_KSIZE, c), lambda i: (0, 0)), vec, vec, vec],
        out_specs=pl.BlockSpec((1, l, c), lambda i: (i, 0, 0)),
        out_shape=jax.ShapeDtypeStruct((b, l, c), BF16),
        scratch_shapes=[pltpu.VMEM((l + 2 * CONV_PAD, c), F32)],
        compiler_params=_cp("parallel"),
        name="conformer",
    )(p, dw_w, dw_b.reshape(1, c), ln_g.reshape(1, c), ln_b.reshape(1, c))


def _rotate(group, cs):
    t = group * cs
    return t + pltpu.roll(t, MLA_ROPE, axis=1)


def _attn_kernel(p_ref, g_ref, w_ref, cs_ref, *refs, n_seg):
    o_ref, q_ref = refs[2 * n_seg], refs[2 * n_seg + 1]
    xn = _rms(p_ref[0].astype(F32), g_ref[...]).astype(BF16)
    cs = cs_ref[...] * Q_SCALE
    for h in range(MLA_HEADS):
        r = _dot(xn, w_ref[0, :, h * QK_HEAD:(h + 1) * QK_HEAD])
        q_ref[:, h * QK_HEAD:h * QK_HEAD + LANES] = (r[:, :LANES] * Q_SCALE).astype(BF16)
        q_ref[:, h * QK_HEAD + LANES:(h + 1) * QK_HEAD] = _rotate(r[:, LANES:], cs).astype(BF16)
    nt = (((1,), (1,)), ((), ()))
    for h in range(MLA_HEADS):
        qk = slice(h * QK_HEAD, (h + 1) * QK_HEAD)
        vd = slice(h * MLA_V_DIM, (h + 1) * MLA_V_DIM)
        q = q_ref[:, qk]
        scores = [lax.dot_general(q, refs[2 * s][0, :, qk], nt, preferred_element_type=F32)
                  for s in range(n_seg)]
        m = functools.reduce(jnp.maximum, [jnp.max(s, axis=-1, keepdims=True) for s in scores])
        acc = 0.0
        for s in range(n_seg):
            p = jnp.exp2(scores[s] - m).astype(BF16)
            acc = acc + _dot(p, refs[2 * s + 1][0, :, h * V_HEAD:(h + 1) * V_HEAD])
        o_ref[0, :, vd] = (acc[:, :MLA_V_DIM] * (1.0 / acc[:, MLA_V_DIM:MLA_V_DIM + 1])).astype(o_ref.dtype)


def _attn_call(p_q, q_g, w_q, cs, segs, layer):
    b, l, r = p_q.shape
    tq = min(l, 512)
    const = lambda shape: pl.BlockSpec(shape, lambda i, j: (0, 0))
    in_specs = [pl.BlockSpec((1, tq, r), lambda i, j: (i, j, 0)), const((1, r)),
                pl.BlockSpec((1,) + w_q.shape[1:], lambda i, j: (layer, 0, 0)),
                pl.BlockSpec((tq, LANES), lambda i, j: (j, 0))]
    args = [p_q, q_g.reshape(1, r), w_q, cs]
    for k, v in segs:
        in_specs.append(pl.BlockSpec((1,) + k.shape[1:], lambda i, j: (i, 0, 0), pipeline_mode=pl.Buffered(1)))
        in_specs.append(pl.BlockSpec((1,) + v.shape[1:], lambda i, j: (i, 0, 0), pipeline_mode=pl.Buffered(1)))
        args += [k, v]
    return pl.pallas_call(
        functools.partial(_attn_kernel, n_seg=len(segs)),
        grid=(b, l // tq),
        in_specs=in_specs,
        out_specs=pl.BlockSpec((1, tq, MLA_HEADS * MLA_V_DIM), lambda i, j: (i, j, 0)),
        out_shape=jax.ShapeDtypeStruct((b, l, MLA_HEADS * MLA_V_DIM), BF16),
        scratch_shapes=[pltpu.VMEM((tq, MLA_HEADS * QK_HEAD), BF16)],
        compiler_params=_cp("parallel", "parallel"),
        name="attention",
    )(*args)


HY_PAD = 8
HY_ROWS = 128


def _hy_short_kernel(p_ref, w_ref, b_ref, o_ref, up_ref):
    l = p_ref.shape[1]
    c = p_ref.shape[2]
    zeros = jnp.zeros((HY_PAD, c), F32)
    up_ref[0:HY_PAD, :] = zeros
    up_ref[HY_PAD + l:HY_PAD + l + HY_PAD, :] = zeros

    def copy(r, carry):
        base = pl.multiple_of(r * HY_ROWS, HY_ROWS)
        up_ref[pl.ds(base + HY_PAD, HY_ROWS), :] = p_ref[0, pl.ds(base, HY_ROWS), :].astype(F32)
        return carry

    lax.fori_loop(0, l // HY_ROWS, copy, 0)
    first = HY_PAD - HY_SHORT // 2

    def conv(r, carry):
        base = pl.multiple_of(r * HY_ROWS, HY_ROWS)
        for cc in range(c // LANES):
            lanes = slice(cc * LANES, (cc + 1) * LANES)
            win = up_ref[pl.ds(base, HY_ROWS + 2 * HY_PAD), lanes]
            acc = jnp.zeros((HY_ROWS, LANES), F32) + b_ref[:, lanes]
            for k in range(HY_SHORT):
                acc = acc + win[first + k:first + k + HY_ROWS, :] * w_ref[k:k + 1, lanes]
            o_ref[0, 0, pl.ds(base, HY_ROWS), lanes] = acc.astype(o_ref.dtype)
        return carry

    lax.fori_loop(0, l // HY_ROWS, conv, 0)


def _hy_short_call(p, w, bias):
    b, l, c3 = p.shape
    c = HYENA_W
    return pl.pallas_call(
        _hy_short_kernel,
        grid=(b, c3 // c),
        in_specs=[pl.BlockSpec((1, l, c), lambda i, g: (i, 0, g)),
                  pl.BlockSpec((HY_SHORT, c), lambda i, g: (0, g)),
                  pl.BlockSpec((1, c), lambda i, g: (0, g))],
        out_specs=pl.BlockSpec((1, 1, l, c), lambda i, g: (g, i, 0, 0)),
        out_shape=jax.ShapeDtypeStruct((c3 // c, b, l, c), BF16),
        scratch_shapes=[pltpu.VMEM((l + 2 * HY_PAD, c), F32)],
        compiler_params=_cp("parallel", "parallel"),
        name="hyena_short",
    )(p, w, bias.reshape(1, c3))


def _hy_filter_kernel(z_ref, w1_ref, b1_ref, f1_ref, w2_ref, b2_ref, f2_ref, w3_ref, b3_ref, win_ref, o_ref):
    hp = lax.Precision.HIGHEST

    def mm(a, b):
        return jnp.dot(a, b, preferred_element_type=F32, precision=hp)

    h = jnp.sin(f1_ref[...] * (mm(z_ref[...], w1_ref[...]) + b1_ref[...]))
    h = jnp.sin(f2_ref[...] * (mm(h, w2_ref[...]) + b2_ref[...]))
    h = mm(h, w3_ref[...]) + b3_ref[...]
    c = HYENA_W
    win = win_ref[...]
    for s in range(h.shape[1] // c):
        o_ref[:, s * c:(s + 1) * c] = h[:, s * c:(s + 1) * c] * win


def _pad2(a, rows, cols):
    return jnp.pad(a, ((0, rows - a.shape[0]), (0, cols - a.shape[1])))


def _hy_filter_call(l, w1, b1, f1, w2, b2, f2, w3, b3):
    t = jnp.arange(l, dtype=F32)
    t_norm = t / max(l - 1, 1)
    ang = (2.0 * math.pi / l) * t[:, None] * jnp.linspace(1e-4, HY_BANDS - 1, HY_BANDS, dtype=F32)[None, :]
    z = jnp.concatenate([t_norm[:, None], jnp.cos(ang), -jnp.sin(ang)], axis=-1)
    deltas = jnp.abs(jnp.linspace(HY_MIN_DECAY, HY_MAX_DECAY, HYENA_W, dtype=F32))
    window = jnp.exp(-t_norm[:, None] * deltas[None, :])
    n3 = w3.shape[1]
    tl = min(l, 256)
    full = lambda shape: pl.BlockSpec(shape, lambda i: (0, 0))
    return pl.pallas_call(
        _hy_filter_kernel,
        grid=(l // tl,),
        in_specs=[pl.BlockSpec((tl, LANES), lambda i: (i, 0)),
                  full((LANES, LANES)), full((1, LANES)), full((1, LANES)),
                  full((LANES, LANES)), full((1, LANES)), full((1, LANES)),
                  full((LANES, n3)), full((1, n3)),
                  pl.BlockSpec((tl, HYENA_W), lambda i: (i, 0))],
        out_specs=pl.BlockSpec((tl, n3), lambda i: (i, 0)),
        out_shape=jax.ShapeDtypeStruct((l, n3), F32),
        compiler_params=_cp("parallel"),
        name="hyena_filter",
    )(_pad2(z, l, LANES), _pad2(w1, LANES, LANES), _pad2(b1[None], 1, LANES), _pad2(f1[None], 1, LANES),
      _pad2(w2, LANES, LANES), _pad2(b2[None], 1, LANES), _pad2(f2[None], 1, LANES),
      _pad2(w3, LANES, n3), b3[None], window)


def _dft_tables_kernel(ac_ref, as_ref, bc_ref, bs_ref, fwd_ref, inv_ref):
    tm, l = fwd_ref.shape[1], fwd_ref.shape[2]
    n = 2 * l
    row = lax.broadcasted_iota(jnp.int32, (tm, LANES), 0) + pl.program_id(0) * tm
    lane = lax.broadcasted_iota(jnp.int32, (tm, LANES), 1)
    row_sign = jnp.where((row & 1) == 0, 1.0, -1.0)
    lane_sign = jnp.where((lane & 1) == 0, 1.0, -1.0)
    bc = bc_ref[...]
    bs = bs_ref[...]
    for a in range(l // LANES):
        cols = slice(a * LANES, (a + 1) * LANES)
        ca = ac_ref[:, a:a + 1]
        sa = as_ref[:, a:a + 1]
        c = ca * bc - sa * bs
        s = sa * bc + ca * bs
        fwd_ref[0, :, cols] = c.astype(BF16)
        fwd_ref[1, :, cols] = jnp.where(row == 0, lane_sign, -s).astype(BF16)
        inv_c = c * (2.0 / n)
        inv_s = -s * (2.0 / n)
        if a == 0:
            inv_c = jnp.where(lane == 0, 1.0 / n, inv_c)
            inv_s = jnp.where(lane == 0, row_sign / n, inv_s)
        inv_ref[:, cols] = inv_c.astype(BF16)
        inv_ref[:, l + a * LANES:l + (a + 1) * LANES] = inv_s.astype(BF16)


def _dft_mats(l):
    n = 2 * l
    f = jnp.arange(l, dtype=jnp.int32)[:, None]

    def table(t):
        ang = ((f * t[None, :]) % n).astype(F32) * (2.0 * math.pi / n)
        return jnp.cos(ang), jnp.sin(ang)

    coarse = table(jnp.arange(l // LANES, dtype=jnp.int32) * LANES)
    fine = table(jnp.arange(LANES, dtype=jnp.int32))
    tm = min(l, 256)
    small = pl.BlockSpec((tm, l // LANES), lambda i: (i, 0))
    wide = pl.BlockSpec((tm, LANES), lambda i: (i, 0))
    return pl.pallas_call(
        _dft_tables_kernel,
        grid=(l // tm,),
        in_specs=[small, small, wide, wide],
        out_specs=[pl.BlockSpec((2, tm, l), lambda i: (0, i, 0)), pl.BlockSpec((tm, n), lambda i: (i, 0))],
        out_shape=[jax.ShapeDtypeStruct((2, l, l), BF16), jax.ShapeDtypeStruct((l, n), BF16)],
        compiler_params=_cp("parallel"),
        name="dft_tables",
    )(*coarse, *fine)


def _conv_blocks(l):
    return 2 if l >= 1024 else 1


def _lag_windows(filt, nb):
    l = filt.shape[0]
    lb = l // nb
    f4 = filt.reshape(l, HY_ORDER, 2, HYENA_W)
    fwd = f4[:, :, 0].reshape(l, HY_ORDER * HYENA_W)
    bwd = f4[:, :, 1].reshape(l, HY_ORDER * HYENA_W)
    zero = jnp.zeros((1, fwd.shape[1]), F32)
    lags = jnp.concatenate([zero, bwd[:0:-1], fwd], axis=0)
    a, b = [], []
    for delta in range(1 - nb, nb):
        o = l + delta * lb
        a.append(lags[o:o + lb])
        b.append(jnp.concatenate([zero, lags[o - lb + 1:o][::-1]], axis=0))
    return jnp.stack(a), jnp.stack(b)


def _spec_kernel(f_ref, a_ref, b_ref, o_ref):
    a = a_ref[0]
    b = b_ref[0]
    s = (a + b).astype(BF16)
    hi = _dot(f_ref[1], (a - b).astype(BF16))
    o_ref[0, 0] = _dot(f_ref[0], s)
    o_ref[0, 1] = hi

    @pl.when(pl.program_id(1) == 0)
    def _():
        nyq = _dot(f_ref[1, 0:16, :], s)
        row = lax.broadcasted_iota(jnp.int32, nyq.shape, 0)
        o_ref[0, 1, 0:16, :] = jnp.where(row == 0, nyq, hi[0:16])


def _spec_call(fwd, a, b):
    _, lf, lb = fwd.shape
    nd, _, n = a.shape
    tm = min(lf, 512)
    win = pl.BlockSpec((1, lb, n), lambda d, i: (d, 0, 0))
    return pl.pallas_call(
        _spec_kernel,
        grid=(nd, lf // tm),
        in_specs=[pl.BlockSpec((2, tm, lb), lambda d, i: (0, i, 0)), win, win],
        out_specs=pl.BlockSpec((1, 2, tm, n), lambda d, i: (d, 0, i, 0)),
        out_shape=jax.ShapeDtypeStruct((nd, 2, lf, n), F32),
        compiler_params=_cp("arbitrary", "arbitrary"),
        name="hyena_spectrum",
    )(fwd, a, b)


def _dft_fwd_kernel(f_ref, u_ref, h_ref, y_ref, *, nb):
    lb = f_ref.shape[2]
    xr, xi = [], []
    for p in range(nb):
        up = u_ref[0, 0, p * lb:(p + 1) * lb, :].astype(BF16)
        xr.append(_dot(f_ref[0], up))
        xi.append(_dot(f_ref[1], up))
    row = lax.broadcasted_iota(jnp.int32, xr[0].shape, 0)
    real_bins = jnp.logical_and(row == 0, pl.program_id(0) == 0)
    for q in range(nb):
        yr = 0.0
        yi = 0.0
        for p in range(nb):
            gr = h_ref[q - p + nb - 1, 0]
            gi = h_ref[q - p + nb - 1, 1]
            ii = xi[p] * gi
            yr = yr + xr[p] * gr - jnp.where(real_bins, 0.0, ii)
            yi = yi + jnp.where(real_bins, ii, xr[p] * gi + xi[p] * gr)
        y_ref[0, q, 0] = yr.astype(BF16)
        y_ref[0, q, 1] = yi.astype(BF16)


def _dft_fwd_call(fwd, u, group, spec, order, nb):
    _, b, l, c = u.shape
    _, lf, lb = fwd.shape
    nd = spec.shape[0]
    tm = min(lf, 512)
    return pl.pallas_call(
        functools.partial(_dft_fwd_kernel, nb=nb),
        grid=(lf // tm, b),
        in_specs=[pl.BlockSpec((2, tm, lb), lambda i, j: (0, i, 0)),
                  pl.BlockSpec((1, 1, l, c), lambda i, j: (group, j, 0, 0)),
                  pl.BlockSpec((nd, 2, tm, c), lambda i, j: (0, 0, i, order))],
        out_specs=pl.BlockSpec((1, nb, 2, tm, c), lambda i, j: (j, 0, 0, i, 0)),
        out_shape=jax.ShapeDtypeStruct((b, nb, 2, lf, c), BF16),
        compiler_params=_cp("arbitrary", "arbitrary"),
        name="hyena_dft_fwd",
    )(fwd, u, spec)


def _dft_inv_kernel(g_ref, y_ref, u_ref, b_ref, x_ref, o_ref):
    r = _dot(g_ref[...], y_ref[0, 0]) + u_ref[0, 0].astype(F32) * b_ref[...]
    o_ref[0, 0] = (x_ref[0, 0] * r).astype(o_ref.dtype)


def _dft_inv_call(inv, y, u, u_group, bias, gate, gate_group):
    _, b, l, c = u.shape
    lb, n = inv.shape
    nb = l // lb
    tm = min(lb, 1024)
    nt = lb // tm
    y = y.reshape(b, nb, n, c)
    row = lambda g: pl.BlockSpec((1, 1, tm, c), lambda q, i, j: (g, j, q * nt + i, 0))
    return pl.pallas_call(
        _dft_inv_kernel,
        grid=(nb, nt, b),
        in_specs=[pl.BlockSpec((tm, n), lambda q, i, j: (i, 0)),
                  pl.BlockSpec((1, 1, n, c), lambda q, i, j: (j, q, 0, 0)),
                  row(u_group), pl.BlockSpec((1, c), lambda q, i, j: (0, 0)), row(gate_group)],
        out_specs=row(0),
        out_shape=jax.ShapeDtypeStruct((1, b, l, c), BF16),
        compiler_params=_cp("parallel", "parallel", "parallel"),
        name="hyena_dft_inv",
    )(inv, y, u, bias.reshape(1, c), gate)


def _hyena_call(p_hy, sh_w, sh_b, spec, hy_bias, fwd, inv, nb):
    u3 = _hy_short_call(p_hy, sh_w, sh_b)
    z = _dft_inv_call(inv, _dft_fwd_call(fwd, u3, 0, spec, 0, nb), u3, 0, hy_bias[0], u3, 1)
    return _dft_inv_call(inv, _dft_fwd_call(fwd, z, 0, spec, 1, nb), z, 0, hy_bias[1], u3, 2)[0]


def _mix_out_kernel(c_ref, a_ref, h_ref, w_ref, x_ref, g1_ref, g_ref, sh_ref, sc_ref,
                    wr_ref, rb_ref, o_ref, h32_ref, idx_ref, gate_ref):
    k1 = c_ref.shape[2]
    k2 = k1 + a_ref.shape[2]
    acc = _dot(c_ref[0], w_ref[0, :k1, :]) + _dot(a_ref[0], w_ref[0, k1:k2, :]) + _dot(h_ref[0], w_ref[0, k2:, :])
    x = x_ref[0] + g1_ref[0] * acc
    o_ref[0] = x
    hm = _rms(x, g_ref[...]) * (1.0 + sc_ref[0]) + sh_ref[0]
    h32_ref[0] = _pack_rows(hm)
    picks, gates = _route(hm, wr_ref[...], rb_ref[...])
    idx_ref[...] = jnp.concatenate(picks, axis=0)
    gate_ref[...] = jnp.concatenate(gates, axis=0)


def _mix_out_call(conv_o, att_o, hy_o, w_out, layer, x, mod, g2, wr, rbias):
    b, l, d = x.shape
    tm = min(l, 512)
    nt = l // tm
    act = lambda k: pl.BlockSpec((1, tm, k), lambda i, j: (i, j, 0))
    row = pl.BlockSpec((1, 1, d), lambda i, j: (i, 0, 0))
    tok = pl.BlockSpec((2, tm), lambda i, j: (0, i * nt + j))
    return pl.pallas_call(
        _mix_out_kernel,
        grid=(b, nt),
        in_specs=[act(conv_o.shape[2]), act(att_o.shape[2]), act(hy_o.shape[2]),
                  pl.BlockSpec((1,) + w_out.shape[1:], lambda i, j: (layer, 0, 0), pipeline_mode=pl.Buffered(1)),
                  act(d), row, pl.BlockSpec((1, d), lambda i, j: (0, 0)), row, row,
                  pl.BlockSpec(wr.shape, lambda i, j: (0, 0)),
                  pl.BlockSpec((N_EXPERTS, 1), lambda i, j: (0, 0))],
        out_specs=[act(d), act(d // 2), tok, tok],
        out_shape=[jax.ShapeDtypeStruct((b, l, d), F32),
                   jax.ShapeDtypeStruct((b, l, d // 2), jnp.int32),
                   jax.ShapeDtypeStruct((2, b * l), jnp.int32),
                   jax.ShapeDtypeStruct((2, b * l), F32)],
        compiler_params=_cp("parallel", "parallel"),
        name="mix_out",
    )(conv_o, att_o, hy_o, w_out, x, mod[2], g2.reshape(1, d), mod[3], mod[4], wr,
      rbias.reshape(N_EXPERTS, 1))


def _route(hm, wr, rbias):
    hi = hm.astype(BF16)
    lo = (hm - hi.astype(F32)).astype(BF16)
    nt = (((1,), (1,)), ((), ()))
    r = (lax.dot_general(wr, hi, nt, preferred_element_type=F32)
         + lax.dot_general(wr, lo, nt, preferred_element_type=F32))
    logits = r[:N_EXPERTS] + r[N_EXPERTS:]
    score = jax.nn.sigmoid(logits)
    sel = score + rbias
    rows = [sel[e:e + 1] for e in range(N_EXPERTS)]
    gscore = []
    for g in range(N_GROUPS):
        a, b, c, d = rows[4 * g:4 * g + 4]
        h1, l1, h2, l2 = jnp.maximum(a, b), jnp.minimum(a, b), jnp.maximum(c, d), jnp.minimum(c, d)
        gscore.append(jnp.maximum(h1, h2) + jnp.maximum(jnp.minimum(h1, h2), jnp.maximum(l1, l2)))
    best = gscore[0]
    gbest = jnp.zeros(best.shape, jnp.int32)
    for g in range(1, N_GROUPS):
        upd = gscore[g] > best
        gbest = jnp.where(upd, g, gbest)
        best = jnp.where(upd, gscore[g], best)
    masked = [jnp.where(gbest == e // EXPERTS_PER_GROUP, rows[e], NEG_INF) for e in range(N_EXPERTS)]
    picks = []
    for _ in range(2):
        val = jnp.full(best.shape, NEG_INF, F32)
        idx = jnp.zeros(best.shape, jnp.int32)
        for e in range(N_EXPERTS):
            cand = masked[e]
            if picks:
                cand = jnp.where(picks[0] == e, NEG_INF, cand)
            upd = cand > val
            idx = jnp.where(upd, e, idx)
            val = jnp.where(upd, cand, val)
        picks.append(idx)
    gates = [functools.reduce(jnp.add, [jnp.where(p == e, score[e:e + 1], 0.0) for e in range(N_EXPERTS)])
             for p in picks]
    den = gates[0] + gates[1]
    return picks, [gates[0] / den, gates[1] / den]


def _expert_changed(te_ref, i):
    return (i == 0) | (te_ref[i] != te_ref[jnp.maximum(i - 1, 0)])


def _pack_rows(x):
    bits = pltpu.bitcast(x.astype(BF16).astype(F32), jnp.uint32)
    half = bits.shape[1] // 2
    return pltpu.bitcast((bits[:, :half] >> 16) | (bits[:, half:] & jnp.uint32(0xFFFF0000)), jnp.int32)


def _unpack_rows(words):
    w = pltpu.bitcast(words, jnp.uint32)
    return pltpu.bitcast(w << 16, F32), pltpu.bitcast(w & jnp.uint32(0xFFFF0000), F32)


def _moe_up_kernel(te_ref, tn_ref, nu_ref, x_ref, wg_hbm, wu_hbm, h_ref, gbuf, ubuf, wgb_ref, wub_ref, sems,
                   *, layer):
    j, i = pl.program_id(0), pl.program_id(1)
    tf = gbuf.shape[1]

    def fetch(e, chunk):
        cols = pl.ds(pl.multiple_of(chunk * tf, tf), tf)
        return (pltpu.make_async_copy(wg_hbm.at[layer, e, :, cols], gbuf, sems.at[0]),
                pltpu.make_async_copy(wu_hbm.at[layer, e, :, cols], ubuf, sems.at[1]))

    def start(e, chunk):
        for copy in fetch(e, chunk):
            copy.start()

    @pl.when((i == 0) & (j == 0))
    def _():
        start(te_ref[0], 0)

    @pl.when(_expert_changed(te_ref, i))
    def _():
        for copy in fetch(te_ref[i], j):
            copy.wait()
        wgb_ref[...] = gbuf[...].astype(BF16)
        wub_ref[...] = ubuf[...].astype(BF16)
        nxt = tn_ref[i]
        more = nxt < nu_ref[0]

        @pl.when(more)
        def _():
            start(te_ref[nxt], j)

        @pl.when(jnp.logical_not(more) & (j + 1 < pl.num_programs(0)))
        def _():
            start(te_ref[0], j + 1)

    @pl.when(i < nu_ref[0])
    def _():
        xa, xb = (v.astype(BF16) for v in _unpack_rows(x_ref[...]))
        half = xa.shape[1]
        a = _dot(xa, wgb_ref[:half, :]) + _dot(xb, wgb_ref[half:, :])
        u = _dot(xa, wub_ref[:half, :]) + _dot(xb, wub_ref[half:, :])
        h_ref[...] = (a * jax.nn.sigmoid(a) * u).astype(BF16)

    @pl.when(i >= nu_ref[0])
    def _():
        h_ref[...] = jnp.zeros(h_ref.shape, BF16)


def _moe_up_call(xs, w_gate, w_up, layer, te, tn, nu, tile):
    p, dw = xs.shape
    _, _, d, f = w_gate.shape
    tf = 512
    hbm = pl.BlockSpec(memory_space=pl.ANY)
    return pl.pallas_call(
        functools.partial(_moe_up_kernel, layer=layer),
        grid_spec=pltpu.PrefetchScalarGridSpec(
            num_scalar_prefetch=3,
            grid=(f // tf, p // tile),
            in_specs=[pl.BlockSpec((tile, dw), lambda j, i, te, tn, nu: (i, 0)), hbm, hbm],
            out_specs=pl.BlockSpec((tile, tf), lambda j, i, te, tn, nu: (i, j)),
            scratch_shapes=[pltpu.VMEM((d, tf), F32), pltpu.VMEM((d, tf), F32),
                            pltpu.VMEM((d, tf), BF16), pltpu.VMEM((d, tf), BF16),
                            pltpu.SemaphoreType.DMA((2,))]),
        out_shape=jax.ShapeDtypeStruct((p, f), BF16),
        compiler_params=_cp("arbitrary", "arbitrary"),
        name="moe_up",
    )(te, tn, nu, xs, w_gate, w_up)


def _moe_down_kernel(te_ref, tn_ref, nu_ref, h_ref, wd_hbm, y_ref, wbuf, wdb_ref, sem, *, layer):
    i = pl.program_id(0)

    def fetch(e):
        return pltpu.make_async_copy(wd_hbm.at[layer, e], wbuf, sem)

    @pl.when(i == 0)
    def _():
        fetch(te_ref[0]).start()

    @pl.when(_expert_changed(te_ref, i))
    def _():
        fetch(te_ref[i]).wait()
        wdb_ref[...] = wbuf[...].astype(BF16)
        nxt = tn_ref[i]

        @pl.when(nxt < nu_ref[0])
        def _():
            fetch(te_ref[nxt]).start()

    @pl.when(i < nu_ref[0])
    def _():
        y_ref[...] = _pack_rows(_dot(h_ref[...], wdb_ref[...]))

    @pl.when(i >= nu_ref[0])
    def _():
        y_ref[...] = jnp.zeros(y_ref.shape, jnp.int32)


def _moe_down_call(h, w_down, layer, te, tn, nu, tile):
    p, f = h.shape
    d = w_down.shape[3]
    return pl.pallas_call(
        functools.partial(_moe_down_kernel, layer=layer),
        grid_spec=pltpu.PrefetchScalarGridSpec(
            num_scalar_prefetch=3,
            grid=(p // tile,),
            in_specs=[pl.BlockSpec((tile, f), lambda i, te, tn, nu: (i, 0)),
                      pl.BlockSpec(memory_space=pl.ANY)],
            out_specs=pl.BlockSpec((tile, d // 2), lambda i, te, tn, nu: (i, 0)),
            scratch_shapes=[pltpu.VMEM((f, d), F32), pltpu.VMEM((f, d), BF16), pltpu.SemaphoreType.DMA(())]),
        out_shape=jax.ShapeDtypeStruct((p, d // 2), jnp.int32),
        compiler_params=_cp("arbitrary"),
        name="moe_down",
    )(te, tn, nu, h, w_down)


PLAN_BLOCK = 512


def _plan_kernel(idx_ref, pos_ref, te_ref, tn_ref, nu_ref, rank_ref, *, tile):
    t = idx_ref.shape[1]
    nb = t // PLAN_BLOCK
    experts = lax.broadcasted_iota(jnp.int32, (N_EXPERTS, PLAN_BLOCK), 0)
    r = lax.broadcasted_iota(jnp.int32, (PLAN_BLOCK, PLAN_BLOCK), 0)
    c = lax.broadcasted_iota(jnp.int32, (PLAN_BLOCK, PLAN_BLOCK), 1)
    before = jnp.where(r < c, 1.0, 0.0).astype(BF16)

    def rank_pass(k):
        def body(j, carry):
            cols = pl.ds(pl.multiple_of(j * PLAN_BLOCK, PLAN_BLOCK), PLAN_BLOCK)
            onehot = jnp.where(experts == idx_ref[k:k + 1, cols], 1.0, 0.0)
            earlier = _dot(onehot.astype(BF16), before) + carry
            rank_ref[k:k + 1, cols] = jnp.sum(onehot * earlier, axis=0, keepdims=True)
            return carry + jnp.sum(onehot, axis=1, keepdims=True)
        return body

    counts = jnp.zeros((N_EXPERTS, 1), F32)
    for k in range(2):
        counts = lax.fori_loop(0, nb, rank_pass(k), counts)
    padded = jnp.ceil(counts * (1.0 / tile)) * tile
    starts, ends = [], []
    acc = jnp.zeros((1, 1), F32)
    for e in range(N_EXPERTS):
        starts.append(acc)
        acc = acc + padded[e:e + 1]
        ends.append(acc)

    def place_pass(k):
        def body(j, carry):
            cols = pl.ds(pl.multiple_of(j * PLAN_BLOCK, PLAN_BLOCK), PLAN_BLOCK)
            row = idx_ref[k:k + 1, cols]
            start = functools.reduce(jnp.add, [jnp.where(row == e, starts[e], 0.0) for e in range(N_EXPERTS)])
            pos_ref[k:k + 1, cols] = (rank_ref[k:k + 1, cols] + start).astype(jnp.int32)
            return carry
        return body

    for k in range(2):
        lax.fori_loop(0, nb, place_pass(k), 0)
    n_used = acc * (1.0 / tile)
    tiles = lax.broadcasted_iota(jnp.int32, te_ref.shape, 1).astype(F32)
    first_row = jnp.minimum(tiles, n_used - 1.0) * tile
    te = functools.reduce(jnp.add, [jnp.where(ends[e] <= first_row, 1.0, 0.0) for e in range(N_EXPERTS)])
    te_ref[...] = te.astype(jnp.int32)
    seg_end = functools.reduce(jnp.add, [jnp.where(te == e, ends[e], 0.0) for e in range(N_EXPERTS)])
    tn_ref[...] = (seg_end * (1.0 / tile)).astype(jnp.int32)
    nu_ref[...] = jnp.broadcast_to(n_used, nu_ref.shape).astype(jnp.int32)


def _plan_call(idx, tile, n_tiles):
    t = idx.shape[1]
    whole = lambda shape: pl.BlockSpec(shape, lambda: (0,) * len(shape))
    lanes = jax.ShapeDtypeStruct((1, LANES), jnp.int32)
    pos, te, tn, nu = pl.pallas_call(
        functools.partial(_plan_kernel, tile=tile),
        in_specs=[whole(idx.shape)],
        out_specs=[whole(idx.shape), whole((1, LANES)), whole((1, LANES)), whole((1, LANES))],
        out_shape=[jax.ShapeDtypeStruct(idx.shape, jnp.int32), lanes, lanes, lanes],
        scratch_shapes=[pltpu.VMEM(idx.shape, F32)],
        name="route_plan",
    )(idx)
    return pos.reshape(2 * t), te[0, :n_tiles], tn[0, :n_tiles], nu[0, :1]


V7X_SC_CORES = 2
V7X_SC_SUBCORES = 16
SC_WORKERS = V7X_SC_CORES * V7X_SC_SUBCORES
SC_ROW_BYTES = 256 * 1024


def _sc_worker():
    return lax.axis_index("s") * V7X_SC_CORES + lax.axis_index("c")


def _sc_dispatch_call(streams, pos, n_rows):
    w = streams[0].shape[1]
    sizes = [s.shape[0] for s in streams]
    t = sum(sizes)
    per_w = t // SC_WORKERS
    ch = min(per_w, SC_ROW_BYTES // (4 * w))
    assert per_w % ch == 0 and all(size % ch == 0 for size in sizes)
    mesh = plsc.VectorSubcoreMesh(core_axis_name="c", subcore_axis_name="s")

    @functools.partial(pl.kernel, mesh=mesh, out_type=jax.ShapeDtypeStruct((n_rows, w), jnp.int32),
                       scratch_types=[pltpu.VMEM((ch,), jnp.int32), pltpu.VMEM((ch,), jnp.int32),
                                      pltpu.VMEM((ch, w), jnp.int32),
                                      pltpu.SemaphoreType.DMA, pltpu.SemaphoreType.DMA],
                       name="sc_dispatch")
    def run(*refs):
        srcs = refs[:len(sizes)]
        pos_hbm, xs_hbm, idx0_v, idx1_v, rows_v, sem0, sem1 = refs[len(sizes):]
        first = _sc_worker() * per_w

        @pl.loop(0, per_w // ch)
        def _(j):
            base = first + j * ch
            start = 0
            for src, size in zip(srcs, sizes):
                def load(src=src, start=start):
                    pltpu.sync_copy(src.at[pl.ds(base - start, ch)], rows_v)
                if len(sizes) == 1:
                    load()
                else:
                    pl.when((base >= start) & (base < start + size))(load)
                start += size
            pltpu.sync_copy(pos_hbm.at[pl.ds(base, ch)], idx0_v)
            pltpu.sync_copy(pos_hbm.at[pl.ds(t + base, ch)], idx1_v)
            first_choice = pltpu.async_copy(rows_v, xs_hbm.at[idx0_v], sem0)
            second_choice = pltpu.async_copy(rows_v, xs_hbm.at[idx1_v], sem1)
            first_choice.wait()
            second_choice.wait()

    return run(*streams, pos)


def _sc_combine_call(ys, pos):
    _, d = ys.shape
    n = pos.shape[0]
    per_w = n // SC_WORKERS
    ch = min(per_w // 2, SC_ROW_BYTES // (8 * d))
    assert per_w % (2 * ch) == 0
    mesh = plsc.VectorSubcoreMesh(core_axis_name="c", subcore_axis_name="s")
    idx_buf = pltpu.VMEM((ch,), jnp.int32)
    row_buf = pltpu.VMEM((ch, d), jnp.int32)

    @functools.partial(pl.kernel, mesh=mesh, out_type=jax.ShapeDtypeStruct((n, d), jnp.int32),
                       scratch_types=[idx_buf, idx_buf, row_buf, row_buf] + [pltpu.SemaphoreType.DMA] * 4,
                       name="sc_combine")
    def run(ys_hbm, pos_hbm, out_hbm, idx_a, idx_b, rows_a, rows_b, gsem_a, gsem_b, wsem_a, wsem_b):
        first = _sc_worker() * per_w

        @pl.loop(0, per_w // (2 * ch))
        def _(j):
            base = first + j * (2 * ch)
            pltpu.sync_copy(pos_hbm.at[pl.ds(base, ch)], idx_a)
            pltpu.sync_copy(pos_hbm.at[pl.ds(base + ch, ch)], idx_b)
            gather_a = pltpu.async_copy(ys_hbm.at[idx_a], rows_a, gsem_a)
            gather_b = pltpu.async_copy(ys_hbm.at[idx_b], rows_b, gsem_b)
            gather_a.wait()
            write_a = pltpu.async_copy(rows_a, out_hbm.at[pl.ds(base, ch)], wsem_a)
            gather_b.wait()
            write_b = pltpu.async_copy(rows_b, out_hbm.at[pl.ds(base + ch, ch)], wsem_b)
            write_a.wait()
            write_b.wait()

    return run(ys, pos)


def _moe_call(streams, idx, w_gate, w_up, w_down, layer):
    t = idx.shape[1]
    tile = min(MOE_TILE, max(t // N_EXPERTS, 128))
    n_tiles = (2 * t) // tile + N_EXPERTS
    assert n_tiles <= LANES
    pos, te, tn, nu = _plan_call(idx, tile, n_tiles)
    xs = _sc_dispatch_call(streams, pos, n_tiles * tile)
    hid = _moe_up_call(xs, w_gate, w_up, layer, te, tn, nu, tile)
    ys = _moe_down_call(hid, w_down, layer, te, tn, nu, tile)
    return _sc_combine_call(ys, pos).reshape(2, t, ys.shape[1])


def _moe_resid_kernel(x_ref, y0_ref, y1_ref, gt_ref, g_ref, *rest, final):
    gt = gt_ref[0]
    halves0 = _unpack_rows(y0_ref[0])
    halves1 = _unpack_rows(y1_ref[0])
    y = jnp.concatenate([gt[:, 0:1] * a + gt[:, 1:2] * b for a, b in zip(halves0, halves1)], axis=1)
    x = x_ref[0] + g_ref[0] * y
    if final:
        fg_ref, o_ref = rest
        x = _rms(x, fg_ref[...])
    else:
        (o_ref,) = rest
    o_ref[0] = x


def _moe_resid_call(x, y, first_row, gates, gate, final_g=None):
    b, l, d = x.shape
    tl = min(l, 512)
    nt = l // tl
    tile = pl.BlockSpec((1, tl, d), lambda i, j: (i, j, 0))
    choice = lambda k: pl.BlockSpec((1, tl, d // 2), lambda i, j: (k, first_row // tl + i * nt + j, 0))
    in_specs = [tile, choice(0), choice(1), pl.BlockSpec((1, tl, 2), lambda i, j: (i, j, 0)),
                pl.BlockSpec((1, 1, d), lambda i, j: (i, 0, 0))]
    args = [x, y, y, gates.T.reshape(b, l, 2), gate]
    if final_g is not None:
        in_specs.append(pl.BlockSpec((1, d), lambda i, j: (0, 0)))
        args.append(final_g.reshape(1, d))
    return pl.pallas_call(
        functools.partial(_moe_resid_kernel, final=final_g is not None),
        grid=(b, l // tl),
        in_specs=in_specs,
        out_specs=tile,
        out_shape=jax.ShapeDtypeStruct((b, l, d), F32),
        compiler_params=_cp("parallel", "parallel"),
        name="moe_residual",
    )(*args)


def _rope_perm(w):
    r = w.reshape(w.shape[:-1] + (2, 2, ROPE_FREQS))
    return jnp.stack([-r[..., 1, :], r[..., 0, :]], axis=-2).reshape(w.shape)


def _rope_table(l, rotary):
    if not rotary:
        return jnp.concatenate([jnp.ones((l, MLA_ROPE), F32), jnp.zeros((l, MLA_ROPE), F32)], axis=1)
    rows = l // GRID_W
    r = jnp.repeat(jnp.arange(rows), GRID_W)
    col = jnp.tile(jnp.arange(GRID_W), rows)
    pos = jnp.stack([r, col], axis=-1).astype(F32)
    inv = ROPE_THETA ** (-jnp.arange(ROPE_FREQS, dtype=F32) / ROPE_FREQS)
    ang = pos[:, :, None] * inv
    lay = lambda a: jnp.concatenate([a[:, 0], a[:, 0], a[:, 1], a[:, 1]], axis=1)
    return jnp.concatenate([lay(jnp.cos(ang)), lay(jnp.sin(ang))], axis=1)


def _prep_weights(w_in, w_uq, w_ukv, w_out):
    depth = w_in.shape[0]
    cast = lambda a: a.astype(BF16)
    kr = cast(w_in[:, :, OFF_KR:OFF_HY])
    w_kv_in = jnp.concatenate([cast(w_in[:, :, OFF_KV:OFF_KR]), kr, _rope_perm(kr)], axis=2)
    w_all = jnp.concatenate([cast(w_in[:, :, :OFF_KV]), cast(w_in[:, :, OFF_HY:]), w_kv_in], axis=2)
    uq = cast(w_uq).reshape(depth, MLA_Q_RANK, MLA_HEADS, MLA_NOPE + MLA_ROPE)
    w_q = jnp.concatenate([uq, _rope_perm(uq[..., MLA_NOPE:])], axis=3).reshape(depth, MLA_Q_RANK, -1)
    ukv = cast(w_ukv).reshape(depth, MLA_KV_RANK, MLA_HEADS, MLA_NOPE + MLA_V_DIM)
    w_k = ukv[..., :MLA_NOPE].reshape(depth, MLA_KV_RANK, MLA_HEADS * MLA_NOPE)
    w_v = ukv[..., MLA_NOPE:].reshape(depth, MLA_KV_RANK, MLA_HEADS * MLA_V_DIM)
    return dict(all=w_all, kv=w_kv_in, uq=w_q, uk=w_k, uv=w_v, out=cast(w_out))


def _router_weights(w_router):
    hi = w_router.astype(BF16)
    lo = (w_router - hi.astype(F32)).astype(BF16)
    return jnp.concatenate([hi.T, lo.T], axis=0)


def _mixer(x, mod, w, layer, lp, cs, spec, fwd, inv, kv_extra, full, wr, rbias):
    kv_args = (lp["kv_norm_g"], w["uk"], w["uv"], cs, layer)
    if not full:
        k, v = _in_proj_call(x, lp["norm1_g"], mod[0], mod[1], w["kv"], [], *kv_args)
        return None, k, v
    splits = [OFF_Q, MLA_Q_RANK, (HY_ORDER + 1) * HYENA_W]
    p_conv, p_q, p_hy, k, v = _in_proj_call(x, lp["norm1_g"], mod[0], mod[1], w["all"], splits, *kv_args)
    conv_o = _conformer_call(p_conv, lp["conv_dw_w"], lp["conv_dw_b"], lp["conv_ln_g"], lp["conv_ln_b"])
    att_o = _attn_call(p_q, lp["q_norm_g"], w["uq"], cs, kv_extra + [(k, v)], layer)
    hy_o = _hyena_call(p_hy, lp["hy_short_w"], lp["hy_short_b"], spec, lp["hy_bias"], fwd, inv,
                       _conv_blocks(x.shape[1]))
    routed = _mix_out_call(conv_o, att_o, hy_o, w["out"], layer, x, mod, lp["norm2_g"], wr, rbias)
    return routed, k, v


def _channel(streams, experts, layer, final_g=None):
    d = streams[0][1][0].shape[2]
    idx = jnp.concatenate([r[2] for _, r in streams], axis=1)
    y = _moe_call([r[1].reshape(-1, d // 2) for _, r in streams], idx, *experts, layer)
    outs = []
    first_row = 0
    for mod, (x, _, _, gates) in streams:
        outs.append(_moe_resid_call(x, y, first_row, gates, mod[5], final_g))
        first_row += x.shape[0] * x.shape[1]
    return outs


def _filter_spectrum(l, lp, fwd):
    filt = _hy_filter_call(l, lp["hy_w1"], lp["hy_b1"], lp["hy_freq1"], lp["hy_w2"], lp["hy_b2"], lp["hy_freq2"],
                           lp["hy_w3"], lp["hy_b3"])
    return _spec_call(fwd, *_lag_windows(filt, _conv_blocks(l)))


def kernel(x, c, ctx, c_ctx, norm1_g, norm2_g, w_ada, b_ada, w_in, conv_dw_w, conv_dw_b, conv_ln_g, conv_ln_b, q_norm_g, w_uq, kv_norm_g, w_ukv, hy_short_w, hy_short_b, hy_w1, hy_b1, hy_freq1, hy_w2, hy_b2, hy_freq2, hy_w3, hy_b3, hy_bias, w_out, w_router, router_bias, w_gate, w_up, w_down, final_norm_g):
    b, l, d = x.shape
    lc = ctx.shape[1]
    depth = w_in.shape[0]
    per_layer = dict(norm1_g=norm1_g, norm2_g=norm2_g, conv_dw_w=conv_dw_w, conv_dw_b=conv_dw_b,
                     conv_ln_g=conv_ln_g, conv_ln_b=conv_ln_b, q_norm_g=q_norm_g, kv_norm_g=kv_norm_g,
                     hy_short_w=hy_short_w, hy_short_b=hy_short_b, hy_w1=hy_w1, hy_b1=hy_b1, hy_freq1=hy_freq1,
                     hy_w2=hy_w2, hy_b2=hy_b2, hy_freq2=hy_freq2, hy_w3=hy_w3, hy_b3=hy_b3, hy_bias=hy_bias)
    experts = (w_gate, w_up, w_down)
    cs_l = _rope_table(l, True)
    cs_c = _rope_table(lc, False)
    fwd_l, inv_l = _dft_mats(l // _conv_blocks(l))
    fwd_c, inv_c = _dft_mats(lc // _conv_blocks(lc))
    wr = _router_weights(w_router)
    w = _prep_weights(w_in, w_uq, w_ukv, w_out)
    rows = 16
    cvec = jnp.concatenate([c, c_ctx[None, :], jnp.zeros((rows - b - 1, d), F32)], axis=0)
    xl, xc = x, ctx
    for i in range(depth):
        last = i == depth - 1
        lp = {k: v[i] for k, v in per_layer.items()}
        m = _ada_call(cvec, w_ada, b_ada, i)
        mod_l = [m[:b, None, j * d:(j + 1) * d] for j in range(6)]
        mod_c = [jnp.broadcast_to(m[b:b + 1, None, j * d:(j + 1) * d], (b, 1, d)) for j in range(6)]
        spec_l = _filter_spectrum(l, lp, fwd_l)
        spec_c = None if last else _filter_spectrum(lc, lp, fwd_c)
        routed_c, kc, vc = _mixer(xc, mod_c, w, i, lp, cs_c, spec_c, fwd_c, inv_c, [], not last, wr, router_bias)
        routed_l, _, _ = _mixer(xl, mod_l, w, i, lp, cs_l, spec_l, fwd_l, inv_l, [(kc, vc)], True, wr, router_bias)
        if last:
            (xl,) = _channel([(mod_l, routed_l)], experts, i, final_norm_g)
        else:
            xl, xc = _channel([(mod_l, routed_l), (mod_c, routed_c)], experts, i)
    return xl
```

```python
import functools
import math

import jax
import jax.numpy as jnp
from jax import lax
from jax.experimental import pallas as pl
from jax.experimental.pallas import tpu as pltpu
from jax.experimental.pallas import tpu_sc as plsc

F32 = jnp.float32
BF16 = jnp.bfloat16

EPS = 1e-6
GRID_W = 64
CONV_W = 512
CONV_KSIZE = 31
MLA_HEADS = 8
MLA_NOPE = 128
MLA_ROPE = 64
MLA_V_DIM = 128
MLA_Q_RANK = 768
MLA_KV_RANK = 512
ROPE_FREQS = MLA_ROPE // 4
ROPE_THETA = 10000.0
MLA_SCALE = (MLA_NOPE + MLA_ROPE) ** -0.5
Q_SCALE = MLA_SCALE * math.log2(math.e)
HYENA_W = 512
HY_ORDER = 2
HY_SHORT = 3
HY_EMB = 33
HY_BANDS = (HY_EMB - 1) // 2
HY_FFN = 64
HY_MIN_DECAY = math.log(1e-2) / 1.5
HY_MAX_DECAY = math.log(1e-2) / 0.3
N_EXPERTS = 16
N_GROUPS = 4
EXPERTS_PER_GROUP = N_EXPERTS // N_GROUPS
OFF_Q = 2 * CONV_W
OFF_KV = OFF_Q + MLA_Q_RANK
OFF_KR = OFF_KV + MLA_KV_RANK
OFF_HY = OFF_KR + MLA_ROPE

LANES = 128
QK_HEAD = 2 * LANES
V_HEAD = MLA_V_DIM + LANES
V7X_VMEM_BYTES = 64 * 1024 * 1024
VMEM_LIMIT = (V7X_VMEM_BYTES * 7) // 8
MOE_TILE = 512
NEG_INF = float("-inf")


def _cp(*sem):
    return pltpu.CompilerParams(dimension_semantics=sem, vmem_limit_bytes=VMEM_LIMIT)


def _dot(a, b):
    return jnp.dot(a, b, preferred_element_type=F32)


def _rms(x, g):
    ms = jnp.mean(x * x, axis=-1, keepdims=True)
    return x * lax.rsqrt(ms + EPS) * g


def _ada_kernel(c_ref, w_ref, b_ref, o_ref):
    c = c_ref[...]
    a = (c * jax.nn.sigmoid(c)).astype(BF16)
    o_ref[...] = _dot(a, w_ref[0].astype(BF16)) + b_ref[0]


def _ada_call(cvec, w, b, layer):
    m, d = cvec.shape
    n = w.shape[2]
    tn = 1024
    return pl.pallas_call(
        _ada_kernel,
        grid=(n // tn,),
        in_specs=[pl.BlockSpec((m, d), lambda j: (0, 0)),
                  pl.BlockSpec((1, d, tn), lambda j: (layer, 0, j)),
                  pl.BlockSpec((1, 1, tn), lambda j: (layer, 0, j))],
        out_specs=pl.BlockSpec((m, tn), lambda j: (0, j)),
        out_shape=jax.ShapeDtypeStruct((m, n), F32),
        compiler_params=_cp("parallel"),
        name="ada",
    )(cvec, w, b.reshape(b.shape[0], 1, n))


def _in_proj_kernel(x_ref, g_ref, sh_ref, sc_ref, w_ref, kg_ref, wk_ref, wv_ref, cs_ref, *o_refs):
    *p_refs, k_ref, v_ref = o_refs
    y = _rms(x_ref[0], g_ref[...])
    h = (y * (1.0 + sc_ref[0]) + sh_ref[0]).astype(BF16)
    off = 0
    for o_ref in p_refs:
        n = o_ref.shape[-1]
        o_ref[0] = _dot(h, w_ref[0, :, off:off + n]).astype(o_ref.dtype)
        off += n
    p = _dot(h, w_ref[0, :, off:])
    xn = _rms(p[:, :MLA_KV_RANK], kg_ref[...]).astype(BF16)
    kk = _dot(xn, wk_ref[0])
    vv = _dot(xn, wv_ref[0])
    rot = _rotate(p[:, MLA_KV_RANK:], cs_ref[...])
    lane = lax.broadcasted_iota(jnp.int32, rot.shape, 1)
    rot = jnp.where(lane < MLA_ROPE, rot, 0.0).astype(BF16)
    ones = jnp.where(lane == 0, 1.0, 0.0).astype(BF16)
    for hd in range(MLA_HEADS):
        k_ref[0, :, hd * QK_HEAD:hd * QK_HEAD + LANES] = kk[:, hd * LANES:(hd + 1) * LANES].astype(BF16)
        k_ref[0, :, hd * QK_HEAD + LANES:(hd + 1) * QK_HEAD] = rot
        v_ref[0, :, hd * V_HEAD:hd * V_HEAD + MLA_V_DIM] = vv[:, hd * MLA_V_DIM:(hd + 1) * MLA_V_DIM].astype(BF16)
        v_ref[0, :, hd * V_HEAD + MLA_V_DIM:(hd + 1) * V_HEAD] = ones


def _in_proj_call(x, g, shift, scale, w, splits, kv_g, w_k, w_v, cs, layer):
    b, l, d = x.shape
    n = w.shape[2]
    tm = min(l, 512)
    r = MLA_KV_RANK
    row = pl.BlockSpec((1, 1, d), lambda i, j: (i, 0, 0))
    const = lambda shape: pl.BlockSpec(shape, lambda i, j: (0, 0))
    of_layer = lambda a, **kw: pl.BlockSpec((1,) + a.shape[1:], lambda i, j: (layer, 0, 0), **kw)
    widths = list(splits) + [MLA_HEADS * QK_HEAD, MLA_HEADS * V_HEAD]
    return pl.pallas_call(
        _in_proj_kernel,
        grid=(b, l // tm),
        in_specs=[pl.BlockSpec((1, tm, d), lambda i, j: (i, j, 0)),
                  const((1, d)), row, row,
                  of_layer(w, pipeline_mode=pl.Buffered(1)),
                  const((1, r)), of_layer(w_k), of_layer(w_v),
                  pl.BlockSpec((tm, LANES), lambda i, j: (j, 0))],
        out_specs=[pl.BlockSpec((1, tm, k), lambda i, j: (i, j, 0)) for k in widths],
        out_shape=[jax.ShapeDtypeStruct((b, l, k), BF16) for k in widths],
        compiler_params=_cp("parallel", "parallel"),
        name="in_proj",
    )(x, g.reshape(1, d), shift, scale, w, kv_g.reshape(1, r), w_k, w_v, cs)


CONV_ROWS = 128
CONV_PAD = 16


def _conformer_kernel(p_ref, w_ref, b_ref, g_ref, be_ref, o_ref, zp_ref):
    l = p_ref.shape[1]
    c = CONV_W
    zeros = jnp.zeros((CONV_PAD, c), F32)
    zp_ref[0:CONV_PAD, :] = zeros
    zp_ref[CONV_PAD + l:CONV_PAD + l + CONV_PAD, :] = zeros

    def glu(r, carry):
        base = pl.multiple_of(r * CONV_ROWS, CONV_ROWS)
        blk = p_ref[0, pl.ds(base, CONV_ROWS), :].astype(F32)
        zp_ref[pl.ds(base + CONV_PAD, CONV_ROWS), :] = blk[:, :c] * jax.nn.sigmoid(blk[:, c:])
        return carry

    lax.fori_loop(0, l // CONV_ROWS, glu, 0)

    win_rows = CONV_ROWS + 2 * CONV_PAD
    first = CONV_PAD - CONV_KSIZE // 2

    def conv(r, carry):
        base = pl.multiple_of(r * CONV_ROWS, CONV_ROWS)
        parts = []
        for cc in range(c // LANES):
            lanes = slice(cc * LANES, (cc + 1) * LANES)
            win = zp_ref[pl.ds(base, win_rows), lanes]
            acc = jnp.zeros((CONV_ROWS, LANES), F32)
            for res in range(8):
                taps = [k for k in range(CONV_KSIZE) if (k + first) % 8 == res]
                if not taps:
                    continue
                shifted = pltpu.roll(win, win_rows - res, axis=0) if res else win
                for k in taps:
                    off = k + first - res
                    acc = acc + shifted[off:off + CONV_ROWS, :] * w_ref[k:k + 1, lanes]
            parts.append(acc)
        z = jnp.concatenate(parts, axis=1) + b_ref[...]
        mu = jnp.mean(z, axis=-1, keepdims=True)
        zc = z - mu
        var = jnp.mean(zc * zc, axis=-1, keepdims=True)
        y = zc * lax.rsqrt(var + EPS) * g_ref[...] + be_ref[...]
        o_ref[0, pl.ds(base, CONV_ROWS), :] = (y * jax.nn.sigmoid(y)).astype(o_ref.dtype)
        return carry

    lax.fori_loop(0, l // CONV_ROWS, conv, 0)


def _conformer_call(p, dw_w, dw_b, ln_g, ln_b):
    b, l, c2 = p.shape
    c = c2 // 2
    vec = pl.BlockSpec((1, c), lambda i: (0, 0))
    return pl.pallas_call(
        _conformer_kernel,
        grid=(b,),
        in_specs=[pl.BlockSpec((1, l, c2), lambda i: (i, 0, 0)),
                  pl.BlockSpec((CONV_KSIZE, c), lambda i: (0, 0)), vec, vec, vec],
        out_specs=pl.BlockSpec((1, l, c), lambda i: (i, 0, 0)),
        out_shape=jax.ShapeDtypeStruct((b, l, c), BF16),
        scratch_shapes=[pltpu.VMEM((l + 2 * CONV_PAD, c), F32)],
        compiler_params=_cp("parallel"),
        name="conformer",
    )(p, dw_w, dw_b.reshape(1, c), ln_g.reshape(1, c), ln_b.reshape(1, c))


def _rotate(group, cs):
    t = group * cs
    return t + pltpu.roll(t, MLA_ROPE, axis=1)


def _attn_kernel(p_ref, g_ref, w_ref, cs_ref, *refs, n_seg):
    o_ref, q_ref = refs[2 * n_seg], refs[2 * n_seg + 1]
    xn = _rms(p_ref[0].astype(F32), g_ref[...]).astype(BF16)
    cs = cs_ref[...] * Q_SCALE
    for h in range(MLA_HEADS):
        r = _dot(xn, w_ref[0, :, h * QK_HEAD:(h + 1) * QK_HEAD])
        q_ref[:, h * QK_HEAD:h * QK_HEAD + LANES] = (r[:, :LANES] * Q_SCALE).astype(BF16)
        q_ref[:, h * QK_HEAD + LANES:(h + 1) * QK_HEAD] = _rotate(r[:, LANES:], cs).astype(BF16)
    nt = (((1,), (1,)), ((), ()))
    for h in range(MLA_HEADS):
        qk = slice(h * QK_HEAD, (h + 1) * QK_HEAD)
        vd = slice(h * MLA_V_DIM, (h + 1) * MLA_V_DIM)
        q = q_ref[:, qk]
        scores = [lax.dot_general(q, refs[2 * s][0, :, qk], nt, preferred_element_type=F32)
                  for s in range(n_seg)]
        m = functools.reduce(jnp.maximum, [jnp.max(s, axis=-1, keepdims=True) for s in scores])
        acc = 0.0
        for s in range(n_seg):
            p = jnp.exp2(scores[s] - m).astype(BF16)
            acc = acc + _dot(p, refs[2 * s + 1][0, :, h * V_HEAD:(h + 1) * V_HEAD])
        o_ref[0, :, vd] = (acc[:, :MLA_V_DIM] * (1.0 / acc[:, MLA_V_DIM:MLA_V_DIM + 1])).astype(o_ref.dtype)


def _attn_call(p_q, q_g, w_q, cs, segs, layer):
    b, l, r = p_q.shape
    tq = min(l, 512)
    const = lambda shape: pl.BlockSpec(shape, lambda i, j: (0, 0))
    in_specs = [pl.BlockSpec((1, tq, r), lambda i, j: (i, j, 0)), const((1, r)),
                pl.BlockSpec((1,) + w_q.shape[1:], lambda i, j: (layer, 0, 0)),
                pl.BlockSpec((tq, LANES), lambda i, j: (j, 0))]
    args = [p_q, q_g.reshape(1, r), w_q, cs]
    for k, v in segs:
        in_specs.append(pl.BlockSpec((1,) + k.shape[1:], lambda i, j: (i, 0, 0), pipeline_mode=pl.Buffered(1)))
        in_specs.append(pl.BlockSpec((1,) + v.shape[1:], lambda i, j: (i, 0, 0), pipeline_mode=pl.Buffered(1)))
        args += [k, v]
    return pl.pallas_call(
        functools.partial(_attn_kernel, n_seg=len(segs)),
        grid=(b, l // tq),
        in_specs=in_specs,
        out_specs=pl.BlockSpec((1, tq, MLA_HEADS * MLA_V_DIM), lambda i, j: (i, j, 0)),
        out_shape=jax.ShapeDtypeStruct((b, l, MLA_HEADS * MLA_V_DIM), BF16),
        scratch_shapes=[pltpu.VMEM((tq, MLA_HEADS * QK_HEAD), BF16)],
        compiler_params=_cp("parallel", "parallel"),
        name="attention",
    )(*args)


HY_PAD = 8
HY_ROWS = 128


def _hy_short_kernel(p_ref, w_ref, b_ref, o_ref, up_ref):
    l = p_ref.shape[1]
    c = p_ref.shape[2]
    zeros = jnp.zeros((HY_PAD, c), F32)
    up_ref[0:HY_PAD, :] = zeros
    up_ref[HY_PAD + l:HY_PAD + l + HY_PAD, :] = zeros

    def copy(r, carry):
        base = pl.multiple_of(r * HY_ROWS, HY_ROWS)
        up_ref[pl.ds(base + HY_PAD, HY_ROWS), :] = p_ref[0, pl.ds(base, HY_ROWS), :].astype(F32)
        return carry

    lax.fori_loop(0, l // HY_ROWS, copy, 0)
    first = HY_PAD - HY_SHORT // 2

    def conv(r, carry):
        base = pl.multiple_of(r * HY_ROWS, HY_ROWS)
        for cc in range(c // LANES):
            lanes = slice(cc * LANES, (cc + 1) * LANES)
            win = up_ref[pl.ds(base, HY_ROWS + 2 * HY_PAD), lanes]
            acc = jnp.zeros((HY_ROWS, LANES), F32) + b_ref[:, lanes]
            for k in range(HY_SHORT):
                acc = acc + win[first + k:first + k + HY_ROWS, :] * w_ref[k:k + 1, lanes]
            o_ref[0, 0, pl.ds(base, HY_ROWS), lanes] = acc.astype(o_ref.dtype)
        return carry

    lax.fori_loop(0, l // HY_ROWS, conv, 0)


def _hy_short_call(p, w, bias):
    b, l, c3 = p.shape
    c = HYENA_W
    return pl.pallas_call(
        _hy_short_kernel,
        grid=(b, c3 // c),
        in_specs=[pl.BlockSpec((1, l, c), lambda i, g: (i, 0, g)),
                  pl.BlockSpec((HY_SHORT, c), lambda i, g: (0, g)),
                  pl.BlockSpec((1, c), lambda i, g: (0, g))],
        out_specs=pl.BlockSpec((1, 1, l, c), lambda i, g: (g, i, 0, 0)),
        out_shape=jax.ShapeDtypeStruct((c3 // c, b, l, c), BF16),
        scratch_shapes=[pltpu.VMEM((l + 2 * HY_PAD, c), F32)],
        compiler_params=_cp("parallel", "parallel"),
        name="hyena_short",
    )(p, w, bias.reshape(1, c3))


def _hy_filter_kernel(z_ref, w1_ref, b1_ref, f1_ref, w2_ref, b2_ref, f2_ref, w3_ref, b3_ref, win_ref, o_ref):
    hp = lax.Precision.HIGHEST

    def mm(a, b):
        return jnp.dot(a, b, preferred_element_type=F32, precision=hp)

    h = jnp.sin(f1_ref[...] * (mm(z_ref[...], w1_ref[...]) + b1_ref[...]))
    h = jnp.sin(f2_ref[...] * (mm(h, w2_ref[...]) + b2_ref[...]))
    h = mm(h, w3_ref[...]) + b3_ref[...]
    c = HYENA_W
    win = win_ref[...]
    for s in range(h.shape[1] // c):
        o_ref[:, s * c:(s + 1) * c] = h[:, s * c:(s + 1) * c] * win


def _pad2(a, rows, cols):
    return jnp.pad(a, ((0, rows - a.shape[0]), (0, cols - a.shape[1])))


def _hy_filter_call(l, t, w1, b1, f1, w2, b2, f2, w3, b3):
    t_norm = t / max(l - 1, 1)
    ang = (2.0 * math.pi / l) * t[:, None] * jnp.linspace(1e-4, HY_BANDS - 1, HY_BANDS, dtype=F32)[None, :]
    z = jnp.concatenate([t_norm[:, None], jnp.cos(ang), -jnp.sin(ang)], axis=-1)
    deltas = jnp.abs(jnp.linspace(HY_MIN_DECAY, HY_MAX_DECAY, HYENA_W, dtype=F32))
    window = jnp.exp(-t_norm[:, None] * deltas[None, :])
    n3 = w3.shape[1]
    by_dir = lambda a: a.reshape(a.shape[0], HY_ORDER, 2, HYENA_W).transpose(0, 2, 1, 3).reshape(a.shape[0], n3)
    w3, b3 = by_dir(w3), by_dir(b3[None])[0]
    l = t.shape[0]
    tl = min(l, 256)
    full = lambda shape: pl.BlockSpec(shape, lambda i: (0, 0))
    return pl.pallas_call(
        _hy_filter_kernel,
        grid=(l // tl,),
        in_specs=[pl.BlockSpec((tl, LANES), lambda i: (i, 0)),
                  full((LANES, LANES)), full((1, LANES)), full((1, LANES)),
                  full((LANES, LANES)), full((1, LANES)), full((1, LANES)),
                  full((LANES, n3)), full((1, n3)),
                  pl.BlockSpec((tl, HYENA_W), lambda i: (i, 0))],
        out_specs=pl.BlockSpec((tl, n3), lambda i: (i, 0)),
        out_shape=jax.ShapeDtypeStruct((l, n3), F32),
        compiler_params=_cp("parallel"),
        name="hyena_filter",
    )(_pad2(z, l, LANES), _pad2(w1, LANES, LANES), _pad2(b1[None], 1, LANES), _pad2(f1[None], 1, LANES),
      _pad2(w2, LANES, LANES), _pad2(b2[None], 1, LANES), _pad2(f2[None], 1, LANES),
      _pad2(w3, LANES, n3), b3[None], window)


def _dft_tables_kernel(ac_ref, as_ref, bc_ref, bs_ref, fwd_ref, inv_ref):
    tm, l = fwd_ref.shape[1], fwd_ref.shape[2]
    n = 2 * l
    row = lax.broadcasted_iota(jnp.int32, (tm, LANES), 0) + pl.program_id(0) * tm
    lane = lax.broadcasted_iota(jnp.int32, (tm, LANES), 1)
    row_sign = jnp.where((row & 1) == 0, 1.0, -1.0)
    lane_sign = jnp.where((lane & 1) == 0, 1.0, -1.0)
    bc = bc_ref[...]
    bs = bs_ref[...]
    for a in range(l // LANES):
        cols = slice(a * LANES, (a + 1) * LANES)
        ca = ac_ref[:, a:a + 1]
        sa = as_ref[:, a:a + 1]
        c = ca * bc - sa * bs
        s = sa * bc + ca * bs
        fwd_ref[0, :, cols] = c.astype(BF16)
        fwd_ref[1, :, cols] = jnp.where(row == 0, lane_sign, -s).astype(BF16)
        inv_c = c * (2.0 / n)
        inv_s = -s * (2.0 / n)
        if a == 0:
            inv_c = jnp.where(lane == 0, 1.0 / n, inv_c)
            inv_s = jnp.where(lane == 0, row_sign / n, inv_s)
        inv_ref[:, cols] = inv_c.astype(BF16)
        inv_ref[:, l + a * LANES:l + (a + 1) * LANES] = inv_s.astype(BF16)


def _dft_mats(l):
    n = 2 * l
    f = jnp.arange(l, dtype=jnp.int32)[:, None]

    def table(t):
        ang = ((f * t[None, :]) % n).astype(F32) * (2.0 * math.pi / n)
        return jnp.cos(ang), jnp.sin(ang)

    coarse = table(jnp.arange(l // LANES, dtype=jnp.int32) * LANES)
    fine = table(jnp.arange(LANES, dtype=jnp.int32))
    tm = min(l, 256)
    small = pl.BlockSpec((tm, l // LANES), lambda i: (i, 0))
    wide = pl.BlockSpec((tm, LANES), lambda i: (i, 0))
    return pl.pallas_call(
        _dft_tables_kernel,
        grid=(l // tm,),
        in_specs=[small, small, wide, wide],
        out_specs=[pl.BlockSpec((2, tm, l), lambda i: (0, i, 0)), pl.BlockSpec((tm, n), lambda i: (i, 0))],
        out_shape=[jax.ShapeDtypeStruct((2, l, l), BF16), jax.ShapeDtypeStruct((l, n), BF16)],
        compiler_params=_cp("parallel"),
        name="dft_tables",
    )(*coarse, *fine)


def _conv_blocks(l):
    return 2 if l >= 1024 else 1


def _lag_window_plan(l):
    nb = _conv_blocks(l)
    lb = l // nb
    pos = jnp.arange(l, dtype=F32)
    if nb == 1:
        return pos, jnp.array([0, 0, 0, 1], jnp.int32)
    pos = jnp.concatenate([pos, lb - jnp.arange(lb, dtype=F32)])
    return pos, jnp.array([2, 0, 1, 1, 0, 0, 1, 0, 2, 1, 1, 0], jnp.int32)


def _spec_kernel(tab_ref, f_ref, a_ref, b_ref, o_ref):
    a = a_ref[...]
    row0 = lax.broadcasted_iota(jnp.int32, a.shape, 0) == 0
    b = jnp.where(row0, 0.0, b_ref[...])
    s = (a + b).astype(BF16)
    hi = _dot(f_ref[1], (a - b).astype(BF16))
    o_ref[0, 0] = _dot(f_ref[0], s)
    o_ref[0, 1] = hi

    @pl.when(pl.program_id(1) == 0)
    def _():
        nyq = _dot(f_ref[1, 0:16, :], s)
        row = lax.broadcasted_iota(jnp.int32, nyq.shape, 0)
        o_ref[0, 1, 0:16, :] = jnp.where(row == 0, nyq, hi[0:16])


def _spec_call(fwd, filt, table):
    _, lf, lb = fwd.shape
    nd = table.shape[0] // 4
    n = filt.shape[1] // 2
    tm = min(lf, 512)
    pick = lambda k: pl.BlockSpec((lb, n), lambda d, i, tab: (tab[2 * k * nd + d], tab[(2 * k + 1) * nd + d]))
    return pl.pallas_call(
        _spec_kernel,
        grid_spec=pltpu.PrefetchScalarGridSpec(
            num_scalar_prefetch=1,
            grid=(nd, lf // tm),
            in_specs=[pl.BlockSpec((2, tm, lb), lambda d, i, tab: (0, i, 0)), pick(0), pick(1)],
            out_specs=pl.BlockSpec((1, 2, tm, n), lambda d, i, tab: (d, 0, i, 0))),
        out_shape=jax.ShapeDtypeStruct((nd, 2, lf, n), F32),
        compiler_params=_cp("arbitrary", "arbitrary"),
        name="hyena_spectrum",
    )(table, fwd, filt, filt)


def _dft_fwd_kernel(f_ref, u_ref, h_ref, y_ref, *, nb):
    lb = f_ref.shape[2]
    xr, xi = [], []
    for p in range(nb):
        up = u_ref[0, 0, p * lb:(p + 1) * lb, :].astype(BF16)
        xr.append(_dot(f_ref[0], up))
        xi.append(_dot(f_ref[1], up))
    row = lax.broadcasted_iota(jnp.int32, xr[0].shape, 0)
    real_bins = jnp.logical_and(row == 0, pl.program_id(0) == 0)
    for q in range(nb):
        yr = 0.0
        yi = 0.0
        for p in range(nb):
            gr = h_ref[q - p + nb - 1, 0]
            gi = h_ref[q - p + nb - 1, 1]
            ii = xi[p] * gi
            yr = yr + xr[p] * gr - jnp.where(real_bins, 0.0, ii)
            yi = yi + jnp.where(real_bins, ii, xr[p] * gi + xi[p] * gr)
        y_ref[0, q, 0] = yr.astype(BF16)
        y_ref[0, q, 1] = yi.astype(BF16)


def _dft_fwd_call(fwd, u, group, spec, order, nb):
    _, b, l, c = u.shape
    _, lf, lb = fwd.shape
    nd = spec.shape[0]
    tm = min(lf, 512)
    return pl.pallas_call(
        functools.partial(_dft_fwd_kernel, nb=nb),
        grid=(lf // tm, b),
        in_specs=[pl.BlockSpec((2, tm, lb), lambda i, j: (0, i, 0)),
                  pl.BlockSpec((1, 1, l, c), lambda i, j: (group, j, 0, 0)),
                  pl.BlockSpec((nd, 2, tm, c), lambda i, j: (0, 0, i, order))],
        out_specs=pl.BlockSpec((1, nb, 2, tm, c), lambda i, j: (j, 0, 0, i, 0)),
        out_shape=jax.ShapeDtypeStruct((b, nb, 2, lf, c), BF16),
        compiler_params=_cp("arbitrary", "arbitrary"),
        name="hyena_dft_fwd",
    )(fwd, u, spec)


def _dft_inv_kernel(g_ref, y_ref, u_ref, b_ref, x_ref, o_ref):
    r = _dot(g_ref[...], y_ref[0, 0]) + u_ref[0, 0].astype(F32) * b_ref[...]
    o_ref[0, 0] = (x_ref[0, 0] * r).astype(o_ref.dtype)


def _dft_inv_call(inv, y, u, u_group, bias, gate, gate_group):
    _, b, l, c = u.shape
    lb, n = inv.shape
    nb = l // lb
    tm = min(lb, 1024)
    nt = lb // tm
    y = y.reshape(b, nb, n, c)
    row = lambda g: pl.BlockSpec((1, 1, tm, c), lambda q, i, j: (g, j, q * nt + i, 0))
    return pl.pallas_call(
        _dft_inv_kernel,
        grid=(nb, nt, b),
        in_specs=[pl.BlockSpec((tm, n), lambda q, i, j: (i, 0)),
                  pl.BlockSpec((1, 1, n, c), lambda q, i, j: (j, q, 0, 0)),
                  row(u_group), pl.BlockSpec((1, c), lambda q, i, j: (0, 0)), row(gate_group)],
        out_specs=row(0),
        out_shape=jax.ShapeDtypeStruct((1, b, l, c), BF16),
        compiler_params=_cp("parallel", "parallel", "parallel"),
        name="hyena_dft_inv",
    )(inv, y, u, bias.reshape(1, c), gate)


def _hyena_call(p_hy, sh_w, sh_b, spec, hy_bias, fwd, inv, nb):
    u3 = _hy_short_call(p_hy, sh_w, sh_b)
    z = _dft_inv_call(inv, _dft_fwd_call(fwd, u3, 0, spec, 0, nb), u3, 0, hy_bias[0], u3, 1)
    return _dft_inv_call(inv, _dft_fwd_call(fwd, z, 0, spec, 1, nb), z, 0, hy_bias[1], u3, 2)[0]


def _mix_out_kernel(c_ref, a_ref, h_ref, w_ref, x_ref, g1_ref, g_ref, sh_ref, sc_ref,
                    wr_ref, rb_ref, o_ref, h32_ref, idx_ref, gate_ref):
    k1 = c_ref.shape[2]
    k2 = k1 + a_ref.shape[2]
    acc = _dot(c_ref[0], w_ref[0, :k1, :]) + _dot(a_ref[0], w_ref[0, k1:k2, :]) + _dot(h_ref[0], w_ref[0, k2:, :])
    x = x_ref[0] + g1_ref[0] * acc
    o_ref[0] = x
    hm = _rms(x, g_ref[...]) * (1.0 + sc_ref[0]) + sh_ref[0]
    h32_ref[0] = _pack_rows(hm)
    picks, gates = _route(hm, wr_ref[...], rb_ref[...])
    idx_ref[...] = jnp.concatenate(picks, axis=0)
    gate_ref[...] = jnp.concatenate(gates, axis=0)


def _mix_out_call(conv_o, att_o, hy_o, w_out, layer, x, mod, g2, wr, rbias):
    b, l, d = x.shape
    tm = min(l, 512)
    nt = l // tm
    act = lambda k: pl.BlockSpec((1, tm, k), lambda i, j: (i, j, 0))
    row = pl.BlockSpec((1, 1, d), lambda i, j: (i, 0, 0))
    tok = pl.BlockSpec((2, tm), lambda i, j: (0, i * nt + j))
    return pl.pallas_call(
        _mix_out_kernel,
        grid=(b, nt),
        in_specs=[act(conv_o.shape[2]), act(att_o.shape[2]), act(hy_o.shape[2]),
                  pl.BlockSpec((1,) + w_out.shape[1:], lambda i, j: (layer, 0, 0), pipeline_mode=pl.Buffered(1)),
                  act(d), row, pl.BlockSpec((1, d), lambda i, j: (0, 0)), row, row,
                  pl.BlockSpec(wr.shape, lambda i, j: (0, 0)),
                  pl.BlockSpec((N_EXPERTS, 1), lambda i, j: (0, 0))],
        out_specs=[act(d), act(d // 2), tok, tok],
        out_shape=[jax.ShapeDtypeStruct((b, l, d), F32),
                   jax.ShapeDtypeStruct((b, l, d // 2), jnp.int32),
                   jax.ShapeDtypeStruct((2, b * l), jnp.int32),
                   jax.ShapeDtypeStruct((2, b * l), F32)],
        compiler_params=_cp("parallel", "parallel"),
        name="mix_out",
    )(conv_o, att_o, hy_o, w_out, x, mod[2], g2.reshape(1, d), mod[3], mod[4], wr,
      rbias.reshape(N_EXPERTS, 1))


def _route(hm, wr, rbias):
    hi = hm.astype(BF16)
    lo = (hm - hi.astype(F32)).astype(BF16)
    nt = (((1,), (1,)), ((), ()))
    r = (lax.dot_general(wr, hi, nt, preferred_element_type=F32)
         + lax.dot_general(wr, lo, nt, preferred_element_type=F32))
    logits = r[:N_EXPERTS] + r[N_EXPERTS:]
    score = jax.nn.sigmoid(logits)
    sel = score + rbias
    rows = [sel[e:e + 1] for e in range(N_EXPERTS)]
    gscore = []
    for g in range(N_GROUPS):
        a, b, c, d = rows[4 * g:4 * g + 4]
        h1, l1, h2, l2 = jnp.maximum(a, b), jnp.minimum(a, b), jnp.maximum(c, d), jnp.minimum(c, d)
        gscore.append(jnp.maximum(h1, h2) + jnp.maximum(jnp.minimum(h1, h2), jnp.maximum(l1, l2)))
    best = gscore[0]
    gbest = jnp.zeros(best.shape, jnp.int32)
    for g in range(1, N_GROUPS):
        upd = gscore[g] > best
        gbest = jnp.where(upd, g, gbest)
        best = jnp.where(upd, gscore[g], best)
    masked = [jnp.where(gbest == e // EXPERTS_PER_GROUP, rows[e], NEG_INF) for e in range(N_EXPERTS)]
    picks = []
    for _ in range(2):
        val = jnp.full(best.shape, NEG_INF, F32)
        idx = jnp.zeros(best.shape, jnp.int32)
        for e in range(N_EXPERTS):
            cand = masked[e]
            if picks:
                cand = jnp.where(picks[0] == e, NEG_INF, cand)
            upd = cand > val
            idx = jnp.where(upd, e, idx)
            val = jnp.where(upd, cand, val)
        picks.append(idx)
    gates = [functools.reduce(jnp.add, [jnp.where(p == e, score[e:e + 1], 0.0) for e in range(N_EXPERTS)])
             for p in picks]
    den = gates[0] + gates[1]
    return picks, [gates[0] / den, gates[1] / den]


def _expert_changed(te_ref, i):
    return (i == 0) | (te_ref[i] != te_ref[jnp.maximum(i - 1, 0)])


def _pack_rows(x):
    bits = pltpu.bitcast(x.astype(BF16).astype(F32), jnp.uint32)
    half = bits.shape[1] // 2
    return pltpu.bitcast((bits[:, :half] >> 16) | (bits[:, half:] & jnp.uint32(0xFFFF0000)), jnp.int32)


def _unpack_rows(words):
    w = pltpu.bitcast(words, jnp.uint32)
    return pltpu.bitcast(w << 16, F32), pltpu.bitcast(w & jnp.uint32(0xFFFF0000), F32)


def _moe_up_kernel(te_ref, tn_ref, nu_ref, x_ref, wg_hbm, wu_hbm, h_ref, gbuf, ubuf, wgb_ref, wub_ref, sems,
                   *, layer):
    j, i = pl.program_id(0), pl.program_id(1)
    tf = gbuf.shape[1]

    def fetch(e, chunk):
        cols = pl.ds(pl.multiple_of(chunk * tf, tf), tf)
        return (pltpu.make_async_copy(wg_hbm.at[layer, e, :, cols], gbuf, sems.at[0]),
                pltpu.make_async_copy(wu_hbm.at[layer, e, :, cols], ubuf, sems.at[1]))

    def start(e, chunk):
        for copy in fetch(e, chunk):
            copy.start()

    @pl.when((i == 0) & (j == 0))
    def _():
        start(te_ref[0], 0)

    @pl.when(_expert_changed(te_ref, i))
    def _():
        for copy in fetch(te_ref[i], j):
            copy.wait()
        wgb_ref[...] = gbuf[...].astype(BF16)
        wub_ref[...] = ubuf[...].astype(BF16)
        nxt = tn_ref[i]
        more = nxt < nu_ref[0]

        @pl.when(more)
        def _():
            start(te_ref[nxt], j)

        @pl.when(jnp.logical_not(more) & (j + 1 < pl.num_programs(0)))
        def _():
            start(te_ref[0], j + 1)

    @pl.when(i < nu_ref[0])
    def _():
        xa, xb = (v.astype(BF16) for v in _unpack_rows(x_ref[...]))
        half = xa.shape[1]
        a = _dot(xa, wgb_ref[:half, :]) + _dot(xb, wgb_ref[half:, :])
        u = _dot(xa, wub_ref[:half, :]) + _dot(xb, wub_ref[half:, :])
        h_ref[...] = (a * jax.nn.sigmoid(a) * u).astype(BF16)

    @pl.when(i >= nu_ref[0])
    def _():
        h_ref[...] = jnp.zeros(h_ref.shape, BF16)


def _moe_up_call(xs, w_gate, w_up, layer, te, tn, nu, tile):
    p, dw = xs.shape
    _, _, d, f = w_gate.shape
    tf = 512
    hbm = pl.BlockSpec(memory_space=pl.ANY)
    return pl.pallas_call(
        functools.partial(_moe_up_kernel, layer=layer),
        grid_spec=pltpu.PrefetchScalarGridSpec(
            num_scalar_prefetch=3,
            grid=(f // tf, p // tile),
            in_specs=[pl.BlockSpec((tile, dw), lambda j, i, te, tn, nu: (i, 0)), hbm, hbm],
            out_specs=pl.BlockSpec((tile, tf), lambda j, i, te, tn, nu: (i, j)),
            scratch_shapes=[pltpu.VMEM((d, tf), F32), pltpu.VMEM((d, tf), F32),
                            pltpu.VMEM((d, tf), BF16), pltpu.VMEM((d, tf), BF16),
                            pltpu.SemaphoreType.DMA((2,))]),
        out_shape=jax.ShapeDtypeStruct((p, f), BF16),
        compiler_params=_cp("arbitrary", "arbitrary"),
        name="moe_up",
    )(te, tn, nu, xs, w_gate, w_up)


def _moe_down_kernel(te_ref, tn_ref, nu_ref, h_ref, wd_hbm, y_ref, wbuf, wdb_ref, sem, *, layer):
    i = pl.program_id(0)

    def fetch(e):
        return pltpu.make_async_copy(wd_hbm.at[layer, e], wbuf, sem)

    @pl.when(i == 0)
    def _():
        fetch(te_ref[0]).start()

    @pl.when(_expert_changed(te_ref, i))
    def _():
        fetch(te_ref[i]).wait()
        wdb_ref[...] = wbuf[...].astype(BF16)
        nxt = tn_ref[i]

        @pl.when(nxt < nu_ref[0])
        def _():
            fetch(te_ref[nxt]).start()

    @pl.when(i < nu_ref[0])
    def _():
        y_ref[...] = _pack_rows(_dot(h_ref[...], wdb_ref[...]))

    @pl.when(i >= nu_ref[0])
    def _():
        y_ref[...] = jnp.zeros(y_ref.shape, jnp.int32)


def _moe_down_call(h, w_down, layer, te, tn, nu, tile):
    p, f = h.shape
    d = w_down.shape[3]
    return pl.pallas_call(
        functools.partial(_moe_down_kernel, layer=layer),
        grid_spec=pltpu.PrefetchScalarGridSpec(
            num_scalar_prefetch=3,
            grid=(p // tile,),
            in_specs=[pl.BlockSpec((tile, f), lambda i, te, tn, nu: (i, 0)),
                      pl.BlockSpec(memory_space=pl.ANY)],
            out_specs=pl.BlockSpec((tile, d // 2), lambda i, te, tn, nu: (i, 0)),
            scratch_shapes=[pltpu.VMEM((f, d), F32), pltpu.VMEM((f, d), BF16), pltpu.SemaphoreType.DMA(())]),
        out_shape=jax.ShapeDtypeStruct((p, d // 2), jnp.int32),
        compiler_params=_cp("arbitrary"),
        name="moe_down",
    )(te, tn, nu, h, w_down)


PLAN_BLOCK = 512


def _plan_kernel(idx_ref, pos_ref, te_ref, tn_ref, nu_ref, rank_ref, *, tile):
    t = idx_ref.shape[1]
    nb = t // PLAN_BLOCK
    experts = lax.broadcasted_iota(jnp.int32, (N_EXPERTS, PLAN_BLOCK), 0)
    r = lax.broadcasted_iota(jnp.int32, (PLAN_BLOCK, PLAN_BLOCK), 0)
    c = lax.broadcasted_iota(jnp.int32, (PLAN_BLOCK, PLAN_BLOCK), 1)
    before = jnp.where(r < c, 1.0, 0.0).astype(BF16)

    def rank_pass(k):
        def body(j, carry):
            cols = pl.ds(pl.multiple_of(j * PLAN_BLOCK, PLAN_BLOCK), PLAN_BLOCK)
            onehot = jnp.where(experts == idx_ref[k:k + 1, cols], 1.0, 0.0)
            earlier = _dot(onehot.astype(BF16), before) + carry
            rank_ref[k:k + 1, cols] = jnp.sum(onehot * earlier, axis=0, keepdims=True)
            return carry + jnp.sum(onehot, axis=1, keepdims=True)
        return body

    counts = jnp.zeros((N_EXPERTS, 1), F32)
    for k in range(2):
        counts = lax.fori_loop(0, nb, rank_pass(k), counts)
    padded = jnp.ceil(counts * (1.0 / tile)) * tile
    starts, ends = [], []
    acc = jnp.zeros((1, 1), F32)
    for e in range(N_EXPERTS):
        starts.append(acc)
        acc = acc + padded[e:e + 1]
        ends.append(acc)

    def place_pass(k):
        def body(j, carry):
            cols = pl.ds(pl.multiple_of(j * PLAN_BLOCK, PLAN_BLOCK), PLAN_BLOCK)
            row = idx_ref[k:k + 1, cols]
            start = functools.reduce(jnp.add, [jnp.where(row == e, starts[e], 0.0) for e in range(N_EXPERTS)])
            pos_ref[k:k + 1, cols] = (rank_ref[k:k + 1, cols] + start).astype(jnp.int32)
            return carry
        return body

    for k in range(2):
        lax.fori_loop(0, nb, place_pass(k), 0)
    n_used = acc * (1.0 / tile)
    tiles = lax.broadcasted_iota(jnp.int32, te_ref.shape, 1).astype(F32)
    first_row = jnp.minimum(tiles, n_used - 1.0) * tile
    te = functools.reduce(jnp.add, [jnp.where(ends[e] <= first_row, 1.0, 0.0) for e in range(N_EXPERTS)])
    te_ref[...] = te.astype(jnp.int32)
    seg_end = functools.reduce(jnp.add, [jnp.where(te == e, ends[e], 0.0) for e in range(N_EXPERTS)])
    tn_ref[...] = (seg_end * (1.0 / tile)).astype(jnp.int32)
    nu_ref[...] = jnp.broadcast_to(n_used, nu_ref.shape).astype(jnp.int32)


def _plan_call(idx, tile, n_tiles):
    t = idx.shape[1]
    whole = lambda shape: pl.BlockSpec(shape, lambda: (0,) * len(shape))
    lanes = jax.ShapeDtypeStruct((1, LANES), jnp.int32)
    pos, te, tn, nu = pl.pallas_call(
        functools.partial(_plan_kernel, tile=tile),
        in_specs=[whole(idx.shape)],
        out_specs=[whole(idx.shape), whole((1, LANES)), whole((1, LANES)), whole((1, LANES))],
        out_shape=[jax.ShapeDtypeStruct(idx.shape, jnp.int32), lanes, lanes, lanes],
        scratch_shapes=[pltpu.VMEM(idx.shape, F32)],
        name="route_plan",
    )(idx)
    return pos.reshape(2 * t), te[0, :n_tiles], tn[0, :n_tiles], nu[0, :1]


V7X_SC_CORES = 2
V7X_SC_SUBCORES = 16
SC_WORKERS = V7X_SC_CORES * V7X_SC_SUBCORES
SC_ROW_BYTES = 256 * 1024


def _sc_worker():
    return lax.axis_index("s") * V7X_SC_CORES + lax.axis_index("c")


def _sc_dispatch_call(streams, pos, n_rows):
    w = streams[0].shape[1]
    sizes = [s.shape[0] for s in streams]
    t = sum(sizes)
    per_w = t // SC_WORKERS
    ch = min(per_w, SC_ROW_BYTES // (4 * w))
    assert per_w % ch == 0 and all(size % ch == 0 for size in sizes)
    mesh = plsc.VectorSubcoreMesh(core_axis_name="c", subcore_axis_name="s")

    @functools.partial(pl.kernel, mesh=mesh, out_type=jax.ShapeDtypeStruct((n_rows, w), jnp.int32),
                       scratch_types=[pltpu.VMEM((ch,), jnp.int32), pltpu.VMEM((ch,), jnp.int32),
                                      pltpu.VMEM((ch, w), jnp.int32),
                                      pltpu.SemaphoreType.DMA, pltpu.SemaphoreType.DMA],
                       name="sc_dispatch")
    def run(*refs):
        srcs = refs[:len(sizes)]
        pos_hbm, xs_hbm, idx0_v, idx1_v, rows_v, sem0, sem1 = refs[len(sizes):]
        first = _sc_worker() * per_w

        @pl.loop(0, per_w // ch)
        def _(j):
            base = first + j * ch
            start = 0
            for src, size in zip(srcs, sizes):
                def load(src=src, start=start):
                    pltpu.sync_copy(src.at[pl.ds(base - start, ch)], rows_v)
                if len(sizes) == 1:
                    load()
                else:
                    pl.when((base >= start) & (base < start + size))(load)
                start += size
            pltpu.sync_copy(pos_hbm.at[pl.ds(base, ch)], idx0_v)
            pltpu.sync_copy(pos_hbm.at[pl.ds(t + base, ch)], idx1_v)
            first_choice = pltpu.async_copy(rows_v, xs_hbm.at[idx0_v], sem0)
            second_choice = pltpu.async_copy(rows_v, xs_hbm.at[idx1_v], sem1)
            first_choice.wait()
            second_choice.wait()

    return run(*streams, pos)


def _sc_combine_call(ys, pos):
    _, d = ys.shape
    n = pos.shape[0]
    per_w = n // SC_WORKERS
    ch = min(per_w // 2, SC_ROW_BYTES // (8 * d))
    assert per_w % (2 * ch) == 0
    mesh = plsc.VectorSubcoreMesh(core_axis_name="c", subcore_axis_name="s")
    idx_buf = pltpu.VMEM((ch,), jnp.int32)
    row_buf = pltpu.VMEM((ch, d), jnp.int32)

    @functools.partial(pl.kernel, mesh=mesh, out_type=jax.ShapeDtypeStruct((n, d), jnp.int32),
                       scratch_types=[idx_buf, idx_buf, row_buf, row_buf] + [pltpu.SemaphoreType.DMA] * 4,
                       name="sc_combine")
    def run(ys_hbm, pos_hbm, out_hbm, idx_a, idx_b, rows_a, rows_b, gsem_a, gsem_b, wsem_a, wsem_b):
        first = _sc_worker() * per_w

        @pl.loop(0, per_w // (2 * ch))
        def _(j):
            base = first + j * (2 * ch)
            pltpu.sync_copy(pos_hbm.at[pl.ds(base, ch)], idx_a)
            pltpu.sync_copy(pos_hbm.at[pl.ds(base + ch, ch)], idx_b)
            gather_a = pltpu.async_copy(ys_hbm.at[idx_a], rows_a, gsem_a)
            gather_b = pltpu.async_copy(ys_hbm.at[idx_b], rows_b, gsem_b)
            gather_a.wait()
            write_a = pltpu.async_copy(rows_a, out_hbm.at[pl.ds(base, ch)], wsem_a)
            gather_b.wait()
            write_b = pltpu.async_copy(rows_b, out_hbm.at[pl.ds(base + ch, ch)], wsem_b)
            write_a.wait()
            write_b.wait()

    return run(ys, pos)


def _moe_call(streams, idx, w_gate, w_up, w_down, layer):
    t = idx.shape[1]
    tile = min(MOE_TILE, max(t // N_EXPERTS, 128))
    n_tiles = (2 * t) // tile + N_EXPERTS
    assert n_tiles <= LANES
    pos, te, tn, nu = _plan_call(idx, tile, n_tiles)
    xs = _sc_dispatch_call(streams, pos, n_tiles * tile)
    hid = _moe_up_call(xs, w_gate, w_up, layer, te, tn, nu, tile)
    ys = _moe_down_call(hid, w_down, layer, te, tn, nu, tile)
    return _sc_combine_call(ys, pos).reshape(2, t, ys.shape[1])


def _moe_resid_kernel(x_ref, y0_ref, y1_ref, gt_ref, g_ref, *rest, final):
    gt = gt_ref[0]
    halves0 = _unpack_rows(y0_ref[0])
    halves1 = _unpack_rows(y1_ref[0])
    y = jnp.concatenate([gt[:, 0:1] * a + gt[:, 1:2] * b for a, b in zip(halves0, halves1)], axis=1)
    x = x_ref[0] + g_ref[0] * y
    if final:
        fg_ref, o_ref = rest
        x = _rms(x, fg_ref[...])
    else:
        (o_ref,) = rest
    o_ref[0] = x


def _moe_resid_call(x, y, first_row, gates, gate, final_g=None):
    b, l, d = x.shape
    tl = min(l, 512)
    nt = l // tl
    tile = pl.BlockSpec((1, tl, d), lambda i, j: (i, j, 0))
    choice = lambda k: pl.BlockSpec((1, tl, d // 2), lambda i, j: (k, first_row // tl + i * nt + j, 0))
    in_specs = [tile, choice(0), choice(1), pl.BlockSpec((1, tl, 2), lambda i, j: (i, j, 0)),
                pl.BlockSpec((1, 1, d), lambda i, j: (i, 0, 0))]
    args = [x, y, y, gates.T.reshape(b, l, 2), gate]
    if final_g is not None:
        in_specs.append(pl.BlockSpec((1, d), lambda i, j: (0, 0)))
        args.append(final_g.reshape(1, d))
    return pl.pallas_call(
        functools.partial(_moe_resid_kernel, final=final_g is not None),
        grid=(b, l // tl),
        in_specs=in_specs,
        out_specs=tile,
        out_shape=jax.ShapeDtypeStruct((b, l, d), F32),
        compiler_params=_cp("parallel", "parallel"),
        name="moe_residual",
    )(*args)


def _rope_perm(w):
    r = w.reshape(w.shape[:-1] + (2, 2, ROPE_FREQS))
    return jnp.stack([-r[..., 1, :], r[..., 0, :]], axis=-2).reshape(w.shape)


def _rope_table(l, rotary):
    if not rotary:
        return jnp.concatenate([jnp.ones((l, MLA_ROPE), F32), jnp.zeros((l, MLA_ROPE), F32)], axis=1)
    rows = l // GRID_W
    r = jnp.repeat(jnp.arange(rows), GRID_W)
    col = jnp.tile(jnp.arange(GRID_W), rows)
    pos = jnp.stack([r, col], axis=-1).astype(F32)
    inv = ROPE_THETA ** (-jnp.arange(ROPE_FREQS, dtype=F32) / ROPE_FREQS)
    ang = pos[:, :, None] * inv
    lay = lambda a: jnp.concatenate([a[:, 0], a[:, 0], a[:, 1], a[:, 1]], axis=1)
    return jnp.concatenate([lay(jnp.cos(ang)), lay(jnp.sin(ang))], axis=1)


def _prep_weights(w_in, w_uq, w_ukv, w_out):
    depth = w_in.shape[0]
    cast = lambda a: a.astype(BF16)
    kr = cast(w_in[:, :, OFF_KR:OFF_HY])
    w_kv_in = jnp.concatenate([cast(w_in[:, :, OFF_KV:OFF_KR]), kr, _rope_perm(kr)], axis=2)
    w_all = jnp.concatenate([cast(w_in[:, :, :OFF_KV]), cast(w_in[:, :, OFF_HY:]), w_kv_in], axis=2)
    uq = cast(w_uq).reshape(depth, MLA_Q_RANK, MLA_HEADS, MLA_NOPE + MLA_ROPE)
    w_q = jnp.concatenate([uq, _rope_perm(uq[..., MLA_NOPE:])], axis=3).reshape(depth, MLA_Q_RANK, -1)
    ukv = cast(w_ukv).reshape(depth, MLA_KV_RANK, MLA_HEADS, MLA_NOPE + MLA_V_DIM)
    w_k = ukv[..., :MLA_NOPE].reshape(depth, MLA_KV_RANK, MLA_HEADS * MLA_NOPE)
    w_v = ukv[..., MLA_NOPE:].reshape(depth, MLA_KV_RANK, MLA_HEADS * MLA_V_DIM)
    return dict(all=w_all, kv=w_kv_in, uq=w_q, uk=w_k, uv=w_v, out=cast(w_out))


def _router_weights(w_router):
    hi = w_router.astype(BF16)
    lo = (w_router - hi.astype(F32)).astype(BF16)
    return jnp.concatenate([hi.T, lo.T], axis=0)


def _mixer(x, mod, w, layer, lp, cs, spec, fwd, inv, kv_extra, full, wr, rbias):
    kv_args = (lp["kv_norm_g"], w["uk"], w["uv"], cs, layer)
    if not full:
        k, v = _in_proj_call(x, lp["norm1_g"], mod[0], mod[1], w["kv"], [], *kv_args)
        return None, k, v
    splits = [OFF_Q, MLA_Q_RANK, (HY_ORDER + 1) * HYENA_W]
    p_conv, p_q, p_hy, k, v = _in_proj_call(x, lp["norm1_g"], mod[0], mod[1], w["all"], splits, *kv_args)
    conv_o = _conformer_call(p_conv, lp["conv_dw_w"], lp["conv_dw_b"], lp["conv_ln_g"], lp["conv_ln_b"])
    att_o = _attn_call(p_q, lp["q_norm_g"], w["uq"], cs, kv_extra + [(k, v)], layer)
    hy_o = _hyena_call(p_hy, lp["hy_short_w"], lp["hy_short_b"], spec, lp["hy_bias"], fwd, inv,
                       _conv_blocks(x.shape[1]))
    routed = _mix_out_call(conv_o, att_o, hy_o, w["out"], layer, x, mod, lp["norm2_g"], wr, rbias)
    return routed, k, v


def _channel(streams, experts, layer, final_g=None):
    d = streams[0][1][0].shape[2]
    idx = jnp.concatenate([r[2] for _, r in streams], axis=1)
    y = _moe_call([r[1].reshape(-1, d // 2) for _, r in streams], idx, *experts, layer)
    outs = []
    first_row = 0
    for mod, (x, _, _, gates) in streams:
        outs.append(_moe_resid_call(x, y, first_row, gates, mod[5], final_g))
        first_row += x.shape[0] * x.shape[1]
    return outs


def _filter_spectrum(l, lp, fwd):
    pos, table = _lag_window_plan(l)
    filt = _hy_filter_call(l, pos, lp["hy_w1"], lp["hy_b1"], lp["hy_freq1"], lp["hy_w2"], lp["hy_b2"],
                           lp["hy_freq2"], lp["hy_w3"], lp["hy_b3"])
    return _spec_call(fwd, filt, table)


def kernel(x, c, ctx, c_ctx, norm1_g, norm2_g, w_ada, b_ada, w_in, conv_dw_w, conv_dw_b, conv_ln_g, conv_ln_b, q_norm_g, w_uq, kv_norm_g, w_ukv, hy_short_w, hy_short_b, hy_w1, hy_b1, hy_freq1, hy_w2, hy_b2, hy_freq2, hy_w3, hy_b3, hy_bias, w_out, w_router, router_bias, w_gate, w_up, w_down, final_norm_g):
    b, l, d = x.shape
    lc = ctx.shape[1]
    depth = w_in.shape[0]
    per_layer = dict(norm1_g=norm1_g, norm2_g=norm2_g, conv_dw_w=conv_dw_w, conv_dw_b=conv_dw_b,
                     conv_ln_g=conv_ln_g, conv_ln_b=conv_ln_b, q_norm_g=q_norm_g, kv_norm_g=kv_norm_g,
                     hy_short_w=hy_short_w, hy_short_b=hy_short_b, hy_w1=hy_w1, hy_b1=hy_b1, hy_freq1=hy_freq1,
                     hy_w2=hy_w2, hy_b2=hy_b2, hy_freq2=hy_freq2, hy_w3=hy_w3, hy_b3=hy_b3, hy_bias=hy_bias)
    experts = (w_gate, w_up, w_down)
    cs_l = _rope_table(l, True)
    cs_c = _rope_table(lc, False)
    fwd_l, inv_l = _dft_mats(l // _conv_blocks(l))
    fwd_c, inv_c = _dft_mats(lc // _conv_blocks(lc))
    wr = _router_weights(w_router)
    w = _prep_weights(w_in, w_uq, w_ukv, w_out)
    rows = 16
    cvec = jnp.concatenate([c, c_ctx[None, :], jnp.zeros((rows - b - 1, d), F32)], axis=0)
    xl, xc = x, ctx
    for i in range(depth):
        last = i == depth - 1
        lp = {k: v[i] for k, v in per_layer.items()}
        m = _ada_call(cvec, w_ada, b_ada, i)
        mod_l = [m[:b, None, j * d:(j + 1) * d] for j in range(6)]
        mod_c = [jnp.broadcast_to(m[b:b + 1, None, j * d:(j + 1) * d], (b, 1, d)) for j in range(6)]
        spec_l = _filter_spectrum(l, lp, fwd_l)
        spec_c = None if last else _filter_spectrum(lc, lp, fwd_c)
        routed_c, kc, vc = _mixer(xc, mod_c, w, i, lp, cs_c, spec_c, fwd_c, inv_c, [], not last, wr, router_bias)
        routed_l, _, _ = _mixer(xl, mod_l, w, i, lp, cs_l, spec_l, fwd_l, inv_l, [(kc, vc)], True, wr, router_bias)
        if last:
            (xl,) = _channel([(mod_l, routed_l)], experts, i, final_norm_g)
        else:
            xl, xc = _channel([(mod_l, routed_l), (mod_c, routed_c)], experts, i)
    return xl
```

```python
import functools
import math

import jax
import jax.numpy as jnp
from jax import lax
from jax.experimental import pallas as pl
from jax.experimental.pallas import tpu as pltpu
from jax.experimental.pallas import tpu_sc as plsc

F32 = jnp.float32
BF16 = jnp.bfloat16

EPS = 1e-6
GRID_W = 64
CONV_W = 512
CONV_KSIZE = 31
MLA_HEADS = 8
MLA_NOPE = 128
MLA_ROPE = 64
MLA_V_DIM = 128
MLA_Q_RANK = 768
MLA_KV_RANK = 512
ROPE_FREQS = MLA_ROPE // 4
ROPE_THETA = 10000.0
MLA_SCALE = (MLA_NOPE + MLA_ROPE) ** -0.5
Q_SCALE = MLA_SCALE * math.log2(math.e)
HYENA_W = 512
HY_ORDER = 2
HY_SHORT = 3
HY_EMB = 33
HY_BANDS = (HY_EMB - 1) // 2
HY_FFN = 64
HY_MIN_DECAY = math.log(1e-2) / 1.5
HY_MAX_DECAY = math.log(1e-2) / 0.3
N_EXPERTS = 16
N_GROUPS = 4
EXPERTS_PER_GROUP = N_EXPERTS // N_GROUPS
OFF_Q = 2 * CONV_W
OFF_KV = OFF_Q + MLA_Q_RANK
OFF_KR = OFF_KV + MLA_KV_RANK
OFF_HY = OFF_KR + MLA_ROPE

LANES = 128
QK_HEAD = 2 * LANES
V_HEAD = MLA_V_DIM + LANES
V7X_VMEM_BYTES = 64 * 1024 * 1024
VMEM_LIMIT = (V7X_VMEM_BYTES * 7) // 8
MOE_TILE = 512
NEG_INF = float("-inf")


def _cp(*sem):
    return pltpu.CompilerParams(dimension_semantics=sem, vmem_limit_bytes=VMEM_LIMIT)


def _dot(a, b):
    return jnp.dot(a, b, preferred_element_type=F32)


def _rms(x, g):
    ms = jnp.mean(x * x, axis=-1, keepdims=True)
    return x * lax.rsqrt(ms + EPS) * g


def _ada_kernel(c_ref, w_ref, b_ref, o_ref):
    c = c_ref[...]
    a = (c * jax.nn.sigmoid(c)).astype(BF16)
    o_ref[...] = _dot(a, w_ref[0].astype(BF16)) + b_ref[0]


def _ada_call(cvec, w, b, layer):
    m, d = cvec.shape
    n = w.shape[2]
    tn = 1024
    return pl.pallas_call(
        _ada_kernel,
        grid=(n // tn,),
        in_specs=[pl.BlockSpec((m, d), lambda j: (0, 0)),
                  pl.BlockSpec((1, d, tn), lambda j: (layer, 0, j)),
                  pl.BlockSpec((1, 1, tn), lambda j: (layer, 0, j))],
        out_specs=pl.BlockSpec((m, tn), lambda j: (0, j)),
        out_shape=jax.ShapeDtypeStruct((m, n), F32),
        compiler_params=_cp("parallel"),
        name="ada",
    )(cvec, w, b.reshape(b.shape[0], 1, n))


def _in_proj_kernel(x_ref, g_ref, sh_ref, sc_ref, *refs, n_proj):
    w_refs, (wkv_ref, kg_ref, wk_ref, wv_ref, cs_ref), o_refs = refs[:n_proj], refs[n_proj:n_proj + 5], refs[n_proj + 5:]
    *p_refs, k_ref, v_ref = o_refs
    y = _rms(x_ref[0], g_ref[...])
    h = (y * (1.0 + sc_ref[0]) + sh_ref[0]).astype(BF16)
    outs = iter(p_refs)
    for w_ref in w_refs:
        off = 0
        while off < w_ref.shape[2]:
            o_ref = next(outs)
            n = o_ref.shape[-1]
            o_ref[0] = _dot(h, w_ref[0, :, off:off + n]).astype(o_ref.dtype)
            off += n
    p = _dot(h, wkv_ref[0])
    xn = _rms(p[:, :MLA_KV_RANK], kg_ref[...]).astype(BF16)
    kk = _dot(xn, wk_ref[0])
    vv = _dot(xn, wv_ref[0])
    rot = _rotate(p[:, MLA_KV_RANK:], cs_ref[...])
    lane = lax.broadcasted_iota(jnp.int32, rot.shape, 1)
    rot = jnp.where(lane < MLA_ROPE, rot, 0.0).astype(BF16)
    ones = jnp.where(lane == 0, 1.0, 0.0).astype(BF16)
    for hd in range(MLA_HEADS):
        k_ref[0, :, hd * QK_HEAD:hd * QK_HEAD + LANES] = kk[:, hd * LANES:(hd + 1) * LANES].astype(BF16)
        k_ref[0, :, hd * QK_HEAD + LANES:(hd + 1) * QK_HEAD] = rot
        v_ref[0, :, hd * V_HEAD:hd * V_HEAD + MLA_V_DIM] = vv[:, hd * MLA_V_DIM:(hd + 1) * MLA_V_DIM].astype(BF16)
        v_ref[0, :, hd * V_HEAD + MLA_V_DIM:(hd + 1) * V_HEAD] = ones


def _in_proj_call(x, g, shift, scale, proj, w_kv, kv_g, w_k, w_v, cs, layer):
    b, l, d = x.shape
    tm = min(l, 512)
    r = MLA_KV_RANK
    row = pl.BlockSpec((1, 1, d), lambda i, j: (i, 0, 0))
    const = lambda shape: pl.BlockSpec(shape, lambda i, j: (0, 0))
    of_layer = lambda a, **kw: pl.BlockSpec((1,) + a.shape[1:], lambda i, j: (layer, 0, 0), **kw)
    weights = [w for w, _ in proj]
    widths = [k for _, ks in proj for k in ks] + [MLA_HEADS * QK_HEAD, MLA_HEADS * V_HEAD]
    return pl.pallas_call(
        functools.partial(_in_proj_kernel, n_proj=len(weights)),
        grid=(b, l // tm),
        in_specs=[pl.BlockSpec((1, tm, d), lambda i, j: (i, j, 0)),
                  const((1, d)), row, row,
                  *[of_layer(w, pipeline_mode=pl.Buffered(1)) for w in weights],
                  of_layer(w_kv, pipeline_mode=pl.Buffered(1)),
                  const((1, r)), of_layer(w_k), of_layer(w_v),
                  pl.BlockSpec((tm, LANES), lambda i, j: (j, 0))],
        out_specs=[pl.BlockSpec((1, tm, k), lambda i, j: (i, j, 0)) for k in widths],
        out_shape=[jax.ShapeDtypeStruct((b, l, k), BF16) for k in widths],
        compiler_params=_cp("parallel", "parallel"),
        name="in_proj",
    )(x, g.reshape(1, d), shift, scale, *weights, w_kv, kv_g.reshape(1, r), w_k, w_v, cs)


CONV_ROWS = 128
CONV_PAD = 16


def _conformer_kernel(p_ref, w_ref, b_ref, g_ref, be_ref, o_ref, zp_ref):
    l = p_ref.shape[1]
    c = CONV_W
    zeros = jnp.zeros((CONV_PAD, c), F32)
    zp_ref[0:CONV_PAD, :] = zeros
    zp_ref[CONV_PAD + l:CONV_PAD + l + CONV_PAD, :] = zeros

    def glu(r, carry):
        base = pl.multiple_of(r * CONV_ROWS, CONV_ROWS)
        blk = p_ref[0, pl.ds(base, CONV_ROWS), :].astype(F32)
        zp_ref[pl.ds(base + CONV_PAD, CONV_ROWS), :] = blk[:, :c] * jax.nn.sigmoid(blk[:, c:])
        return carry

    lax.fori_loop(0, l // CONV_ROWS, glu, 0)

    win_rows = CONV_ROWS + 2 * CONV_PAD
    first = CONV_PAD - CONV_KSIZE // 2

    def conv(r, carry):
        base = pl.multiple_of(r * CONV_ROWS, CONV_ROWS)
        parts = []
        for cc in range(c // LANES):
            lanes = slice(cc * LANES, (cc + 1) * LANES)
            win = zp_ref[pl.ds(base, win_rows), lanes]
            acc = jnp.zeros((CONV_ROWS, LANES), F32)
            for res in range(8):
                taps = [k for k in range(CONV_KSIZE) if (k + first) % 8 == res]
                if not taps:
                    continue
                shifted = pltpu.roll(win, win_rows - res, axis=0) if res else win
                for k in taps:
                    off = k + first - res
                    acc = acc + shifted[off:off + CONV_ROWS, :] * w_ref[k:k + 1, lanes]
            parts.append(acc)
        z = jnp.concatenate(parts, axis=1) + b_ref[...]
        mu = jnp.mean(z, axis=-1, keepdims=True)
        zc = z - mu
        var = jnp.mean(zc * zc, axis=-1, keepdims=True)
        y = zc * lax.rsqrt(var + EPS) * g_ref[...] + be_ref[...]
        o_ref[0, pl.ds(base, CONV_ROWS), :] = (y * jax.nn.sigmoid(y)).astype(o_ref.dtype)
        return carry

    lax.fori_loop(0, l // CONV_ROWS, conv, 0)


def _conformer_call(p, dw_w, dw_b, ln_g, ln_b):
    b, l, c2 = p.shape
    c = c2 // 2
    vec = pl.BlockSpec((1, c), lambda i: (0, 0))
    return pl.pallas_call(
        _conformer_kernel,
        grid=(b,),
        in_specs=[pl.BlockSpec((1, l, c2), lambda i: (i, 0, 0)),
                  pl.BlockSpec((CONV_KSIZE, c), lambda i: (0, 0)), vec, vec, vec],
        out_specs=pl.BlockSpec((1, l, c), lambda i: (i, 0, 0)),
        out_shape=jax.ShapeDtypeStruct((b, l, c), BF16),
        scratch_shapes=[pltpu.VMEM((l + 2 * CONV_PAD, c), F32)],
        compiler_params=_cp("parallel"),
        name="conformer",
    )(p, dw_w, dw_b.reshape(1, c), ln_g.reshape(1, c), ln_b.reshape(1, c))


def _rotate(group, cs):
    t = group * cs
    return t + pltpu.roll(t, MLA_ROPE, axis=1)


def _attn_kernel(p_ref, g_ref, w_ref, cs_ref, *refs, n_seg):
    o_ref, q_ref = refs[2 * n_seg], refs[2 * n_seg + 1]
    xn = _rms(p_ref[0].astype(F32), g_ref[...]).astype(BF16)
    cs = cs_ref[...] * Q_SCALE
    for h in range(MLA_HEADS):
        r = _dot(xn, w_ref[0, :, h * QK_HEAD:(h + 1) * QK_HEAD])
        q_ref[:, h * QK_HEAD:h * QK_HEAD + LANES] = (r[:, :LANES] * Q_SCALE).astype(BF16)
        q_ref[:, h * QK_HEAD + LANES:(h + 1) * QK_HEAD] = _rotate(r[:, LANES:], cs).astype(BF16)
    nt = (((1,), (1,)), ((), ()))
    for h in range(MLA_HEADS):
        qk = slice(h * QK_HEAD, (h + 1) * QK_HEAD)
        vd = slice(h * MLA_V_DIM, (h + 1) * MLA_V_DIM)
        q = q_ref[:, qk]
        scores = [lax.dot_general(q, refs[2 * s][0, :, qk], nt, preferred_element_type=F32)
                  for s in range(n_seg)]
        m = functools.reduce(jnp.maximum, [jnp.max(s, axis=-1, keepdims=True) for s in scores])
        acc = 0.0
        for s in range(n_seg):
            p = jnp.exp2(scores[s] - m).astype(BF16)
            acc = acc + _dot(p, refs[2 * s + 1][0, :, h * V_HEAD:(h + 1) * V_HEAD])
        o_ref[0, :, vd] = (acc[:, :MLA_V_DIM] * (1.0 / acc[:, MLA_V_DIM:MLA_V_DIM + 1])).astype(o_ref.dtype)


def _attn_call(p_q, q_g, w_q, cs, segs, layer):
    b, l, r = p_q.shape
    tq = min(l, 512)
    const = lambda shape: pl.BlockSpec(shape, lambda i, j: (0, 0))
    in_specs = [pl.BlockSpec((1, tq, r), lambda i, j: (i, j, 0)), const((1, r)),
                pl.BlockSpec((1,) + w_q.shape[1:], lambda i, j: (layer, 0, 0)),
                pl.BlockSpec((tq, LANES), lambda i, j: (j, 0))]
    args = [p_q, q_g.reshape(1, r), w_q, cs]
    for k, v in segs:
        in_specs.append(pl.BlockSpec((1,) + k.shape[1:], lambda i, j: (i, 0, 0), pipeline_mode=pl.Buffered(1)))
        in_specs.append(pl.BlockSpec((1,) + v.shape[1:], lambda i, j: (i, 0, 0), pipeline_mode=pl.Buffered(1)))
        args += [k, v]
    return pl.pallas_call(
        functools.partial(_attn_kernel, n_seg=len(segs)),
        grid=(b, l // tq),
        in_specs=in_specs,
        out_specs=pl.BlockSpec((1, tq, MLA_HEADS * MLA_V_DIM), lambda i, j: (i, j, 0)),
        out_shape=jax.ShapeDtypeStruct((b, l, MLA_HEADS * MLA_V_DIM), BF16),
        scratch_shapes=[pltpu.VMEM((tq, MLA_HEADS * QK_HEAD), BF16)],
        compiler_params=_cp("parallel", "parallel"),
        name="attention",
    )(*args)


HY_PAD = 8
HY_ROWS = 128


def _hy_short_kernel(p_ref, w_ref, b_ref, o_ref, up_ref):
    l = p_ref.shape[1]
    c = p_ref.shape[2]
    zeros = jnp.zeros((HY_PAD, c), F32)
    up_ref[0:HY_PAD, :] = zeros
    up_ref[HY_PAD + l:HY_PAD + l + HY_PAD, :] = zeros

    def copy(r, carry):
        base = pl.multiple_of(r * HY_ROWS, HY_ROWS)
        up_ref[pl.ds(base + HY_PAD, HY_ROWS), :] = p_ref[0, pl.ds(base, HY_ROWS), :].astype(F32)
        return carry

    lax.fori_loop(0, l // HY_ROWS, copy, 0)
    first = HY_PAD - HY_SHORT // 2

    def conv(r, carry):
        base = pl.multiple_of(r * HY_ROWS, HY_ROWS)
        for cc in range(c // LANES):
            lanes = slice(cc * LANES, (cc + 1) * LANES)
            win = up_ref[pl.ds(base, HY_ROWS + 2 * HY_PAD), lanes]
            acc = jnp.zeros((HY_ROWS, LANES), F32) + b_ref[:, lanes]
            for k in range(HY_SHORT):
                acc = acc + win[first + k:first + k + HY_ROWS, :] * w_ref[k:k + 1, lanes]
            o_ref[0, 0, pl.ds(base, HY_ROWS), lanes] = acc.astype(o_ref.dtype)
        return carry

    lax.fori_loop(0, l // HY_ROWS, conv, 0)


def _hy_short_call(p, w, bias):
    b, l, c3 = p.shape
    c = HYENA_W
    return pl.pallas_call(
        _hy_short_kernel,
        grid=(b, c3 // c),
        in_specs=[pl.BlockSpec((1, l, c), lambda i, g: (i, 0, g)),
                  pl.BlockSpec((HY_SHORT, c), lambda i, g: (0, g)),
                  pl.BlockSpec((1, c), lambda i, g: (0, g))],
        out_specs=pl.BlockSpec((1, 1, l, c), lambda i, g: (g, i, 0, 0)),
        out_shape=jax.ShapeDtypeStruct((c3 // c, b, l, c), BF16),
        scratch_shapes=[pltpu.VMEM((l + 2 * HY_PAD, c), F32)],
        compiler_params=_cp("parallel", "parallel"),
        name="hyena_short",
    )(p, w, bias.reshape(1, c3))


def _hy_filter_kernel(z_ref, w1_ref, b1_ref, f1_ref, w2_ref, b2_ref, f2_ref, w3_ref, b3_ref, win_ref, o_ref):
    hp = lax.Precision.HIGHEST

    def mm(a, b):
        return jnp.dot(a, b, preferred_element_type=F32, precision=hp)

    h = jnp.sin(f1_ref[...] * (mm(z_ref[...], w1_ref[...]) + b1_ref[...]))
    h = jnp.sin(f2_ref[...] * (mm(h, w2_ref[...]) + b2_ref[...]))
    h = mm(h, w3_ref[...]) + b3_ref[...]
    c = HYENA_W
    win = win_ref[...]
    for s in range(h.shape[1] // c):
        o_ref[:, s * c:(s + 1) * c] = h[:, s * c:(s + 1) * c] * win


def _pad2(a, rows, cols):
    return jnp.pad(a, ((0, rows - a.shape[0]), (0, cols - a.shape[1])))


def _hy_filter_call(l, t, w1, b1, f1, w2, b2, f2, w3, b3):
    t_norm = t / max(l - 1, 1)
    ang = (2.0 * math.pi / l) * t[:, None] * jnp.linspace(1e-4, HY_BANDS - 1, HY_BANDS, dtype=F32)[None, :]
    z = jnp.concatenate([t_norm[:, None], jnp.cos(ang), -jnp.sin(ang)], axis=-1)
    deltas = jnp.abs(jnp.linspace(HY_MIN_DECAY, HY_MAX_DECAY, HYENA_W, dtype=F32))
    window = jnp.exp(-t_norm[:, None] * deltas[None, :])
    n3 = w3.shape[1]
    by_dir = lambda a: a.reshape(a.shape[0], HY_ORDER, 2, HYENA_W).transpose(0, 2, 1, 3).reshape(a.shape[0], n3)
    w3, b3 = by_dir(w3), by_dir(b3[None])[0]
    l = t.shape[0]
    tl = min(l, 256)
    full = lambda shape: pl.BlockSpec(shape, lambda i: (0, 0))
    return pl.pallas_call(
        _hy_filter_kernel,
        grid=(l // tl,),
        in_specs=[pl.BlockSpec((tl, LANES), lambda i: (i, 0)),
                  full((LANES, LANES)), full((1, LANES)), full((1, LANES)),
                  full((LANES, LANES)), full((1, LANES)), full((1, LANES)),
                  full((LANES, n3)), full((1, n3)),
                  pl.BlockSpec((tl, HYENA_W), lambda i: (i, 0))],
        out_specs=pl.BlockSpec((tl, n3), lambda i: (i, 0)),
        out_shape=jax.ShapeDtypeStruct((l, n3), F32),
        compiler_params=_cp("parallel"),
        name="hyena_filter",
    )(_pad2(z, l, LANES), _pad2(w1, LANES, LANES), _pad2(b1[None], 1, LANES), _pad2(f1[None], 1, LANES),
      _pad2(w2, LANES, LANES), _pad2(b2[None], 1, LANES), _pad2(f2[None], 1, LANES),
      _pad2(w3, LANES, n3), b3[None], window)


def _dft_tables_kernel(ac_ref, as_ref, bc_ref, bs_ref, fwd_ref, inv_ref):
    tm, l = fwd_ref.shape[1], fwd_ref.shape[2]
    n = 2 * l
    row = lax.broadcasted_iota(jnp.int32, (tm, LANES), 0) + pl.program_id(0) * tm
    lane = lax.broadcasted_iota(jnp.int32, (tm, LANES), 1)
    row_sign = jnp.where((row & 1) == 0, 1.0, -1.0)
    lane_sign = jnp.where((lane & 1) == 0, 1.0, -1.0)
    bc = bc_ref[...]
    bs = bs_ref[...]
    for a in range(l // LANES):
        cols = slice(a * LANES, (a + 1) * LANES)
        ca = ac_ref[:, a:a + 1]
        sa = as_ref[:, a:a + 1]
        c = ca * bc - sa * bs
        s = sa * bc + ca * bs
        fwd_ref[0, :, cols] = c.astype(BF16)
        fwd_ref[1, :, cols] = jnp.where(row == 0, lane_sign, -s).astype(BF16)
        inv_c = c * (2.0 / n)
        inv_s = -s * (2.0 / n)
        if a == 0:
            inv_c = jnp.where(lane == 0, 1.0 / n, inv_c)
            inv_s = jnp.where(lane == 0, row_sign / n, inv_s)
        inv_ref[:, cols] = inv_c.astype(BF16)
        inv_ref[:, l + a * LANES:l + (a + 1) * LANES] = inv_s.astype(BF16)


def _dft_mats(l):
    n = 2 * l
    f = jnp.arange(l, dtype=jnp.int32)[:, None]

    def table(t):
        ang = ((f * t[None, :]) % n).astype(F32) * (2.0 * math.pi / n)
        return jnp.cos(ang), jnp.sin(ang)

    coarse = table(jnp.arange(l // LANES, dtype=jnp.int32) * LANES)
    fine = table(jnp.arange(LANES, dtype=jnp.int32))
    tm = min(l, 256)
    small = pl.BlockSpec((tm, l // LANES), lambda i: (i, 0))
    wide = pl.BlockSpec((tm, LANES), lambda i: (i, 0))
    return pl.pallas_call(
        _dft_tables_kernel,
        grid=(l // tm,),
        in_specs=[small, small, wide, wide],
        out_specs=[pl.BlockSpec((2, tm, l), lambda i: (0, i, 0)), pl.BlockSpec((tm, n), lambda i: (i, 0))],
        out_shape=[jax.ShapeDtypeStruct((2, l, l), BF16), jax.ShapeDtypeStruct((l, n), BF16)],
        compiler_params=_cp("parallel"),
        name="dft_tables",
    )(*coarse, *fine)


def _conv_blocks(l):
    return 2 if l >= 1024 else 1


def _lag_window_plan(l):
    nb = _conv_blocks(l)
    lb = l // nb
    pos = jnp.arange(l, dtype=F32)
    if nb == 1:
        return pos, jnp.array([0, 0, 0, 1], jnp.int32)
    pos = jnp.concatenate([pos, lb - jnp.arange(lb, dtype=F32)])
    return pos, jnp.array([2, 0, 1, 1, 0, 0, 1, 0, 2, 1, 1, 0], jnp.int32)


def _spec_kernel(tab_ref, f_ref, a_ref, b_ref, o_ref):
    a = a_ref[...]
    row0 = lax.broadcasted_iota(jnp.int32, a.shape, 0) == 0
    b = jnp.where(row0, 0.0, b_ref[...])
    s = (a + b).astype(BF16)
    hi = _dot(f_ref[1], (a - b).astype(BF16))
    o_ref[0, 0] = _dot(f_ref[0], s)
    o_ref[0, 1] = hi

    @pl.when(pl.program_id(1) == 0)
    def _():
        nyq = _dot(f_ref[1, 0:16, :], s)
        row = lax.broadcasted_iota(jnp.int32, nyq.shape, 0)
        o_ref[0, 1, 0:16, :] = jnp.where(row == 0, nyq, hi[0:16])


def _spec_call(fwd, filt, table):
    _, lf, lb = fwd.shape
    nd = table.shape[0] // 4
    n = filt.shape[1] // 2
    tm = min(lf, 512)
    pick = lambda k: pl.BlockSpec((lb, n), lambda d, i, tab: (tab[2 * k * nd + d], tab[(2 * k + 1) * nd + d]))
    return pl.pallas_call(
        _spec_kernel,
        grid_spec=pltpu.PrefetchScalarGridSpec(
            num_scalar_prefetch=1,
            grid=(nd, lf // tm),
            in_specs=[pl.BlockSpec((2, tm, lb), lambda d, i, tab: (0, i, 0)), pick(0), pick(1)],
            out_specs=pl.BlockSpec((1, 2, tm, n), lambda d, i, tab: (d, 0, i, 0))),
        out_shape=jax.ShapeDtypeStruct((nd, 2, lf, n), F32),
        compiler_params=_cp("arbitrary", "arbitrary"),
        name="hyena_spectrum",
    )(table, fwd, filt, filt)


def _dft_fwd_kernel(f_ref, u_ref, h_ref, y_ref, *, nb):
    lb = f_ref.shape[2]
    xr, xi = [], []
    for p in range(nb):
        up = u_ref[0, 0, p * lb:(p + 1) * lb, :].astype(BF16)
        xr.append(_dot(f_ref[0], up))
        xi.append(_dot(f_ref[1], up))
    def products(rows, real_bins):
        for q in range(nb):
            yr = 0.0
            yi = 0.0
            for p in range(nb):
                gr = h_ref[q - p + nb - 1, 0, rows, :]
                gi = h_ref[q - p + nb - 1, 1, rows, :]
                ii = xi[p][rows] * gi
                cross = xr[p][rows] * gi + xi[p][rows] * gr
                yr = yr + xr[p][rows] * gr - (ii if real_bins is None else jnp.where(real_bins, 0.0, ii))
                yi = yi + (cross if real_bins is None else jnp.where(real_bins, ii, cross))
            y_ref[0, q, 0, rows, :] = yr.astype(BF16)
            y_ref[0, q, 1, rows, :] = yi.astype(BF16)

    products(slice(None), None)

    @pl.when(pl.program_id(0) == 0)
    def _():
        head = slice(0, 16)
        products(head, lax.broadcasted_iota(jnp.int32, (16, u_ref.shape[3]), 0) == 0)


def _dft_fwd_call(fwd, u, group, spec, order, nb):
    _, b, l, c = u.shape
    _, lf, lb = fwd.shape
    nd = spec.shape[0]
    tm = min(lf, 512)
    return pl.pallas_call(
        functools.partial(_dft_fwd_kernel, nb=nb),
        grid=(lf // tm, b),
        in_specs=[pl.BlockSpec((2, tm, lb), lambda i, j: (0, i, 0)),
                  pl.BlockSpec((1, 1, l, c), lambda i, j: (group, j, 0, 0)),
                  pl.BlockSpec((nd, 2, tm, c), lambda i, j: (0, 0, i, order))],
        out_specs=pl.BlockSpec((1, nb, 2, tm, c), lambda i, j: (j, 0, 0, i, 0)),
        out_shape=jax.ShapeDtypeStruct((b, nb, 2, lf, c), BF16),
        compiler_params=_cp("arbitrary", "arbitrary"),
        name="hyena_dft_fwd",
    )(fwd, u, spec)


def _dft_inv_kernel(g_ref, y_ref, u_ref, b_ref, x_ref, o_ref):
    r = _dot(g_ref[...], y_ref[0, 0]) + u_ref[0, 0].astype(F32) * b_ref[...]
    o_ref[0, 0] = (x_ref[0, 0] * r).astype(o_ref.dtype)


def _dft_inv_call(inv, y, u, u_group, bias, gate, gate_group):
    _, b, l, c = u.shape
    lb, n = inv.shape
    nb = l // lb
    tm = min(lb, 1024)
    nt = lb // tm
    y = y.reshape(b, nb, n, c)
    row = lambda g: pl.BlockSpec((1, 1, tm, c), lambda q, i, j: (g, j, q * nt + i, 0))
    return pl.pallas_call(
        _dft_inv_kernel,
        grid=(nb, nt, b),
        in_specs=[pl.BlockSpec((tm, n), lambda q, i, j: (i, 0)),
                  pl.BlockSpec((1, 1, n, c), lambda q, i, j: (j, q, 0, 0)),
                  row(u_group), pl.BlockSpec((1, c), lambda q, i, j: (0, 0)), row(gate_group)],
        out_specs=row(0),
        out_shape=jax.ShapeDtypeStruct((1, b, l, c), BF16),
        compiler_params=_cp("parallel", "parallel", "parallel"),
        name="hyena_dft_inv",
    )(inv, y, u, bias.reshape(1, c), gate)


def _hyena_call(p_hy, sh_w, sh_b, spec, hy_bias, fwd, inv, nb):
    u3 = _hy_short_call(p_hy, sh_w, sh_b)
    z = _dft_inv_call(inv, _dft_fwd_call(fwd, u3, 0, spec, 0, nb), u3, 0, hy_bias[0], u3, 1)
    return _dft_inv_call(inv, _dft_fwd_call(fwd, z, 0, spec, 1, nb), z, 0, hy_bias[1], u3, 2)[0]


def _mix_out_kernel(c_ref, a_ref, h_ref, w_ref, x_ref, g1_ref, g_ref, sh_ref, sc_ref,
                    wr_ref, rb_ref, o_ref, h32_ref, idx_ref, gate_ref):
    k1 = c_ref.shape[2]
    k2 = k1 + a_ref.shape[2]
    acc = _dot(c_ref[0], w_ref[0, :k1, :]) + _dot(a_ref[0], w_ref[0, k1:k2, :]) + _dot(h_ref[0], w_ref[0, k2:, :])
    x = x_ref[0] + g1_ref[0] * acc
    o_ref[0] = x
    hm = _rms(x, g_ref[...]) * (1.0 + sc_ref[0]) + sh_ref[0]
    h32_ref[0] = _pack_rows(hm)
    picks, gates = _route(hm, wr_ref[...], rb_ref[...])
    idx_ref[...] = jnp.concatenate(picks, axis=0)
    gate_ref[...] = jnp.concatenate(gates, axis=0)


def _mix_out_call(conv_o, att_o, hy_o, w_out, layer, x, mod, g2, wr, rbias):
    b, l, d = x.shape
    tm = min(l, 512)
    nt = l // tm
    act = lambda k: pl.BlockSpec((1, tm, k), lambda i, j: (i, j, 0))
    row = pl.BlockSpec((1, 1, d), lambda i, j: (i, 0, 0))
    tok = pl.BlockSpec((2, tm), lambda i, j: (0, i * nt + j))
    return pl.pallas_call(
        _mix_out_kernel,
        grid=(b, nt),
        in_specs=[act(conv_o.shape[2]), act(att_o.shape[2]), act(hy_o.shape[2]),
                  pl.BlockSpec((1,) + w_out.shape[1:], lambda i, j: (layer, 0, 0), pipeline_mode=pl.Buffered(1)),
                  act(d), row, pl.BlockSpec((1, d), lambda i, j: (0, 0)), row, row,
                  pl.BlockSpec(wr.shape, lambda i, j: (0, 0)),
                  pl.BlockSpec((N_EXPERTS, 1), lambda i, j: (0, 0))],
        out_specs=[act(d), act(d // 2), tok, tok],
        out_shape=[jax.ShapeDtypeStruct((b, l, d), F32),
                   jax.ShapeDtypeStruct((b, l, d // 2), jnp.int32),
                   jax.ShapeDtypeStruct((2, b * l), jnp.int32),
                   jax.ShapeDtypeStruct((2, b * l), F32)],
        compiler_params=_cp("parallel", "parallel"),
        name="mix_out",
    )(conv_o, att_o, hy_o, w_out, x, mod[2], g2.reshape(1, d), mod[3], mod[4], wr,
      rbias.reshape(N_EXPERTS, 1))


def _route(hm, wr, rbias):
    hi = hm.astype(BF16)
    lo = (hm - hi.astype(F32)).astype(BF16)
    nt = (((1,), (1,)), ((), ()))
    r = (lax.dot_general(wr, hi, nt, preferred_element_type=F32)
         + lax.dot_general(wr, lo, nt, preferred_element_type=F32))
    logits = r[:N_EXPERTS] + r[N_EXPERTS:]
    score = jax.nn.sigmoid(logits)
    sel = score + rbias
    rows = [sel[e:e + 1] for e in range(N_EXPERTS)]
    gscore = []
    for g in range(N_GROUPS):
        a, b, c, d = rows[4 * g:4 * g + 4]
        h1, l1, h2, l2 = jnp.maximum(a, b), jnp.minimum(a, b), jnp.maximum(c, d), jnp.minimum(c, d)
        gscore.append(jnp.maximum(h1, h2) + jnp.maximum(jnp.minimum(h1, h2), jnp.maximum(l1, l2)))
    best = gscore[0]
    gbest = jnp.zeros(best.shape, jnp.int32)
    for g in range(1, N_GROUPS):
        upd = gscore[g] > best
        gbest = jnp.where(upd, g, gbest)
        best = jnp.where(upd, gscore[g], best)
    masked = [jnp.where(gbest == e // EXPERTS_PER_GROUP, rows[e], NEG_INF) for e in range(N_EXPERTS)]
    picks = []
    for _ in range(2):
        val = jnp.full(best.shape, NEG_INF, F32)
        idx = jnp.zeros(best.shape, jnp.int32)
        for e in range(N_EXPERTS):
            cand = masked[e]
            if picks:
                cand = jnp.where(picks[0] == e, NEG_INF, cand)
            upd = cand > val
            idx = jnp.where(upd, e, idx)
            val = jnp.where(upd, cand, val)
        picks.append(idx)
    gates = [functools.reduce(jnp.add, [jnp.where(p == e, score[e:e + 1], 0.0) for e in range(N_EXPERTS)])
             for p in picks]
    den = gates[0] + gates[1]
    return picks, [gates[0] / den, gates[1] / den]


def _expert_changed(te_ref, i):
    return (i == 0) | (te_ref[i] != te_ref[jnp.maximum(i - 1, 0)])


def _pack_rows(x):
    bits = pltpu.bitcast(x.astype(BF16).astype(F32), jnp.uint32)
    half = bits.shape[1] // 2
    return pltpu.bitcast((bits[:, :half] >> 16) | (bits[:, half:] & jnp.uint32(0xFFFF0000)), jnp.int32)


def _unpack_rows(words):
    w = pltpu.bitcast(words, jnp.uint32)
    return pltpu.bitcast(w << 16, F32), pltpu.bitcast(w & jnp.uint32(0xFFFF0000), F32)


def _moe_up_kernel(te_ref, tn_ref, nu_ref, x_ref, wg_hbm, wu_hbm, h_ref, gbuf, ubuf, wgb_ref, wub_ref, sems,
                   *, layer):
    j, i = pl.program_id(0), pl.program_id(1)
    tf = gbuf.shape[1]

    def fetch(e, chunk):
        cols = pl.ds(pl.multiple_of(chunk * tf, tf), tf)
        return (pltpu.make_async_copy(wg_hbm.at[layer, e, :, cols], gbuf, sems.at[0]),
                pltpu.make_async_copy(wu_hbm.at[layer, e, :, cols], ubuf, sems.at[1]))

    def start(e, chunk):
        for copy in fetch(e, chunk):
            copy.start()

    @pl.when((i == 0) & (j == 0))
    def _():
        start(te_ref[0], 0)

    @pl.when(_expert_changed(te_ref, i))
    def _():
        for copy in fetch(te_ref[i], j):
            copy.wait()
        wgb_ref[...] = gbuf[...].astype(BF16)
        wub_ref[...] = ubuf[...].astype(BF16)
        nxt = tn_ref[i]
        more = nxt < nu_ref[0]

        @pl.when(more)
        def _():
            start(te_ref[nxt], j)

        @pl.when(jnp.logical_not(more) & (j + 1 < pl.num_programs(0)))
        def _():
            start(te_ref[0], j + 1)

    @pl.when(i < nu_ref[0])
    def _():
        xa, xb = (v.astype(BF16) for v in _unpack_rows(x_ref[...]))
        half = xa.shape[1]
        a = _dot(xa, wgb_ref[:half, :]) + _dot(xb, wgb_ref[half:, :])
        u = _dot(xa, wub_ref[:half, :]) + _dot(xb, wub_ref[half:, :])
        h_ref[...] = (a * jax.nn.sigmoid(a) * u).astype(BF16)

    @pl.when(i >= nu_ref[0])
    def _():
        h_ref[...] = jnp.zeros(h_ref.shape, BF16)


def _moe_up_call(xs, w_gate, w_up, layer, te, tn, nu, tile):
    p, dw = xs.shape
    _, _, d, f = w_gate.shape
    tf = 512
    hbm = pl.BlockSpec(memory_space=pl.ANY)
    return pl.pallas_call(
        functools.partial(_moe_up_kernel, layer=layer),
        grid_spec=pltpu.PrefetchScalarGridSpec(
            num_scalar_prefetch=3,
            grid=(f // tf, p // tile),
            in_specs=[pl.BlockSpec((tile, dw), lambda j, i, te, tn, nu: (i, 0)), hbm, hbm],
            out_specs=pl.BlockSpec((tile, tf), lambda j, i, te, tn, nu: (i, j)),
            scratch_shapes=[pltpu.VMEM((d, tf), F32), pltpu.VMEM((d, tf), F32),
                            pltpu.VMEM((d, tf), BF16), pltpu.VMEM((d, tf), BF16),
                            pltpu.SemaphoreType.DMA((2,))]),
        out_shape=jax.ShapeDtypeStruct((p, f), BF16),
        compiler_params=_cp("arbitrary", "arbitrary"),
        name="moe_up",
    )(te, tn, nu, xs, w_gate, w_up)


def _moe_down_kernel(te_ref, tn_ref, nu_ref, h_ref, wd_hbm, y_ref, wbuf, wdb_ref, sem, *, layer):
    i = pl.program_id(0)

    def fetch(e):
        return pltpu.make_async_copy(wd_hbm.at[layer, e], wbuf, sem)

    @pl.when(i == 0)
    def _():
        fetch(te_ref[0]).start()

    @pl.when(_expert_changed(te_ref, i))
    def _():
        fetch(te_ref[i]).wait()
        wdb_ref[...] = wbuf[...].astype(BF16)
        nxt = tn_ref[i]

        @pl.when(nxt < nu_ref[0])
        def _():
            fetch(te_ref[nxt]).start()

    @pl.when(i < nu_ref[0])
    def _():
        y_ref[...] = _pack_rows(_dot(h_ref[...], wdb_ref[...]))

    @pl.when(i >= nu_ref[0])
    def _():
        y_ref[...] = jnp.zeros(y_ref.shape, jnp.int32)


def _moe_down_call(h, w_down, layer, te, tn, nu, tile):
    p, f = h.shape
    d = w_down.shape[3]
    return pl.pallas_call(
        functools.partial(_moe_down_kernel, layer=layer),
        grid_spec=pltpu.PrefetchScalarGridSpec(
            num_scalar_prefetch=3,
            grid=(p // tile,),
            in_specs=[pl.BlockSpec((tile, f), lambda i, te, tn, nu: (i, 0)),
                      pl.BlockSpec(memory_space=pl.ANY)],
            out_specs=pl.BlockSpec((tile, d // 2), lambda i, te, tn, nu: (i, 0)),
            scratch_shapes=[pltpu.VMEM((f, d), F32), pltpu.VMEM((f, d), BF16), pltpu.SemaphoreType.DMA(())]),
        out_shape=jax.ShapeDtypeStruct((p, d // 2), jnp.int32),
        compiler_params=_cp("arbitrary"),
        name="moe_down",
    )(te, tn, nu, h, w_down)


PLAN_BLOCK = 512


def _plan_kernel(idx_ref, pos_ref, te_ref, tn_ref, nu_ref, rank_ref, *, tile):
    t = idx_ref.shape[1]
    nb = t // PLAN_BLOCK
    experts = lax.broadcasted_iota(jnp.int32, (N_EXPERTS, PLAN_BLOCK), 0)
    r = lax.broadcasted_iota(jnp.int32, (PLAN_BLOCK, PLAN_BLOCK), 0)
    c = lax.broadcasted_iota(jnp.int32, (PLAN_BLOCK, PLAN_BLOCK), 1)
    before = jnp.where(r < c, 1.0, 0.0).astype(BF16)

    def rank_pass(k):
        def body(j, carry):
            cols = pl.ds(pl.multiple_of(j * PLAN_BLOCK, PLAN_BLOCK), PLAN_BLOCK)
            onehot = jnp.where(experts == idx_ref[k:k + 1, cols], 1.0, 0.0)
            earlier = _dot(onehot.astype(BF16), before) + carry
            rank_ref[k:k + 1, cols] = jnp.sum(onehot * earlier, axis=0, keepdims=True)
            return carry + jnp.sum(onehot, axis=1, keepdims=True)
        return body

    counts = jnp.zeros((N_EXPERTS, 1), F32)
    for k in range(2):
        counts = lax.fori_loop(0, nb, rank_pass(k), counts)
    padded = jnp.ceil(counts * (1.0 / tile)) * tile
    starts, ends = [], []
    acc = jnp.zeros((1, 1), F32)
    for e in range(N_EXPERTS):
        starts.append(acc)
        acc = acc + padded[e:e + 1]
        ends.append(acc)

    def place_pass(k):
        def body(j, carry):
            cols = pl.ds(pl.multiple_of(j * PLAN_BLOCK, PLAN_BLOCK), PLAN_BLOCK)
            row = idx_ref[k:k + 1, cols]
            start = functools.reduce(jnp.add, [jnp.where(row == e, starts[e], 0.0) for e in range(N_EXPERTS)])
            pos_ref[k:k + 1, cols] = (rank_ref[k:k + 1, cols] + start).astype(jnp.int32)
            return carry
        return body

    for k in range(2):
        lax.fori_loop(0, nb, place_pass(k), 0)
    n_used = acc * (1.0 / tile)
    tiles = lax.broadcasted_iota(jnp.int32, te_ref.shape, 1).astype(F32)
    first_row = jnp.minimum(tiles, n_used - 1.0) * tile
    te = functools.reduce(jnp.add, [jnp.where(ends[e] <= first_row, 1.0, 0.0) for e in range(N_EXPERTS)])
    te_ref[...] = te.astype(jnp.int32)
    seg_end = functools.reduce(jnp.add, [jnp.where(te == e, ends[e], 0.0) for e in range(N_EXPERTS)])
    tn_ref[...] = (seg_end * (1.0 / tile)).astype(jnp.int32)
    nu_ref[...] = jnp.broadcast_to(n_used, nu_ref.shape).astype(jnp.int32)


def _plan_call(idx, tile, n_tiles):
    t = idx.shape[1]
    whole = lambda shape: pl.BlockSpec(shape, lambda: (0,) * len(shape))
    lanes = jax.ShapeDtypeStruct((1, LANES), jnp.int32)
    pos, te, tn, nu = pl.pallas_call(
        functools.partial(_plan_kernel, tile=tile),
        in_specs=[whole(idx.shape)],
        out_specs=[whole(idx.shape), whole((1, LANES)), whole((1, LANES)), whole((1, LANES))],
        out_shape=[jax.ShapeDtypeStruct(idx.shape, jnp.int32), lanes, lanes, lanes],
        scratch_shapes=[pltpu.VMEM(idx.shape, F32)],
        name="route_plan",
    )(idx)
    return pos.reshape(2 * t), te[0, :n_tiles], tn[0, :n_tiles], nu[0, :1]


V7X_SC_CORES = 2
V7X_SC_SUBCORES = 16
SC_WORKERS = V7X_SC_CORES * V7X_SC_SUBCORES
SC_ROW_BYTES = 256 * 1024


def _sc_worker():
    return lax.axis_index("s") * V7X_SC_CORES + lax.axis_index("c")


def _sc_dispatch_call(streams, pos, n_rows):
    w = streams[0].shape[1]
    sizes = [s.shape[0] for s in streams]
    t = sum(sizes)
    per_w = t // SC_WORKERS
    ch = min(per_w, SC_ROW_BYTES // (4 * w))
    assert per_w % ch == 0 and all(size % ch == 0 for size in sizes)
    mesh = plsc.VectorSubcoreMesh(core_axis_name="c", subcore_axis_name="s")

    @functools.partial(pl.kernel, mesh=mesh, out_type=jax.ShapeDtypeStruct((n_rows, w), jnp.int32),
                       scratch_types=[pltpu.VMEM((ch,), jnp.int32), pltpu.VMEM((ch,), jnp.int32),
                                      pltpu.VMEM((ch, w), jnp.int32),
                                      pltpu.SemaphoreType.DMA, pltpu.SemaphoreType.DMA],
                       name="sc_dispatch")
    def run(*refs):
        srcs = refs[:len(sizes)]
        pos_hbm, xs_hbm, idx0_v, idx1_v, rows_v, sem0, sem1 = refs[len(sizes):]
        first = _sc_worker() * per_w

        @pl.loop(0, per_w // ch)
        def _(j):
            base = first + j * ch
            start = 0
            for src, size in zip(srcs, sizes):
                def load(src=src, start=start):
                    pltpu.sync_copy(src.at[pl.ds(base - start, ch)], rows_v)
                if len(sizes) == 1:
                    load()
                else:
                    pl.when((base >= start) & (base < start + size))(load)
                start += size
            pltpu.sync_copy(pos_hbm.at[pl.ds(base, ch)], idx0_v)
            pltpu.sync_copy(pos_hbm.at[pl.ds(t + base, ch)], idx1_v)
            first_choice = pltpu.async_copy(rows_v, xs_hbm.at[idx0_v], sem0)
            second_choice = pltpu.async_copy(rows_v, xs_hbm.at[idx1_v], sem1)
            first_choice.wait()
            second_choice.wait()

    return run(*streams, pos)


def _sc_combine_call(ys, pos):
    _, d = ys.shape
    n = pos.shape[0]
    per_w = n // SC_WORKERS
    ch = min(per_w // 2, SC_ROW_BYTES // (8 * d))
    assert per_w % (2 * ch) == 0
    mesh = plsc.VectorSubcoreMesh(core_axis_name="c", subcore_axis_name="s")
    idx_buf = pltpu.VMEM((ch,), jnp.int32)
    row_buf = pltpu.VMEM((ch, d), jnp.int32)

    @functools.partial(pl.kernel, mesh=mesh, out_type=jax.ShapeDtypeStruct((n, d), jnp.int32),
                       scratch_types=[idx_buf, idx_buf, row_buf, row_buf] + [pltpu.SemaphoreType.DMA] * 4,
                       name="sc_combine")
    def run(ys_hbm, pos_hbm, out_hbm, idx_a, idx_b, rows_a, rows_b, gsem_a, gsem_b, wsem_a, wsem_b):
        first = _sc_worker() * per_w

        @pl.loop(0, per_w // (2 * ch))
        def _(j):
            base = first + j * (2 * ch)
            pltpu.sync_copy(pos_hbm.at[pl.ds(base, ch)], idx_a)
            pltpu.sync_copy(pos_hbm.at[pl.ds(base + ch, ch)], idx_b)
            gather_a = pltpu.async_copy(ys_hbm.at[idx_a], rows_a, gsem_a)
            gather_b = pltpu.async_copy(ys_hbm.at[idx_b], rows_b, gsem_b)
            gather_a.wait()
            write_a = pltpu.async_copy(rows_a, out_hbm.at[pl.ds(base, ch)], wsem_a)
            gather_b.wait()
            write_b = pltpu.async_copy(rows_b, out_hbm.at[pl.ds(base + ch, ch)], wsem_b)
            write_a.wait()
            write_b.wait()

    return run(ys, pos)


def _moe_call(streams, idx, w_gate, w_up, w_down, layer):
    t = idx.shape[1]
    tile = min(MOE_TILE, max(t // N_EXPERTS, 128))
    n_tiles = (2 * t) // tile + N_EXPERTS
    assert n_tiles <= LANES
    pos, te, tn, nu = _plan_call(idx, tile, n_tiles)
    xs = _sc_dispatch_call(streams, pos, n_tiles * tile)
    hid = _moe_up_call(xs, w_gate, w_up, layer, te, tn, nu, tile)
    ys = _moe_down_call(hid, w_down, layer, te, tn, nu, tile)
    return _sc_combine_call(ys, pos).reshape(2, t, ys.shape[1])


def _moe_resid_kernel(x_ref, y0_ref, y1_ref, gt_ref, g_ref, *rest, final):
    gt = gt_ref[0]
    halves0 = _unpack_rows(y0_ref[0])
    halves1 = _unpack_rows(y1_ref[0])
    y = jnp.concatenate([gt[:, 0:1] * a + gt[:, 1:2] * b for a, b in zip(halves0, halves1)], axis=1)
    x = x_ref[0] + g_ref[0] * y
    if final:
        fg_ref, o_ref = rest
        x = _rms(x, fg_ref[...])
    else:
        (o_ref,) = rest
    o_ref[0] = x


def _moe_resid_call(x, y, first_row, gates, gate, final_g=None):
    b, l, d = x.shape
    tl = min(l, 512)
    nt = l // tl
    tile = pl.BlockSpec((1, tl, d), lambda i, j: (i, j, 0))
    choice = lambda k: pl.BlockSpec((1, tl, d // 2), lambda i, j: (k, first_row // tl + i * nt + j, 0))
    in_specs = [tile, choice(0), choice(1), pl.BlockSpec((1, tl, 2), lambda i, j: (i, j, 0)),
                pl.BlockSpec((1, 1, d), lambda i, j: (i, 0, 0))]
    args = [x, y, y, gates.T.reshape(b, l, 2), gate]
    if final_g is not None:
        in_specs.append(pl.BlockSpec((1, d), lambda i, j: (0, 0)))
        args.append(final_g.reshape(1, d))
    return pl.pallas_call(
        functools.partial(_moe_resid_kernel, final=final_g is not None),
        grid=(b, l // tl),
        in_specs=in_specs,
        out_specs=tile,
        out_shape=jax.ShapeDtypeStruct((b, l, d), F32),
        compiler_params=_cp("parallel", "parallel"),
        name="moe_residual",
    )(*args)


def _rope_perm(w):
    r = w.reshape(w.shape[:-1] + (2, 2, ROPE_FREQS))
    return jnp.stack([-r[..., 1, :], r[..., 0, :]], axis=-2).reshape(w.shape)


def _rope_table(l, rotary):
    if not rotary:
        return jnp.concatenate([jnp.ones((l, MLA_ROPE), F32), jnp.zeros((l, MLA_ROPE), F32)], axis=1)
    rows = l // GRID_W
    r = jnp.repeat(jnp.arange(rows), GRID_W)
    col = jnp.tile(jnp.arange(GRID_W), rows)
    pos = jnp.stack([r, col], axis=-1).astype(F32)
    inv = ROPE_THETA ** (-jnp.arange(ROPE_FREQS, dtype=F32) / ROPE_FREQS)
    ang = pos[:, :, None] * inv
    lay = lambda a: jnp.concatenate([a[:, 0], a[:, 0], a[:, 1], a[:, 1]], axis=1)
    return jnp.concatenate([lay(jnp.cos(ang)), lay(jnp.sin(ang))], axis=1)


def _prep_weights(w_in, w_uq, w_ukv, w_out):
    depth = w_in.shape[0]
    cast = lambda a: a.astype(BF16)
    kr = cast(w_in[:, :, OFF_KR:OFF_HY])
    w_kv_in = jnp.concatenate([cast(w_in[:, :, OFF_KV:OFF_KR]), kr, _rope_perm(kr)], axis=2)
    w_cq, w_hy = cast(w_in[:, :, :OFF_KV]), cast(w_in[:, :, OFF_HY:])
    uq = cast(w_uq).reshape(depth, MLA_Q_RANK, MLA_HEADS, MLA_NOPE + MLA_ROPE)
    w_q = jnp.concatenate([uq, _rope_perm(uq[..., MLA_NOPE:])], axis=3).reshape(depth, MLA_Q_RANK, -1)
    ukv = cast(w_ukv).reshape(depth, MLA_KV_RANK, MLA_HEADS, MLA_NOPE + MLA_V_DIM)
    w_k = ukv[..., :MLA_NOPE].reshape(depth, MLA_KV_RANK, MLA_HEADS * MLA_NOPE)
    w_v = ukv[..., MLA_NOPE:].reshape(depth, MLA_KV_RANK, MLA_HEADS * MLA_V_DIM)
    return dict(cq=w_cq, hy=w_hy, kv=w_kv_in, uq=w_q, uk=w_k, uv=w_v, out=cast(w_out))


def _router_weights(w_router):
    hi = w_router.astype(BF16)
    lo = (w_router - hi.astype(F32)).astype(BF16)
    return jnp.concatenate([hi.T, lo.T], axis=0)


def _mixer(x, mod, w, layer, lp, cs, spec, fwd, inv, kv_extra, full, wr, rbias):
    kv_args = (w["kv"], lp["kv_norm_g"], w["uk"], w["uv"], cs, layer)
    if not full:
        k, v = _in_proj_call(x, lp["norm1_g"], mod[0], mod[1], [], *kv_args)
        return None, k, v
    proj = [(w["cq"], [OFF_Q, MLA_Q_RANK]), (w["hy"], [(HY_ORDER + 1) * HYENA_W])]
    p_conv, p_q, p_hy, k, v = _in_proj_call(x, lp["norm1_g"], mod[0], mod[1], proj, *kv_args)
    conv_o = _conformer_call(p_conv, lp["conv_dw_w"], lp["conv_dw_b"], lp["conv_ln_g"], lp["conv_ln_b"])
    att_o = _attn_call(p_q, lp["q_norm_g"], w["uq"], cs, kv_extra + [(k, v)], layer)
    hy_o = _hyena_call(p_hy, lp["hy_short_w"], lp["hy_short_b"], spec, lp["hy_bias"], fwd, inv,
                       _conv_blocks(x.shape[1]))
    routed = _mix_out_call(conv_o, att_o, hy_o, w["out"], layer, x, mod, lp["norm2_g"], wr, rbias)
    return routed, k, v


def _channel(streams, experts, layer, final_g=None):
    d = streams[0][1][0].shape[2]
    idx = jnp.concatenate([r[2] for _, r in streams], axis=1)
    y = _moe_call([r[1].reshape(-1, d // 2) for _, r in streams], idx, *experts, layer)
    outs = []
    first_row = 0
    for mod, (x, _, _, gates) in streams:
        outs.append(_moe_resid_call(x, y, first_row, gates, mod[5], final_g))
        first_row += x.shape[0] * x.shape[1]
    return outs


def _filter_spectrum(l, lp, fwd):
    pos, table = _lag_window_plan(l)
    filt = _hy_filter_call(l, pos, lp["hy_w1"], lp["hy_b1"], lp["hy_freq1"], lp["hy_w2"], lp["hy_b2"],
                           lp["hy_freq2"], lp["hy_w3"], lp["hy_b3"])
    return _spec_call(fwd, filt, table)


def kernel(x, c, ctx, c_ctx, norm1_g, norm2_g, w_ada, b_ada, w_in, conv_dw_w, conv_dw_b, conv_ln_g, conv_ln_b, q_norm_g, w_uq, kv_norm_g, w_ukv, hy_short_w, hy_short_b, hy_w1, hy_b1, hy_freq1, hy_w2, hy_b2, hy_freq2, hy_w3, hy_b3, hy_bias, w_out, w_router, router_bias, w_gate, w_up, w_down, final_norm_g):
    b, l, d = x.shape
    lc = ctx.shape[1]
    depth = w_in.shape[0]
    per_layer = dict(norm1_g=norm1_g, norm2_g=norm2_g, conv_dw_w=conv_dw_w, conv_dw_b=conv_dw_b,
                     conv_ln_g=conv_ln_g, conv_ln_b=conv_ln_b, q_norm_g=q_norm_g, kv_norm_g=kv_norm_g,
                     hy_short_w=hy_short_w, hy_short_b=hy_short_b, hy_w1=hy_w1, hy_b1=hy_b1, hy_freq1=hy_freq1,
                     hy_w2=hy_w2, hy_b2=hy_b2, hy_freq2=hy_freq2, hy_w3=hy_w3, hy_b3=hy_b3, hy_bias=hy_bias)
    experts = (w_gate, w_up, w_down)
    cs_l = _rope_table(l, True)
    cs_c = _rope_table(lc, False)
    fwd_l, inv_l = _dft_mats(l // _conv_blocks(l))
    fwd_c, inv_c = _dft_mats(lc // _conv_blocks(lc))
    wr = _router_weights(w_router)
    w = _prep_weights(w_in, w_uq, w_ukv, w_out)
    rows = 16
    cvec = jnp.concatenate([c, c_ctx[None, :], jnp.zeros((rows - b - 1, d), F32)], axis=0)
    xl, xc = x, ctx
    for i in range(depth):
        last = i == depth - 1
        lp = {k: v[i] for k, v in per_layer.items()}
        m = _ada_call(cvec, w_ada, b_ada, i)
        mod_l = [m[:b, None, j * d:(j + 1) * d] for j in range(6)]
        mod_c = [jnp.broadcast_to(m[b:b + 1, None, j * d:(j + 1) * d], (b, 1, d)) for j in range(6)]
        spec_l = _filter_spectrum(l, lp, fwd_l)
        spec_c = None if last else _filter_spectrum(lc, lp, fwd_c)
        routed_c, kc, vc = _mixer(xc, mod_c, w, i, lp, cs_c, spec_c, fwd_c, inv_c, [], not last, wr, router_bias)
        routed_l, _, _ = _mixer(xl, mod_l, w, i, lp, cs_l, spec_l, fwd_l, inv_l, [(kc, vc)], True, wr, router_bias)
        if last:
            (xl,) = _channel([(mod_l, routed_l)], experts, i, final_norm_g)
        else:
            xl, xc = _channel([(mod_l, routed_l), (mod_c, routed_c)], experts, i)
    return xl
```

```python
import functools
import math

import jax
import jax.numpy as jnp
from jax import lax
from jax.experimental import pallas as pl
from jax.experimental.pallas import tpu as pltpu
from jax.experimental.pallas import tpu_sc as plsc

F32 = jnp.float32
BF16 = jnp.bfloat16

EPS = 1e-6
GRID_W = 64
CONV_W = 512
CONV_KSIZE = 31
MLA_HEADS = 8
MLA_NOPE = 128
MLA_ROPE = 64
MLA_V_DIM = 128
MLA_Q_RANK = 768
MLA_KV_RANK = 512
ROPE_FREQS = MLA_ROPE // 4
ROPE_THETA = 10000.0
MLA_SCALE = (MLA_NOPE + MLA_ROPE) ** -0.5
Q_SCALE = MLA_SCALE * math.log2(math.e)
HYENA_W = 512
HY_ORDER = 2
HY_SHORT = 3
HY_EMB = 33
HY_BANDS = (HY_EMB - 1) // 2
HY_FFN = 64
HY_MIN_DECAY = math.log(1e-2) / 1.5
HY_MAX_DECAY = math.log(1e-2) / 0.3
N_EXPERTS = 16
N_GROUPS = 4
EXPERTS_PER_GROUP = N_EXPERTS // N_GROUPS
OFF_Q = 2 * CONV_W
OFF_KV = OFF_Q + MLA_Q_RANK
OFF_KR = OFF_KV + MLA_KV_RANK
OFF_HY = OFF_KR + MLA_ROPE

LANES = 128
QK_HEAD = 2 * LANES
V_HEAD = MLA_V_DIM + LANES
V7X_VMEM_BYTES = 64 * 1024 * 1024
VMEM_LIMIT = (V7X_VMEM_BYTES * 7) // 8
MOE_TILE = 512
NEG_INF = float("-inf")


def _cp(*sem):
    return pltpu.CompilerParams(dimension_semantics=sem, vmem_limit_bytes=VMEM_LIMIT)


def _dot(a, b):
    return jnp.dot(a, b, preferred_element_type=F32)


def _rms(x, g):
    ms = jnp.mean(x * x, axis=-1, keepdims=True)
    return x * lax.rsqrt(ms + EPS) * g


def _ada_kernel(c_ref, w_ref, b_ref, o_ref):
    c = c_ref[...]
    a = (c * jax.nn.sigmoid(c)).astype(BF16)
    o_ref[...] = _dot(a, w_ref[0].astype(BF16)) + b_ref[0]


def _ada_call(cvec, w, b, layer):
    m, d = cvec.shape
    n = w.shape[2]
    tn = 1024
    return pl.pallas_call(
        _ada_kernel,
        grid=(n // tn,),
        in_specs=[pl.BlockSpec((m, d), lambda j: (0, 0)),
                  pl.BlockSpec((1, d, tn), lambda j: (layer, 0, j)),
                  pl.BlockSpec((1, 1, tn), lambda j: (layer, 0, j))],
        out_specs=pl.BlockSpec((m, tn), lambda j: (0, j)),
        out_shape=jax.ShapeDtypeStruct((m, n), F32),
        compiler_params=_cp("parallel"),
        name="ada",
    )(cvec, w, b.reshape(b.shape[0], 1, n))


def _in_proj_kernel(x_ref, g_ref, sh_ref, sc_ref, *refs, n_proj):
    w_refs, (wkv_ref, kg_ref, wk_ref, wv_ref, cs_ref), o_refs = refs[:n_proj], refs[n_proj:n_proj + 5], refs[n_proj + 5:]
    *p_refs, k_ref, v_ref = o_refs
    y = _rms(x_ref[0], g_ref[...])
    h = (y * (1.0 + sc_ref[0]) + sh_ref[0]).astype(BF16)
    outs = iter(p_refs)
    for w_ref in w_refs:
        off = 0
        while off < w_ref.shape[2]:
            o_ref = next(outs)
            n = o_ref.shape[-1]
            o_ref[0] = _dot(h, w_ref[0, :, off:off + n]).astype(o_ref.dtype)
            off += n
    p = _dot(h, wkv_ref[0])
    xn = _rms(p[:, :MLA_KV_RANK], kg_ref[...]).astype(BF16)
    kk = _dot(xn, wk_ref[0])
    vv = _dot(xn, wv_ref[0])
    rot = _rotate(p[:, MLA_KV_RANK:], cs_ref[...])
    lane = lax.broadcasted_iota(jnp.int32, rot.shape, 1)
    rot = jnp.where(lane < MLA_ROPE, rot, 0.0).astype(BF16)
    ones = jnp.where(lane == 0, 1.0, 0.0).astype(BF16)
    for hd in range(MLA_HEADS):
        k_ref[0, :, hd * QK_HEAD:hd * QK_HEAD + LANES] = kk[:, hd * LANES:(hd + 1) * LANES].astype(BF16)
        k_ref[0, :, hd * QK_HEAD + LANES:(hd + 1) * QK_HEAD] = rot
        v_ref[0, :, hd * V_HEAD:hd * V_HEAD + MLA_V_DIM] = vv[:, hd * MLA_V_DIM:(hd + 1) * MLA_V_DIM].astype(BF16)
        v_ref[0, :, hd * V_HEAD + MLA_V_DIM:(hd + 1) * V_HEAD] = ones


def _in_proj_call(x, g, shift, scale, proj, w_kv, kv_g, w_k, w_v, cs, layer):
    b, l, d = x.shape
    tm = min(l, 512)
    r = MLA_KV_RANK
    row = pl.BlockSpec((1, 1, d), lambda i, j: (i, 0, 0))
    const = lambda shape: pl.BlockSpec(shape, lambda i, j: (0, 0))
    of_layer = lambda a, **kw: pl.BlockSpec((1,) + a.shape[1:], lambda i, j: (layer, 0, 0), **kw)
    weights = [w for w, _ in proj]
    widths = [k for _, ks in proj for k in ks] + [MLA_HEADS * QK_HEAD, MLA_HEADS * V_HEAD]
    return pl.pallas_call(
        functools.partial(_in_proj_kernel, n_proj=len(weights)),
        grid=(b, l // tm),
        in_specs=[pl.BlockSpec((1, tm, d), lambda i, j: (i, j, 0)),
                  const((1, d)), row, row,
                  *[of_layer(w, pipeline_mode=pl.Buffered(1)) for w in weights],
                  of_layer(w_kv, pipeline_mode=pl.Buffered(1)),
                  const((1, r)), of_layer(w_k), of_layer(w_v),
                  pl.BlockSpec((tm, LANES), lambda i, j: (j, 0))],
        out_specs=[pl.BlockSpec((1, tm, k), lambda i, j: (i, j, 0)) for k in widths],
        out_shape=[jax.ShapeDtypeStruct((b, l, k), BF16) for k in widths],
        compiler_params=_cp("parallel", "parallel"),
        name="in_proj",
    )(x, g.reshape(1, d), shift, scale, *weights, w_kv, kv_g.reshape(1, r), w_k, w_v, cs)


CONV_ROWS = 128
CONV_PAD = 16


def _conformer_kernel(p_ref, w_ref, b_ref, g_ref, be_ref, o_ref, zp_ref):
    l = p_ref.shape[1]
    c = CONV_W
    zeros = jnp.zeros((CONV_PAD, c), F32)
    zp_ref[0:CONV_PAD, :] = zeros
    zp_ref[CONV_PAD + l:CONV_PAD + l + CONV_PAD, :] = zeros

    def glu(r, carry):
        base = pl.multiple_of(r * CONV_ROWS, CONV_ROWS)
        blk = p_ref[0, pl.ds(base, CONV_ROWS), :].astype(F32)
        zp_ref[pl.ds(base + CONV_PAD, CONV_ROWS), :] = blk[:, :c] * jax.nn.sigmoid(blk[:, c:])
        return carry

    lax.fori_loop(0, l // CONV_ROWS, glu, 0)

    win_rows = CONV_ROWS + 2 * CONV_PAD
    first = CONV_PAD - CONV_KSIZE // 2

    def conv(r, carry):
        base = pl.multiple_of(r * CONV_ROWS, CONV_ROWS)
        parts = []
        for cc in range(c // LANES):
            lanes = slice(cc * LANES, (cc + 1) * LANES)
            win = zp_ref[pl.ds(base, win_rows), lanes]
            acc = jnp.zeros((CONV_ROWS, LANES), F32)
            for res in range(8):
                taps = [k for k in range(CONV_KSIZE) if (k + first) % 8 == res]
                if not taps:
                    continue
                shifted = pltpu.roll(win, win_rows - res, axis=0) if res else win
                for k in taps:
                    off = k + first - res
                    acc = acc + shifted[off:off + CONV_ROWS, :] * w_ref[k:k + 1, lanes]
            parts.append(acc)
        z = jnp.concatenate(parts, axis=1) + b_ref[...]
        mu = jnp.mean(z, axis=-1, keepdims=True)
        zc = z - mu
        var = jnp.mean(zc * zc, axis=-1, keepdims=True)
        y = zc * lax.rsqrt(var + EPS) * g_ref[...] + be_ref[...]
        o_ref[0, pl.ds(base, CONV_ROWS), :] = (y * jax.nn.sigmoid(y)).astype(o_ref.dtype)
        return carry

    lax.fori_loop(0, l // CONV_ROWS, conv, 0)


def _conformer_call(p, dw_w, dw_b, ln_g, ln_b):
    b, l, c2 = p.shape
    c = c2 // 2
    vec = pl.BlockSpec((1, c), lambda i: (0, 0))
    return pl.pallas_call(
        _conformer_kernel,
        grid=(b,),
        in_specs=[pl.BlockSpec((1, l, c2), lambda i: (i, 0, 0)),
                  pl.BlockSpec((CONV_KSIZE, c), lambda i: (0, 0)), vec, vec, vec],
        out_specs=pl.BlockSpec((1, l, c), lambda i: (i, 0, 0)),
        out_shape=jax.ShapeDtypeStruct((b, l, c), BF16),
        scratch_shapes=[pltpu.VMEM((l + 2 * CONV_PAD, c), F32)],
        compiler_params=_cp("parallel"),
        name="conformer",
    )(p, dw_w, dw_b.reshape(1, c), ln_g.reshape(1, c), ln_b.reshape(1, c))


def _rotate(group, cs):
    t = group * cs
    return t + pltpu.roll(t, MLA_ROPE, axis=1)


def _attn_kernel(p_ref, g_ref, w_ref, cs_ref, *refs, n_seg):
    o_ref, q_ref = refs[2 * n_seg], refs[2 * n_seg + 1]
    xn = _rms(p_ref[0].astype(F32), g_ref[...]).astype(BF16)
    cs = cs_ref[...] * Q_SCALE
    for h in range(MLA_HEADS):
        r = _dot(xn, w_ref[0, :, h * QK_HEAD:(h + 1) * QK_HEAD])
        q_ref[:, h * QK_HEAD:h * QK_HEAD + LANES] = (r[:, :LANES] * Q_SCALE).astype(BF16)
        q_ref[:, h * QK_HEAD + LANES:(h + 1) * QK_HEAD] = _rotate(r[:, LANES:], cs).astype(BF16)
    nt = (((1,), (1,)), ((), ()))
    for h in range(MLA_HEADS):
        qk = slice(h * QK_HEAD, (h + 1) * QK_HEAD)
        vd = slice(h * MLA_V_DIM, (h + 1) * MLA_V_DIM)
        q = q_ref[:, qk]
        scores = [lax.dot_general(q, refs[2 * s][0, :, qk], nt, preferred_element_type=F32).astype(BF16)
                  for s in range(n_seg)]
        m = functools.reduce(jnp.maximum, [jnp.max(s, axis=-1, keepdims=True) for s in scores])
        acc = 0.0
        for s in range(n_seg):
            p = jnp.exp2(scores[s] - m)
            acc = acc + _dot(p, refs[2 * s + 1][0, :, h * V_HEAD:(h + 1) * V_HEAD])
        o_ref[0, :, vd] = (acc[:, :MLA_V_DIM] * (1.0 / acc[:, MLA_V_DIM:MLA_V_DIM + 1])).astype(o_ref.dtype)


def _attn_call(p_q, q_g, w_q, cs, segs, layer):
    b, l, r = p_q.shape
    tq = min(l, 512)
    const = lambda shape: pl.BlockSpec(shape, lambda i, j: (0, 0))
    in_specs = [pl.BlockSpec((1, tq, r), lambda i, j: (i, j, 0)), const((1, r)),
                pl.BlockSpec((1,) + w_q.shape[1:], lambda i, j: (layer, 0, 0)),
                pl.BlockSpec((tq, LANES), lambda i, j: (j, 0))]
    args = [p_q, q_g.reshape(1, r), w_q, cs]
    for k, v in segs:
        in_specs.append(pl.BlockSpec((1,) + k.shape[1:], lambda i, j: (i, 0, 0)))
        in_specs.append(pl.BlockSpec((1,) + v.shape[1:], lambda i, j: (i, 0, 0)))
        args += [k, v]
    return pl.pallas_call(
        functools.partial(_attn_kernel, n_seg=len(segs)),
        grid=(b, l // tq),
        in_specs=in_specs,
        out_specs=pl.BlockSpec((1, tq, MLA_HEADS * MLA_V_DIM), lambda i, j: (i, j, 0)),
        out_shape=jax.ShapeDtypeStruct((b, l, MLA_HEADS * MLA_V_DIM), BF16),
        scratch_shapes=[pltpu.VMEM((tq, MLA_HEADS * QK_HEAD), BF16)],
        compiler_params=_cp("parallel", "parallel"),
        name="attention",
    )(*args)


HY_PAD = 8
HY_ROWS = 128


def _hy_short_kernel(p_ref, w_ref, b_ref, o_ref, up_ref):
    l = p_ref.shape[1]
    c = p_ref.shape[2]
    zeros = jnp.zeros((HY_PAD, c), F32)
    up_ref[0:HY_PAD, :] = zeros
    up_ref[HY_PAD + l:HY_PAD + l + HY_PAD, :] = zeros

    def copy(r, carry):
        base = pl.multiple_of(r * HY_ROWS, HY_ROWS)
        up_ref[pl.ds(base + HY_PAD, HY_ROWS), :] = p_ref[0, pl.ds(base, HY_ROWS), :].astype(F32)
        return carry

    lax.fori_loop(0, l // HY_ROWS, copy, 0)
    first = HY_PAD - HY_SHORT // 2

    def conv(r, carry):
        base = pl.multiple_of(r * HY_ROWS, HY_ROWS)
        for cc in range(c // LANES):
            lanes = slice(cc * LANES, (cc + 1) * LANES)
            win = up_ref[pl.ds(base, HY_ROWS + 2 * HY_PAD), lanes]
            acc = jnp.zeros((HY_ROWS, LANES), F32) + b_ref[:, lanes]
            for k in range(HY_SHORT):
                acc = acc + win[first + k:first + k + HY_ROWS, :] * w_ref[k:k + 1, lanes]
            o_ref[0, 0, pl.ds(base, HY_ROWS), lanes] = acc.astype(o_ref.dtype)
        return carry

    lax.fori_loop(0, l // HY_ROWS, conv, 0)


def _hy_short_call(p, w, bias):
    b, l, c3 = p.shape
    c = HYENA_W
    return pl.pallas_call(
        _hy_short_kernel,
        grid=(b, c3 // c),
        in_specs=[pl.BlockSpec((1, l, c), lambda i, g: (i, 0, g)),
                  pl.BlockSpec((HY_SHORT, c), lambda i, g: (0, g)),
                  pl.BlockSpec((1, c), lambda i, g: (0, g))],
        out_specs=pl.BlockSpec((1, 1, l, c), lambda i, g: (g, i, 0, 0)),
        out_shape=jax.ShapeDtypeStruct((c3 // c, b, l, c), BF16),
        scratch_shapes=[pltpu.VMEM((l + 2 * HY_PAD, c), F32)],
        compiler_params=_cp("parallel", "parallel"),
        name="hyena_short",
    )(p, w, bias.reshape(1, c3))


def _hy_filter_kernel(z_ref, w1_ref, b1_ref, f1_ref, w2_ref, b2_ref, f2_ref, w3_ref, b3_ref, win_ref, o_ref):
    hp = lax.Precision.HIGHEST

    def mm(a, b):
        return jnp.dot(a, b, preferred_element_type=F32, precision=hp)

    h = jnp.sin(f1_ref[...] * (mm(z_ref[...], w1_ref[...]) + b1_ref[...]))
    h = jnp.sin(f2_ref[...] * (mm(h, w2_ref[...]) + b2_ref[...]))
    h = mm(h, w3_ref[...]) + b3_ref[...]
    c = HYENA_W
    win = win_ref[...]
    for s in range(h.shape[1] // c):
        o_ref[:, s * c:(s + 1) * c] = h[:, s * c:(s + 1) * c] * win


def _pad2(a, rows, cols):
    return jnp.pad(a, ((0, rows - a.shape[0]), (0, cols - a.shape[1])))


def _hy_filter_call(l, t, w1, b1, f1, w2, b2, f2, w3, b3):
    t_norm = t / max(l - 1, 1)
    ang = (2.0 * math.pi / l) * t[:, None] * jnp.linspace(1e-4, HY_BANDS - 1, HY_BANDS, dtype=F32)[None, :]
    z = jnp.concatenate([t_norm[:, None], jnp.cos(ang), -jnp.sin(ang)], axis=-1)
    deltas = jnp.abs(jnp.linspace(HY_MIN_DECAY, HY_MAX_DECAY, HYENA_W, dtype=F32))
    window = jnp.exp(-t_norm[:, None] * deltas[None, :])
    n3 = w3.shape[1]
    by_dir = lambda a: a.reshape(a.shape[0], HY_ORDER, 2, HYENA_W).transpose(0, 2, 1, 3).reshape(a.shape[0], n3)
    w3, b3 = by_dir(w3), by_dir(b3[None])[0]
    l = t.shape[0]
    tl = min(l, 256)
    full = lambda shape: pl.BlockSpec(shape, lambda i: (0, 0))
    return pl.pallas_call(
        _hy_filter_kernel,
        grid=(l // tl,),
        in_specs=[pl.BlockSpec((tl, LANES), lambda i: (i, 0)),
                  full((LANES, LANES)), full((1, LANES)), full((1, LANES)),
                  full((LANES, LANES)), full((1, LANES)), full((1, LANES)),
                  full((LANES, n3)), full((1, n3)),
                  pl.BlockSpec((tl, HYENA_W), lambda i: (i, 0))],
        out_specs=pl.BlockSpec((tl, n3), lambda i: (i, 0)),
        out_shape=jax.ShapeDtypeStruct((l, n3), F32),
        compiler_params=_cp("parallel"),
        name="hyena_filter",
    )(_pad2(z, l, LANES), _pad2(w1, LANES, LANES), _pad2(b1[None], 1, LANES), _pad2(f1[None], 1, LANES),
      _pad2(w2, LANES, LANES), _pad2(b2[None], 1, LANES), _pad2(f2[None], 1, LANES),
      _pad2(w3, LANES, n3), b3[None], window)


def _dft_tables_kernel(ac_ref, as_ref, bc_ref, bs_ref, fwd_ref, inv_ref):
    tm, l = fwd_ref.shape[1], fwd_ref.shape[2]
    n = 2 * l
    row = lax.broadcasted_iota(jnp.int32, (tm, LANES), 0) + pl.program_id(0) * tm
    lane = lax.broadcasted_iota(jnp.int32, (tm, LANES), 1)
    row_sign = jnp.where((row & 1) == 0, 1.0, -1.0)
    lane_sign = jnp.where((lane & 1) == 0, 1.0, -1.0)
    bc = bc_ref[...]
    bs = bs_ref[...]
    for a in range(l // LANES):
        cols = slice(a * LANES, (a + 1) * LANES)
        ca = ac_ref[:, a:a + 1]
        sa = as_ref[:, a:a + 1]
        c = ca * bc - sa * bs
        s = sa * bc + ca * bs
        fwd_ref[0, :, cols] = c.astype(BF16)
        fwd_ref[1, :, cols] = jnp.where(row == 0, lane_sign, -s).astype(BF16)
        inv_c = c * (2.0 / n)
        inv_s = -s * (2.0 / n)
        if a == 0:
            inv_c = jnp.where(lane == 0, 1.0 / n, inv_c)
            inv_s = jnp.where(lane == 0, row_sign / n, inv_s)
        inv_ref[:, cols] = inv_c.astype(BF16)
        inv_ref[:, l + a * LANES:l + (a + 1) * LANES] = inv_s.astype(BF16)


def _dft_mats(l):
    n = 2 * l
    f = jnp.arange(l, dtype=jnp.int32)[:, None]

    def table(t):
        ang = ((f * t[None, :]) % n).astype(F32) * (2.0 * math.pi / n)
        return jnp.cos(ang), jnp.sin(ang)

    coarse = table(jnp.arange(l // LANES, dtype=jnp.int32) * LANES)
    fine = table(jnp.arange(LANES, dtype=jnp.int32))
    tm = min(l, 256)
    small = pl.BlockSpec((tm, l // LANES), lambda i: (i, 0))
    wide = pl.BlockSpec((tm, LANES), lambda i: (i, 0))
    return pl.pallas_call(
        _dft_tables_kernel,
        grid=(l // tm,),
        in_specs=[small, small, wide, wide],
        out_specs=[pl.BlockSpec((2, tm, l), lambda i: (0, i, 0)), pl.BlockSpec((tm, n), lambda i: (i, 0))],
        out_shape=[jax.ShapeDtypeStruct((2, l, l), BF16), jax.ShapeDtypeStruct((l, n), BF16)],
        compiler_params=_cp("parallel"),
        name="dft_tables",
    )(*coarse, *fine)


def _conv_blocks(l):
    return 2 if l >= 1024 else 1


def _lag_window_plan(l):
    nb = _conv_blocks(l)
    lb = l // nb
    pos = jnp.arange(l, dtype=F32)
    if nb == 1:
        return pos, jnp.array([0, 0, 0, 1], jnp.int32)
    pos = jnp.concatenate([pos, lb - jnp.arange(lb, dtype=F32)])
    return pos, jnp.array([2, 0, 1, 1, 0, 0, 1, 0, 2, 1, 1, 0], jnp.int32)


def _spec_kernel(tab_ref, f_ref, a_ref, b_ref, o_ref):
    a = a_ref[...]
    row0 = lax.broadcasted_iota(jnp.int32, a.shape, 0) == 0
    b = jnp.where(row0, 0.0, b_ref[...])
    s = (a + b).astype(BF16)
    hi = _dot(f_ref[1], (a - b).astype(BF16))
    o_ref[0, 0] = _dot(f_ref[0], s)
    o_ref[0, 1] = hi

    @pl.when(pl.program_id(1) == 0)
    def _():
        nyq = _dot(f_ref[1, 0:16, :], s)
        row = lax.broadcasted_iota(jnp.int32, nyq.shape, 0)
        o_ref[0, 1, 0:16, :] = jnp.where(row == 0, nyq, hi[0:16])


def _spec_call(fwd, filt, table):
    _, lf, lb = fwd.shape
    nd = table.shape[0] // 4
    n = filt.shape[1] // 2
    tm = min(lf, 512)
    pick = lambda k: pl.BlockSpec((lb, n), lambda d, i, tab: (tab[2 * k * nd + d], tab[(2 * k + 1) * nd + d]))
    return pl.pallas_call(
        _spec_kernel,
        grid_spec=pltpu.PrefetchScalarGridSpec(
            num_scalar_prefetch=1,
            grid=(nd, lf // tm),
            in_specs=[pl.BlockSpec((2, tm, lb), lambda d, i, tab: (0, i, 0)), pick(0), pick(1)],
            out_specs=pl.BlockSpec((1, 2, tm, n), lambda d, i, tab: (d, 0, i, 0))),
        out_shape=jax.ShapeDtypeStruct((nd, 2, lf, n), F32),
        compiler_params=_cp("arbitrary", "arbitrary"),
        name="hyena_spectrum",
    )(table, fwd, filt, filt)


def _dft_fwd_kernel(f_ref, u_ref, h_ref, y_ref, *, nb):
    lb = f_ref.shape[2]
    xr, xi = [], []
    for p in range(nb):
        up = u_ref[0, 0, p * lb:(p + 1) * lb, :].astype(BF16)
        xr.append(_dot(f_ref[0], up))
        xi.append(_dot(f_ref[1], up))
    def products(rows, real_bins):
        for q in range(nb):
            yr = 0.0
            yi = 0.0
            for p in range(nb):
                gr = h_ref[q - p + nb - 1, 0, rows, :]
                gi = h_ref[q - p + nb - 1, 1, rows, :]
                ii = xi[p][rows] * gi
                cross = xr[p][rows] * gi + xi[p][rows] * gr
                yr = yr + xr[p][rows] * gr - (ii if real_bins is None else jnp.where(real_bins, 0.0, ii))
                yi = yi + (cross if real_bins is None else jnp.where(real_bins, ii, cross))
            y_ref[0, q, 0, rows, :] = yr.astype(BF16)
            y_ref[0, q, 1, rows, :] = yi.astype(BF16)

    products(slice(None), None)

    @pl.when(pl.program_id(0) == 0)
    def _():
        head = slice(0, 16)
        products(head, lax.broadcasted_iota(jnp.int32, (16, u_ref.shape[3]), 0) == 0)


def _dft_fwd_call(fwd, u, group, spec, order, nb):
    _, b, l, c = u.shape
    _, lf, lb = fwd.shape
    nd = spec.shape[0]
    tm = min(lf, 512)
    return pl.pallas_call(
        functools.partial(_dft_fwd_kernel, nb=nb),
        grid=(lf // tm, b),
        in_specs=[pl.BlockSpec((2, tm, lb), lambda i, j: (0, i, 0)),
                  pl.BlockSpec((1, 1, l, c), lambda i, j: (group, j, 0, 0)),
                  pl.BlockSpec((nd, 2, tm, c), lambda i, j: (0, 0, i, order))],
        out_specs=pl.BlockSpec((1, nb, 2, tm, c), lambda i, j: (j, 0, 0, i, 0)),
        out_shape=jax.ShapeDtypeStruct((b, nb, 2, lf, c), BF16),
        compiler_params=_cp("arbitrary", "arbitrary"),
        name="hyena_dft_fwd",
    )(fwd, u, spec)


def _dft_inv_kernel(g_ref, y_ref, u_ref, b_ref, x_ref, o_ref):
    r = _dot(g_ref[...], y_ref[0, 0]) + u_ref[0, 0].astype(F32) * b_ref[...]
    o_ref[0, 0] = (x_ref[0, 0] * r).astype(o_ref.dtype)


def _dft_inv_call(inv, y, u, u_group, bias, gate, gate_group):
    _, b, l, c = u.shape
    lb, n = inv.shape
    nb = l // lb
    tm = min(lb, 1024)
    nt = lb // tm
    y = y.reshape(b, nb, n, c)
    row = lambda g: pl.BlockSpec((1, 1, tm, c), lambda q, i, j: (g, j, q * nt + i, 0))
    return pl.pallas_call(
        _dft_inv_kernel,
        grid=(nb, nt, b),
        in_specs=[pl.BlockSpec((tm, n), lambda q, i, j: (i, 0)),
                  pl.BlockSpec((1, 1, n, c), lambda q, i, j: (j, q, 0, 0)),
                  row(u_group), pl.BlockSpec((1, c), lambda q, i, j: (0, 0)), row(gate_group)],
        out_specs=row(0),
        out_shape=jax.ShapeDtypeStruct((1, b, l, c), BF16),
        compiler_params=_cp("parallel", "parallel", "parallel"),
        name="hyena_dft_inv",
    )(inv, y, u, bias.reshape(1, c), gate)


def _hyena_call(p_hy, sh_w, sh_b, spec, hy_bias, fwd, inv, nb):
    u3 = _hy_short_call(p_hy, sh_w, sh_b)
    z = _dft_inv_call(inv, _dft_fwd_call(fwd, u3, 0, spec, 0, nb), u3, 0, hy_bias[0], u3, 1)
    return _dft_inv_call(inv, _dft_fwd_call(fwd, z, 0, spec, 1, nb), z, 0, hy_bias[1], u3, 2)[0]


def _mix_out_kernel(c_ref, a_ref, h_ref, w_ref, x_ref, g1_ref, g_ref, sh_ref, sc_ref,
                    wr_ref, rb_ref, o_ref, h32_ref, idx_ref, gate_ref):
    k1 = c_ref.shape[2]
    k2 = k1 + a_ref.shape[2]
    acc = _dot(c_ref[0], w_ref[0, :k1, :]) + _dot(a_ref[0], w_ref[0, k1:k2, :]) + _dot(h_ref[0], w_ref[0, k2:, :])
    x = x_ref[0] + g1_ref[0] * acc
    o_ref[0] = x
    hm = _rms(x, g_ref[...]) * (1.0 + sc_ref[0]) + sh_ref[0]
    h32_ref[0] = _pack_rows(hm)
    picks, gates = _route(hm, wr_ref[...], rb_ref[...])
    idx_ref[...] = jnp.concatenate(picks, axis=0)
    gate_ref[...] = jnp.concatenate(gates, axis=0)


def _mix_out_call(conv_o, att_o, hy_o, w_out, layer, x, mod, g2, wr, rbias):
    b, l, d = x.shape
    tm = min(l, 512)
    nt = l // tm
    act = lambda k: pl.BlockSpec((1, tm, k), lambda i, j: (i, j, 0))
    row = pl.BlockSpec((1, 1, d), lambda i, j: (i, 0, 0))
    tok = pl.BlockSpec((2, tm), lambda i, j: (0, i * nt + j))
    return pl.pallas_call(
        _mix_out_kernel,
        grid=(b, nt),
        in_specs=[act(conv_o.shape[2]), act(att_o.shape[2]), act(hy_o.shape[2]),
                  pl.BlockSpec((1,) + w_out.shape[1:], lambda i, j: (layer, 0, 0), pipeline_mode=pl.Buffered(1)),
                  act(d), row, pl.BlockSpec((1, d), lambda i, j: (0, 0)), row, row,
                  pl.BlockSpec(wr.shape, lambda i, j: (0, 0)),
                  pl.BlockSpec((N_EXPERTS, 1), lambda i, j: (0, 0))],
        out_specs=[act(d), act(d // 2), tok, tok],
        out_shape=[jax.ShapeDtypeStruct((b, l, d), F32),
                   jax.ShapeDtypeStruct((b, l, d // 2), jnp.int32),
                   jax.ShapeDtypeStruct((2, b * l), jnp.int32),
                   jax.ShapeDtypeStruct((2, b * l), F32)],
        compiler_params=_cp("parallel", "parallel"),
        name="mix_out",
    )(conv_o, att_o, hy_o, w_out, x, mod[2], g2.reshape(1, d), mod[3], mod[4], wr,
      rbias.reshape(N_EXPERTS, 1))


def _route(hm, wr, rbias):
    hi = hm.astype(BF16)
    lo = (hm - hi.astype(F32)).astype(BF16)
    nt = (((1,), (1,)), ((), ()))
    r = (lax.dot_general(wr, hi, nt, preferred_element_type=F32)
         + lax.dot_general(wr, lo, nt, preferred_element_type=F32))
    logits = r[:N_EXPERTS] + r[N_EXPERTS:]
    score = jax.nn.sigmoid(logits)
    sel = score + rbias
    rows = [sel[e:e + 1] for e in range(N_EXPERTS)]
    gscore = []
    for g in range(N_GROUPS):
        a, b, c, d = rows[4 * g:4 * g + 4]
        h1, l1, h2, l2 = jnp.maximum(a, b), jnp.minimum(a, b), jnp.maximum(c, d), jnp.minimum(c, d)
        gscore.append(jnp.maximum(h1, h2) + jnp.maximum(jnp.minimum(h1, h2), jnp.maximum(l1, l2)))
    best = gscore[0]
    gbest = jnp.zeros(best.shape, jnp.int32)
    for g in range(1, N_GROUPS):
        upd = gscore[g] > best
        gbest = jnp.where(upd, g, gbest)
        best = jnp.where(upd, gscore[g], best)
    masked = [jnp.where(gbest == e // EXPERTS_PER_GROUP, rows[e], NEG_INF) for e in range(N_EXPERTS)]
    picks = []
    for _ in range(2):
        val = jnp.full(best.shape, NEG_INF, F32)
        idx = jnp.zeros(best.shape, jnp.int32)
        for e in range(N_EXPERTS):
            cand = masked[e]
            if picks:
                cand = jnp.where(picks[0] == e, NEG_INF, cand)
            upd = cand > val
            idx = jnp.where(upd, e, idx)
            val = jnp.where(upd, cand, val)
        picks.append(idx)
    gates = [functools.reduce(jnp.add, [jnp.where(p == e, score[e:e + 1], 0.0) for e in range(N_EXPERTS)])
             for p in picks]
    den = gates[0] + gates[1]
    return picks, [gates[0] / den, gates[1] / den]


def _expert_changed(te_ref, i):
    return (i == 0) | (te_ref[i] != te_ref[jnp.maximum(i - 1, 0)])


def _pack_rows(x):
    bits = pltpu.bitcast(x.astype(BF16).astype(F32), jnp.uint32)
    half = bits.shape[1] // 2
    return pltpu.bitcast((bits[:, :half] >> 16) | (bits[:, half:] & jnp.uint32(0xFFFF0000)), jnp.int32)


def _unpack_rows(words):
    w = pltpu.bitcast(words, jnp.uint32)
    return pltpu.bitcast(w << 16, F32), pltpu.bitcast(w & jnp.uint32(0xFFFF0000), F32)


def _moe_up_kernel(te_ref, tn_ref, nu_ref, x_ref, wg_hbm, wu_hbm, h_ref, gbuf, ubuf, wgb_ref, wub_ref, sems,
                   *, layer):
    j, i = pl.program_id(0), pl.program_id(1)
    tf = gbuf.shape[1]

    def fetch(e, chunk):
        cols = pl.ds(pl.multiple_of(chunk * tf, tf), tf)
        return (pltpu.make_async_copy(wg_hbm.at[layer, e, :, cols], gbuf, sems.at[0]),
                pltpu.make_async_copy(wu_hbm.at[layer, e, :, cols], ubuf, sems.at[1]))

    def start(e, chunk):
        for copy in fetch(e, chunk):
            copy.start()

    @pl.when((i == 0) & (j == 0))
    def _():
        start(te_ref[0], 0)

    @pl.when(_expert_changed(te_ref, i))
    def _():
        for copy in fetch(te_ref[i], j):
            copy.wait()
        wgb_ref[...] = gbuf[...].astype(BF16)
        wub_ref[...] = ubuf[...].astype(BF16)
        nxt = tn_ref[i]
        more = nxt < nu_ref[0]

        @pl.when(more)
        def _():
            start(te_ref[nxt], j)

        @pl.when(jnp.logical_not(more) & (j + 1 < pl.num_programs(0)))
        def _():
            start(te_ref[0], j + 1)

    @pl.when(i < nu_ref[0])
    def _():
        xa, xb = (v.astype(BF16) for v in _unpack_rows(x_ref[...]))
        half = xa.shape[1]
        a = _dot(xa, wgb_ref[:half, :]) + _dot(xb, wgb_ref[half:, :])
        u = _dot(xa, wub_ref[:half, :]) + _dot(xb, wub_ref[half:, :])
        h_ref[...] = (a * jax.nn.sigmoid(a) * u).astype(BF16)

    @pl.when(i >= nu_ref[0])
    def _():
        h_ref[...] = jnp.zeros(h_ref.shape, BF16)


def _moe_up_call(xs, w_gate, w_up, layer, te, tn, nu, tile):
    p, dw = xs.shape
    _, _, d, f = w_gate.shape
    tf = 512
    hbm = pl.BlockSpec(memory_space=pl.ANY)
    return pl.pallas_call(
        functools.partial(_moe_up_kernel, layer=layer),
        grid_spec=pltpu.PrefetchScalarGridSpec(
            num_scalar_prefetch=3,
            grid=(f // tf, p // tile),
            in_specs=[pl.BlockSpec((tile, dw), lambda j, i, te, tn, nu: (i, 0)), hbm, hbm],
            out_specs=pl.BlockSpec((tile, tf), lambda j, i, te, tn, nu: (i, j)),
            scratch_shapes=[pltpu.VMEM((d, tf), F32), pltpu.VMEM((d, tf), F32),
                            pltpu.VMEM((d, tf), BF16), pltpu.VMEM((d, tf), BF16),
                            pltpu.SemaphoreType.DMA((2,))]),
        out_shape=jax.ShapeDtypeStruct((p, f), BF16),
        compiler_params=_cp("arbitrary", "arbitrary"),
        name="moe_up",
    )(te, tn, nu, xs, w_gate, w_up)


def _moe_down_kernel(te_ref, tn_ref, nu_ref, h_ref, wd_hbm, y_ref, wbuf, wdb_ref, sem, *, layer):
    i = pl.program_id(0)

    def fetch(e):
        return pltpu.make_async_copy(wd_hbm.at[layer, e], wbuf, sem)

    @pl.when(i == 0)
    def _():
        fetch(te_ref[0]).start()

    @pl.when(_expert_changed(te_ref, i))
    def _():
        fetch(te_ref[i]).wait()
        wdb_ref[...] = wbuf[...].astype(BF16)
        nxt = tn_ref[i]

        @pl.when(nxt < nu_ref[0])
        def _():
            fetch(te_ref[nxt]).start()

    @pl.when(i < nu_ref[0])
    def _():
        y_ref[...] = _pack_rows(_dot(h_ref[...], wdb_ref[...]))

    @pl.when(i >= nu_ref[0])
    def _():
        y_ref[...] = jnp.zeros(y_ref.shape, jnp.int32)


def _moe_down_call(h, w_down, layer, te, tn, nu, tile):
    p, f = h.shape
    d = w_down.shape[3]
    return pl.pallas_call(
        functools.partial(_moe_down_kernel, layer=layer),
        grid_spec=pltpu.PrefetchScalarGridSpec(
            num_scalar_prefetch=3,
            grid=(p // tile,),
            in_specs=[pl.BlockSpec((tile, f), lambda i, te, tn, nu: (i, 0)),
                      pl.BlockSpec(memory_space=pl.ANY)],
            out_specs=pl.BlockSpec((tile, d // 2), lambda i, te, tn, nu: (i, 0)),
            scratch_shapes=[pltpu.VMEM((f, d), F32), pltpu.VMEM((f, d), BF16), pltpu.SemaphoreType.DMA(())]),
        out_shape=jax.ShapeDtypeStruct((p, d // 2), jnp.int32),
        compiler_params=_cp("arbitrary"),
        name="moe_down",
    )(te, tn, nu, h, w_down)


PLAN_BLOCK = 512


def _plan_kernel(idx_ref, pos_ref, te_ref, tn_ref, nu_ref, rank_ref, *, tile):
    t = idx_ref.shape[1]
    nb = t // PLAN_BLOCK
    experts = lax.broadcasted_iota(jnp.int32, (N_EXPERTS, PLAN_BLOCK), 0)
    r = lax.broadcasted_iota(jnp.int32, (PLAN_BLOCK, PLAN_BLOCK), 0)
    c = lax.broadcasted_iota(jnp.int32, (PLAN_BLOCK, PLAN_BLOCK), 1)
    before = jnp.where(r < c, 1.0, 0.0).astype(BF16)

    def rank_pass(k):
        def body(j, carry):
            cols = pl.ds(pl.multiple_of(j * PLAN_BLOCK, PLAN_BLOCK), PLAN_BLOCK)
            onehot = jnp.where(experts == idx_ref[k:k + 1, cols], 1.0, 0.0)
            earlier = _dot(onehot.astype(BF16), before) + carry
            rank_ref[k:k + 1, cols] = jnp.sum(onehot * earlier, axis=0, keepdims=True)
            return carry + jnp.sum(onehot, axis=1, keepdims=True)
        return body

    counts = jnp.zeros((N_EXPERTS, 1), F32)
    for k in range(2):
        counts = lax.fori_loop(0, nb, rank_pass(k), counts)
    padded = jnp.ceil(counts * (1.0 / tile)) * tile
    starts, ends = [], []
    acc = jnp.zeros((1, 1), F32)
    for e in range(N_EXPERTS):
        starts.append(acc)
        acc = acc + padded[e:e + 1]
        ends.append(acc)

    def place_pass(k):
        def body(j, carry):
            cols = pl.ds(pl.multiple_of(j * PLAN_BLOCK, PLAN_BLOCK), PLAN_BLOCK)
            row = idx_ref[k:k + 1, cols]
            start = functools.reduce(jnp.add, [jnp.where(row == e, starts[e], 0.0) for e in range(N_EXPERTS)])
            pos_ref[k:k + 1, cols] = (rank_ref[k:k + 1, cols] + start).astype(jnp.int32)
            return carry
        return body

    for k in range(2):
        lax.fori_loop(0, nb, place_pass(k), 0)
    n_used = acc * (1.0 / tile)
    tiles = lax.broadcasted_iota(jnp.int32, te_ref.shape, 1).astype(F32)
    first_row = jnp.minimum(tiles, n_used - 1.0) * tile
    te = functools.reduce(jnp.add, [jnp.where(ends[e] <= first_row, 1.0, 0.0) for e in range(N_EXPERTS)])
    te_ref[...] = te.astype(jnp.int32)
    seg_end = functools.reduce(jnp.add, [jnp.where(te == e, ends[e], 0.0) for e in range(N_EXPERTS)])
    tn_ref[...] = (seg_end * (1.0 / tile)).astype(jnp.int32)
    nu_ref[...] = jnp.broadcast_to(n_used, nu_ref.shape).astype(jnp.int32)


def _plan_call(idx, tile, n_tiles):
    t = idx.shape[1]
    whole = lambda shape: pl.BlockSpec(shape, lambda: (0,) * len(shape))
    lanes = jax.ShapeDtypeStruct((1, LANES), jnp.int32)
    pos, te, tn, nu = pl.pallas_call(
        functools.partial(_plan_kernel, tile=tile),
        in_specs=[whole(idx.shape)],
        out_specs=[whole(idx.shape), whole((1, LANES)), whole((1, LANES)), whole((1, LANES))],
        out_shape=[jax.ShapeDtypeStruct(idx.shape, jnp.int32), lanes, lanes, lanes],
        scratch_shapes=[pltpu.VMEM(idx.shape, F32)],
        name="route_plan",
    )(idx)
    return pos.reshape(2 * t), te[0, :n_tiles], tn[0, :n_tiles], nu[0, :1]


V7X_SC_CORES = 2
V7X_SC_SUBCORES = 16
SC_WORKERS = V7X_SC_CORES * V7X_SC_SUBCORES
SC_ROW_BYTES = 256 * 1024


def _sc_worker():
    return lax.axis_index("s") * V7X_SC_CORES + lax.axis_index("c")


def _sc_dispatch_call(streams, pos, n_rows):
    w = streams[0].shape[1]
    sizes = [s.shape[0] for s in streams]
    t = sum(sizes)
    per_w = t // SC_WORKERS
    ch = min(per_w, SC_ROW_BYTES // (4 * w))
    assert per_w % ch == 0 and all(size % ch == 0 for size in sizes)
    mesh = plsc.VectorSubcoreMesh(core_axis_name="c", subcore_axis_name="s")

    @functools.partial(pl.kernel, mesh=mesh, out_type=jax.ShapeDtypeStruct((n_rows, w), jnp.int32),
                       scratch_types=[pltpu.VMEM((ch,), jnp.int32), pltpu.VMEM((ch,), jnp.int32),
                                      pltpu.VMEM((ch, w), jnp.int32),
                                      pltpu.SemaphoreType.DMA, pltpu.SemaphoreType.DMA],
                       name="sc_dispatch")
    def run(*refs):
        srcs = refs[:len(sizes)]
        pos_hbm, xs_hbm, idx0_v, idx1_v, rows_v, sem0, sem1 = refs[len(sizes):]
        first = _sc_worker() * per_w

        @pl.loop(0, per_w // ch)
        def _(j):
            base = first + j * ch
            start = 0
            for src, size in zip(srcs, sizes):
                def load(src=src, start=start):
                    pltpu.sync_copy(src.at[pl.ds(base - start, ch)], rows_v)
                if len(sizes) == 1:
                    load()
                else:
                    pl.when((base >= start) & (base < start + size))(load)
                start += size
            pltpu.sync_copy(pos_hbm.at[pl.ds(base, ch)], idx0_v)
            pltpu.sync_copy(pos_hbm.at[pl.ds(t + base, ch)], idx1_v)
            first_choice = pltpu.async_copy(rows_v, xs_hbm.at[idx0_v], sem0)
            second_choice = pltpu.async_copy(rows_v, xs_hbm.at[idx1_v], sem1)
            first_choice.wait()
            second_choice.wait()

    return run(*streams, pos)


def _sc_combine_call(ys, pos):
    _, d = ys.shape
    n = pos.shape[0]
    per_w = n // SC_WORKERS
    ch = min(per_w // 2, SC_ROW_BYTES // (8 * d))
    assert per_w % (2 * ch) == 0
    mesh = plsc.VectorSubcoreMesh(core_axis_name="c", subcore_axis_name="s")
    idx_buf = pltpu.VMEM((ch,), jnp.int32)
    row_buf = pltpu.VMEM((ch, d), jnp.int32)

    @functools.partial(pl.kernel, mesh=mesh, out_type=jax.ShapeDtypeStruct((n, d), jnp.int32),
                       scratch_types=[idx_buf, idx_buf, row_buf, row_buf] + [pltpu.SemaphoreType.DMA] * 4,
                       name="sc_combine")
    def run(ys_hbm, pos_hbm, out_hbm, idx_a, idx_b, rows_a, rows_b, gsem_a, gsem_b, wsem_a, wsem_b):
        first = _sc_worker() * per_w

        @pl.loop(0, per_w // (2 * ch))
        def _(j):
            base = first + j * (2 * ch)
            pltpu.sync_copy(pos_hbm.at[pl.ds(base, ch)], idx_a)
            pltpu.sync_copy(pos_hbm.at[pl.ds(base + ch, ch)], idx_b)
            gather_a = pltpu.async_copy(ys_hbm.at[idx_a], rows_a, gsem_a)
            gather_b = pltpu.async_copy(ys_hbm.at[idx_b], rows_b, gsem_b)
            gather_a.wait()
            write_a = pltpu.async_copy(rows_a, out_hbm.at[pl.ds(base, ch)], wsem_a)
            gather_b.wait()
            write_b = pltpu.async_copy(rows_b, out_hbm.at[pl.ds(base + ch, ch)], wsem_b)
            write_a.wait()
            write_b.wait()

    return run(ys, pos)


def _moe_call(streams, idx, w_gate, w_up, w_down, layer):
    t = idx.shape[1]
    tile = min(MOE_TILE, max(t // N_EXPERTS, 128))
    n_tiles = (2 * t) // tile + N_EXPERTS
    assert n_tiles <= LANES
    pos, te, tn, nu = _plan_call(idx, tile, n_tiles)
    xs = _sc_dispatch_call(streams, pos, n_tiles * tile)
    hid = _moe_up_call(xs, w_gate, w_up, layer, te, tn, nu, tile)
    ys = _moe_down_call(hid, w_down, layer, te, tn, nu, tile)
    return _sc_combine_call(ys, pos).reshape(2, t, ys.shape[1])


def _moe_resid_kernel(x_ref, y0_ref, y1_ref, gt_ref, g_ref, *rest, final):
    gt = gt_ref[0]
    halves0 = _unpack_rows(y0_ref[0])
    halves1 = _unpack_rows(y1_ref[0])
    y = jnp.concatenate([gt[:, 0:1] * a + gt[:, 1:2] * b for a, b in zip(halves0, halves1)], axis=1)
    x = x_ref[0] + g_ref[0] * y
    if final:
        fg_ref, o_ref = rest
        x = _rms(x, fg_ref[...])
    else:
        (o_ref,) = rest
    o_ref[0] = x


def _moe_resid_call(x, y, first_row, gates, gate, final_g=None):
    b, l, d = x.shape
    tl = min(l, 512)
    nt = l // tl
    tile = pl.BlockSpec((1, tl, d), lambda i, j: (i, j, 0))
    choice = lambda k: pl.BlockSpec((1, tl, d // 2), lambda i, j: (k, first_row // tl + i * nt + j, 0))
    in_specs = [tile, choice(0), choice(1), pl.BlockSpec((1, tl, 2), lambda i, j: (i, j, 0)),
                pl.BlockSpec((1, 1, d), lambda i, j: (i, 0, 0))]
    args = [x, y, y, gates.T.reshape(b, l, 2), gate]
    if final_g is not None:
        in_specs.append(pl.BlockSpec((1, d), lambda i, j: (0, 0)))
        args.append(final_g.reshape(1, d))
    return pl.pallas_call(
        functools.partial(_moe_resid_kernel, final=final_g is not None),
        grid=(b, l // tl),
        in_specs=in_specs,
        out_specs=tile,
        out_shape=jax.ShapeDtypeStruct((b, l, d), F32),
        compiler_params=_cp("parallel", "parallel"),
        name="moe_residual",
    )(*args)


def _rope_perm(w):
    r = w.reshape(w.shape[:-1] + (2, 2, ROPE_FREQS))
    return jnp.stack([-r[..., 1, :], r[..., 0, :]], axis=-2).reshape(w.shape)


def _rope_table(l, rotary):
    if not rotary:
        return jnp.concatenate([jnp.ones((l, MLA_ROPE), F32), jnp.zeros((l, MLA_ROPE), F32)], axis=1)
    rows = l // GRID_W
    r = jnp.repeat(jnp.arange(rows), GRID_W)
    col = jnp.tile(jnp.arange(GRID_W), rows)
    pos = jnp.stack([r, col], axis=-1).astype(F32)
    inv = ROPE_THETA ** (-jnp.arange(ROPE_FREQS, dtype=F32) / ROPE_FREQS)
    ang = pos[:, :, None] * inv
    lay = lambda a: jnp.concatenate([a[:, 0], a[:, 0], a[:, 1], a[:, 1]], axis=1)
    return jnp.concatenate([lay(jnp.cos(ang)), lay(jnp.sin(ang))], axis=1)


def _prep_weights(w_in, w_uq, w_ukv, w_out):
    depth = w_in.shape[0]
    cast = lambda a: a.astype(BF16)
    kr = cast(w_in[:, :, OFF_KR:OFF_HY])
    w_kv_in = jnp.concatenate([cast(w_in[:, :, OFF_KV:OFF_KR]), kr, _rope_perm(kr)], axis=2)
    w_cq, w_hy = cast(w_in[:, :, :OFF_KV]), cast(w_in[:, :, OFF_HY:])
    uq = cast(w_uq).reshape(depth, MLA_Q_RANK, MLA_HEADS, MLA_NOPE + MLA_ROPE)
    w_q = jnp.concatenate([uq, _rope_perm(uq[..., MLA_NOPE:])], axis=3).reshape(depth, MLA_Q_RANK, -1)
    ukv = cast(w_ukv).reshape(depth, MLA_KV_RANK, MLA_HEADS, MLA_NOPE + MLA_V_DIM)
    w_k = ukv[..., :MLA_NOPE].reshape(depth, MLA_KV_RANK, MLA_HEADS * MLA_NOPE)
    w_v = ukv[..., MLA_NOPE:].reshape(depth, MLA_KV_RANK, MLA_HEADS * MLA_V_DIM)
    return dict(cq=w_cq, hy=w_hy, kv=w_kv_in, uq=w_q, uk=w_k, uv=w_v, out=cast(w_out))


def _router_weights(w_router):
    hi = w_router.astype(BF16)
    lo = (w_router - hi.astype(F32)).astype(BF16)
    return jnp.concatenate([hi.T, lo.T], axis=0)


def _mixer(x, mod, w, layer, lp, cs, spec, fwd, inv, kv_extra, full, wr, rbias):
    kv_args = (w["kv"], lp["kv_norm_g"], w["uk"], w["uv"], cs, layer)
    if not full:
        k, v = _in_proj_call(x, lp["norm1_g"], mod[0], mod[1], [], *kv_args)
        return None, k, v
    proj = [(w["cq"], [OFF_Q, MLA_Q_RANK]), (w["hy"], [(HY_ORDER + 1) * HYENA_W])]
    p_conv, p_q, p_hy, k, v = _in_proj_call(x, lp["norm1_g"], mod[0], mod[1], proj, *kv_args)
    conv_o = _conformer_call(p_conv, lp["conv_dw_w"], lp["conv_dw_b"], lp["conv_ln_g"], lp["conv_ln_b"])
    att_o = _attn_call(p_q, lp["q_norm_g"], w["uq"], cs, kv_extra + [(k, v)], layer)
    hy_o = _hyena_call(p_hy, lp["hy_short_w"], lp["hy_short_b"], spec, lp["hy_bias"], fwd, inv,
                       _conv_blocks(x.shape[1]))
    routed = _mix_out_call(conv_o, att_o, hy_o, w["out"], layer, x, mod, lp["norm2_g"], wr, rbias)
    return routed, k, v


def _channel(streams, experts, layer, final_g=None):
    d = streams[0][1][0].shape[2]
    idx = jnp.concatenate([r[2] for _, r in streams], axis=1)
    y = _moe_call([r[1].reshape(-1, d // 2) for _, r in streams], idx, *experts, layer)
    outs = []
    first_row = 0
    for mod, (x, _, _, gates) in streams:
        outs.append(_moe_resid_call(x, y, first_row, gates, mod[5], final_g))
        first_row += x.shape[0] * x.shape[1]
    return outs


def _filter_spectrum(l, lp, fwd):
    pos, table = _lag_window_plan(l)
    filt = _hy_filter_call(l, pos, lp["hy_w1"], lp["hy_b1"], lp["hy_freq1"], lp["hy_w2"], lp["hy_b2"],
                           lp["hy_freq2"], lp["hy_w3"], lp["hy_b3"])
    return _spec_call(fwd, filt, table)


def kernel(x, c, ctx, c_ctx, norm1_g, norm2_g, w_ada, b_ada, w_in, conv_dw_w, conv_dw_b, conv_ln_g, conv_ln_b, q_norm_g, w_uq, kv_norm_g, w_ukv, hy_short_w, hy_short_b, hy_w1, hy_b1, hy_freq1, hy_w2, hy_b2, hy_freq2, hy_w3, hy_b3, hy_bias, w_out, w_router, router_bias, w_gate, w_up, w_down, final_norm_g):
    b, l, d = x.shape
    lc = ctx.shape[1]
    depth = w_in.shape[0]
    per_layer = dict(norm1_g=norm1_g, norm2_g=norm2_g, conv_dw_w=conv_dw_w, conv_dw_b=conv_dw_b,
                     conv_ln_g=conv_ln_g, conv_ln_b=conv_ln_b, q_norm_g=q_norm_g, kv_norm_g=kv_norm_g,
                     hy_short_w=hy_short_w, hy_short_b=hy_short_b, hy_w1=hy_w1, hy_b1=hy_b1, hy_freq1=hy_freq1,
                     hy_w2=hy_w2, hy_b2=hy_b2, hy_freq2=hy_freq2, hy_w3=hy_w3, hy_b3=hy_b3, hy_bias=hy_bias)
    experts = (w_gate, w_up, w_down)
    cs_l = _rope_table(l, True)
    cs_c = _rope_table(lc, False)
    fwd_l, inv_l = _dft_mats(l // _conv_blocks(l))
    fwd_c, inv_c = _dft_mats(lc // _conv_blocks(lc))
    wr = _router_weights(w_router)
    w = _prep_weights(w_in, w_uq, w_ukv, w_out)
    rows = 16
    cvec = jnp.concatenate([c, c_ctx[None, :], jnp.zeros((rows - b - 1, d), F32)], axis=0)
    xl, xc = x, ctx
    for i in range(depth):
        last = i == depth - 1
        lp = {k: v[i] for k, v in per_layer.items()}
        m = _ada_call(cvec, w_ada, b_ada, i)
        mod_l = [m[:b, None, j * d:(j + 1) * d] for j in range(6)]
        mod_c = [jnp.broadcast_to(m[b:b + 1, None, j * d:(j + 1) * d], (b, 1, d)) for j in range(6)]
        spec_l = _filter_spectrum(l, lp, fwd_l)
        spec_c = None if last else _filter_spectrum(lc, lp, fwd_c)
        routed_c, kc, vc = _mixer(xc, mod_c, w, i, lp, cs_c, spec_c, fwd_c, inv_c, [], not last, wr, router_bias)
        routed_l, _, _ = _mixer(xl, mod_l, w, i, lp, cs_l, spec_l, fwd_l, inv_l, [(kc, vc)], True, wr, router_bias)
        if last:
            (xl,) = _channel([(mod_l, routed_l)], experts, i, final_norm_g)
        else:
            xl, xc = _channel([(mod_l, routed_l), (mod_c, routed_c)], experts, i)
    return xl
```

```python
import functools
import math

import jax
import jax.numpy as jnp
from jax import lax
from jax.experimental import pallas as pl
from jax.experimental.pallas import tpu as pltpu
from jax.experimental.pallas import tpu_sc as plsc

F32 = jnp.float32
BF16 = jnp.bfloat16

EPS = 1e-6
GRID_W = 64
CONV_W = 512
CONV_KSIZE = 31
MLA_HEADS = 8
MLA_NOPE = 128
MLA_ROPE = 64
MLA_V_DIM = 128
MLA_Q_RANK = 768
MLA_KV_RANK = 512
ROPE_FREQS = MLA_ROPE // 4
ROPE_THETA = 10000.0
MLA_SCALE = (MLA_NOPE + MLA_ROPE) ** -0.5
Q_SCALE = MLA_SCALE * math.log2(math.e)
HYENA_W = 512
HY_ORDER = 2
HY_SHORT = 3
HY_EMB = 33
HY_BANDS = (HY_EMB - 1) // 2
HY_FFN = 64
HY_MIN_DECAY = math.log(1e-2) / 1.5
HY_MAX_DECAY = math.log(1e-2) / 0.3
N_EXPERTS = 16
N_GROUPS = 4
EXPERTS_PER_GROUP = N_EXPERTS // N_GROUPS
OFF_Q = 2 * CONV_W
OFF_KV = OFF_Q + MLA_Q_RANK
OFF_KR = OFF_KV + MLA_KV_RANK
OFF_HY = OFF_KR + MLA_ROPE

LANES = 128
QK_HEAD = 2 * LANES
V_HEAD = MLA_V_DIM + LANES
V7X_VMEM_BYTES = 64 * 1024 * 1024
VMEM_LIMIT = (V7X_VMEM_BYTES * 7) // 8
MOE_TILE = 512
NEG_INF = float("-inf")


def _cp(*sem):
    return pltpu.CompilerParams(dimension_semantics=sem, vmem_limit_bytes=VMEM_LIMIT)


def _dot(a, b):
    return jnp.dot(a, b, preferred_element_type=F32)


def _rms(x, g):
    ms = jnp.mean(x * x, axis=-1, keepdims=True)
    return x * lax.rsqrt(ms + EPS) * g


def _ada_kernel(c_ref, w_ref, b_ref, o_ref):
    c = c_ref[...]
    a = (c * jax.nn.sigmoid(c)).astype(BF16)
    o_ref[...] = _dot(a, w_ref[0].astype(BF16)) + b_ref[0]


def _ada_call(cvec, w, b, layer):
    m, d = cvec.shape
    n = w.shape[2]
    tn = 1024
    return pl.pallas_call(
        _ada_kernel,
        grid=(n // tn,),
        in_specs=[pl.BlockSpec((m, d), lambda j: (0, 0)),
                  pl.BlockSpec((1, d, tn), lambda j: (layer, 0, j)),
                  pl.BlockSpec((1, 1, tn), lambda j: (layer, 0, j))],
        out_specs=pl.BlockSpec((m, tn), lambda j: (0, j)),
        out_shape=jax.ShapeDtypeStruct((m, n), F32),
        compiler_params=_cp("parallel"),
        name="ada",
    )(cvec, w, b.reshape(b.shape[0], 1, n))


def _in_proj_kernel(x_ref, g_ref, sh_ref, sc_ref, *refs, n_proj):
    w_refs, (wkv_ref, kg_ref, wk_ref, wv_ref, cs_ref), o_refs = refs[:n_proj], refs[n_proj:n_proj + 5], refs[n_proj + 5:]
    *p_refs, k_ref, v_ref = o_refs
    y = _rms(x_ref[0], g_ref[...])
    h = (y * (1.0 + sc_ref[0]) + sh_ref[0]).astype(BF16)
    outs = iter(p_refs)
    for w_ref in w_refs:
        off = 0
        while off < w_ref.shape[2]:
            o_ref = next(outs)
            n = o_ref.shape[-1]
            o_ref[0] = _dot(h, w_ref[0, :, off:off + n]).astype(o_ref.dtype)
            off += n
    p = _dot(h, wkv_ref[0])
    xn = _rms(p[:, :MLA_KV_RANK], kg_ref[...]).astype(BF16)
    kk = _dot(xn, wk_ref[0])
    vv = _dot(xn, wv_ref[0])
    rot = _rotate(p[:, MLA_KV_RANK:], cs_ref[...])
    lane = lax.broadcasted_iota(jnp.int32, rot.shape, 1)
    rot = jnp.where(lane < MLA_ROPE, rot, 0.0).astype(BF16)
    ones = jnp.where(lane == 0, 1.0, 0.0).astype(BF16)
    for hd in range(MLA_HEADS):
        k_ref[0, :, hd * QK_HEAD:hd * QK_HEAD + LANES] = kk[:, hd * LANES:(hd + 1) * LANES].astype(BF16)
        k_ref[0, :, hd * QK_HEAD + LANES:(hd + 1) * QK_HEAD] = rot
        v_ref[0, :, hd * V_HEAD:hd * V_HEAD + MLA_V_DIM] = vv[:, hd * MLA_V_DIM:(hd + 1) * MLA_V_DIM].astype(BF16)
        v_ref[0, :, hd * V_HEAD + MLA_V_DIM:(hd + 1) * V_HEAD] = ones


def _in_proj_call(x, g, shift, scale, proj, w_kv, kv_g, w_k, w_v, cs, layer):
    b, l, d = x.shape
    tm = min(l, 512)
    r = MLA_KV_RANK
    row = pl.BlockSpec((1, 1, d), lambda i, j: (i, 0, 0))
    const = lambda shape: pl.BlockSpec(shape, lambda i, j: (0, 0))
    of_layer = lambda a, **kw: pl.BlockSpec((1,) + a.shape[1:], lambda i, j: (layer, 0, 0), **kw)
    weights = [w for w, _ in proj]
    widths = [k for _, ks in proj for k in ks] + [MLA_HEADS * QK_HEAD, MLA_HEADS * V_HEAD]
    return pl.pallas_call(
        functools.partial(_in_proj_kernel, n_proj=len(weights)),
        grid=(b, l // tm),
        in_specs=[pl.BlockSpec((1, tm, d), lambda i, j: (i, j, 0)),
                  const((1, d)), row, row,
                  *[of_layer(w, pipeline_mode=pl.Buffered(1)) for w in weights],
                  of_layer(w_kv, pipeline_mode=pl.Buffered(1)),
                  const((1, r)), of_layer(w_k), of_layer(w_v),
                  pl.BlockSpec((tm, LANES), lambda i, j: (j, 0))],
        out_specs=[pl.BlockSpec((1, tm, k), lambda i, j: (i, j, 0)) for k in widths],
        out_shape=[jax.ShapeDtypeStruct((b, l, k), BF16) for k in widths],
        compiler_params=_cp("parallel", "parallel"),
        name="in_proj",
    )(x, g.reshape(1, d), shift, scale, *weights, w_kv, kv_g.reshape(1, r), w_k, w_v, cs)


CONV_ROWS = 128
CONV_PAD = 16


def _conformer_kernel(p_ref, w_ref, b_ref, g_ref, be_ref, o_ref, zp_ref):
    l = p_ref.shape[1]
    c = CONV_W
    zeros = jnp.zeros((CONV_PAD, c), F32)
    zp_ref[0:CONV_PAD, :] = zeros
    zp_ref[CONV_PAD + l:CONV_PAD + l + CONV_PAD, :] = zeros

    def glu(r, carry):
        base = pl.multiple_of(r * CONV_ROWS, CONV_ROWS)
        blk = p_ref[0, pl.ds(base, CONV_ROWS), :].astype(F32)
        zp_ref[pl.ds(base + CONV_PAD, CONV_ROWS), :] = blk[:, :c] * jax.nn.sigmoid(blk[:, c:])
        return carry

    lax.fori_loop(0, l // CONV_ROWS, glu, 0)

    win_rows = CONV_ROWS + 2 * CONV_PAD
    first = CONV_PAD - CONV_KSIZE // 2

    def conv(r, carry):
        base = pl.multiple_of(r * CONV_ROWS, CONV_ROWS)
        parts = []
        for cc in range(c // LANES):
            lanes = slice(cc * LANES, (cc + 1) * LANES)
            win = zp_ref[pl.ds(base, win_rows), lanes]
            acc = jnp.zeros((CONV_ROWS, LANES), F32)
            for res in range(8):
                taps = [k for k in range(CONV_KSIZE) if (k + first) % 8 == res]
                if not taps:
                    continue
                shifted = pltpu.roll(win, win_rows - res, axis=0) if res else win
                for k in taps:
                    off = k + first - res
                    acc = acc + shifted[off:off + CONV_ROWS, :] * w_ref[k:k + 1, lanes]
            parts.append(acc)
        z = jnp.concatenate(parts, axis=1) + b_ref[...]
        mu = jnp.mean(z, axis=-1, keepdims=True)
        zc = z - mu
        var = jnp.mean(zc * zc, axis=-1, keepdims=True)
        y = zc * lax.rsqrt(var + EPS) * g_ref[...] + be_ref[...]
        o_ref[0, pl.ds(base, CONV_ROWS), :] = (y * jax.nn.sigmoid(y)).astype(o_ref.dtype)
        return carry

    lax.fori_loop(0, l // CONV_ROWS, conv, 0)


def _conformer_call(p, dw_w, dw_b, ln_g, ln_b):
    b, l, c2 = p.shape
    c = c2 // 2
    vec = pl.BlockSpec((1, c), lambda i: (0, 0))
    return pl.pallas_call(
        _conformer_kernel,
        grid=(b,),
        in_specs=[pl.BlockSpec((1, l, c2), lambda i: (i, 0, 0)),
                  pl.BlockSpec((CONV_KSIZE, c), lambda i: (0, 0)), vec, vec, vec],
        out_specs=pl.BlockSpec((1, l, c), lambda i: (i, 0, 0)),
        out_shape=jax.ShapeDtypeStruct((b, l, c), BF16),
        scratch_shapes=[pltpu.VMEM((l + 2 * CONV_PAD, c), F32)],
        compiler_params=_cp("parallel"),
        name="conformer",
    )(p, dw_w, dw_b.reshape(1, c), ln_g.reshape(1, c), ln_b.reshape(1, c))


def _rotate(group, cs):
    t = group * cs
    return t + pltpu.roll(t, MLA_ROPE, axis=1)


def _attn_kernel(p_ref, g_ref, w_ref, cs_ref, *refs, n_seg):
    o_ref, q_ref = refs[2 * n_seg], refs[2 * n_seg + 1]
    xn = _rms(p_ref[0].astype(F32), g_ref[...]).astype(BF16)
    cs = cs_ref[...] * Q_SCALE
    for h in range(MLA_HEADS):
        r = _dot(xn, w_ref[0, :, h * QK_HEAD:(h + 1) * QK_HEAD])
        q_ref[:, h * QK_HEAD:h * QK_HEAD + LANES] = (r[:, :LANES] * Q_SCALE).astype(BF16)
        q_ref[:, h * QK_HEAD + LANES:(h + 1) * QK_HEAD] = _rotate(r[:, LANES:], cs).astype(BF16)
    nt = (((1,), (1,)), ((), ()))
    for h in range(MLA_HEADS):
        qk = slice(h * QK_HEAD, (h + 1) * QK_HEAD)
        vd = slice(h * MLA_V_DIM, (h + 1) * MLA_V_DIM)
        q = q_ref[:, qk]
        scores = [lax.dot_general(q, refs[2 * s][0, :, qk], nt, preferred_element_type=F32).astype(BF16)
                  for s in range(n_seg)]
        m = functools.reduce(jnp.maximum, [jnp.max(s, axis=-1, keepdims=True) for s in scores])
        acc = 0.0
        for s in range(n_seg):
            p = jnp.exp2(scores[s] - m)
            acc = acc + _dot(p, refs[2 * s + 1][0, :, h * V_HEAD:(h + 1) * V_HEAD])
        o_ref[0, :, vd] = (acc[:, :MLA_V_DIM] * (1.0 / acc[:, MLA_V_DIM:MLA_V_DIM + 1])).astype(o_ref.dtype)


def _attn_call(p_q, q_g, w_q, cs, segs, layer):
    b, l, r = p_q.shape
    tq = min(l, 512)
    const = lambda shape: pl.BlockSpec(shape, lambda i, j: (0, 0))
    in_specs = [pl.BlockSpec((1, tq, r), lambda i, j: (i, j, 0)), const((1, r)),
                pl.BlockSpec((1,) + w_q.shape[1:], lambda i, j: (layer, 0, 0)),
                pl.BlockSpec((tq, LANES), lambda i, j: (j, 0))]
    args = [p_q, q_g.reshape(1, r), w_q, cs]
    for k, v in segs:
        in_specs.append(pl.BlockSpec((1,) + k.shape[1:], lambda i, j: (i, 0, 0)))
        in_specs.append(pl.BlockSpec((1,) + v.shape[1:], lambda i, j: (i, 0, 0)))
        args += [k, v]
    return pl.pallas_call(
        functools.partial(_attn_kernel, n_seg=len(segs)),
        grid=(b, l // tq),
        in_specs=in_specs,
        out_specs=pl.BlockSpec((1, tq, MLA_HEADS * MLA_V_DIM), lambda i, j: (i, j, 0)),
        out_shape=jax.ShapeDtypeStruct((b, l, MLA_HEADS * MLA_V_DIM), BF16),
        scratch_shapes=[pltpu.VMEM((tq, MLA_HEADS * QK_HEAD), BF16)],
        compiler_params=_cp("parallel", "parallel"),
        name="attention",
    )(*args)


HY_PAD = 8
HY_ROWS = 128


def _hy_short_kernel(p_ref, w_ref, b_ref, o_ref, up_ref):
    l = p_ref.shape[1]
    c = p_ref.shape[2]
    zeros = jnp.zeros((HY_PAD, c), F32)
    up_ref[0:HY_PAD, :] = zeros
    up_ref[HY_PAD + l:HY_PAD + l + HY_PAD, :] = zeros

    def copy(r, carry):
        base = pl.multiple_of(r * HY_ROWS, HY_ROWS)
        up_ref[pl.ds(base + HY_PAD, HY_ROWS), :] = p_ref[0, pl.ds(base, HY_ROWS), :].astype(F32)
        return carry

    lax.fori_loop(0, l // HY_ROWS, copy, 0)
    first = HY_PAD - HY_SHORT // 2

    def conv(r, carry):
        base = pl.multiple_of(r * HY_ROWS, HY_ROWS)
        for cc in range(c // LANES):
            lanes = slice(cc * LANES, (cc + 1) * LANES)
            win = up_ref[pl.ds(base, HY_ROWS + 2 * HY_PAD), lanes]
            acc = jnp.zeros((HY_ROWS, LANES), F32) + b_ref[:, lanes]
            for k in range(HY_SHORT):
                acc = acc + win[first + k:first + k + HY_ROWS, :] * w_ref[k:k + 1, lanes]
            o_ref[0, 0, pl.ds(base, HY_ROWS), lanes] = acc.astype(o_ref.dtype)
        return carry

    lax.fori_loop(0, l // HY_ROWS, conv, 0)


def _hy_short_call(p, w, bias):
    b, l, c3 = p.shape
    c = HYENA_W
    return pl.pallas_call(
        _hy_short_kernel,
        grid=(b, c3 // c),
        in_specs=[pl.BlockSpec((1, l, c), lambda i, g: (i, 0, g)),
                  pl.BlockSpec((HY_SHORT, c), lambda i, g: (0, g)),
                  pl.BlockSpec((1, c), lambda i, g: (0, g))],
        out_specs=pl.BlockSpec((1, 1, l, c), lambda i, g: (g, i, 0, 0)),
        out_shape=jax.ShapeDtypeStruct((c3 // c, b, l, c), BF16),
        scratch_shapes=[pltpu.VMEM((l + 2 * HY_PAD, c), F32)],
        compiler_params=_cp("parallel", "parallel"),
        name="hyena_short",
    )(p, w, bias.reshape(1, c3))


def _hy_filter_kernel(z_ref, w1_ref, b1_ref, f1_ref, w2_ref, b2_ref, f2_ref, w3_ref, b3_ref, win_ref, o_ref):
    hp = lax.Precision.HIGHEST

    def mm(a, b):
        return jnp.dot(a, b, preferred_element_type=F32, precision=hp)

    h = jnp.sin(f1_ref[...] * (mm(z_ref[...], w1_ref[...]) + b1_ref[...]))
    h = jnp.sin(f2_ref[...] * (mm(h, w2_ref[...]) + b2_ref[...]))
    h = mm(h, w3_ref[...]) + b3_ref[...]
    c = HYENA_W
    win = win_ref[...]
    for s in range(h.shape[1] // c):
        o_ref[:, s * c:(s + 1) * c] = h[:, s * c:(s + 1) * c] * win


def _pad2(a, rows, cols):
    return jnp.pad(a, ((0, rows - a.shape[0]), (0, cols - a.shape[1])))


def _hy_filter_call(l, t, w1, b1, f1, w2, b2, f2, w3, b3):
    t_norm = t / max(l - 1, 1)
    ang = (2.0 * math.pi / l) * t[:, None] * jnp.linspace(1e-4, HY_BANDS - 1, HY_BANDS, dtype=F32)[None, :]
    z = jnp.concatenate([t_norm[:, None], jnp.cos(ang), -jnp.sin(ang)], axis=-1)
    deltas = jnp.abs(jnp.linspace(HY_MIN_DECAY, HY_MAX_DECAY, HYENA_W, dtype=F32))
    window = jnp.exp(-t_norm[:, None] * deltas[None, :])
    n3 = w3.shape[1]
    by_dir = lambda a: a.reshape(a.shape[0], HY_ORDER, 2, HYENA_W).transpose(0, 2, 1, 3).reshape(a.shape[0], n3)
    w3, b3 = by_dir(w3), by_dir(b3[None])[0]
    l = t.shape[0]
    tl = min(l, 256)
    full = lambda shape: pl.BlockSpec(shape, lambda i: (0, 0))
    return pl.pallas_call(
        _hy_filter_kernel,
        grid=(l // tl,),
        in_specs=[pl.BlockSpec((tl, LANES), lambda i: (i, 0)),
                  full((LANES, LANES)), full((1, LANES)), full((1, LANES)),
                  full((LANES, LANES)), full((1, LANES)), full((1, LANES)),
                  full((LANES, n3)), full((1, n3)),
                  pl.BlockSpec((tl, HYENA_W), lambda i: (i, 0))],
        out_specs=pl.BlockSpec((tl, n3), lambda i: (i, 0)),
        out_shape=jax.ShapeDtypeStruct((l, n3), F32),
        compiler_params=_cp("parallel"),
        name="hyena_filter",
    )(_pad2(z, l, LANES), _pad2(w1, LANES, LANES), _pad2(b1[None], 1, LANES), _pad2(f1[None], 1, LANES),
      _pad2(w2, LANES, LANES), _pad2(b2[None], 1, LANES), _pad2(f2[None], 1, LANES),
      _pad2(w3, LANES, n3), b3[None], window)


def _dft_tables_kernel(ac_ref, as_ref, bc_ref, bs_ref, fwd_ref, inv_ref):
    tm, l = fwd_ref.shape[1], fwd_ref.shape[2]
    n = 2 * l
    row = lax.broadcasted_iota(jnp.int32, (tm, LANES), 0) + pl.program_id(0) * tm
    lane = lax.broadcasted_iota(jnp.int32, (tm, LANES), 1)
    row_sign = jnp.where((row & 1) == 0, 1.0, -1.0)
    lane_sign = jnp.where((lane & 1) == 0, 1.0, -1.0)
    bc = bc_ref[...]
    bs = bs_ref[...]
    for a in range(l // LANES):
        cols = slice(a * LANES, (a + 1) * LANES)
        ca = ac_ref[:, a:a + 1]
        sa = as_ref[:, a:a + 1]
        c = ca * bc - sa * bs
        s = sa * bc + ca * bs
        fwd_ref[0, :, cols] = c.astype(BF16)
        fwd_ref[1, :, cols] = jnp.where(row == 0, lane_sign, -s).astype(BF16)
        inv_c = c * (2.0 / n)
        inv_s = -s * (2.0 / n)
        if a == 0:
            inv_c = jnp.where(lane == 0, 1.0 / n, inv_c)
            inv_s = jnp.where(lane == 0, row_sign / n, inv_s)
        inv_ref[:, cols] = inv_c.astype(BF16)
        inv_ref[:, l + a * LANES:l + (a + 1) * LANES] = inv_s.astype(BF16)


def _dft_mats(l):
    n = 2 * l
    f = jnp.arange(l, dtype=jnp.int32)[:, None]

    def table(t):
        ang = ((f * t[None, :]) % n).astype(F32) * (2.0 * math.pi / n)
        return jnp.cos(ang), jnp.sin(ang)

    coarse = table(jnp.arange(l // LANES, dtype=jnp.int32) * LANES)
    fine = table(jnp.arange(LANES, dtype=jnp.int32))
    tm = min(l, 256)
    small = pl.BlockSpec((tm, l // LANES), lambda i: (i, 0))
    wide = pl.BlockSpec((tm, LANES), lambda i: (i, 0))
    return pl.pallas_call(
        _dft_tables_kernel,
        grid=(l // tm,),
        in_specs=[small, small, wide, wide],
        out_specs=[pl.BlockSpec((2, tm, l), lambda i: (0, i, 0)), pl.BlockSpec((tm, n), lambda i: (i, 0))],
        out_shape=[jax.ShapeDtypeStruct((2, l, l), BF16), jax.ShapeDtypeStruct((l, n), BF16)],
        compiler_params=_cp("parallel"),
        name="dft_tables",
    )(*coarse, *fine)


def _conv_blocks(l):
    return 2 if l >= 1024 else 1


def _lag_window_plan(l):
    nb = _conv_blocks(l)
    lb = l // nb
    pos = jnp.arange(l, dtype=F32)
    if nb == 1:
        return pos, jnp.array([0, 0, 0, 1], jnp.int32)
    pos = jnp.concatenate([pos, lb - jnp.arange(lb, dtype=F32)])
    return pos, jnp.array([2, 0, 1, 1, 0, 0, 1, 0, 2, 1, 1, 0], jnp.int32)


def _spec_kernel(tab_ref, f_ref, a_ref, b_ref, o_ref):
    a = a_ref[...]
    row0 = lax.broadcasted_iota(jnp.int32, a.shape, 0) == 0
    b = jnp.where(row0, 0.0, b_ref[...])
    s = (a + b).astype(BF16)
    hi = _dot(f_ref[1], (a - b).astype(BF16))
    o_ref[0, 0] = _dot(f_ref[0], s)
    o_ref[0, 1] = hi

    @pl.when(pl.program_id(1) == 0)
    def _():
        nyq = _dot(f_ref[1, 0:16, :], s)
        row = lax.broadcasted_iota(jnp.int32, nyq.shape, 0)
        o_ref[0, 1, 0:16, :] = jnp.where(row == 0, nyq, hi[0:16])


def _spec_call(fwd, filt, table):
    _, lf, lb = fwd.shape
    nd = table.shape[0] // 4
    n = filt.shape[1] // 2
    tm = min(lf, 512)
    pick = lambda k: pl.BlockSpec((lb, n), lambda d, i, tab: (tab[2 * k * nd + d], tab[(2 * k + 1) * nd + d]))
    return pl.pallas_call(
        _spec_kernel,
        grid_spec=pltpu.PrefetchScalarGridSpec(
            num_scalar_prefetch=1,
            grid=(nd, lf // tm),
            in_specs=[pl.BlockSpec((2, tm, lb), lambda d, i, tab: (0, i, 0)), pick(0), pick(1)],
            out_specs=pl.BlockSpec((1, 2, tm, n), lambda d, i, tab: (d, 0, i, 0))),
        out_shape=jax.ShapeDtypeStruct((nd, 2, lf, n), F32),
        compiler_params=_cp("arbitrary", "arbitrary"),
        name="hyena_spectrum",
    )(table, fwd, filt, filt)


def _dft_fwd_kernel(f_ref, u_ref, h_ref, y_ref, *, nb):
    lb = f_ref.shape[2]
    xr, xi = [], []
    for p in range(nb):
        up = u_ref[0, 0, p * lb:(p + 1) * lb, :].astype(BF16)
        xr.append(_dot(f_ref[0], up))
        xi.append(_dot(f_ref[1], up))
    def products(rows, real_bins):
        for q in range(nb):
            yr = 0.0
            yi = 0.0
            for p in range(nb):
                gr = h_ref[q - p + nb - 1, 0, rows, :]
                gi = h_ref[q - p + nb - 1, 1, rows, :]
                ii = xi[p][rows] * gi
                cross = xr[p][rows] * gi + xi[p][rows] * gr
                yr = yr + xr[p][rows] * gr - (ii if real_bins is None else jnp.where(real_bins, 0.0, ii))
                yi = yi + (cross if real_bins is None else jnp.where(real_bins, ii, cross))
            y_ref[0, q, 0, rows, :] = yr.astype(BF16)
            y_ref[0, q, 1, rows, :] = yi.astype(BF16)

    products(slice(None), None)

    @pl.when(pl.program_id(0) == 0)
    def _():
        head = slice(0, 16)
        products(head, lax.broadcasted_iota(jnp.int32, (16, u_ref.shape[3]), 0) == 0)


def _dft_fwd_call(fwd, u, group, spec, order, nb):
    _, b, l, c = u.shape
    _, lf, lb = fwd.shape
    nd = spec.shape[0]
    tm = min(lf, 512)
    return pl.pallas_call(
        functools.partial(_dft_fwd_kernel, nb=nb),
        grid=(lf // tm, b),
        in_specs=[pl.BlockSpec((2, tm, lb), lambda i, j: (0, i, 0)),
                  pl.BlockSpec((1, 1, l, c), lambda i, j: (group, j, 0, 0)),
                  pl.BlockSpec((nd, 2, tm, c), lambda i, j: (0, 0, i, order))],
        out_specs=pl.BlockSpec((1, nb, 2, tm, c), lambda i, j: (j, 0, 0, i, 0)),
        out_shape=jax.ShapeDtypeStruct((b, nb, 2, lf, c), BF16),
        compiler_params=_cp("arbitrary", "arbitrary"),
        name="hyena_dft_fwd",
    )(fwd, u, spec)


def _dft_inv_kernel(g_ref, y_ref, u_ref, b_ref, x_ref, o_ref):
    r = _dot(g_ref[...], y_ref[0, 0]) + u_ref[0, 0].astype(F32) * b_ref[...]
    o_ref[0, 0] = (x_ref[0, 0] * r).astype(o_ref.dtype)


def _dft_inv_call(inv, y, u, u_group, bias, gate, gate_group):
    _, b, l, c = u.shape
    lb, n = inv.shape
    nb = l // lb
    tm = min(lb, 1024)
    nt = lb // tm
    y = y.reshape(b, nb, n, c)
    row = lambda g: pl.BlockSpec((1, 1, tm, c), lambda q, i, j: (g, j, q * nt + i, 0))
    return pl.pallas_call(
        _dft_inv_kernel,
        grid=(nb, nt, b),
        in_specs=[pl.BlockSpec((tm, n), lambda q, i, j: (i, 0)),
                  pl.BlockSpec((1, 1, n, c), lambda q, i, j: (j, q, 0, 0)),
                  row(u_group), pl.BlockSpec((1, c), lambda q, i, j: (0, 0)), row(gate_group)],
        out_specs=row(0),
        out_shape=jax.ShapeDtypeStruct((1, b, l, c), BF16),
        compiler_params=_cp("parallel", "parallel", "parallel"),
        name="hyena_dft_inv",
    )(inv, y, u, bias.reshape(1, c), gate)


def _hyena_call(p_hy, sh_w, sh_b, spec, hy_bias, fwd, inv, nb):
    u3 = _hy_short_call(p_hy, sh_w, sh_b)
    z = _dft_inv_call(inv, _dft_fwd_call(fwd, u3, 0, spec, 0, nb), u3, 0, hy_bias[0], u3, 1)
    return _dft_inv_call(inv, _dft_fwd_call(fwd, z, 0, spec, 1, nb), z, 0, hy_bias[1], u3, 2)[0]


def _mix_out_kernel(c_ref, a_ref, h_ref, w_ref, x_ref, g1_ref, g_ref, sh_ref, sc_ref,
                    wr_ref, rb_ref, o_ref, h32_ref, idx_ref, gate_ref):
    k1 = c_ref.shape[2]
    k2 = k1 + a_ref.shape[2]
    acc = _dot(c_ref[0], w_ref[0, :k1, :]) + _dot(a_ref[0], w_ref[0, k1:k2, :]) + _dot(h_ref[0], w_ref[0, k2:, :])
    x = x_ref[0] + g1_ref[0] * acc
    o_ref[0] = x
    hm = _rms(x, g_ref[...]) * (1.0 + sc_ref[0]) + sh_ref[0]
    h32_ref[0] = _pack_rows(hm)
    picks, gates = _route(hm, wr_ref[...], rb_ref[...])
    idx_ref[...] = jnp.concatenate(picks, axis=0)
    gate_ref[...] = jnp.concatenate(gates, axis=0)


def _mix_out_call(conv_o, att_o, hy_o, w_out, layer, x, mod, g2, wr, rbias):
    b, l, d = x.shape
    tm = min(l, 512)
    nt = l // tm
    act = lambda k: pl.BlockSpec((1, tm, k), lambda i, j: (i, j, 0))
    row = pl.BlockSpec((1, 1, d), lambda i, j: (i, 0, 0))
    tok = pl.BlockSpec((2, tm), lambda i, j: (0, i * nt + j))
    return pl.pallas_call(
        _mix_out_kernel,
        grid=(b, nt),
        in_specs=[act(conv_o.shape[2]), act(att_o.shape[2]), act(hy_o.shape[2]),
                  pl.BlockSpec((1,) + w_out.shape[1:], lambda i, j: (layer, 0, 0), pipeline_mode=pl.Buffered(1)),
                  act(d), row, pl.BlockSpec((1, d), lambda i, j: (0, 0)), row, row,
                  pl.BlockSpec(wr.shape, lambda i, j: (0, 0)),
                  pl.BlockSpec((N_EXPERTS, 1), lambda i, j: (0, 0))],
        out_specs=[act(d), act(d // 2), tok, tok],
        out_shape=[jax.ShapeDtypeStruct((b, l, d), F32),
                   jax.ShapeDtypeStruct((b, l, d // 2), jnp.int32),
                   jax.ShapeDtypeStruct((2, b * l), jnp.int32),
                   jax.ShapeDtypeStruct((2, b * l), F32)],
        compiler_params=_cp("parallel", "parallel"),
        name="mix_out",
    )(conv_o, att_o, hy_o, w_out, x, mod[2], g2.reshape(1, d), mod[3], mod[4], wr,
      rbias.reshape(N_EXPERTS, 1))


def _route(hm, wr, rbias):
    hi = hm.astype(BF16)
    lo = (hm - hi.astype(F32)).astype(BF16)
    nt = (((1,), (1,)), ((), ()))
    r = (lax.dot_general(wr, hi, nt, preferred_element_type=F32)
         + lax.dot_general(wr, lo, nt, preferred_element_type=F32))
    logits = r[:N_EXPERTS] + r[N_EXPERTS:]
    score = jax.nn.sigmoid(logits)
    sel = score + rbias
    rows = [sel[e:e + 1] for e in range(N_EXPERTS)]
    gscore = []
    for g in range(N_GROUPS):
        a, b, c, d = rows[4 * g:4 * g + 4]
        h1, l1, h2, l2 = jnp.maximum(a, b), jnp.minimum(a, b), jnp.maximum(c, d), jnp.minimum(c, d)
        gscore.append(jnp.maximum(h1, h2) + jnp.maximum(jnp.minimum(h1, h2), jnp.maximum(l1, l2)))
    best = gscore[0]
    gbest = jnp.zeros(best.shape, jnp.int32)
    for g in range(1, N_GROUPS):
        upd = gscore[g] > best
        gbest = jnp.where(upd, g, gbest)
        best = jnp.where(upd, gscore[g], best)
    masked = [jnp.where(gbest == e // EXPERTS_PER_GROUP, rows[e], NEG_INF) for e in range(N_EXPERTS)]
    picks = []
    for _ in range(2):
        val = jnp.full(best.shape, NEG_INF, F32)
        idx = jnp.zeros(best.shape, jnp.int32)
        for e in range(N_EXPERTS):
            cand = masked[e]
            if picks:
                cand = jnp.where(picks[0] == e, NEG_INF, cand)
            upd = cand > val
            idx = jnp.where(upd, e, idx)
            val = jnp.where(upd, cand, val)
        picks.append(idx)
    gates = [functools.reduce(jnp.add, [jnp.where(p == e, score[e:e + 1], 0.0) for e in range(N_EXPERTS)])
             for p in picks]
    den = gates[0] + gates[1]
    return picks, [gates[0] / den, gates[1] / den]


def _expert_changed(te_ref, i):
    return (i == 0) | (te_ref[i] != te_ref[jnp.maximum(i - 1, 0)])


def _pack_rows(x):
    bits = pltpu.bitcast(x.astype(BF16).astype(F32), jnp.uint32)
    half = bits.shape[1] // 2
    return pltpu.bitcast((bits[:, :half] >> 16) | (bits[:, half:] & jnp.uint32(0xFFFF0000)), jnp.int32)


def _unpack_rows(words):
    w = pltpu.bitcast(words, jnp.uint32)
    return pltpu.bitcast(w << 16, F32), pltpu.bitcast(w & jnp.uint32(0xFFFF0000), F32)


def _moe_up_kernel(te_ref, tn_ref, nu_ref, x_ref, wg_hbm, wu_hbm, h_ref, gbuf, ubuf, wgb_ref, wub_ref, sems,
                   *, layer):
    j, i = pl.program_id(0), pl.program_id(1)
    tf = gbuf.shape[1]

    def fetch(e, chunk):
        cols = pl.ds(pl.multiple_of(chunk * tf, tf), tf)
        return (pltpu.make_async_copy(wg_hbm.at[layer, e, :, cols], gbuf, sems.at[0]),
                pltpu.make_async_copy(wu_hbm.at[layer, e, :, cols], ubuf, sems.at[1]))

    def start(e, chunk):
        for copy in fetch(e, chunk):
            copy.start()

    @pl.when((i == 0) & (j == 0))
    def _():
        start(te_ref[0], 0)

    @pl.when(_expert_changed(te_ref, i))
    def _():
        for copy in fetch(te_ref[i], j):
            copy.wait()
        wgb_ref[...] = gbuf[...].astype(BF16)
        wub_ref[...] = ubuf[...].astype(BF16)
        nxt = tn_ref[i]
        more = nxt < nu_ref[0]

        @pl.when(more)
        def _():
            start(te_ref[nxt], j)

        @pl.when(jnp.logical_not(more) & (j + 1 < pl.num_programs(0)))
        def _():
            start(te_ref[0], j + 1)

    @pl.when(i < nu_ref[0])
    def _():
        xa, xb = (v.astype(BF16) for v in _unpack_rows(x_ref[...]))
        half = xa.shape[1]
        a = _dot(xa, wgb_ref[:half, :]) + _dot(xb, wgb_ref[half:, :])
        u = _dot(xa, wub_ref[:half, :]) + _dot(xb, wub_ref[half:, :])
        h_ref[...] = (a * jax.nn.sigmoid(a) * u).astype(BF16)

    @pl.when(i >= nu_ref[0])
    def _():
        h_ref[...] = jnp.zeros(h_ref.shape, BF16)


def _moe_up_call(xs, w_gate, w_up, layer, te, tn, nu, tile):
    p, dw = xs.shape
    _, _, d, f = w_gate.shape
    tf = f
    hbm = pl.BlockSpec(memory_space=pl.ANY)
    return pl.pallas_call(
        functools.partial(_moe_up_kernel, layer=layer),
        grid_spec=pltpu.PrefetchScalarGridSpec(
            num_scalar_prefetch=3,
            grid=(f // tf, p // tile),
            in_specs=[pl.BlockSpec((tile, dw), lambda j, i, te, tn, nu: (i, 0)), hbm, hbm],
            out_specs=pl.BlockSpec((tile, tf), lambda j, i, te, tn, nu: (i, j)),
            scratch_shapes=[pltpu.VMEM((d, tf), F32), pltpu.VMEM((d, tf), F32),
                            pltpu.VMEM((d, tf), BF16), pltpu.VMEM((d, tf), BF16),
                            pltpu.SemaphoreType.DMA((2,))]),
        out_shape=jax.ShapeDtypeStruct((p, f), BF16),
        compiler_params=_cp("arbitrary", "arbitrary"),
        name="moe_up",
    )(te, tn, nu, xs, w_gate, w_up)


def _moe_down_kernel(te_ref, tn_ref, nu_ref, h_ref, wd_hbm, y_ref, wbuf, wdb_ref, sem, *, layer):
    i = pl.program_id(0)

    def fetch(e):
        return pltpu.make_async_copy(wd_hbm.at[layer, e], wbuf, sem)

    @pl.when(i == 0)
    def _():
        fetch(te_ref[0]).start()

    @pl.when(_expert_changed(te_ref, i))
    def _():
        fetch(te_ref[i]).wait()
        wdb_ref[...] = wbuf[...].astype(BF16)
        nxt = tn_ref[i]

        @pl.when(nxt < nu_ref[0])
        def _():
            fetch(te_ref[nxt]).start()

    @pl.when(i < nu_ref[0])
    def _():
        y_ref[...] = _pack_rows(_dot(h_ref[...], wdb_ref[...]))

    @pl.when(i >= nu_ref[0])
    def _():
        y_ref[...] = jnp.zeros(y_ref.shape, jnp.int32)


def _moe_down_call(h, w_down, layer, te, tn, nu, tile):
    p, f = h.shape
    d = w_down.shape[3]
    return pl.pallas_call(
        functools.partial(_moe_down_kernel, layer=layer),
        grid_spec=pltpu.PrefetchScalarGridSpec(
            num_scalar_prefetch=3,
            grid=(p // tile,),
            in_specs=[pl.BlockSpec((tile, f), lambda i, te, tn, nu: (i, 0)),
                      pl.BlockSpec(memory_space=pl.ANY)],
            out_specs=pl.BlockSpec((tile, d // 2), lambda i, te, tn, nu: (i, 0)),
            scratch_shapes=[pltpu.VMEM((f, d), F32), pltpu.VMEM((f, d), BF16), pltpu.SemaphoreType.DMA(())]),
        out_shape=jax.ShapeDtypeStruct((p, d // 2), jnp.int32),
        compiler_params=_cp("arbitrary"),
        name="moe_down",
    )(te, tn, nu, h, w_down)


PLAN_BLOCK = 512


def _plan_kernel(idx_ref, pos_ref, te_ref, tn_ref, nu_ref, rank_ref, *, tile):
    t = idx_ref.shape[1]
    nb = t // PLAN_BLOCK
    experts = lax.broadcasted_iota(jnp.int32, (N_EXPERTS, PLAN_BLOCK), 0)
    r = lax.broadcasted_iota(jnp.int32, (PLAN_BLOCK, PLAN_BLOCK), 0)
    c = lax.broadcasted_iota(jnp.int32, (PLAN_BLOCK, PLAN_BLOCK), 1)
    before = jnp.where(r < c, 1.0, 0.0).astype(BF16)

    def rank_pass(k):
        def body(j, carry):
            cols = pl.ds(pl.multiple_of(j * PLAN_BLOCK, PLAN_BLOCK), PLAN_BLOCK)
            onehot = jnp.where(experts == idx_ref[k:k + 1, cols], 1.0, 0.0)
            earlier = _dot(onehot.astype(BF16), before) + carry
            rank_ref[k:k + 1, cols] = jnp.sum(onehot * earlier, axis=0, keepdims=True)
            return carry + jnp.sum(onehot, axis=1, keepdims=True)
        return body

    counts = jnp.zeros((N_EXPERTS, 1), F32)
    for k in range(2):
        counts = lax.fori_loop(0, nb, rank_pass(k), counts)
    padded = jnp.ceil(counts * (1.0 / tile)) * tile
    starts, ends = [], []
    acc = jnp.zeros((1, 1), F32)
    for e in range(N_EXPERTS):
        starts.append(acc)
        acc = acc + padded[e:e + 1]
        ends.append(acc)

    def place_pass(k):
        def body(j, carry):
            cols = pl.ds(pl.multiple_of(j * PLAN_BLOCK, PLAN_BLOCK), PLAN_BLOCK)
            row = idx_ref[k:k + 1, cols]
            start = functools.reduce(jnp.add, [jnp.where(row == e, starts[e], 0.0) for e in range(N_EXPERTS)])
            pos_ref[k:k + 1, cols] = (rank_ref[k:k + 1, cols] + start).astype(jnp.int32)
            return carry
        return body

    for k in range(2):
        lax.fori_loop(0, nb, place_pass(k), 0)
    n_used = acc * (1.0 / tile)
    tiles = lax.broadcasted_iota(jnp.int32, te_ref.shape, 1).astype(F32)
    first_row = jnp.minimum(tiles, n_used - 1.0) * tile
    te = functools.reduce(jnp.add, [jnp.where(ends[e] <= first_row, 1.0, 0.0) for e in range(N_EXPERTS)])
    te_ref[...] = te.astype(jnp.int32)
    seg_end = functools.reduce(jnp.add, [jnp.where(te == e, ends[e], 0.0) for e in range(N_EXPERTS)])
    tn_ref[...] = (seg_end * (1.0 / tile)).astype(jnp.int32)
    nu_ref[...] = jnp.broadcast_to(n_used, nu_ref.shape).astype(jnp.int32)


def _plan_call(idx, tile, n_tiles):
    t = idx.shape[1]
    whole = lambda shape: pl.BlockSpec(shape, lambda: (0,) * len(shape))
    lanes = jax.ShapeDtypeStruct((1, LANES), jnp.int32)
    pos, te, tn, nu = pl.pallas_call(
        functools.partial(_plan_kernel, tile=tile),
        in_specs=[whole(idx.shape)],
        out_specs=[whole(idx.shape), whole((1, LANES)), whole((1, LANES)), whole((1, LANES))],
        out_shape=[jax.ShapeDtypeStruct(idx.shape, jnp.int32), lanes, lanes, lanes],
        scratch_shapes=[pltpu.VMEM(idx.shape, F32)],
        name="route_plan",
    )(idx)
    return pos.reshape(2 * t), te[0, :n_tiles], tn[0, :n_tiles], nu[0, :1]


V7X_SC_CORES = 2
V7X_SC_SUBCORES = 16
SC_WORKERS = V7X_SC_CORES * V7X_SC_SUBCORES
SC_ROW_BYTES = 256 * 1024


def _sc_worker():
    return lax.axis_index("s") * V7X_SC_CORES + lax.axis_index("c")


def _sc_dispatch_call(streams, pos, n_rows):
    w = streams[0].shape[1]
    sizes = [s.shape[0] for s in streams]
    t = sum(sizes)
    per_w = t // SC_WORKERS
    ch = min(per_w, SC_ROW_BYTES // (4 * w))
    assert per_w % ch == 0 and all(size % ch == 0 for size in sizes)
    mesh = plsc.VectorSubcoreMesh(core_axis_name="c", subcore_axis_name="s")

    @functools.partial(pl.kernel, mesh=mesh, out_type=jax.ShapeDtypeStruct((n_rows, w), jnp.int32),
                       scratch_types=[pltpu.VMEM((ch,), jnp.int32), pltpu.VMEM((ch,), jnp.int32),
                                      pltpu.VMEM((ch, w), jnp.int32),
                                      pltpu.SemaphoreType.DMA, pltpu.SemaphoreType.DMA],
                       name="sc_dispatch")
    def run(*refs):
        srcs = refs[:len(sizes)]
        pos_hbm, xs_hbm, idx0_v, idx1_v, rows_v, sem0, sem1 = refs[len(sizes):]
        first = _sc_worker() * per_w

        @pl.loop(0, per_w // ch)
        def _(j):
            base = first + j * ch
            start = 0
            for src, size in zip(srcs, sizes):
                def load(src=src, start=start):
                    pltpu.sync_copy(src.at[pl.ds(base - start, ch)], rows_v)
                if len(sizes) == 1:
                    load()
                else:
                    pl.when((base >= start) & (base < start + size))(load)
                start += size
            pltpu.sync_copy(pos_hbm.at[pl.ds(base, ch)], idx0_v)
            pltpu.sync_copy(pos_hbm.at[pl.ds(t + base, ch)], idx1_v)
            first_choice = pltpu.async_copy(rows_v, xs_hbm.at[idx0_v], sem0)
            second_choice = pltpu.async_copy(rows_v, xs_hbm.at[idx1_v], sem1)
            first_choice.wait()
            second_choice.wait()

    return run(*streams, pos)


def _sc_combine_call(ys, pos):
    _, d = ys.shape
    n = pos.shape[0]
    per_w = n // SC_WORKERS
    ch = min(per_w // 2, SC_ROW_BYTES // (8 * d))
    assert per_w % (2 * ch) == 0
    mesh = plsc.VectorSubcoreMesh(core_axis_name="c", subcore_axis_name="s")
    idx_buf = pltpu.VMEM((ch,), jnp.int32)
    row_buf = pltpu.VMEM((ch, d), jnp.int32)

    @functools.partial(pl.kernel, mesh=mesh, out_type=jax.ShapeDtypeStruct((n, d), jnp.int32),
                       scratch_types=[idx_buf, idx_buf, row_buf, row_buf] + [pltpu.SemaphoreType.DMA] * 4,
                       name="sc_combine")
    def run(ys_hbm, pos_hbm, out_hbm, idx_a, idx_b, rows_a, rows_b, gsem_a, gsem_b, wsem_a, wsem_b):
        first = _sc_worker() * per_w

        @pl.loop(0, per_w // (2 * ch))
        def _(j):
            base = first + j * (2 * ch)
            pltpu.sync_copy(pos_hbm.at[pl.ds(base, ch)], idx_a)
            pltpu.sync_copy(pos_hbm.at[pl.ds(base + ch, ch)], idx_b)
            gather_a = pltpu.async_copy(ys_hbm.at[idx_a], rows_a, gsem_a)
            gather_b = pltpu.async_copy(ys_hbm.at[idx_b], rows_b, gsem_b)
            gather_a.wait()
            write_a = pltpu.async_copy(rows_a, out_hbm.at[pl.ds(base, ch)], wsem_a)
            gather_b.wait()
            write_b = pltpu.async_copy(rows_b, out_hbm.at[pl.ds(base + ch, ch)], wsem_b)
            write_a.wait()
            write_b.wait()

    return run(ys, pos)


def _moe_call(streams, idx, w_gate, w_up, w_down, layer):
    t = idx.shape[1]
    tile = min(MOE_TILE, max(t // N_EXPERTS, 128))
    n_tiles = (2 * t) // tile + N_EXPERTS
    assert n_tiles <= LANES
    pos, te, tn, nu = _plan_call(idx, tile, n_tiles)
    xs = _sc_dispatch_call(streams, pos, n_tiles * tile)
    hid = _moe_up_call(xs, w_gate, w_up, layer, te, tn, nu, tile)
    ys = _moe_down_call(hid, w_down, layer, te, tn, nu, tile)
    return _sc_combine_call(ys, pos).reshape(2, t, ys.shape[1])


def _moe_resid_kernel(x_ref, y0_ref, y1_ref, gt_ref, g_ref, *rest, final):
    gt = gt_ref[0]
    halves0 = _unpack_rows(y0_ref[0])
    halves1 = _unpack_rows(y1_ref[0])
    y = jnp.concatenate([gt[:, 0:1] * a + gt[:, 1:2] * b for a, b in zip(halves0, halves1)], axis=1)
    x = x_ref[0] + g_ref[0] * y
    if final:
        fg_ref, o_ref = rest
        x = _rms(x, fg_ref[...])
    else:
        (o_ref,) = rest
    o_ref[0] = x


def _moe_resid_call(x, y, first_row, gates, gate, final_g=None):
    b, l, d = x.shape
    tl = min(l, 512)
    nt = l // tl
    tile = pl.BlockSpec((1, tl, d), lambda i, j: (i, j, 0))
    choice = lambda k: pl.BlockSpec((1, tl, d // 2), lambda i, j: (k, first_row // tl + i * nt + j, 0))
    in_specs = [tile, choice(0), choice(1), pl.BlockSpec((1, tl, 2), lambda i, j: (i, j, 0)),
                pl.BlockSpec((1, 1, d), lambda i, j: (i, 0, 0))]
    args = [x, y, y, gates.T.reshape(b, l, 2), gate]
    if final_g is not None:
        in_specs.append(pl.BlockSpec((1, d), lambda i, j: (0, 0)))
        args.append(final_g.reshape(1, d))
    return pl.pallas_call(
        functools.partial(_moe_resid_kernel, final=final_g is not None),
        grid=(b, l // tl),
        in_specs=in_specs,
        out_specs=tile,
        out_shape=jax.ShapeDtypeStruct((b, l, d), F32),
        compiler_params=_cp("parallel", "parallel"),
        name="moe_residual",
    )(*args)


def _rope_perm(w):
    r = w.reshape(w.shape[:-1] + (2, 2, ROPE_FREQS))
    return jnp.stack([-r[..., 1, :], r[..., 0, :]], axis=-2).reshape(w.shape)


def _rope_table(l, rotary):
    if not rotary:
        return jnp.concatenate([jnp.ones((l, MLA_ROPE), F32), jnp.zeros((l, MLA_ROPE), F32)], axis=1)
    rows = l // GRID_W
    r = jnp.repeat(jnp.arange(rows), GRID_W)
    col = jnp.tile(jnp.arange(GRID_W), rows)
    pos = jnp.stack([r, col], axis=-1).astype(F32)
    inv = ROPE_THETA ** (-jnp.arange(ROPE_FREQS, dtype=F32) / ROPE_FREQS)
    ang = pos[:, :, None] * inv
    lay = lambda a: jnp.concatenate([a[:, 0], a[:, 0], a[:, 1], a[:, 1]], axis=1)
    return jnp.concatenate([lay(jnp.cos(ang)), lay(jnp.sin(ang))], axis=1)


def _prep_weights(w_in, w_uq, w_ukv, w_out):
    depth = w_in.shape[0]
    cast = lambda a: a.astype(BF16)
    kr = cast(w_in[:, :, OFF_KR:OFF_HY])
    w_kv_in = jnp.concatenate([cast(w_in[:, :, OFF_KV:OFF_KR]), kr, _rope_perm(kr)], axis=2)
    w_cq, w_hy = cast(w_in[:, :, :OFF_KV]), cast(w_in[:, :, OFF_HY:])
    uq = cast(w_uq).reshape(depth, MLA_Q_RANK, MLA_HEADS, MLA_NOPE + MLA_ROPE)
    w_q = jnp.concatenate([uq, _rope_perm(uq[..., MLA_NOPE:])], axis=3).reshape(depth, MLA_Q_RANK, -1)
    ukv = cast(w_ukv).reshape(depth, MLA_KV_RANK, MLA_HEADS, MLA_NOPE + MLA_V_DIM)
    w_k = ukv[..., :MLA_NOPE].reshape(depth, MLA_KV_RANK, MLA_HEADS * MLA_NOPE)
    w_v = ukv[..., MLA_NOPE:].reshape(depth, MLA_KV_RANK, MLA_HEADS * MLA_V_DIM)
    return dict(cq=w_cq, hy=w_hy, kv=w_kv_in, uq=w_q, uk=w_k, uv=w_v, out=cast(w_out))


def _router_weights(w_router):
    hi = w_router.astype(BF16)
    lo = (w_router - hi.astype(F32)).astype(BF16)
    return jnp.concatenate([hi.T, lo.T], axis=0)


def _mixer(x, mod, w, layer, lp, cs, spec, fwd, inv, kv_extra, full, wr, rbias):
    kv_args = (w["kv"], lp["kv_norm_g"], w["uk"], w["uv"], cs, layer)
    if not full:
        k, v = _in_proj_call(x, lp["norm1_g"], mod[0], mod[1], [], *kv_args)
        return None, k, v
    proj = [(w["cq"], [OFF_Q, MLA_Q_RANK]), (w["hy"], [(HY_ORDER + 1) * HYENA_W])]
    p_conv, p_q, p_hy, k, v = _in_proj_call(x, lp["norm1_g"], mod[0], mod[1], proj, *kv_args)
    conv_o = _conformer_call(p_conv, lp["conv_dw_w"], lp["conv_dw_b"], lp["conv_ln_g"], lp["conv_ln_b"])
    att_o = _attn_call(p_q, lp["q_norm_g"], w["uq"], cs, kv_extra + [(k, v)], layer)
    hy_o = _hyena_call(p_hy, lp["hy_short_w"], lp["hy_short_b"], spec, lp["hy_bias"], fwd, inv,
                       _conv_blocks(x.shape[1]))
    routed = _mix_out_call(conv_o, att_o, hy_o, w["out"], layer, x, mod, lp["norm2_g"], wr, rbias)
    return routed, k, v


def _channel(streams, experts, layer, final_g=None):
    d = streams[0][1][0].shape[2]
    idx = jnp.concatenate([r[2] for _, r in streams], axis=1)
    y = _moe_call([r[1].reshape(-1, d // 2) for _, r in streams], idx, *experts, layer)
    outs = []
    first_row = 0
    for mod, (x, _, _, gates) in streams:
        outs.append(_moe_resid_call(x, y, first_row, gates, mod[5], final_g))
        first_row += x.shape[0] * x.shape[1]
    return outs


def _filter_spectrum(l, lp, fwd):
    pos, table = _lag_window_plan(l)
    filt = _hy_filter_call(l, pos, lp["hy_w1"], lp["hy_b1"], lp["hy_freq1"], lp["hy_w2"], lp["hy_b2"],
                           lp["hy_freq2"], lp["hy_w3"], lp["hy_b3"])
    return _spec_call(fwd, filt, table)


def kernel(x, c, ctx, c_ctx, norm1_g, norm2_g, w_ada, b_ada, w_in, conv_dw_w, conv_dw_b, conv_ln_g, conv_ln_b, q_norm_g, w_uq, kv_norm_g, w_ukv, hy_short_w, hy_short_b, hy_w1, hy_b1, hy_freq1, hy_w2, hy_b2, hy_freq2, hy_w3, hy_b3, hy_bias, w_out, w_router, router_bias, w_gate, w_up, w_down, final_norm_g):
    b, l, d = x.shape
    lc = ctx.shape[1]
    depth = w_in.shape[0]
    per_layer = dict(norm1_g=norm1_g, norm2_g=norm2_g, conv_dw_w=conv_dw_w, conv_dw_b=conv_dw_b,
                     conv_ln_g=conv_ln_g, conv_ln_b=conv_ln_b, q_norm_g=q_norm_g, kv_norm_g=kv_norm_g,
                     hy_short_w=hy_short_w, hy_short_b=hy_short_b, hy_w1=hy_w1, hy_b1=hy_b1, hy_freq1=hy_freq1,
                     hy_w2=hy_w2, hy_b2=hy_b2, hy_freq2=hy_freq2, hy_w3=hy_w3, hy_b3=hy_b3, hy_bias=hy_bias)
    experts = (w_gate, w_up, w_down)
    cs_l = _rope_table(l, True)
    cs_c = _rope_table(lc, False)
    fwd_l, inv_l = _dft_mats(l // _conv_blocks(l))
    fwd_c, inv_c = _dft_mats(lc // _conv_blocks(lc))
    wr = _router_weights(w_router)
    w = _prep_weights(w_in, w_uq, w_ukv, w_out)
    rows = 16
    cvec = jnp.concatenate([c, c_ctx[None, :], jnp.zeros((rows - b - 1, d), F32)], axis=0)
    xl, xc = x, ctx
    for i in range(depth):
        last = i == depth - 1
        lp = {k: v[i] for k, v in per_layer.items()}
        m = _ada_call(cvec, w_ada, b_ada, i)
        mod_l = [m[:b, None, j * d:(j + 1) * d] for j in range(6)]
        mod_c = [jnp.broadcast_to(m[b:b + 1, None, j * d:(j + 1) * d], (b, 1, d)) for j in range(6)]
        spec_l = _filter_spectrum(l, lp, fwd_l)
        spec_c = None if last else _filter_spectrum(lc, lp, fwd_c)
        routed_c, kc, vc = _mixer(xc, mod_c, w, i, lp, cs_c, spec_c, fwd_c, inv_c, [], not last, wr, router_bias)
        routed_l, _, _ = _mixer(xl, mod_l, w, i, lp, cs_l, spec_l, fwd_l, inv_l, [(kc, vc)], True, wr, router_bias)
        if last:
            (xl,) = _channel([(mod_l, routed_l)], experts, i, final_norm_g)
        else:
            xl, xc = _channel([(mod_l, routed_l), (mod_c, routed_c)], experts, i)
    return xl
```

```python
import functools
import math

import jax
import jax.numpy as jnp
from jax import lax
from jax.experimental import pallas as pl
from jax.experimental.pallas import tpu as pltpu
from jax.experimental.pallas import tpu_sc as plsc

F32 = jnp.float32
BF16 = jnp.bfloat16

EPS = 1e-6
GRID_W = 64
CONV_W = 512
CONV_KSIZE = 31
MLA_HEADS = 8
MLA_NOPE = 128
MLA_ROPE = 64
MLA_V_DIM = 128
MLA_Q_RANK = 768
MLA_KV_RANK = 512
ROPE_FREQS = MLA_ROPE // 4
ROPE_THETA = 10000.0
MLA_SCALE = (MLA_NOPE + MLA_ROPE) ** -0.5
Q_SCALE = MLA_SCALE * math.log2(math.e)
HYENA_W = 512
HY_ORDER = 2
HY_SHORT = 3
HY_EMB = 33
HY_BANDS = (HY_EMB - 1) // 2
HY_MIN_DECAY = math.log(1e-2) / 1.5
HY_MAX_DECAY = math.log(1e-2) / 0.3
N_EXPERTS = 16
N_GROUPS = 4
EXPERTS_PER_GROUP = N_EXPERTS // N_GROUPS
OFF_Q = 2 * CONV_W
OFF_KV = OFF_Q + MLA_Q_RANK
OFF_KR = OFF_KV + MLA_KV_RANK
OFF_HY = OFF_KR + MLA_ROPE

LANES = 128
QK_HEAD = 2 * LANES
V_HEAD = MLA_V_DIM + LANES
V7X_VMEM_BYTES = 64 * 1024 * 1024
VMEM_LIMIT = (V7X_VMEM_BYTES * 7) // 8
MOE_TILE = 512
NEG_INF = float("-inf")


def _cp(*sem):
    return pltpu.CompilerParams(dimension_semantics=sem, vmem_limit_bytes=VMEM_LIMIT)


def _dot(a, b):
    return jnp.dot(a, b, preferred_element_type=F32)


def _rms(x, g):
    ms = jnp.mean(x * x, axis=-1, keepdims=True)
    return x * lax.rsqrt(ms + EPS) * g


def _ada_kernel(c_ref, w_ref, b_ref, o_ref):
    c = c_ref[...]
    a = (c * jax.nn.sigmoid(c)).astype(BF16)
    o_ref[...] = _dot(a, w_ref[0].astype(BF16)) + b_ref[0]


def _ada_call(cvec, w, b, layer):
    m, d = cvec.shape
    n = w.shape[2]
    tn = 1024
    return pl.pallas_call(
        _ada_kernel,
        grid=(n // tn,),
        in_specs=[pl.BlockSpec((m, d), lambda j: (0, 0)),
                  pl.BlockSpec((1, d, tn), lambda j: (layer, 0, j)),
                  pl.BlockSpec((1, 1, tn), lambda j: (layer, 0, j))],
        out_specs=pl.BlockSpec((m, tn), lambda j: (0, j)),
        out_shape=jax.ShapeDtypeStruct((m, n), F32),
        compiler_params=_cp("parallel"),
        name="ada",
    )(cvec, w, b.reshape(b.shape[0], 1, n))


def _in_proj_kernel(x_ref, g_ref, sh_ref, sc_ref, *refs, n_proj):
    w_refs, (wkv_ref, kg_ref, wk_ref, wv_ref, cs_ref), o_refs = refs[:n_proj], refs[n_proj:n_proj + 5], refs[n_proj + 5:]
    *p_refs, k_ref, v_ref = o_refs
    y = _rms(x_ref[0], g_ref[...])
    h = (y * (1.0 + sc_ref[0]) + sh_ref[0]).astype(BF16)
    outs = iter(p_refs)
    for w_ref in w_refs:
        off = 0
        while off < w_ref.shape[2]:
            o_ref = next(outs)
            n = o_ref.shape[-1]
            o_ref[0] = _dot(h, w_ref[0, :, off:off + n]).astype(o_ref.dtype)
            off += n
    p = _dot(h, wkv_ref[0])
    xn = _rms(p[:, :MLA_KV_RANK], kg_ref[...]).astype(BF16)
    kk = _dot(xn, wk_ref[0])
    vv = _dot(xn, wv_ref[0])
    rot = _rotate(p[:, MLA_KV_RANK:], cs_ref[...])
    lane = lax.broadcasted_iota(jnp.int32, rot.shape, 1)
    rot = jnp.where(lane < MLA_ROPE, rot, 0.0).astype(BF16)
    ones = jnp.where(lane == 0, 1.0, 0.0).astype(BF16)
    for hd in range(MLA_HEADS):
        k_ref[0, :, hd * QK_HEAD:hd * QK_HEAD + LANES] = kk[:, hd * LANES:(hd + 1) * LANES].astype(BF16)
        k_ref[0, :, hd * QK_HEAD + LANES:(hd + 1) * QK_HEAD] = rot
        v_ref[0, :, hd * V_HEAD:hd * V_HEAD + MLA_V_DIM] = vv[:, hd * MLA_V_DIM:(hd + 1) * MLA_V_DIM].astype(BF16)
        v_ref[0, :, hd * V_HEAD + MLA_V_DIM:(hd + 1) * V_HEAD] = ones


def _in_proj_call(x, g, shift, scale, proj, w_kv, kv_g, w_k, w_v, cs, layer):
    b, l, d = x.shape
    tm = min(l, 512)
    r = MLA_KV_RANK
    row = pl.BlockSpec((1, 1, d), lambda i, j: (i, 0, 0))
    const = lambda shape: pl.BlockSpec(shape, lambda i, j: (0, 0))
    of_layer = lambda a, **kw: pl.BlockSpec((1,) + a.shape[1:], lambda i, j: (layer, 0, 0), **kw)
    weights = [w for w, _ in proj]
    widths = [k for _, ks in proj for k in ks] + [MLA_HEADS * QK_HEAD, MLA_HEADS * V_HEAD]
    return pl.pallas_call(
        functools.partial(_in_proj_kernel, n_proj=len(weights)),
        grid=(b, l // tm),
        in_specs=[pl.BlockSpec((1, tm, d), lambda i, j: (i, j, 0)),
                  const((1, d)), row, row,
                  *[of_layer(w, pipeline_mode=pl.Buffered(1)) for w in weights],
                  of_layer(w_kv, pipeline_mode=pl.Buffered(1)),
                  const((1, r)), of_layer(w_k), of_layer(w_v),
                  pl.BlockSpec((tm, LANES), lambda i, j: (j, 0))],
        out_specs=[pl.BlockSpec((1, tm, k), lambda i, j: (i, j, 0)) for k in widths],
        out_shape=[jax.ShapeDtypeStruct((b, l, k), BF16) for k in widths],
        compiler_params=_cp("parallel", "parallel"),
        name="in_proj",
    )(x, g.reshape(1, d), shift, scale, *weights, w_kv, kv_g.reshape(1, r), w_k, w_v, cs)


CONV_ROWS = 128
CONV_PAD = 16


def _conformer_kernel(p_ref, w_ref, b_ref, g_ref, be_ref, o_ref, zp_ref):
    l = p_ref.shape[1]
    c = CONV_W
    zeros = jnp.zeros((CONV_PAD, c), F32)
    zp_ref[0:CONV_PAD, :] = zeros
    zp_ref[CONV_PAD + l:CONV_PAD + l + CONV_PAD, :] = zeros

    def glu(r, carry):
        base = pl.multiple_of(r * CONV_ROWS, CONV_ROWS)
        blk = p_ref[0, pl.ds(base, CONV_ROWS), :].astype(F32)
        zp_ref[pl.ds(base + CONV_PAD, CONV_ROWS), :] = blk[:, :c] * jax.nn.sigmoid(blk[:, c:])
        return carry

    lax.fori_loop(0, l // CONV_ROWS, glu, 0)

    win_rows = CONV_ROWS + 2 * CONV_PAD
    first = CONV_PAD - CONV_KSIZE // 2

    def conv(r, carry):
        base = pl.multiple_of(r * CONV_ROWS, CONV_ROWS)
        parts = []
        for cc in range(c // LANES):
            lanes = slice(cc * LANES, (cc + 1) * LANES)
            win = zp_ref[pl.ds(base, win_rows), lanes]
            acc = jnp.zeros((CONV_ROWS, LANES), F32)
            for res in range(8):
                taps = [k for k in range(CONV_KSIZE) if (k + first) % 8 == res]
                if not taps:
                    continue
                shifted = pltpu.roll(win, win_rows - res, axis=0) if res else win
                for k in taps:
                    off = k + first - res
                    acc = acc + shifted[off:off + CONV_ROWS, :] * w_ref[k:k + 1, lanes]
            parts.append(acc)
        z = jnp.concatenate(parts, axis=1) + b_ref[...]
        mu = jnp.mean(z, axis=-1, keepdims=True)
        zc = z - mu
        var = jnp.mean(zc * zc, axis=-1, keepdims=True)
        y = zc * lax.rsqrt(var + EPS) * g_ref[...] + be_ref[...]
        o_ref[0, pl.ds(base, CONV_ROWS), :] = (y * jax.nn.sigmoid(y)).astype(o_ref.dtype)
        return carry

    lax.fori_loop(0, l // CONV_ROWS, conv, 0)


def _conformer_call(p, dw_w, dw_b, ln_g, ln_b):
    b, l, c2 = p.shape
    c = c2 // 2
    vec = pl.BlockSpec((1, c), lambda i: (0, 0))
    return pl.pallas_call(
        _conformer_kernel,
        grid=(b,),
        in_specs=[pl.BlockSpec((1, l, c2), lambda i: (i, 0, 0)),
                  pl.BlockSpec((CONV_KSIZE, c), lambda i: (0, 0)), vec, vec, vec],
        out_specs=pl.BlockSpec((1, l, c), lambda i: (i, 0, 0)),
        out_shape=jax.ShapeDtypeStruct((b, l, c), BF16),
        scratch_shapes=[pltpu.VMEM((l + 2 * CONV_PAD, c), F32)],
        compiler_params=_cp("parallel"),
        name="conformer",
    )(p, dw_w, dw_b.reshape(1, c), ln_g.reshape(1, c), ln_b.reshape(1, c))


def _rotate(group, cs):
    t = group * cs
    return t + pltpu.roll(t, MLA_ROPE, axis=1)


def _attn_kernel(p_ref, g_ref, w_ref, cs_ref, *refs, n_seg):
    o_ref, q_ref = refs[2 * n_seg], refs[2 * n_seg + 1]
    xn = _rms(p_ref[0].astype(F32), g_ref[...]).astype(BF16)
    cs = cs_ref[...] * Q_SCALE
    for h in range(MLA_HEADS):
        r = _dot(xn, w_ref[0, :, h * QK_HEAD:(h + 1) * QK_HEAD])
        q_ref[:, h * QK_HEAD:h * QK_HEAD + LANES] = (r[:, :LANES] * Q_SCALE).astype(BF16)
        q_ref[:, h * QK_HEAD + LANES:(h + 1) * QK_HEAD] = _rotate(r[:, LANES:], cs).astype(BF16)
    nt = (((1,), (1,)), ((), ()))
    for h in range(MLA_HEADS):
        qk = slice(h * QK_HEAD, (h + 1) * QK_HEAD)
        vd = slice(h * MLA_V_DIM, (h + 1) * MLA_V_DIM)
        q = q_ref[:, qk]
        scores = [lax.dot_general(q, refs[2 * s][0, :, qk], nt, preferred_element_type=F32).astype(BF16)
                  for s in range(n_seg)]
        m = functools.reduce(jnp.maximum, [jnp.max(s, axis=-1, keepdims=True) for s in scores])
        acc = 0.0
        for s in range(n_seg):
            p = jnp.exp2(scores[s] - m)
            acc = acc + _dot(p, refs[2 * s + 1][0, :, h * V_HEAD:(h + 1) * V_HEAD])
        o_ref[0, :, vd] = (acc[:, :MLA_V_DIM] * (1.0 / acc[:, MLA_V_DIM:MLA_V_DIM + 1])).astype(o_ref.dtype)


def _attn_call(p_q, q_g, w_q, cs, segs, layer):
    b, l, r = p_q.shape
    tq = min(l, 512)
    const = lambda shape: pl.BlockSpec(shape, lambda i, j: (0, 0))
    in_specs = [pl.BlockSpec((1, tq, r), lambda i, j: (i, j, 0)), const((1, r)),
                pl.BlockSpec((1,) + w_q.shape[1:], lambda i, j: (layer, 0, 0)),
                pl.BlockSpec((tq, LANES), lambda i, j: (j, 0))]
    args = [p_q, q_g.reshape(1, r), w_q, cs]
    for k, v in segs:
        in_specs.append(pl.BlockSpec((1,) + k.shape[1:], lambda i, j: (i, 0, 0)))
        in_specs.append(pl.BlockSpec((1,) + v.shape[1:], lambda i, j: (i, 0, 0)))
        args += [k, v]
    return pl.pallas_call(
        functools.partial(_attn_kernel, n_seg=len(segs)),
        grid=(b, l // tq),
        in_specs=in_specs,
        out_specs=pl.BlockSpec((1, tq, MLA_HEADS * MLA_V_DIM), lambda i, j: (i, j, 0)),
        out_shape=jax.ShapeDtypeStruct((b, l, MLA_HEADS * MLA_V_DIM), BF16),
        scratch_shapes=[pltpu.VMEM((tq, MLA_HEADS * QK_HEAD), BF16)],
        compiler_params=_cp("parallel", "parallel"),
        name="attention",
    )(*args)


HY_PAD = 8
HY_ROWS = 128


def _hy_short_kernel(p_ref, w_ref, b_ref, o_ref, up_ref):
    l = p_ref.shape[1]
    c = p_ref.shape[2]
    zeros = jnp.zeros((HY_PAD, c), F32)
    up_ref[0:HY_PAD, :] = zeros
    up_ref[HY_PAD + l:HY_PAD + l + HY_PAD, :] = zeros

    def copy(r, carry):
        base = pl.multiple_of(r * HY_ROWS, HY_ROWS)
        up_ref[pl.ds(base + HY_PAD, HY_ROWS), :] = p_ref[0, pl.ds(base, HY_ROWS), :].astype(F32)
        return carry

    lax.fori_loop(0, l // HY_ROWS, copy, 0)
    first = HY_PAD - HY_SHORT // 2

    def conv(r, carry):
        base = pl.multiple_of(r * HY_ROWS, HY_ROWS)
        for cc in range(c // LANES):
            lanes = slice(cc * LANES, (cc + 1) * LANES)
            win = up_ref[pl.ds(base, HY_ROWS + 2 * HY_PAD), lanes]
            acc = jnp.zeros((HY_ROWS, LANES), F32) + b_ref[:, lanes]
            for k in range(HY_SHORT):
                acc = acc + win[first + k:first + k + HY_ROWS, :] * w_ref[k:k + 1, lanes]
            o_ref[0, 0, pl.ds(base, HY_ROWS), lanes] = acc.astype(o_ref.dtype)
        return carry

    lax.fori_loop(0, l // HY_ROWS, conv, 0)


def _hy_short_call(p, w, bias):
    b, l, c3 = p.shape
    c = HYENA_W
    return pl.pallas_call(
        _hy_short_kernel,
        grid=(b, c3 // c),
        in_specs=[pl.BlockSpec((1, l, c), lambda i, g: (i, 0, g)),
                  pl.BlockSpec((HY_SHORT, c), lambda i, g: (0, g)),
                  pl.BlockSpec((1, c), lambda i, g: (0, g))],
        out_specs=pl.BlockSpec((1, 1, l, c), lambda i, g: (g, i, 0, 0)),
        out_shape=jax.ShapeDtypeStruct((c3 // c, b, l, c), BF16),
        scratch_shapes=[pltpu.VMEM((l + 2 * HY_PAD, c), F32)],
        compiler_params=_cp("parallel", "parallel"),
        name="hyena_short",
    )(p, w, bias.reshape(1, c3))


def _hy_filter_kernel(z_ref, w1_ref, b1_ref, f1_ref, w2_ref, b2_ref, f2_ref, w3_ref, b3_ref, win_ref, o_ref):
    hp = lax.Precision.HIGHEST

    def mm(a, b):
        return jnp.dot(a, b, preferred_element_type=F32, precision=hp)

    h = jnp.sin(f1_ref[...] * (mm(z_ref[...], w1_ref[...]) + b1_ref[...]))
    h = jnp.sin(f2_ref[...] * (mm(h, w2_ref[...]) + b2_ref[...]))
    h = mm(h, w3_ref[...]) + b3_ref[...]
    c = HYENA_W
    win = win_ref[...]
    for s in range(h.shape[1] // c):
        o_ref[:, s * c:(s + 1) * c] = h[:, s * c:(s + 1) * c] * win


def _pad2(a, rows, cols):
    return jnp.pad(a, ((0, rows - a.shape[0]), (0, cols - a.shape[1])))


def _hy_filter_call(l, t, w1, b1, f1, w2, b2, f2, w3, b3):
    t_norm = t / max(l - 1, 1)
    ang = (2.0 * math.pi / l) * t[:, None] * jnp.linspace(1e-4, HY_BANDS - 1, HY_BANDS, dtype=F32)[None, :]
    z = jnp.concatenate([t_norm[:, None], jnp.cos(ang), -jnp.sin(ang)], axis=-1)
    deltas = jnp.abs(jnp.linspace(HY_MIN_DECAY, HY_MAX_DECAY, HYENA_W, dtype=F32))
    window = jnp.exp(-t_norm[:, None] * deltas[None, :])
    n3 = w3.shape[1]
    by_dir = lambda a: a.reshape(a.shape[0], HY_ORDER, 2, HYENA_W).transpose(0, 2, 1, 3).reshape(a.shape[0], n3)
    w3, b3 = by_dir(w3), by_dir(b3[None])[0]
    l = t.shape[0]
    tl = min(l, 256)
    full = lambda shape: pl.BlockSpec(shape, lambda i: (0, 0))
    return pl.pallas_call(
        _hy_filter_kernel,
        grid=(l // tl,),
        in_specs=[pl.BlockSpec((tl, LANES), lambda i: (i, 0)),
                  full((LANES, LANES)), full((1, LANES)), full((1, LANES)),
                  full((LANES, LANES)), full((1, LANES)), full((1, LANES)),
                  full((LANES, n3)), full((1, n3)),
                  pl.BlockSpec((tl, HYENA_W), lambda i: (i, 0))],
        out_specs=pl.BlockSpec((tl, n3), lambda i: (i, 0)),
        out_shape=jax.ShapeDtypeStruct((l, n3), F32),
        compiler_params=_cp("parallel"),
        name="hyena_filter",
    )(_pad2(z, l, LANES), _pad2(w1, LANES, LANES), _pad2(b1[None], 1, LANES), _pad2(f1[None], 1, LANES),
      _pad2(w2, LANES, LANES), _pad2(b2[None], 1, LANES), _pad2(f2[None], 1, LANES),
      _pad2(w3, LANES, n3), b3[None], window)


def _dft_tables_kernel(ac_ref, as_ref, bc_ref, bs_ref, fwd_ref, inv_ref):
    tm, l = fwd_ref.shape[1], fwd_ref.shape[2]
    n = 2 * l
    row = lax.broadcasted_iota(jnp.int32, (tm, LANES), 0) + pl.program_id(0) * tm
    lane = lax.broadcasted_iota(jnp.int32, (tm, LANES), 1)
    row_sign = jnp.where((row & 1) == 0, 1.0, -1.0)
    lane_sign = jnp.where((lane & 1) == 0, 1.0, -1.0)
    bc = bc_ref[...]
    bs = bs_ref[...]
    for a in range(l // LANES):
        cols = slice(a * LANES, (a + 1) * LANES)
        ca = ac_ref[:, a:a + 1]
        sa = as_ref[:, a:a + 1]
        c = ca * bc - sa * bs
        s = sa * bc + ca * bs
        fwd_ref[0, :, cols] = c.astype(BF16)
        fwd_ref[1, :, cols] = jnp.where(row == 0, lane_sign, -s).astype(BF16)
        inv_c = c * (2.0 / n)
        inv_s = -s * (2.0 / n)
        if a == 0:
            inv_c = jnp.where(lane == 0, 1.0 / n, inv_c)
            inv_s = jnp.where(lane == 0, row_sign / n, inv_s)
        inv_ref[:, cols] = inv_c.astype(BF16)
        inv_ref[:, l + a * LANES:l + (a + 1) * LANES] = inv_s.astype(BF16)


def _dft_mats(l):
    n = 2 * l
    f = jnp.arange(l, dtype=jnp.int32)[:, None]

    def table(t):
        ang = ((f * t[None, :]) % n).astype(F32) * (2.0 * math.pi / n)
        return jnp.cos(ang), jnp.sin(ang)

    coarse = table(jnp.arange(l // LANES, dtype=jnp.int32) * LANES)
    fine = table(jnp.arange(LANES, dtype=jnp.int32))
    tm = min(l, 256)
    small = pl.BlockSpec((tm, l // LANES), lambda i: (i, 0))
    wide = pl.BlockSpec((tm, LANES), lambda i: (i, 0))
    return pl.pallas_call(
        _dft_tables_kernel,
        grid=(l // tm,),
        in_specs=[small, small, wide, wide],
        out_specs=[pl.BlockSpec((2, tm, l), lambda i: (0, i, 0)), pl.BlockSpec((tm, n), lambda i: (i, 0))],
        out_shape=[jax.ShapeDtypeStruct((2, l, l), BF16), jax.ShapeDtypeStruct((l, n), BF16)],
        compiler_params=_cp("parallel"),
        name="dft_tables",
    )(*coarse, *fine)


def _conv_blocks(l):
    return 2 if l >= 1024 else 1


def _lag_window_plan(l):
    nb = _conv_blocks(l)
    lb = l // nb
    pos = jnp.arange(l, dtype=F32)
    if nb == 1:
        return pos, jnp.array([0, 0, 0, 1], jnp.int32)
    pos = jnp.concatenate([pos, lb - jnp.arange(lb, dtype=F32)])
    return pos, jnp.array([2, 0, 1, 1, 0, 0, 1, 0, 2, 1, 1, 0], jnp.int32)


def _spec_kernel(tab_ref, f_ref, a_ref, b_ref, o_ref):
    a = a_ref[...]
    row0 = lax.broadcasted_iota(jnp.int32, a.shape, 0) == 0
    b = jnp.where(row0, 0.0, b_ref[...])
    s = (a + b).astype(BF16)
    hi = _dot(f_ref[1], (a - b).astype(BF16))
    o_ref[0, 0] = _dot(f_ref[0], s)
    o_ref[0, 1] = hi

    @pl.when(pl.program_id(1) == 0)
    def _():
        nyq = _dot(f_ref[1, 0:16, :], s)
        row = lax.broadcasted_iota(jnp.int32, nyq.shape, 0)
        o_ref[0, 1, 0:16, :] = jnp.where(row == 0, nyq, hi[0:16])


def _spec_call(fwd, filt, table):
    _, lf, lb = fwd.shape
    nd = table.shape[0] // 4
    n = filt.shape[1] // 2
    tm = min(lf, 512)
    pick = lambda k: pl.BlockSpec((lb, n), lambda d, i, tab: (tab[2 * k * nd + d], tab[(2 * k + 1) * nd + d]))
    return pl.pallas_call(
        _spec_kernel,
        grid_spec=pltpu.PrefetchScalarGridSpec(
            num_scalar_prefetch=1,
            grid=(nd, lf // tm),
            in_specs=[pl.BlockSpec((2, tm, lb), lambda d, i, tab: (0, i, 0)), pick(0), pick(1)],
            out_specs=pl.BlockSpec((1, 2, tm, n), lambda d, i, tab: (d, 0, i, 0))),
        out_shape=jax.ShapeDtypeStruct((nd, 2, lf, n), F32),
        compiler_params=_cp("arbitrary", "arbitrary"),
        name="hyena_spectrum",
    )(table, fwd, filt, filt)


def _dft_fwd_kernel(f_ref, u_ref, h_ref, y_ref, *, nb):
    lb = f_ref.shape[2]
    xr, xi = [], []
    for p in range(nb):
        up = u_ref[0, 0, p * lb:(p + 1) * lb, :].astype(BF16)
        xr.append(_dot(f_ref[0], up))
        xi.append(_dot(f_ref[1], up))
    def products(rows, real_bins):
        for q in range(nb):
            yr = 0.0
            yi = 0.0
            for p in range(nb):
                gr = h_ref[q - p + nb - 1, 0, rows, :]
                gi = h_ref[q - p + nb - 1, 1, rows, :]
                ii = xi[p][rows] * gi
                cross = xr[p][rows] * gi + xi[p][rows] * gr
                yr = yr + xr[p][rows] * gr - (ii if real_bins is None else jnp.where(real_bins, 0.0, ii))
                yi = yi + (cross if real_bins is None else jnp.where(real_bins, ii, cross))
            y_ref[0, q, 0, rows, :] = yr.astype(BF16)
            y_ref[0, q, 1, rows, :] = yi.astype(BF16)

    products(slice(None), None)

    @pl.when(pl.program_id(0) == 0)
    def _():
        head = slice(0, 16)
        products(head, lax.broadcasted_iota(jnp.int32, (16, u_ref.shape[3]), 0) == 0)


def _dft_fwd_call(fwd, u, group, spec, order, nb):
    _, b, l, c = u.shape
    _, lf, lb = fwd.shape
    nd = spec.shape[0]
    tm = min(lf, 512)
    return pl.pallas_call(
        functools.partial(_dft_fwd_kernel, nb=nb),
        grid=(lf // tm, b),
        in_specs=[pl.BlockSpec((2, tm, lb), lambda i, j: (0, i, 0)),
                  pl.BlockSpec((1, 1, l, c), lambda i, j: (group, j, 0, 0)),
                  pl.BlockSpec((nd, 2, tm, c), lambda i, j: (0, 0, i, order))],
        out_specs=pl.BlockSpec((1, nb, 2, tm, c), lambda i, j: (j, 0, 0, i, 0)),
        out_shape=jax.ShapeDtypeStruct((b, nb, 2, lf, c), BF16),
        compiler_params=_cp("arbitrary", "arbitrary"),
        name="hyena_dft_fwd",
    )(fwd, u, spec)


def _dft_inv_kernel(g_ref, y_ref, u_ref, b_ref, x_ref, o_ref):
    r = _dot(g_ref[...], y_ref[0, 0]) + u_ref[0, 0].astype(F32) * b_ref[...]
    o_ref[0, 0] = (x_ref[0, 0] * r).astype(o_ref.dtype)


def _dft_inv_call(inv, y, u, u_group, bias, gate, gate_group):
    _, b, l, c = u.shape
    lb, n = inv.shape
    nb = l // lb
    tm = min(lb, 1024)
    nt = lb // tm
    y = y.reshape(b, nb, n, c)
    row = lambda g: pl.BlockSpec((1, 1, tm, c), lambda q, i, j: (g, j, q * nt + i, 0))
    return pl.pallas_call(
        _dft_inv_kernel,
        grid=(nb, nt, b),
        in_specs=[pl.BlockSpec((tm, n), lambda q, i, j: (i, 0)),
                  pl.BlockSpec((1, 1, n, c), lambda q, i, j: (j, q, 0, 0)),
                  row(u_group), pl.BlockSpec((1, c), lambda q, i, j: (0, 0)), row(gate_group)],
        out_specs=row(0),
        out_shape=jax.ShapeDtypeStruct((1, b, l, c), BF16),
        compiler_params=_cp("parallel", "parallel", "parallel"),
        name="hyena_dft_inv",
    )(inv, y, u, bias.reshape(1, c), gate)


def _hyena_call(p_hy, sh_w, sh_b, spec, hy_bias, fwd, inv, nb):
    u3 = _hy_short_call(p_hy, sh_w, sh_b)
    z = _dft_inv_call(inv, _dft_fwd_call(fwd, u3, 0, spec, 0, nb), u3, 0, hy_bias[0], u3, 1)
    return _dft_inv_call(inv, _dft_fwd_call(fwd, z, 0, spec, 1, nb), z, 0, hy_bias[1], u3, 2)[0]


def _mix_out_kernel(c_ref, a_ref, h_ref, w_ref, x_ref, g1_ref, g_ref, sh_ref, sc_ref,
                    wr_ref, rb_ref, o_ref, h32_ref, idx_ref, gate_ref):
    k1 = c_ref.shape[2]
    k2 = k1 + a_ref.shape[2]
    acc = _dot(c_ref[0], w_ref[0, :k1, :]) + _dot(a_ref[0], w_ref[0, k1:k2, :]) + _dot(h_ref[0], w_ref[0, k2:, :])
    x = x_ref[0] + g1_ref[0] * acc
    o_ref[0] = x
    hm = _rms(x, g_ref[...]) * (1.0 + sc_ref[0]) + sh_ref[0]
    h32_ref[0] = _pack_rows(hm)
    picks, gates = _route(hm, wr_ref[...], rb_ref[...])
    idx_ref[...] = jnp.concatenate(picks, axis=0)
    gate_ref[...] = jnp.concatenate(gates, axis=0)


def _mix_out_call(conv_o, att_o, hy_o, w_out, layer, x, mod, g2, wr, rbias):
    b, l, d = x.shape
    tm = min(l, 512)
    nt = l // tm
    act = lambda k: pl.BlockSpec((1, tm, k), lambda i, j: (i, j, 0))
    row = pl.BlockSpec((1, 1, d), lambda i, j: (i, 0, 0))
    tok = pl.BlockSpec((2, tm), lambda i, j: (0, i * nt + j))
    return pl.pallas_call(
        _mix_out_kernel,
        grid=(b, nt),
        in_specs=[act(conv_o.shape[2]), act(att_o.shape[2]), act(hy_o.shape[2]),
                  pl.BlockSpec((1,) + w_out.shape[1:], lambda i, j: (layer, 0, 0), pipeline_mode=pl.Buffered(1)),
                  act(d), row, pl.BlockSpec((1, d), lambda i, j: (0, 0)), row, row,
                  pl.BlockSpec(wr.shape, lambda i, j: (0, 0)),
                  pl.BlockSpec((N_EXPERTS, 1), lambda i, j: (0, 0))],
        out_specs=[act(d), act(d // 2), tok, tok],
        out_shape=[jax.ShapeDtypeStruct((b, l, d), F32),
                   jax.ShapeDtypeStruct((b, l, d // 2), jnp.int32),
                   jax.ShapeDtypeStruct((2, b * l), jnp.int32),
                   jax.ShapeDtypeStruct((2, b * l), F32)],
        compiler_params=_cp("parallel", "parallel"),
        name="mix_out",
    )(conv_o, att_o, hy_o, w_out, x, mod[2], g2.reshape(1, d), mod[3], mod[4], wr,
      rbias.reshape(N_EXPERTS, 1))


def _route(hm, wr, rbias):
    hi = hm.astype(BF16)
    lo = (hm - hi.astype(F32)).astype(BF16)
    nt = (((1,), (1,)), ((), ()))
    r = (lax.dot_general(wr, hi, nt, preferred_element_type=F32)
         + lax.dot_general(wr, lo, nt, preferred_element_type=F32))
    logits = r[:N_EXPERTS] + r[N_EXPERTS:]
    score = jax.nn.sigmoid(logits)
    sel = score + rbias
    rows = [sel[e:e + 1] for e in range(N_EXPERTS)]
    gscore = []
    for g in range(N_GROUPS):
        a, b, c, d = rows[4 * g:4 * g + 4]
        h1, l1, h2, l2 = jnp.maximum(a, b), jnp.minimum(a, b), jnp.maximum(c, d), jnp.minimum(c, d)
        gscore.append(jnp.maximum(h1, h2) + jnp.maximum(jnp.minimum(h1, h2), jnp.maximum(l1, l2)))
    best = gscore[0]
    gbest = jnp.zeros(best.shape, jnp.int32)
    for g in range(1, N_GROUPS):
        upd = gscore[g] > best
        gbest = jnp.where(upd, g, gbest)
        best = jnp.where(upd, gscore[g], best)
    masked = [jnp.where(gbest == e // EXPERTS_PER_GROUP, rows[e], NEG_INF) for e in range(N_EXPERTS)]
    picks = []
    for _ in range(2):
        val = jnp.full(best.shape, NEG_INF, F32)
        idx = jnp.zeros(best.shape, jnp.int32)
        for e in range(N_EXPERTS):
            cand = masked[e]
            if picks:
                cand = jnp.where(picks[0] == e, NEG_INF, cand)
            upd = cand > val
            idx = jnp.where(upd, e, idx)
            val = jnp.where(upd, cand, val)
        picks.append(idx)
    gates = [functools.reduce(jnp.add, [jnp.where(p == e, score[e:e + 1], 0.0) for e in range(N_EXPERTS)])
             for p in picks]
    den = gates[0] + gates[1]
    return picks, [gates[0] / den, gates[1] / den]


def _expert_changed(te_ref, i):
    return (i == 0) | (te_ref[i] != te_ref[jnp.maximum(i - 1, 0)])


def _pack_rows(x):
    bits = pltpu.bitcast(x.astype(BF16).astype(F32), jnp.uint32)
    half = bits.shape[1] // 2
    return pltpu.bitcast((bits[:, :half] >> 16) | (bits[:, half:] & jnp.uint32(0xFFFF0000)), jnp.int32)


def _unpack_rows(words):
    w = pltpu.bitcast(words, jnp.uint32)
    return pltpu.bitcast(w << 16, F32), pltpu.bitcast(w & jnp.uint32(0xFFFF0000), F32)


def _moe_up_kernel(te_ref, tn_ref, nu_ref, x_ref, wg_hbm, wu_hbm, h_ref, gbuf, ubuf, wgb_ref, wub_ref, sems,
                   *, layer):
    j, i = pl.program_id(0), pl.program_id(1)
    tf = gbuf.shape[1]

    def fetch(e, chunk):
        cols = pl.ds(pl.multiple_of(chunk * tf, tf), tf)
        return (pltpu.make_async_copy(wg_hbm.at[layer, e, :, cols], gbuf, sems.at[0]),
                pltpu.make_async_copy(wu_hbm.at[layer, e, :, cols], ubuf, sems.at[1]))

    def start(e, chunk):
        for copy in fetch(e, chunk):
            copy.start()

    @pl.when((i == 0) & (j == 0))
    def _():
        start(te_ref[0], 0)

    @pl.when(_expert_changed(te_ref, i))
    def _():
        for copy in fetch(te_ref[i], j):
            copy.wait()
        wgb_ref[...] = gbuf[...].astype(BF16)
        wub_ref[...] = ubuf[...].astype(BF16)
        nxt = tn_ref[i]
        more = nxt < nu_ref[0]

        @pl.when(more)
        def _():
            start(te_ref[nxt], j)

        @pl.when(jnp.logical_not(more) & (j + 1 < pl.num_programs(0)))
        def _():
            start(te_ref[0], j + 1)

    @pl.when(i < nu_ref[0])
    def _():
        xa, xb = (v.astype(BF16) for v in _unpack_rows(x_ref[...]))
        half = xa.shape[1]
        a = _dot(xa, wgb_ref[:half, :]) + _dot(xb, wgb_ref[half:, :])
        u = _dot(xa, wub_ref[:half, :]) + _dot(xb, wub_ref[half:, :])
        h_ref[...] = (a * jax.nn.sigmoid(a) * u).astype(BF16)

    @pl.when(i >= nu_ref[0])
    def _():
        h_ref[...] = jnp.zeros(h_ref.shape, BF16)


def _moe_up_call(xs, w_gate, w_up, layer, te, tn, nu, tile):
    p, dw = xs.shape
    _, _, d, f = w_gate.shape
    tf = f
    hbm = pl.BlockSpec(memory_space=pl.ANY)
    return pl.pallas_call(
        functools.partial(_moe_up_kernel, layer=layer),
        grid_spec=pltpu.PrefetchScalarGridSpec(
            num_scalar_prefetch=3,
            grid=(f // tf, p // tile),
            in_specs=[pl.BlockSpec((tile, dw), lambda j, i, te, tn, nu: (i, 0)), hbm, hbm],
            out_specs=pl.BlockSpec((tile, tf), lambda j, i, te, tn, nu: (i, j)),
            scratch_shapes=[pltpu.VMEM((d, tf), F32), pltpu.VMEM((d, tf), F32),
                            pltpu.VMEM((d, tf), BF16), pltpu.VMEM((d, tf), BF16),
                            pltpu.SemaphoreType.DMA((2,))]),
        out_shape=jax.ShapeDtypeStruct((p, f), BF16),
        compiler_params=_cp("arbitrary", "arbitrary"),
        name="moe_up",
    )(te, tn, nu, xs, w_gate, w_up)


def _moe_down_kernel(te_ref, tn_ref, nu_ref, h_ref, wd_hbm, y_ref, wbuf, wdb_ref, sem, *, layer):
    i = pl.program_id(0)

    def fetch(e):
        return pltpu.make_async_copy(wd_hbm.at[layer, e], wbuf, sem)

    @pl.when(i == 0)
    def _():
        fetch(te_ref[0]).start()

    @pl.when(_expert_changed(te_ref, i))
    def _():
        fetch(te_ref[i]).wait()
        wdb_ref[...] = wbuf[...].astype(BF16)
        nxt = tn_ref[i]

        @pl.when(nxt < nu_ref[0])
        def _():
            fetch(te_ref[nxt]).start()

    @pl.when(i < nu_ref[0])
    def _():
        y_ref[...] = _pack_rows(_dot(h_ref[...], wdb_ref[...]))

    @pl.when(i >= nu_ref[0])
    def _():
        y_ref[...] = jnp.zeros(y_ref.shape, jnp.int32)


def _moe_down_call(h, w_down, layer, te, tn, nu, tile):
    p, f = h.shape
    d = w_down.shape[3]
    return pl.pallas_call(
        functools.partial(_moe_down_kernel, layer=layer),
        grid_spec=pltpu.PrefetchScalarGridSpec(
            num_scalar_prefetch=3,
            grid=(p // tile,),
            in_specs=[pl.BlockSpec((tile, f), lambda i, te, tn, nu: (i, 0)),
                      pl.BlockSpec(memory_space=pl.ANY)],
            out_specs=pl.BlockSpec((tile, d // 2), lambda i, te, tn, nu: (i, 0)),
            scratch_shapes=[pltpu.VMEM((f, d), F32), pltpu.VMEM((f, d), BF16), pltpu.SemaphoreType.DMA(())]),
        out_shape=jax.ShapeDtypeStruct((p, d // 2), jnp.int32),
        compiler_params=_cp("arbitrary"),
        name="moe_down",
    )(te, tn, nu, h, w_down)


PLAN_BLOCK = 512


def _plan_kernel(idx_ref, pos_ref, te_ref, tn_ref, nu_ref, rank_ref, *, tile):
    t = idx_ref.shape[1]
    nb = t // PLAN_BLOCK
    experts = lax.broadcasted_iota(jnp.int32, (N_EXPERTS, PLAN_BLOCK), 0)
    r = lax.broadcasted_iota(jnp.int32, (PLAN_BLOCK, PLAN_BLOCK), 0)
    c = lax.broadcasted_iota(jnp.int32, (PLAN_BLOCK, PLAN_BLOCK), 1)
    before = jnp.where(r < c, 1.0, 0.0).astype(BF16)

    def rank_pass(k):
        def body(j, carry):
            cols = pl.ds(pl.multiple_of(j * PLAN_BLOCK, PLAN_BLOCK), PLAN_BLOCK)
            onehot = jnp.where(experts == idx_ref[k:k + 1, cols], 1.0, 0.0)
            earlier = _dot(onehot.astype(BF16), before) + carry
            rank_ref[k:k + 1, cols] = jnp.sum(onehot * earlier, axis=0, keepdims=True)
            return carry + jnp.sum(onehot, axis=1, keepdims=True)
        return body

    counts = jnp.zeros((N_EXPERTS, 1), F32)
    for k in range(2):
        counts = lax.fori_loop(0, nb, rank_pass(k), counts)
    padded = jnp.ceil(counts * (1.0 / tile)) * tile
    starts, ends = [], []
    acc = jnp.zeros((1, 1), F32)
    for e in range(N_EXPERTS):
        starts.append(acc)
        acc = acc + padded[e:e + 1]
        ends.append(acc)

    def place_pass(k):
        def body(j, carry):
            cols = pl.ds(pl.multiple_of(j * PLAN_BLOCK, PLAN_BLOCK), PLAN_BLOCK)
            row = idx_ref[k:k + 1, cols]
            start = functools.reduce(jnp.add, [jnp.where(row == e, starts[e], 0.0) for e in range(N_EXPERTS)])
            pos_ref[k:k + 1, cols] = (rank_ref[k:k + 1, cols] + start).astype(jnp.int32)
            return carry
        return body

    for k in range(2):
        lax.fori_loop(0, nb, place_pass(k), 0)
    n_used = acc * (1.0 / tile)
    tiles = lax.broadcasted_iota(jnp.int32, te_ref.shape, 1).astype(F32)
    first_row = jnp.minimum(tiles, n_used - 1.0) * tile
    te = functools.reduce(jnp.add, [jnp.where(ends[e] <= first_row, 1.0, 0.0) for e in range(N_EXPERTS)])
    te_ref[...] = te.astype(jnp.int32)
    seg_end = functools.reduce(jnp.add, [jnp.where(te == e, ends[e], 0.0) for e in range(N_EXPERTS)])
    tn_ref[...] = (seg_end * (1.0 / tile)).astype(jnp.int32)
    nu_ref[...] = jnp.broadcast_to(n_used, nu_ref.shape).astype(jnp.int32)


def _plan_call(idx, tile, n_tiles):
    t = idx.shape[1]
    whole = lambda shape: pl.BlockSpec(shape, lambda: (0,) * len(shape))
    lanes = jax.ShapeDtypeStruct((1, LANES), jnp.int32)
    pos, te, tn, nu = pl.pallas_call(
        functools.partial(_plan_kernel, tile=tile),
        in_specs=[whole(idx.shape)],
        out_specs=[whole(idx.shape), whole((1, LANES)), whole((1, LANES)), whole((1, LANES))],
        out_shape=[jax.ShapeDtypeStruct(idx.shape, jnp.int32), lanes, lanes, lanes],
        scratch_shapes=[pltpu.VMEM(idx.shape, F32)],
        name="route_plan",
    )(idx)
    return pos.reshape(2 * t), te[0, :n_tiles], tn[0, :n_tiles], nu[0, :1]


V7X_SC_CORES = 2
V7X_SC_SUBCORES = 16
SC_WORKERS = V7X_SC_CORES * V7X_SC_SUBCORES
SC_ROW_BYTES = 256 * 1024


def _sc_worker():
    return lax.axis_index("s") * V7X_SC_CORES + lax.axis_index("c")


def _sc_dispatch_call(streams, pos, n_rows):
    w = streams[0].shape[1]
    sizes = [s.shape[0] for s in streams]
    t = sum(sizes)
    per_w = t // SC_WORKERS
    ch = min(per_w, SC_ROW_BYTES // (4 * w))
    assert per_w % ch == 0 and all(size % ch == 0 for size in sizes)
    mesh = plsc.VectorSubcoreMesh(core_axis_name="c", subcore_axis_name="s")

    @functools.partial(pl.kernel, mesh=mesh, out_type=jax.ShapeDtypeStruct((n_rows, w), jnp.int32),
                       scratch_types=[pltpu.VMEM((ch,), jnp.int32), pltpu.VMEM((ch,), jnp.int32),
                                      pltpu.VMEM((ch, w), jnp.int32),
                                      pltpu.SemaphoreType.DMA, pltpu.SemaphoreType.DMA],
                       name="sc_dispatch")
    def run(*refs):
        srcs = refs[:len(sizes)]
        pos_hbm, xs_hbm, idx0_v, idx1_v, rows_v, sem0, sem1 = refs[len(sizes):]
        first = _sc_worker() * per_w

        @pl.loop(0, per_w // ch)
        def _(j):
            base = first + j * ch
            start = 0
            for src, size in zip(srcs, sizes):
                def load(src=src, start=start):
                    pltpu.sync_copy(src.at[pl.ds(base - start, ch)], rows_v)
                if len(sizes) == 1:
                    load()
                else:
                    pl.when((base >= start) & (base < start + size))(load)
                start += size
            pltpu.sync_copy(pos_hbm.at[pl.ds(base, ch)], idx0_v)
            pltpu.sync_copy(pos_hbm.at[pl.ds(t + base, ch)], idx1_v)
            first_choice = pltpu.async_copy(rows_v, xs_hbm.at[idx0_v], sem0)
            second_choice = pltpu.async_copy(rows_v, xs_hbm.at[idx1_v], sem1)
            first_choice.wait()
            second_choice.wait()

    return run(*streams, pos)


def _sc_combine_call(ys, pos):
    _, d = ys.shape
    n = pos.shape[0]
    per_w = n // SC_WORKERS
    ch = min(per_w // 2, SC_ROW_BYTES // (8 * d))
    assert per_w % (2 * ch) == 0
    mesh = plsc.VectorSubcoreMesh(core_axis_name="c", subcore_axis_name="s")
    idx_buf = pltpu.VMEM((ch,), jnp.int32)
    row_buf = pltpu.VMEM((ch, d), jnp.int32)

    @functools.partial(pl.kernel, mesh=mesh, out_type=jax.ShapeDtypeStruct((n, d), jnp.int32),
                       scratch_types=[idx_buf, idx_buf, row_buf, row_buf] + [pltpu.SemaphoreType.DMA] * 4,
                       name="sc_combine")
    def run(ys_hbm, pos_hbm, out_hbm, idx_a, idx_b, rows_a, rows_b, gsem_a, gsem_b, wsem_a, wsem_b):
        first = _sc_worker() * per_w

        @pl.loop(0, per_w // (2 * ch))
        def _(j):
            base = first + j * (2 * ch)
            pltpu.sync_copy(pos_hbm.at[pl.ds(base, ch)], idx_a)
            pltpu.sync_copy(pos_hbm.at[pl.ds(base + ch, ch)], idx_b)
            gather_a = pltpu.async_copy(ys_hbm.at[idx_a], rows_a, gsem_a)
            gather_b = pltpu.async_copy(ys_hbm.at[idx_b], rows_b, gsem_b)
            gather_a.wait()
            write_a = pltpu.async_copy(rows_a, out_hbm.at[pl.ds(base, ch)], wsem_a)
            gather_b.wait()
            write_b = pltpu.async_copy(rows_b, out_hbm.at[pl.ds(base + ch, ch)], wsem_b)
            write_a.wait()
            write_b.wait()

    return run(ys, pos)


def _moe_call(streams, idx, w_gate, w_up, w_down, layer):
    t = idx.shape[1]
    tile = min(MOE_TILE, max(t // N_EXPERTS, 128))
    n_tiles = (2 * t) // tile + N_EXPERTS
    assert n_tiles <= LANES
    pos, te, tn, nu = _plan_call(idx, tile, n_tiles)
    xs = _sc_dispatch_call(streams, pos, n_tiles * tile)
    hid = _moe_up_call(xs, w_gate, w_up, layer, te, tn, nu, tile)
    ys = _moe_down_call(hid, w_down, layer, te, tn, nu, tile)
    return _sc_combine_call(ys, pos).reshape(2, t, ys.shape[1])


def _moe_resid_kernel(x_ref, y0_ref, y1_ref, gt_ref, g_ref, *rest, final):
    gt = gt_ref[0]
    halves0 = _unpack_rows(y0_ref[0])
    halves1 = _unpack_rows(y1_ref[0])
    y = jnp.concatenate([gt[:, 0:1] * a + gt[:, 1:2] * b for a, b in zip(halves0, halves1)], axis=1)
    x = x_ref[0] + g_ref[0] * y
    if final:
        fg_ref, o_ref = rest
        x = _rms(x, fg_ref[...])
    else:
        (o_ref,) = rest
    o_ref[0] = x


def _moe_resid_call(x, y, first_row, gates, gate, final_g=None):
    b, l, d = x.shape
    tl = min(l, 512)
    nt = l // tl
    tile = pl.BlockSpec((1, tl, d), lambda i, j: (i, j, 0))
    choice = lambda k: pl.BlockSpec((1, tl, d // 2), lambda i, j: (k, first_row // tl + i * nt + j, 0))
    in_specs = [tile, choice(0), choice(1), pl.BlockSpec((1, tl, 2), lambda i, j: (i, j, 0)),
                pl.BlockSpec((1, 1, d), lambda i, j: (i, 0, 0))]
    args = [x, y, y, gates.T.reshape(b, l, 2), gate]
    if final_g is not None:
        in_specs.append(pl.BlockSpec((1, d), lambda i, j: (0, 0)))
        args.append(final_g.reshape(1, d))
    return pl.pallas_call(
        functools.partial(_moe_resid_kernel, final=final_g is not None),
        grid=(b, l // tl),
        in_specs=in_specs,
        out_specs=tile,
        out_shape=jax.ShapeDtypeStruct((b, l, d), F32),
        compiler_params=_cp("parallel", "parallel"),
        name="moe_residual",
    )(*args)


def _rope_perm(w):
    r = w.reshape(w.shape[:-1] + (2, 2, ROPE_FREQS))
    return jnp.stack([-r[..., 1, :], r[..., 0, :]], axis=-2).reshape(w.shape)


def _rope_table(l, rotary):
    if not rotary:
        return jnp.concatenate([jnp.ones((l, MLA_ROPE), F32), jnp.zeros((l, MLA_ROPE), F32)], axis=1)
    rows = l // GRID_W
    r = jnp.repeat(jnp.arange(rows), GRID_W)
    col = jnp.tile(jnp.arange(GRID_W), rows)
    pos = jnp.stack([r, col], axis=-1).astype(F32)
    inv = ROPE_THETA ** (-jnp.arange(ROPE_FREQS, dtype=F32) / ROPE_FREQS)
    ang = pos[:, :, None] * inv
    lay = lambda a: jnp.concatenate([a[:, 0], a[:, 0], a[:, 1], a[:, 1]], axis=1)
    return jnp.concatenate([lay(jnp.cos(ang)), lay(jnp.sin(ang))], axis=1)


def _prep_weights(w_in, w_uq, w_ukv, w_out):
    depth = w_in.shape[0]
    cast = lambda a: a.astype(BF16)
    kr = cast(w_in[:, :, OFF_KR:OFF_HY])
    w_kv_in = jnp.concatenate([cast(w_in[:, :, OFF_KV:OFF_KR]), kr, _rope_perm(kr)], axis=2)
    w_cq, w_hy = cast(w_in[:, :, :OFF_KV]), cast(w_in[:, :, OFF_HY:])
    uq = cast(w_uq).reshape(depth, MLA_Q_RANK, MLA_HEADS, MLA_NOPE + MLA_ROPE)
    w_q = jnp.concatenate([uq, _rope_perm(uq[..., MLA_NOPE:])], axis=3).reshape(depth, MLA_Q_RANK, -1)
    ukv = cast(w_ukv).reshape(depth, MLA_KV_RANK, MLA_HEADS, MLA_NOPE + MLA_V_DIM)
    w_k = ukv[..., :MLA_NOPE].reshape(depth, MLA_KV_RANK, MLA_HEADS * MLA_NOPE)
    w_v = ukv[..., MLA_NOPE:].reshape(depth, MLA_KV_RANK, MLA_HEADS * MLA_V_DIM)
    return dict(cq=w_cq, hy=w_hy, kv=w_kv_in, uq=w_q, uk=w_k, uv=w_v, out=cast(w_out))


def _router_weights(w_router):
    hi = w_router.astype(BF16)
    lo = (w_router - hi.astype(F32)).astype(BF16)
    return jnp.concatenate([hi.T, lo.T], axis=0)


def _mixer(x, mod, w, layer, lp, cs, spec, fwd, inv, kv_extra, full, wr, rbias):
    kv_args = (w["kv"], lp["kv_norm_g"], w["uk"], w["uv"], cs, layer)
    if not full:
        k, v = _in_proj_call(x, lp["norm1_g"], mod[0], mod[1], [], *kv_args)
        return None, k, v
    proj = [(w["cq"], [OFF_Q, MLA_Q_RANK]), (w["hy"], [(HY_ORDER + 1) * HYENA_W])]
    p_conv, p_q, p_hy, k, v = _in_proj_call(x, lp["norm1_g"], mod[0], mod[1], proj, *kv_args)
    conv_o = _conformer_call(p_conv, lp["conv_dw_w"], lp["conv_dw_b"], lp["conv_ln_g"], lp["conv_ln_b"])
    att_o = _attn_call(p_q, lp["q_norm_g"], w["uq"], cs, kv_extra + [(k, v)], layer)
    hy_o = _hyena_call(p_hy, lp["hy_short_w"], lp["hy_short_b"], spec, lp["hy_bias"], fwd, inv,
                       _conv_blocks(x.shape[1]))
    routed = _mix_out_call(conv_o, att_o, hy_o, w["out"], layer, x, mod, lp["norm2_g"], wr, rbias)
    return routed, k, v


def _channel(streams, experts, layer, final_g=None):
    d = streams[0][1][0].shape[2]
    idx = jnp.concatenate([r[2] for _, r in streams], axis=1)
    y = _moe_call([r[1].reshape(-1, d // 2) for _, r in streams], idx, *experts, layer)
    outs = []
    first_row = 0
    for mod, (x, _, _, gates) in streams:
        outs.append(_moe_resid_call(x, y, first_row, gates, mod[5], final_g))
        first_row += x.shape[0] * x.shape[1]
    return outs


def _filter_spectrum(l, lp, fwd):
    pos, table = _lag_window_plan(l)
    filt = _hy_filter_call(l, pos, lp["hy_w1"], lp["hy_b1"], lp["hy_freq1"], lp["hy_w2"], lp["hy_b2"],
                           lp["hy_freq2"], lp["hy_w3"], lp["hy_b3"])
    return _spec_call(fwd, filt, table)


def kernel(x, c, ctx, c_ctx, norm1_g, norm2_g, w_ada, b_ada, w_in, conv_dw_w, conv_dw_b, conv_ln_g, conv_ln_b, q_norm_g, w_uq, kv_norm_g, w_ukv, hy_short_w, hy_short_b, hy_w1, hy_b1, hy_freq1, hy_w2, hy_b2, hy_freq2, hy_w3, hy_b3, hy_bias, w_out, w_router, router_bias, w_gate, w_up, w_down, final_norm_g):
    b, l, d = x.shape
    lc = ctx.shape[1]
    depth = w_in.shape[0]
    per_layer = dict(norm1_g=norm1_g, norm2_g=norm2_g, conv_dw_w=conv_dw_w, conv_dw_b=conv_dw_b,
                     conv_ln_g=conv_ln_g, conv_ln_b=conv_ln_b, q_norm_g=q_norm_g, kv_norm_g=kv_norm_g,
                     hy_short_w=hy_short_w, hy_short_b=hy_short_b, hy_w1=hy_w1, hy_b1=hy_b1, hy_freq1=hy_freq1,
                     hy_w2=hy_w2, hy_b2=hy_b2, hy_freq2=hy_freq2, hy_w3=hy_w3, hy_b3=hy_b3, hy_bias=hy_bias)
    experts = (w_gate, w_up, w_down)
    cs_l = _rope_table(l, True)
    cs_c = _rope_table(lc, False)
    fwd_l, inv_l = _dft_mats(l // _conv_blocks(l))
    fwd_c, inv_c = _dft_mats(lc // _conv_blocks(lc))
    wr = _router_weights(w_router)
    w = _prep_weights(w_in, w_uq, w_ukv, w_out)
    rows = 16
    cvec = jnp.concatenate([c, c_ctx[None, :], jnp.zeros((rows - b - 1, d), F32)], axis=0)
    xl, xc = x, ctx
    for i in range(depth):
        last = i == depth - 1
        lp = {k: v[i] for k, v in per_layer.items()}
        m = _ada_call(cvec, w_ada, b_ada, i)
        mod_l = [m[:b, None, j * d:(j + 1) * d] for j in range(6)]
        mod_c = [jnp.broadcast_to(m[b:b + 1, None, j * d:(j + 1) * d], (b, 1, d)) for j in range(6)]
        spec_l = _filter_spectrum(l, lp, fwd_l)
        spec_c = None if last else _filter_spectrum(lc, lp, fwd_c)
        routed_c, kc, vc = _mixer(xc, mod_c, w, i, lp, cs_c, spec_c, fwd_c, inv_c, [], not last, wr, router_bias)
        routed_l, _, _ = _mixer(xl, mod_l, w, i, lp, cs_l, spec_l, fwd_l, inv_l, [(kc, vc)], True, wr, router_bias)
        if last:
            (xl,) = _channel([(mod_l, routed_l)], experts, i, final_norm_g)
        else:
            xl, xc = _channel([(mod_l, routed_l), (mod_c, routed_c)], experts, i)
    return xl
```

```python
import functools
import math

import jax
import jax.numpy as jnp
from jax import lax
from jax.experimental import pallas as pl
from jax.experimental.pallas import tpu as pltpu
from jax.experimental.pallas import tpu_sc as plsc

F32 = jnp.float32
BF16 = jnp.bfloat16

EPS = 1e-6
GRID_W = 64
CONV_W = 512
CONV_KSIZE = 31
MLA_HEADS = 8
MLA_NOPE = 128
MLA_ROPE = 64
MLA_V_DIM = 128
MLA_Q_RANK = 768
MLA_KV_RANK = 512
ROPE_FREQS = MLA_ROPE // 4
ROPE_THETA = 10000.0
MLA_SCALE = (MLA_NOPE + MLA_ROPE) ** -0.5
Q_SCALE = MLA_SCALE * math.log2(math.e)
HYENA_W = 512
HY_ORDER = 2
HY_SHORT = 3
HY_EMB = 33
HY_BANDS = (HY_EMB - 1) // 2
HY_MIN_DECAY = math.log(1e-2) / 1.5
HY_MAX_DECAY = math.log(1e-2) / 0.3
N_EXPERTS = 16
N_GROUPS = 4
EXPERTS_PER_GROUP = N_EXPERTS // N_GROUPS
OFF_Q = 2 * CONV_W
OFF_KV = OFF_Q + MLA_Q_RANK
OFF_KR = OFF_KV + MLA_KV_RANK
OFF_HY = OFF_KR + MLA_ROPE

LANES = 128
QK_HEAD = 2 * LANES
V_HEAD = MLA_V_DIM + LANES
V7X_VMEM_BYTES = 64 * 1024 * 1024
VMEM_LIMIT = (V7X_VMEM_BYTES * 7) // 8
MOE_TILE = 512
NEG_INF = float("-inf")


def _cp(*sem):
    return pltpu.CompilerParams(dimension_semantics=sem, vmem_limit_bytes=VMEM_LIMIT)


def _dot(a, b):
    return jnp.dot(a, b, preferred_element_type=F32)


def _rms(x, g):
    ms = jnp.mean(x * x, axis=-1, keepdims=True)
    return x * lax.rsqrt(ms + EPS) * g


def _ada_kernel(c_ref, w_ref, b_ref, o_ref):
    c = c_ref[...]
    a = (c * jax.nn.sigmoid(c)).astype(BF16)
    o_ref[...] = _dot(a, w_ref[0].astype(BF16)) + b_ref[0]


def _ada_call(cvec, w, b, layer):
    m, d = cvec.shape
    n = w.shape[2]
    tn = 1024
    return pl.pallas_call(
        _ada_kernel,
        grid=(n // tn,),
        in_specs=[pl.BlockSpec((m, d), lambda j: (0, 0)),
                  pl.BlockSpec((1, d, tn), lambda j: (layer, 0, j)),
                  pl.BlockSpec((1, 1, tn), lambda j: (layer, 0, j))],
        out_specs=pl.BlockSpec((m, tn), lambda j: (0, j)),
        out_shape=jax.ShapeDtypeStruct((m, n), F32),
        compiler_params=_cp("parallel"),
        name="ada",
    )(cvec, w, b.reshape(b.shape[0], 1, n))


def _in_proj_kernel(x_ref, g_ref, sh_ref, sc_ref, *refs, n_proj):
    w_refs, (wkv_ref, kg_ref, wk_ref, wv_ref, cs_ref), o_refs = refs[:n_proj], refs[n_proj:n_proj + 5], refs[n_proj + 5:]
    *p_refs, k_ref, v_ref = o_refs
    y = _rms(x_ref[0], g_ref[...])
    h = (y * (1.0 + sc_ref[0]) + sh_ref[0]).astype(BF16)
    outs = iter(p_refs)
    for w_ref in w_refs:
        off = 0
        while off < w_ref.shape[2]:
            o_ref = next(outs)
            n = o_ref.shape[-1]
            o_ref[0] = _dot(h, w_ref[0, :, off:off + n]).astype(o_ref.dtype)
            off += n
    p = _dot(h, wkv_ref[0])
    xn = _rms(p[:, :MLA_KV_RANK], kg_ref[...]).astype(BF16)
    kk = _dot(xn, wk_ref[0])
    vv = _dot(xn, wv_ref[0])
    rot = _rotate(p[:, MLA_KV_RANK:], cs_ref[...])
    lane = lax.broadcasted_iota(jnp.int32, rot.shape, 1)
    rot = jnp.where(lane < MLA_ROPE, rot, 0.0).astype(BF16)
    ones = jnp.where(lane == 0, 1.0, 0.0).astype(BF16)
    for hd in range(MLA_HEADS):
        k_ref[0, :, hd * QK_HEAD:hd * QK_HEAD + LANES] = kk[:, hd * LANES:(hd + 1) * LANES].astype(BF16)
        k_ref[0, :, hd * QK_HEAD + LANES:(hd + 1) * QK_HEAD] = rot
        v_ref[0, :, hd * V_HEAD:hd * V_HEAD + MLA_V_DIM] = vv[:, hd * MLA_V_DIM:(hd + 1) * MLA_V_DIM].astype(BF16)
        v_ref[0, :, hd * V_HEAD + MLA_V_DIM:(hd + 1) * V_HEAD] = ones


def _in_proj_call(x, g, shift, scale, proj, w_kv, kv_g, w_k, w_v, cs, layer):
    b, l, d = x.shape
    tm = min(l, 512)
    r = MLA_KV_RANK
    row = pl.BlockSpec((1, 1, d), lambda i, j: (i, 0, 0))
    const = lambda shape: pl.BlockSpec(shape, lambda i, j: (0, 0))
    of_layer = lambda a, **kw: pl.BlockSpec((1,) + a.shape[1:], lambda i, j: (layer, 0, 0), **kw)
    weights = [w for w, _ in proj]
    widths = [k for _, ks in proj for k in ks] + [MLA_HEADS * QK_HEAD, MLA_HEADS * V_HEAD]
    return pl.pallas_call(
        functools.partial(_in_proj_kernel, n_proj=len(weights)),
        grid=(b, l // tm),
        in_specs=[pl.BlockSpec((1, tm, d), lambda i, j: (i, j, 0)),
                  const((1, d)), row, row,
                  *[of_layer(w, pipeline_mode=pl.Buffered(1)) for w in weights],
                  of_layer(w_kv, pipeline_mode=pl.Buffered(1)),
                  const((1, r)), of_layer(w_k), of_layer(w_v),
                  pl.BlockSpec((tm, LANES), lambda i, j: (j, 0))],
        out_specs=[pl.BlockSpec((1, tm, k), lambda i, j: (i, j, 0)) for k in widths],
        out_shape=[jax.ShapeDtypeStruct((b, l, k), BF16) for k in widths],
        compiler_params=_cp("parallel", "parallel"),
        name="in_proj",
    )(x, g.reshape(1, d), shift, scale, *weights, w_kv, kv_g.reshape(1, r), w_k, w_v, cs)


CONV_ROWS = 128
CONV_PAD = 16


def _conformer_kernel(p_ref, w_ref, b_ref, g_ref, be_ref, o_ref, zp_ref):
    l = p_ref.shape[1]
    c = CONV_W
    zeros = jnp.zeros((CONV_PAD, c), F32)
    zp_ref[0:CONV_PAD, :] = zeros
    zp_ref[CONV_PAD + l:CONV_PAD + l + CONV_PAD, :] = zeros

    def glu(r, carry):
        base = pl.multiple_of(r * CONV_ROWS, CONV_ROWS)
        blk = p_ref[0, pl.ds(base, CONV_ROWS), :].astype(F32)
        zp_ref[pl.ds(base + CONV_PAD, CONV_ROWS), :] = blk[:, :c] * jax.nn.sigmoid(blk[:, c:])
        return carry

    lax.fori_loop(0, l // CONV_ROWS, glu, 0)

    win_rows = CONV_ROWS + 2 * CONV_PAD
    first = CONV_PAD - CONV_KSIZE // 2

    def conv(r, carry):
        base = pl.multiple_of(r * CONV_ROWS, CONV_ROWS)
        parts = []
        for cc in range(c // LANES):
            lanes = slice(cc * LANES, (cc + 1) * LANES)
            win = zp_ref[pl.ds(base, win_rows), lanes]
            acc = jnp.zeros((CONV_ROWS, LANES), F32)
            for res in range(8):
                taps = [k for k in range(CONV_KSIZE) if (k + first) % 8 == res]
                if not taps:
                    continue
                shifted = pltpu.roll(win, win_rows - res, axis=0) if res else win
                for k in taps:
                    off = k + first - res
                    acc = acc + shifted[off:off + CONV_ROWS, :] * w_ref[k:k + 1, lanes]
            parts.append(acc)
        z = jnp.concatenate(parts, axis=1) + b_ref[...]
        mu = jnp.mean(z, axis=-1, keepdims=True)
        zc = z - mu
        var = jnp.mean(zc * zc, axis=-1, keepdims=True)
        y = zc * lax.rsqrt(var + EPS) * g_ref[...] + be_ref[...]
        o_ref[0, pl.ds(base, CONV_ROWS), :] = (y * jax.nn.sigmoid(y)).astype(o_ref.dtype)
        return carry

    lax.fori_loop(0, l // CONV_ROWS, conv, 0)


def _conformer_call(p, dw_w, dw_b, ln_g, ln_b):
    b, l, c2 = p.shape
    c = c2 // 2
    vec = pl.BlockSpec((1, c), lambda i: (0, 0))
    return pl.pallas_call(
        _conformer_kernel,
        grid=(b,),
        in_specs=[pl.BlockSpec((1, l, c2), lambda i: (i, 0, 0)),
                  pl.BlockSpec((CONV_KSIZE, c), lambda i: (0, 0)), vec, vec, vec],
        out_specs=pl.BlockSpec((1, l, c), lambda i: (i, 0, 0)),
        out_shape=jax.ShapeDtypeStruct((b, l, c), BF16),
        scratch_shapes=[pltpu.VMEM((l + 2 * CONV_PAD, c), F32)],
        compiler_params=_cp("parallel"),
        name="conformer",
    )(p, dw_w, dw_b.reshape(1, c), ln_g.reshape(1, c), ln_b.reshape(1, c))


def _rotate(group, cs):
    t = group * cs
    return t + pltpu.roll(t, MLA_ROPE, axis=1)


def _attn_kernel(p_ref, g_ref, w_ref, cs_ref, *refs, n_seg):
    o_ref, q_ref = refs[2 * n_seg], refs[2 * n_seg + 1]
    xn = _rms(p_ref[0].astype(F32), g_ref[...]).astype(BF16)
    cs = cs_ref[...] * Q_SCALE
    for h in range(MLA_HEADS):
        r = _dot(xn, w_ref[0, :, h * QK_HEAD:(h + 1) * QK_HEAD])
        q_ref[:, h * QK_HEAD:h * QK_HEAD + LANES] = (r[:, :LANES] * Q_SCALE).astype(BF16)
        q_ref[:, h * QK_HEAD + LANES:(h + 1) * QK_HEAD] = _rotate(r[:, LANES:], cs).astype(BF16)
    nt = (((1,), (1,)), ((), ()))
    for h in range(MLA_HEADS):
        qk = slice(h * QK_HEAD, (h + 1) * QK_HEAD)
        vd = slice(h * MLA_V_DIM, (h + 1) * MLA_V_DIM)
        q = q_ref[:, qk]
        scores = [lax.dot_general(q, refs[2 * s][0, :, qk], nt, preferred_element_type=F32).astype(BF16)
                  for s in range(n_seg)]
        m = functools.reduce(jnp.maximum, [jnp.max(s, axis=-1, keepdims=True) for s in scores])
        acc = 0.0
        for s in range(n_seg):
            p = jnp.exp2(scores[s] - m)
            acc = acc + _dot(p, refs[2 * s + 1][0, :, h * V_HEAD:(h + 1) * V_HEAD])
        o_ref[0, :, vd] = (acc[:, :MLA_V_DIM] * (1.0 / acc[:, MLA_V_DIM:MLA_V_DIM + 1])).astype(o_ref.dtype)


def _attn_call(p_q, q_g, w_q, cs, segs, layer):
    b, l, r = p_q.shape
    tq = min(l, 512)
    const = lambda shape: pl.BlockSpec(shape, lambda i, j: (0, 0))
    in_specs = [pl.BlockSpec((1, tq, r), lambda i, j: (i, j, 0)), const((1, r)),
                pl.BlockSpec((1,) + w_q.shape[1:], lambda i, j: (layer, 0, 0)),
                pl.BlockSpec((tq, LANES), lambda i, j: (j, 0))]
    args = [p_q, q_g.reshape(1, r), w_q, cs]
    for k, v in segs:
        in_specs.append(pl.BlockSpec((1,) + k.shape[1:], lambda i, j: (i, 0, 0)))
        in_specs.append(pl.BlockSpec((1,) + v.shape[1:], lambda i, j: (i, 0, 0)))
        args += [k, v]
    return pl.pallas_call(
        functools.partial(_attn_kernel, n_seg=len(segs)),
        grid=(b, l // tq),
        in_specs=in_specs,
        out_specs=pl.BlockSpec((1, tq, MLA_HEADS * MLA_V_DIM), lambda i, j: (i, j, 0)),
        out_shape=jax.ShapeDtypeStruct((b, l, MLA_HEADS * MLA_V_DIM), BF16),
        scratch_shapes=[pltpu.VMEM((tq, MLA_HEADS * QK_HEAD), BF16)],
        compiler_params=_cp("parallel", "parallel"),
        name="attention",
    )(*args)


HY_PAD = 8
HY_ROWS = 128


def _hy_short_kernel(p_ref, w_ref, b_ref, o_ref, up_ref):
    l = p_ref.shape[1]
    c = p_ref.shape[2]
    zeros = jnp.zeros((HY_PAD, c), F32)
    up_ref[0:HY_PAD, :] = zeros
    up_ref[HY_PAD + l:HY_PAD + l + HY_PAD, :] = zeros

    def copy(r, carry):
        base = pl.multiple_of(r * HY_ROWS, HY_ROWS)
        up_ref[pl.ds(base + HY_PAD, HY_ROWS), :] = p_ref[0, pl.ds(base, HY_ROWS), :].astype(F32)
        return carry

    lax.fori_loop(0, l // HY_ROWS, copy, 0)
    first = HY_PAD - HY_SHORT // 2

    def conv(r, carry):
        base = pl.multiple_of(r * HY_ROWS, HY_ROWS)
        for cc in range(c // LANES):
            lanes = slice(cc * LANES, (cc + 1) * LANES)
            win = up_ref[pl.ds(base, HY_ROWS + 2 * HY_PAD), lanes]
            acc = jnp.zeros((HY_ROWS, LANES), F32) + b_ref[:, lanes]
            for k in range(HY_SHORT):
                acc = acc + win[first + k:first + k + HY_ROWS, :] * w_ref[k:k + 1, lanes]
            o_ref[0, 0, pl.ds(base, HY_ROWS), lanes] = acc.astype(o_ref.dtype)
        return carry

    lax.fori_loop(0, l // HY_ROWS, conv, 0)


def _hy_short_call(p, w, bias):
    b, l, c3 = p.shape
    c = HYENA_W
    return pl.pallas_call(
        _hy_short_kernel,
        grid=(b, c3 // c),
        in_specs=[pl.BlockSpec((1, l, c), lambda i, g: (i, 0, g)),
                  pl.BlockSpec((HY_SHORT, c), lambda i, g: (0, g)),
                  pl.BlockSpec((1, c), lambda i, g: (0, g))],
        out_specs=pl.BlockSpec((1, 1, l, c), lambda i, g: (g, i, 0, 0)),
        out_shape=jax.ShapeDtypeStruct((c3 // c, b, l, c), BF16),
        scratch_shapes=[pltpu.VMEM((l + 2 * HY_PAD, c), F32)],
        compiler_params=_cp("parallel", "parallel"),
        name="hyena_short",
    )(p, w, bias.reshape(1, c3))


def _hy_filter_kernel(z_ref, w1_ref, b1_ref, f1_ref, w2_ref, b2_ref, f2_ref, w3_ref, b3_ref, win_ref, o_ref):
    hp = lax.Precision.HIGHEST

    def mm(a, b):
        return jnp.dot(a, b, preferred_element_type=F32, precision=hp)

    h = jnp.sin(f1_ref[...] * (mm(z_ref[...], w1_ref[...]) + b1_ref[...]))
    h = jnp.sin(f2_ref[...] * (mm(h, w2_ref[...]) + b2_ref[...]))
    h = mm(h, w3_ref[...]) + b3_ref[...]
    c = HYENA_W
    win = win_ref[...]
    for s in range(h.shape[1] // c):
        o_ref[:, s * c:(s + 1) * c] = h[:, s * c:(s + 1) * c] * win


def _pad2(a, rows, cols):
    return jnp.pad(a, ((0, rows - a.shape[0]), (0, cols - a.shape[1])))


def _hy_filter_call(l, t, w1, b1, f1, w2, b2, f2, w3, b3):
    t_norm = t / max(l - 1, 1)
    ang = (2.0 * math.pi / l) * t[:, None] * jnp.linspace(1e-4, HY_BANDS - 1, HY_BANDS, dtype=F32)[None, :]
    z = jnp.concatenate([t_norm[:, None], jnp.cos(ang), -jnp.sin(ang)], axis=-1)
    deltas = jnp.abs(jnp.linspace(HY_MIN_DECAY, HY_MAX_DECAY, HYENA_W, dtype=F32))
    window = jnp.exp(-t_norm[:, None] * deltas[None, :])
    n3 = w3.shape[1]
    by_dir = lambda a: a.reshape(a.shape[0], HY_ORDER, 2, HYENA_W).transpose(0, 2, 1, 3).reshape(a.shape[0], n3)
    w3, b3 = by_dir(w3), by_dir(b3[None])[0]
    l = t.shape[0]
    tl = min(l, 256)
    full = lambda shape: pl.BlockSpec(shape, lambda i: (0, 0))
    return pl.pallas_call(
        _hy_filter_kernel,
        grid=(l // tl,),
        in_specs=[pl.BlockSpec((tl, LANES), lambda i: (i, 0)),
                  full((LANES, LANES)), full((1, LANES)), full((1, LANES)),
                  full((LANES, LANES)), full((1, LANES)), full((1, LANES)),
                  full((LANES, n3)), full((1, n3)),
                  pl.BlockSpec((tl, HYENA_W), lambda i: (i, 0))],
        out_specs=pl.BlockSpec((tl, n3), lambda i: (i, 0)),
        out_shape=jax.ShapeDtypeStruct((l, n3), F32),
        compiler_params=_cp("parallel"),
        name="hyena_filter",
    )(_pad2(z, l, LANES), _pad2(w1, LANES, LANES), _pad2(b1[None], 1, LANES), _pad2(f1[None], 1, LANES),
      _pad2(w2, LANES, LANES), _pad2(b2[None], 1, LANES), _pad2(f2[None], 1, LANES),
      _pad2(w3, LANES, n3), b3[None], window)


def _dft_tables_kernel(ac_ref, as_ref, bc_ref, bs_ref, fwd_ref, inv_ref):
    tm, l = fwd_ref.shape[1], fwd_ref.shape[2]
    n = 2 * l
    row = lax.broadcasted_iota(jnp.int32, (tm, LANES), 0) + pl.program_id(0) * tm
    lane = lax.broadcasted_iota(jnp.int32, (tm, LANES), 1)
    row_sign = jnp.where((row & 1) == 0, 1.0, -1.0)
    lane_sign = jnp.where((lane & 1) == 0, 1.0, -1.0)
    bc = bc_ref[...]
    bs = bs_ref[...]
    for a in range(l // LANES):
        cols = slice(a * LANES, (a + 1) * LANES)
        ca = ac_ref[:, a:a + 1]
        sa = as_ref[:, a:a + 1]
        c = ca * bc - sa * bs
        s = sa * bc + ca * bs
        fwd_ref[0, :, cols] = c.astype(BF16)
        fwd_ref[1, :, cols] = jnp.where(row == 0, lane_sign, -s).astype(BF16)
        inv_c = c * (2.0 / n)
        inv_s = -s * (2.0 / n)
        if a == 0:
            inv_c = jnp.where(lane == 0, 1.0 / n, inv_c)
            inv_s = jnp.where(lane == 0, row_sign / n, inv_s)
        inv_ref[:, cols] = inv_c.astype(BF16)
        inv_ref[:, l + a * LANES:l + (a + 1) * LANES] = inv_s.astype(BF16)


def _dft_mats(l):
    n = 2 * l
    f = jnp.arange(l, dtype=jnp.int32)[:, None]

    def table(t):
        ang = ((f * t[None, :]) % n).astype(F32) * (2.0 * math.pi / n)
        return jnp.cos(ang), jnp.sin(ang)

    coarse = table(jnp.arange(l // LANES, dtype=jnp.int32) * LANES)
    fine = table(jnp.arange(LANES, dtype=jnp.int32))
    tm = min(l, 256)
    small = pl.BlockSpec((tm, l // LANES), lambda i: (i, 0))
    wide = pl.BlockSpec((tm, LANES), lambda i: (i, 0))
    return pl.pallas_call(
        _dft_tables_kernel,
        grid=(l // tm,),
        in_specs=[small, small, wide, wide],
        out_specs=[pl.BlockSpec((2, tm, l), lambda i: (0, i, 0)), pl.BlockSpec((tm, n), lambda i: (i, 0))],
        out_shape=[jax.ShapeDtypeStruct((2, l, l), BF16), jax.ShapeDtypeStruct((l, n), BF16)],
        compiler_params=_cp("parallel"),
        name="dft_tables",
    )(*coarse, *fine)


def _conv_blocks(l):
    return 2 if l >= 1024 else 1


def _lag_window_plan(l):
    nb = _conv_blocks(l)
    lb = l // nb
    pos = jnp.arange(l, dtype=F32)
    if nb == 1:
        return pos, jnp.array([0, 0, 0, 1], jnp.int32)
    pos = jnp.concatenate([pos, lb - jnp.arange(lb, dtype=F32)])
    return pos, jnp.array([2, 0, 1, 1, 0, 0, 1, 0, 2, 1, 1, 0], jnp.int32)


def _spec_kernel(tab_ref, f_ref, a_ref, b_ref, o_ref):
    a = a_ref[...]
    row0 = lax.broadcasted_iota(jnp.int32, a.shape, 0) == 0
    b = jnp.where(row0, 0.0, b_ref[...])
    s = (a + b).astype(BF16)
    hi = _dot(f_ref[1], (a - b).astype(BF16))
    o_ref[0, 0] = _dot(f_ref[0], s)
    o_ref[0, 1] = hi

    @pl.when(pl.program_id(1) == 0)
    def _():
        nyq = _dot(f_ref[1, 0:16, :], s)
        row = lax.broadcasted_iota(jnp.int32, nyq.shape, 0)
        o_ref[0, 1, 0:16, :] = jnp.where(row == 0, nyq, hi[0:16])


def _spec_call(fwd, filt, table):
    _, lf, lb = fwd.shape
    nd = table.shape[0] // 4
    n = filt.shape[1] // 2
    tm = min(lf, 512)
    pick = lambda k: pl.BlockSpec((lb, n), lambda d, i, tab: (tab[2 * k * nd + d], tab[(2 * k + 1) * nd + d]))
    return pl.pallas_call(
        _spec_kernel,
        grid_spec=pltpu.PrefetchScalarGridSpec(
            num_scalar_prefetch=1,
            grid=(nd, lf // tm),
            in_specs=[pl.BlockSpec((2, tm, lb), lambda d, i, tab: (0, i, 0)), pick(0), pick(1)],
            out_specs=pl.BlockSpec((1, 2, tm, n), lambda d, i, tab: (d, 0, i, 0))),
        out_shape=jax.ShapeDtypeStruct((nd, 2, lf, n), F32),
        compiler_params=_cp("arbitrary", "arbitrary"),
        name="hyena_spectrum",
    )(table, fwd, filt, filt)


def _dft_fwd_kernel(f_ref, u_ref, h_ref, y_ref, *, nb):
    lb = f_ref.shape[2]
    xr, xi = [], []
    for p in range(nb):
        up = u_ref[0, 0, p * lb:(p + 1) * lb, :].astype(BF16)
        xr.append(_dot(f_ref[0], up))
        xi.append(_dot(f_ref[1], up))
    def products(rows, real_bins):
        for q in range(nb):
            yr = 0.0
            yi = 0.0
            for p in range(nb):
                gr = h_ref[q - p + nb - 1, 0, rows, :]
                gi = h_ref[q - p + nb - 1, 1, rows, :]
                ii = xi[p][rows] * gi
                cross = xr[p][rows] * gi + xi[p][rows] * gr
                yr = yr + xr[p][rows] * gr - (ii if real_bins is None else jnp.where(real_bins, 0.0, ii))
                yi = yi + (cross if real_bins is None else jnp.where(real_bins, ii, cross))
            y_ref[0, q, 0, rows, :] = yr.astype(BF16)
            y_ref[0, q, 1, rows, :] = yi.astype(BF16)

    products(slice(None), None)

    @pl.when(pl.program_id(0) == 0)
    def _():
        head = slice(0, 16)
        products(head, lax.broadcasted_iota(jnp.int32, (16, u_ref.shape[3]), 0) == 0)


def _dft_fwd_call(fwd, u, group, spec, order, nb):
    _, b, l, c = u.shape
    _, lf, lb = fwd.shape
    nd = spec.shape[0]
    tm = min(lf, 512)
    return pl.pallas_call(
        functools.partial(_dft_fwd_kernel, nb=nb),
        grid=(lf // tm, b),
        in_specs=[pl.BlockSpec((2, tm, lb), lambda i, j: (0, i, 0)),
                  pl.BlockSpec((1, 1, l, c), lambda i, j: (group, j, 0, 0)),
                  pl.BlockSpec((nd, 2, tm, c), lambda i, j: (0, 0, i, order))],
        out_specs=pl.BlockSpec((1, nb, 2, tm, c), lambda i, j: (j, 0, 0, i, 0)),
        out_shape=jax.ShapeDtypeStruct((b, nb, 2, lf, c), BF16),
        compiler_params=_cp("arbitrary", "arbitrary"),
        name="hyena_dft_fwd",
    )(fwd, u, spec)


def _dft_inv_kernel(g_ref, y_ref, u_ref, b_ref, x_ref, o_ref):
    r = _dot(g_ref[...], y_ref[0, 0]) + u_ref[0, 0].astype(F32) * b_ref[...]
    o_ref[0, 0] = (x_ref[0, 0] * r).astype(o_ref.dtype)


def _dft_inv_call(inv, y, u, u_group, bias, gate, gate_group):
    _, b, l, c = u.shape
    lb, n = inv.shape
    nb = l // lb
    tm = min(lb, 1024)
    nt = lb // tm
    y = y.reshape(b, nb, n, c)
    row = lambda g: pl.BlockSpec((1, 1, tm, c), lambda q, i, j: (g, j, q * nt + i, 0))
    return pl.pallas_call(
        _dft_inv_kernel,
        grid=(nb, nt, b),
        in_specs=[pl.BlockSpec((tm, n), lambda q, i, j: (i, 0)),
                  pl.BlockSpec((1, 1, n, c), lambda q, i, j: (j, q, 0, 0)),
                  row(u_group), pl.BlockSpec((1, c), lambda q, i, j: (0, 0)), row(gate_group)],
        out_specs=row(0),
        out_shape=jax.ShapeDtypeStruct((1, b, l, c), BF16),
        compiler_params=_cp("parallel", "parallel", "parallel"),
        name="hyena_dft_inv",
    )(inv, y, u, bias.reshape(1, c), gate)


def _hyena_call(p_hy, sh_w, sh_b, spec, hy_bias, fwd, inv, nb):
    u3 = _hy_short_call(p_hy, sh_w, sh_b)
    z = _dft_inv_call(inv, _dft_fwd_call(fwd, u3, 0, spec, 0, nb), u3, 0, hy_bias[0], u3, 1)
    return _dft_inv_call(inv, _dft_fwd_call(fwd, z, 0, spec, 1, nb), z, 0, hy_bias[1], u3, 2)[0]


def _mix_out_kernel(c_ref, a_ref, h_ref, w_ref, x_ref, g1_ref, g_ref, sh_ref, sc_ref,
                    wr_ref, rb_ref, o_ref, h32_ref, idx_ref, gate_ref):
    k1 = c_ref.shape[2]
    k2 = k1 + a_ref.shape[2]
    acc = _dot(c_ref[0], w_ref[0, :k1, :]) + _dot(a_ref[0], w_ref[0, k1:k2, :]) + _dot(h_ref[0], w_ref[0, k2:, :])
    x = x_ref[0] + g1_ref[0] * acc
    o_ref[0] = x
    hm = _rms(x, g_ref[...]) * (1.0 + sc_ref[0]) + sh_ref[0]
    h32_ref[0] = _pack_rows(hm)
    picks, gates = _route(hm, wr_ref[...], rb_ref[...])
    idx_ref[...] = jnp.concatenate(picks, axis=0)
    gate_ref[...] = jnp.concatenate(gates, axis=0)


def _mix_out_call(conv_o, att_o, hy_o, w_out, layer, x, mod, g2, wr, rbias):
    b, l, d = x.shape
    tm = min(l, 512)
    nt = l // tm
    act = lambda k: pl.BlockSpec((1, tm, k), lambda i, j: (i, j, 0))
    row = pl.BlockSpec((1, 1, d), lambda i, j: (i, 0, 0))
    tok = pl.BlockSpec((2, tm), lambda i, j: (0, i * nt + j))
    return pl.pallas_call(
        _mix_out_kernel,
        grid=(b, nt),
        in_specs=[act(conv_o.shape[2]), act(att_o.shape[2]), act(hy_o.shape[2]),
                  pl.BlockSpec((1,) + w_out.shape[1:], lambda i, j: (layer, 0, 0), pipeline_mode=pl.Buffered(1)),
                  act(d), row, pl.BlockSpec((1, d), lambda i, j: (0, 0)), row, row,
                  pl.BlockSpec(wr.shape, lambda i, j: (0, 0)),
                  pl.BlockSpec((N_EXPERTS, 1), lambda i, j: (0, 0))],
        out_specs=[act(d), act(d // 2), tok, tok],
        out_shape=[jax.ShapeDtypeStruct((b, l, d), F32),
                   jax.ShapeDtypeStruct((b, l, d // 2), jnp.int32),
                   jax.ShapeDtypeStruct((2, b * l), jnp.int32),
                   jax.ShapeDtypeStruct((2, b * l), F32)],
        compiler_params=_cp("parallel", "parallel"),
        name="mix_out",
    )(conv_o, att_o, hy_o, w_out, x, mod[2], g2.reshape(1, d), mod[3], mod[4], wr,
      rbias.reshape(N_EXPERTS, 1))


def _route(hm, wr, rbias):
    hi = hm.astype(BF16)
    lo = (hm - hi.astype(F32)).astype(BF16)
    nt = (((1,), (1,)), ((), ()))
    r = (lax.dot_general(wr, hi, nt, preferred_element_type=F32)
         + lax.dot_general(wr, lo, nt, preferred_element_type=F32))
    logits = r[:N_EXPERTS] + r[N_EXPERTS:]
    score = jax.nn.sigmoid(logits)
    sel = score + rbias
    rows = [sel[e:e + 1] for e in range(N_EXPERTS)]
    gscore = []
    for g in range(N_GROUPS):
        a, b, c, d = rows[4 * g:4 * g + 4]
        h1, l1, h2, l2 = jnp.maximum(a, b), jnp.minimum(a, b), jnp.maximum(c, d), jnp.minimum(c, d)
        gscore.append(jnp.maximum(h1, h2) + jnp.maximum(jnp.minimum(h1, h2), jnp.maximum(l1, l2)))
    best = gscore[0]
    gbest = jnp.zeros(best.shape, jnp.int32)
    for g in range(1, N_GROUPS):
        upd = gscore[g] > best
        gbest = jnp.where(upd, g, gbest)
        best = jnp.where(upd, gscore[g], best)
    masked = [jnp.where(gbest == e // EXPERTS_PER_GROUP, rows[e], NEG_INF) for e in range(N_EXPERTS)]
    picks = []
    for _ in range(2):
        val = jnp.full(best.shape, NEG_INF, F32)
        idx = jnp.zeros(best.shape, jnp.int32)
        for e in range(N_EXPERTS):
            cand = masked[e]
            if picks:
                cand = jnp.where(picks[0] == e, NEG_INF, cand)
            upd = cand > val
            idx = jnp.where(upd, e, idx)
            val = jnp.where(upd, cand, val)
        picks.append(idx)
    gates = [functools.reduce(jnp.add, [jnp.where(p == e, score[e:e + 1], 0.0) for e in range(N_EXPERTS)])
             for p in picks]
    den = gates[0] + gates[1]
    return picks, [gates[0] / den, gates[1] / den]


def _expert_changed(te_ref, i):
    return (i == 0) | (te_ref[i] != te_ref[jnp.maximum(i - 1, 0)])


def _pack_rows(x):
    bits = pltpu.bitcast(x.astype(BF16).astype(F32), jnp.uint32)
    half = bits.shape[1] // 2
    return pltpu.bitcast((bits[:, :half] >> 16) | (bits[:, half:] & jnp.uint32(0xFFFF0000)), jnp.int32)


def _unpack_rows(words):
    w = pltpu.bitcast(words, jnp.uint32)
    return pltpu.bitcast(w << 16, F32), pltpu.bitcast(w & jnp.uint32(0xFFFF0000), F32)


def _moe_up_kernel(te_ref, tn_ref, nu_ref, x_ref, wg_hbm, wu_hbm, h_ref, gbuf, ubuf, wgb_ref, wub_ref, sems,
                   *, layer):
    j, i = pl.program_id(0), pl.program_id(1)
    tf = gbuf.shape[1]

    def fetch(e, chunk):
        cols = pl.ds(pl.multiple_of(chunk * tf, tf), tf)
        return (pltpu.make_async_copy(wg_hbm.at[layer, e, :, cols], gbuf, sems.at[0]),
                pltpu.make_async_copy(wu_hbm.at[layer, e, :, cols], ubuf, sems.at[1]))

    def start(e, chunk):
        for copy in fetch(e, chunk):
            copy.start()

    @pl.when((i == 0) & (j == 0))
    def _():
        start(te_ref[0], 0)

    @pl.when(_expert_changed(te_ref, i))
    def _():
        for copy in fetch(te_ref[i], j):
            copy.wait()
        wgb_ref[...] = gbuf[...].astype(BF16)
        wub_ref[...] = ubuf[...].astype(BF16)
        nxt = tn_ref[i]
        more = nxt < nu_ref[0]

        @pl.when(more)
        def _():
            start(te_ref[nxt], j)

        @pl.when(jnp.logical_not(more) & (j + 1 < pl.num_programs(0)))
        def _():
            start(te_ref[0], j + 1)

    @pl.when(i < nu_ref[0])
    def _():
        xa, xb = (v.astype(BF16) for v in _unpack_rows(x_ref[...]))
        half = xa.shape[1]
        a = _dot(xa, wgb_ref[:half, :]) + _dot(xb, wgb_ref[half:, :])
        u = _dot(xa, wub_ref[:half, :]) + _dot(xb, wub_ref[half:, :])
        h_ref[...] = (a * jax.nn.sigmoid(a) * u).astype(BF16)

    @pl.when(i >= nu_ref[0])
    def _():
        h_ref[...] = jnp.zeros(h_ref.shape, BF16)


def _moe_up_call(xs, w_gate, w_up, layer, te, tn, nu, tile):
    p, dw = xs.shape
    _, _, d, f = w_gate.shape
    tf = f
    hbm = pl.BlockSpec(memory_space=pl.ANY)
    return pl.pallas_call(
        functools.partial(_moe_up_kernel, layer=layer),
        grid_spec=pltpu.PrefetchScalarGridSpec(
            num_scalar_prefetch=3,
            grid=(f // tf, p // tile),
            in_specs=[pl.BlockSpec((tile, dw), lambda j, i, te, tn, nu: (i, 0)), hbm, hbm],
            out_specs=pl.BlockSpec((tile, tf), lambda j, i, te, tn, nu: (i, j)),
            scratch_shapes=[pltpu.VMEM((d, tf), F32), pltpu.VMEM((d, tf), F32),
                            pltpu.VMEM((d, tf), BF16), pltpu.VMEM((d, tf), BF16),
                            pltpu.SemaphoreType.DMA((2,))]),
        out_shape=jax.ShapeDtypeStruct((p, f), BF16),
        compiler_params=_cp("arbitrary", "arbitrary"),
        name="moe_up",
    )(te, tn, nu, xs, w_gate, w_up)


def _moe_down_kernel(te_ref, tn_ref, nu_ref, h_ref, wd_hbm, y_ref, wbuf, wdb_ref, sem, *, layer):
    i = pl.program_id(0)

    def fetch(e):
        return pltpu.make_async_copy(wd_hbm.at[layer, e], wbuf, sem)

    @pl.when(i == 0)
    def _():
        fetch(te_ref[0]).start()

    @pl.when(_expert_changed(te_ref, i))
    def _():
        fetch(te_ref[i]).wait()
        wdb_ref[...] = wbuf[...].astype(BF16)
        nxt = tn_ref[i]

        @pl.when(nxt < nu_ref[0])
        def _():
            fetch(te_ref[nxt]).start()

    @pl.when(i < nu_ref[0])
    def _():
        y_ref[...] = _pack_rows(_dot(h_ref[...], wdb_ref[...]))

    @pl.when(i >= nu_ref[0])
    def _():
        y_ref[...] = jnp.zeros(y_ref.shape, jnp.int32)


def _moe_down_call(h, w_down, layer, te, tn, nu, tile):
    p, f = h.shape
    d = w_down.shape[3]
    return pl.pallas_call(
        functools.partial(_moe_down_kernel, layer=layer),
        grid_spec=pltpu.PrefetchScalarGridSpec(
            num_scalar_prefetch=3,
            grid=(p // tile,),
            in_specs=[pl.BlockSpec((tile, f), lambda i, te, tn, nu: (i, 0)),
                      pl.BlockSpec(memory_space=pl.ANY)],
            out_specs=pl.BlockSpec((tile, d // 2), lambda i, te, tn, nu: (i, 0)),
            scratch_shapes=[pltpu.VMEM((f, d), F32), pltpu.VMEM((f, d), BF16), pltpu.SemaphoreType.DMA(())]),
        out_shape=jax.ShapeDtypeStruct((p, d // 2), jnp.int32),
        compiler_params=_cp("arbitrary"),
        name="moe_down",
    )(te, tn, nu, h, w_down)


PLAN_BLOCK = 512


def _plan_kernel(idx_ref, pos_ref, te_ref, tn_ref, nu_ref, rank_ref, *, tile):
    t = idx_ref.shape[1]
    nb = t // PLAN_BLOCK
    experts = lax.broadcasted_iota(jnp.int32, (N_EXPERTS, PLAN_BLOCK), 0)
    r = lax.broadcasted_iota(jnp.int32, (PLAN_BLOCK, PLAN_BLOCK), 0)
    c = lax.broadcasted_iota(jnp.int32, (PLAN_BLOCK, PLAN_BLOCK), 1)
    before = jnp.where(r < c, 1.0, 0.0).astype(BF16)

    def rank_pass(k):
        def body(j, carry):
            cols = pl.ds(pl.multiple_of(j * PLAN_BLOCK, PLAN_BLOCK), PLAN_BLOCK)
            onehot = jnp.where(experts == idx_ref[k:k + 1, cols], 1.0, 0.0)
            earlier = _dot(onehot.astype(BF16), before) + carry
            rank_ref[k:k + 1, cols] = jnp.sum(onehot * earlier, axis=0, keepdims=True)
            return carry + jnp.sum(onehot, axis=1, keepdims=True)
        return body

    counts = jnp.zeros((N_EXPERTS, 1), F32)
    for k in range(2):
        counts = lax.fori_loop(0, nb, rank_pass(k), counts)
    padded = jnp.ceil(counts * (1.0 / tile)) * tile
    starts, ends = [], []
    acc = jnp.zeros((1, 1), F32)
    for e in range(N_EXPERTS):
        starts.append(acc)
        acc = acc + padded[e:e + 1]
        ends.append(acc)

    def place_pass(k):
        def body(j, carry):
            cols = pl.ds(pl.multiple_of(j * PLAN_BLOCK, PLAN_BLOCK), PLAN_BLOCK)
            row = idx_ref[k:k + 1, cols]
            start = functools.reduce(jnp.add, [jnp.where(row == e, starts[e], 0.0) for e in range(N_EXPERTS)])
            pos_ref[k:k + 1, cols] = (rank_ref[k:k + 1, cols] + start).astype(jnp.int32)
            return carry
        return body

    for k in range(2):
        lax.fori_loop(0, nb, place_pass(k), 0)
    n_used = acc * (1.0 / tile)
    tiles = lax.broadcasted_iota(jnp.int32, te_ref.shape, 1).astype(F32)
    first_row = jnp.minimum(tiles, n_used - 1.0) * tile
    te = functools.reduce(jnp.add, [jnp.where(ends[e] <= first_row, 1.0, 0.0) for e in range(N_EXPERTS)])
    te_ref[...] = te.astype(jnp.int32)
    seg_end = functools.reduce(jnp.add, [jnp.where(te == e, ends[e], 0.0) for e in range(N_EXPERTS)])
    tn_ref[...] = (seg_end * (1.0 / tile)).astype(jnp.int32)
    nu_ref[...] = jnp.broadcast_to(n_used, nu_ref.shape).astype(jnp.int32)


def _plan_call(idx, tile, n_tiles):
    t = idx.shape[1]
    whole = lambda shape: pl.BlockSpec(shape, lambda: (0,) * len(shape))
    lanes = jax.ShapeDtypeStruct((1, LANES), jnp.int32)
    pos, te, tn, nu = pl.pallas_call(
        functools.partial(_plan_kernel, tile=tile),
        in_specs=[whole(idx.shape)],
        out_specs=[whole(idx.shape), whole((1, LANES)), whole((1, LANES)), whole((1, LANES))],
        out_shape=[jax.ShapeDtypeStruct(idx.shape, jnp.int32), lanes, lanes, lanes],
        scratch_shapes=[pltpu.VMEM(idx.shape, F32)],
        name="route_plan",
    )(idx)
    return pos.reshape(2 * t), te[0, :n_tiles], tn[0, :n_tiles], nu[0, :1]


V7X_SC_CORES = 2
V7X_SC_SUBCORES = 16
SC_WORKERS = V7X_SC_CORES * V7X_SC_SUBCORES
SC_ROW_BYTES = 256 * 1024


def _sc_worker():
    return lax.axis_index("s") * V7X_SC_CORES + lax.axis_index("c")


def _sc_dispatch_call(streams, pos, n_rows):
    w = streams[0].shape[1]
    sizes = [s.shape[0] for s in streams]
    t = sum(sizes)
    per_w = t // SC_WORKERS
    ch = min(per_w, SC_ROW_BYTES // (4 * w))
    assert per_w % ch == 0 and all(size % ch == 0 for size in sizes)
    mesh = plsc.VectorSubcoreMesh(core_axis_name="c", subcore_axis_name="s")

    @functools.partial(pl.kernel, mesh=mesh, out_type=jax.ShapeDtypeStruct((n_rows, w), jnp.int32),
                       scratch_types=[pltpu.VMEM((ch,), jnp.int32), pltpu.VMEM((ch,), jnp.int32),
                                      pltpu.VMEM((ch, w), jnp.int32),
                                      pltpu.SemaphoreType.DMA, pltpu.SemaphoreType.DMA],
                       name="sc_dispatch")
    def run(*refs):
        srcs = refs[:len(sizes)]
        pos_hbm, xs_hbm, idx0_v, idx1_v, rows_v, sem0, sem1 = refs[len(sizes):]
        first = _sc_worker() * per_w

        @pl.loop(0, per_w // ch)
        def _(j):
            base = first + j * ch
            start = 0
            for src, size in zip(srcs, sizes):
                def load(src=src, start=start):
                    pltpu.sync_copy(src.at[pl.ds(base - start, ch)], rows_v)
                if len(sizes) == 1:
                    load()
                else:
                    pl.when((base >= start) & (base < start + size))(load)
                start += size
            pltpu.sync_copy(pos_hbm.at[pl.ds(base, ch)], idx0_v)
            pltpu.sync_copy(pos_hbm.at[pl.ds(t + base, ch)], idx1_v)
            first_choice = pltpu.async_copy(rows_v, xs_hbm.at[idx0_v], sem0)
            second_choice = pltpu.async_copy(rows_v, xs_hbm.at[idx1_v], sem1)
            first_choice.wait()
            second_choice.wait()

    return run(*streams, pos)


def _sc_combine_call(ys, pos):
    _, d = ys.shape
    n = pos.shape[0]
    per_w = n // SC_WORKERS
    ch = min(per_w // 2, SC_ROW_BYTES // (8 * d))
    assert per_w % (2 * ch) == 0
    mesh = plsc.VectorSubcoreMesh(core_axis_name="c", subcore_axis_name="s")
    idx_buf = pltpu.VMEM((ch,), jnp.int32)
    row_buf = pltpu.VMEM((ch, d), jnp.int32)

    @functools.partial(pl.kernel, mesh=mesh, out_type=jax.ShapeDtypeStruct((n, d), jnp.int32),
                       scratch_types=[idx_buf, idx_buf, row_buf, row_buf] + [pltpu.SemaphoreType.DMA] * 4,
                       name="sc_combine")
    def run(ys_hbm, pos_hbm, out_hbm, idx_a, idx_b, rows_a, rows_b, gsem_a, gsem_b, wsem_a, wsem_b):
        first = _sc_worker() * per_w

        @pl.loop(0, per_w // (2 * ch))
        def _(j):
            base = first + j * (2 * ch)
            pltpu.sync_copy(pos_hbm.at[pl.ds(base, ch)], idx_a)
            pltpu.sync_copy(pos_hbm.at[pl.ds(base + ch, ch)], idx_b)
            gather_a = pltpu.async_copy(ys_hbm.at[idx_a], rows_a, gsem_a)
            gather_b = pltpu.async_copy(ys_hbm.at[idx_b], rows_b, gsem_b)
            gather_a.wait()
            write_a = pltpu.async_copy(rows_a, out_hbm.at[pl.ds(base, ch)], wsem_a)
            gather_b.wait()
            write_b = pltpu.async_copy(rows_b, out_hbm.at[pl.ds(base + ch, ch)], wsem_b)
            write_a.wait()
            write_b.wait()

    return run(ys, pos)


def _moe_call(streams, idx, w_gate, w_up, w_down, layer):
    t = idx.shape[1]
    tile = min(MOE_TILE, max(t // N_EXPERTS, 128))
    n_tiles = (2 * t) // tile + N_EXPERTS
    assert n_tiles <= LANES
    pos, te, tn, nu = _plan_call(idx, tile, n_tiles)
    xs = _sc_dispatch_call(streams, pos, n_tiles * tile)
    hid = _moe_up_call(xs, w_gate, w_up, layer, te, tn, nu, tile)
    ys = _moe_down_call(hid, w_down, layer, te, tn, nu, tile)
    return ys, pos.reshape(2, t)


def _moe_resid_kernel(x_ref, y0_ref, y1_ref, gt_ref, g_ref, *rest, final):
    gt = gt_ref[0]
    halves0 = _unpack_rows(y0_ref[0])
    halves1 = _unpack_rows(y1_ref[0])
    y = jnp.concatenate([gt[:, 0:1] * a + gt[:, 1:2] * b for a, b in zip(halves0, halves1)], axis=1)
    x = x_ref[0] + g_ref[0] * y
    if final:
        fg_ref, o_ref = rest
        x = _rms(x, fg_ref[...])
    else:
        (o_ref,) = rest
    o_ref[0] = x


def _moe_resid_call(x, y, first_row, gates, gate, final_g, batch0, nbatch):
    b, l, d = x.shape
    tl = min(l, 512)
    nt = l // tl
    tile = pl.BlockSpec((1, tl, d), lambda i, j: (batch0 + i, j, 0))
    choice = lambda k: pl.BlockSpec((1, tl, d // 2), lambda i, j: (k, first_row // tl + i * nt + j, 0))
    in_specs = [tile, choice(0), choice(1), pl.BlockSpec((1, tl, 2), lambda i, j: (batch0 + i, j, 0)),
                pl.BlockSpec((1, 1, d), lambda i, j: (batch0 + i, 0, 0))]
    args = [x, y, y, gates.T.reshape(b, l, 2), gate]
    if final_g is not None:
        in_specs.append(pl.BlockSpec((1, d), lambda i, j: (0, 0)))
        args.append(final_g.reshape(1, d))
    return pl.pallas_call(
        functools.partial(_moe_resid_kernel, final=final_g is not None),
        grid=(nbatch, nt),
        in_specs=in_specs,
        out_specs=tile,
        out_shape=jax.ShapeDtypeStruct((b, l, d), F32),
        input_output_aliases={0: 0},
        compiler_params=_cp("parallel", "parallel"),
        name="moe_residual",
    )(*args)


def _rope_perm(w):
    r = w.reshape(w.shape[:-1] + (2, 2, ROPE_FREQS))
    return jnp.stack([-r[..., 1, :], r[..., 0, :]], axis=-2).reshape(w.shape)


def _rope_table(l, rotary):
    if not rotary:
        return jnp.concatenate([jnp.ones((l, MLA_ROPE), F32), jnp.zeros((l, MLA_ROPE), F32)], axis=1)
    rows = l // GRID_W
    r = jnp.repeat(jnp.arange(rows), GRID_W)
    col = jnp.tile(jnp.arange(GRID_W), rows)
    pos = jnp.stack([r, col], axis=-1).astype(F32)
    inv = ROPE_THETA ** (-jnp.arange(ROPE_FREQS, dtype=F32) / ROPE_FREQS)
    ang = pos[:, :, None] * inv
    lay = lambda a: jnp.concatenate([a[:, 0], a[:, 0], a[:, 1], a[:, 1]], axis=1)
    return jnp.concatenate([lay(jnp.cos(ang)), lay(jnp.sin(ang))], axis=1)


def _prep_weights(w_in, w_uq, w_ukv, w_out):
    depth = w_in.shape[0]
    cast = lambda a: a.astype(BF16)
    kr = cast(w_in[:, :, OFF_KR:OFF_HY])
    w_kv_in = jnp.concatenate([cast(w_in[:, :, OFF_KV:OFF_KR]), kr, _rope_perm(kr)], axis=2)
    w_cq, w_hy = cast(w_in[:, :, :OFF_KV]), cast(w_in[:, :, OFF_HY:])
    uq = cast(w_uq).reshape(depth, MLA_Q_RANK, MLA_HEADS, MLA_NOPE + MLA_ROPE)
    w_q = jnp.concatenate([uq, _rope_perm(uq[..., MLA_NOPE:])], axis=3).reshape(depth, MLA_Q_RANK, -1)
    ukv = cast(w_ukv).reshape(depth, MLA_KV_RANK, MLA_HEADS, MLA_NOPE + MLA_V_DIM)
    w_k = ukv[..., :MLA_NOPE].reshape(depth, MLA_KV_RANK, MLA_HEADS * MLA_NOPE)
    w_v = ukv[..., MLA_NOPE:].reshape(depth, MLA_KV_RANK, MLA_HEADS * MLA_V_DIM)
    return dict(cq=w_cq, hy=w_hy, kv=w_kv_in, uq=w_q, uk=w_k, uv=w_v, out=cast(w_out))


def _router_weights(w_router):
    hi = w_router.astype(BF16)
    lo = (w_router - hi.astype(F32)).astype(BF16)
    return jnp.concatenate([hi.T, lo.T], axis=0)


def _mixer(x, mod, w, layer, lp, cs, spec, fwd, inv, kv_extra, full, wr, rbias):
    kv_args = (w["kv"], lp["kv_norm_g"], w["uk"], w["uv"], cs, layer)
    if not full:
        k, v = _in_proj_call(x, lp["norm1_g"], mod[0], mod[1], [], *kv_args)
        return None, k, v
    proj = [(w["cq"], [OFF_Q, MLA_Q_RANK]), (w["hy"], [(HY_ORDER + 1) * HYENA_W])]
    p_conv, p_q, p_hy, k, v = _in_proj_call(x, lp["norm1_g"], mod[0], mod[1], proj, *kv_args)
    conv_o = _conformer_call(p_conv, lp["conv_dw_w"], lp["conv_dw_b"], lp["conv_ln_g"], lp["conv_ln_b"])
    att_o = _attn_call(p_q, lp["q_norm_g"], w["uq"], cs, kv_extra + [(k, v)], layer)
    hy_o = _hyena_call(p_hy, lp["hy_short_w"], lp["hy_short_b"], spec, lp["hy_bias"], fwd, inv,
                       _conv_blocks(x.shape[1]))
    routed = _mix_out_call(conv_o, att_o, hy_o, w["out"], layer, x, mod, lp["norm2_g"], wr, rbias)
    return routed, k, v


def _channel(streams, experts, layer, final_g=None):
    d = streams[0][1][0].shape[2]
    idx = jnp.concatenate([r[2] for _, r in streams], axis=1)
    ys, pos = _moe_call([r[1].reshape(-1, d // 2) for _, r in streams], idx, *experts, layer)
    t = idx.shape[1]
    outs = [r[0] for _, r in streams]
    b0, l0, _ = outs[0].shape
    half = (b0 // 2) * l0
    for t0, t1 in ((0, half), (half, t)):
        y = _sc_combine_call(ys, pos[:, t0:t1].reshape(-1)).reshape(2, t1 - t0, d // 2)
        start = 0
        for k, (mod, (x, _, _, gates)) in enumerate(streams):
            b, l, _ = x.shape
            lo, hi = max(t0, start), min(t1, start + b * l)
            if lo < hi:
                outs[k] = _moe_resid_call(outs[k], y, lo - t0, gates, mod[5], final_g, (lo - start) // l,
                                          (hi - lo) // l)
            start += b * l
    return outs


def _filter_spectrum(l, lp, fwd):
    pos, table = _lag_window_plan(l)
    filt = _hy_filter_call(l, pos, lp["hy_w1"], lp["hy_b1"], lp["hy_freq1"], lp["hy_w2"], lp["hy_b2"],
                           lp["hy_freq2"], lp["hy_w3"], lp["hy_b3"])
    return _spec_call(fwd, filt, table)


def kernel(x, c, ctx, c_ctx, norm1_g, norm2_g, w_ada, b_ada, w_in, conv_dw_w, conv_dw_b, conv_ln_g, conv_ln_b, q_norm_g, w_uq, kv_norm_g, w_ukv, hy_short_w, hy_short_b, hy_w1, hy_b1, hy_freq1, hy_w2, hy_b2, hy_freq2, hy_w3, hy_b3, hy_bias, w_out, w_router, router_bias, w_gate, w_up, w_down, final_norm_g):
    b, l, d = x.shape
    lc = ctx.shape[1]
    depth = w_in.shape[0]
    per_layer = dict(norm1_g=norm1_g, norm2_g=norm2_g, conv_dw_w=conv_dw_w, conv_dw_b=conv_dw_b,
                     conv_ln_g=conv_ln_g, conv_ln_b=conv_ln_b, q_norm_g=q_norm_g, kv_norm_g=kv_norm_g,
                     hy_short_w=hy_short_w, hy_short_b=hy_short_b, hy_w1=hy_w1, hy_b1=hy_b1, hy_freq1=hy_freq1,
                     hy_w2=hy_w2, hy_b2=hy_b2, hy_freq2=hy_freq2, hy_w3=hy_w3, hy_b3=hy_b3, hy_bias=hy_bias)
    experts = (w_gate, w_up, w_down)
    cs_l = _rope_table(l, True)
    cs_c = _rope_table(lc, False)
    fwd_l, inv_l = _dft_mats(l // _conv_blocks(l))
    fwd_c, inv_c = _dft_mats(lc // _conv_blocks(lc))
    wr = _router_weights(w_router)
    w = _prep_weights(w_in, w_uq, w_ukv, w_out)
    rows = 16
    cvec = jnp.concatenate([c, c_ctx[None, :], jnp.zeros((rows - b - 1, d), F32)], axis=0)
    xl, xc = x, ctx
    for i in range(depth):
        last = i == depth - 1
        lp = {k: v[i] for k, v in per_layer.items()}
        m = _ada_call(cvec, w_ada, b_ada, i)
        mod_l = [m[:b, None, j * d:(j + 1) * d] for j in range(6)]
        mod_c = [jnp.broadcast_to(m[b:b + 1, None, j * d:(j + 1) * d], (b, 1, d)) for j in range(6)]
        spec_l = _filter_spectrum(l, lp, fwd_l)
        spec_c = None if last else _filter_spectrum(lc, lp, fwd_c)
        routed_c, kc, vc = _mixer(xc, mod_c, w, i, lp, cs_c, spec_c, fwd_c, inv_c, [], not last, wr, router_bias)
        routed_l, _, _ = _mixer(xl, mod_l, w, i, lp, cs_l, spec_l, fwd_l, inv_l, [(kc, vc)], True, wr, router_bias)
        if last:
            (xl,) = _channel([(mod_l, routed_l)], experts, i, final_norm_g)
        else:
            xl, xc = _channel([(mod_l, routed_l), (mod_c, routed_c)], experts, i)
    return xl
```
